```python
import math
import jax, jax.numpy as jnp
from jax import lax
import numpy as np

D_MODEL = 2048
BATCH = 16
SEQ = 2048
DEPTH = 2

SSD_HEADS = 32
SSD_HEAD_DIM = 64
SSD_INNER = SSD_HEADS * SSD_HEAD_DIM
SSD_GROUPS = 8
SSD_STATE = 128
SSD_CHUNK = 128
CONV_WIDTH = 5
XBC_WIDTH = SSD_INNER + 2 * SSD_GROUPS * SSD_STATE
DT_MIN = 0.001
DT_MAX = 0.1
HEAD_DIM = 128
ROPE_DIM = HEAD_DIM // 4
ROPE_THETA = 500000.0
DIL_PATTERNS = ((128, 1), (512, 4), (2048, 16))
DIL_GROUPS = len(DIL_PATTERNS)
DIL_HEADS = 8
DIL_WIDTH = DIL_HEADS * HEAD_DIM
WIN_Q_HEADS = 16
WIN_KV_HEADS = 4
WIN_HALF = 128
D_FF = 4 * D_MODEL
N_BRANCH = 3
EPS = 1e-6
NEG_INF = -1e30

SPLITS = (SSD_INNER,
          XBC_WIDTH,
          2 * SSD_HEADS,
          3 * DIL_GROUPS * DIL_WIDTH,
          WIN_Q_HEADS * HEAD_DIM,
          WIN_KV_HEADS * HEAD_DIM,
          WIN_KV_HEADS * HEAD_DIM,
          N_BRANCH * D_MODEL)
N_IN = sum(SPLITS)
SPLIT_POINTS = tuple(int(v) for v in np.cumsum(SPLITS)[:-1])

kernel_name = "gated_parallel_ssd_dilated_window_hybrid"


def rms_norm(x, g):
    xf = x.astype(jnp.float32)
    y = xf * lax.rsqrt(jnp.mean(xf * xf, axis=-1, keepdims=True) + EPS)
    return (y * g.astype(jnp.float32)).astype(x.dtype)


def rope_tables(seq):
    inv = ROPE_THETA ** (-jnp.arange(0, ROPE_DIM, 2, dtype=jnp.float32) / ROPE_DIM)
    ang = jnp.arange(seq, dtype=jnp.float32)[:, None] * inv[None, :]
    return jnp.cos(ang), jnp.sin(ang)


def partial_rope(t, cos, sin):
    shp = (1, cos.shape[0]) + (1,) * (t.ndim - 3) + (cos.shape[1],)
    c, s = cos.reshape(shp), sin.reshape(shp)
    t1, t2 = jnp.split(t[..., :ROPE_DIM].astype(jnp.float32), 2, axis=-1)
    rot = jnp.concatenate([t1 * c - t2 * s, t1 * s + t2 * c], axis=-1).astype(t.dtype)
    return jnp.concatenate([rot, t[..., ROPE_DIM:]], axis=-1)


def banded_attention(q, k, v, half_window, sink_logits=None):
    b, l, hq, dh = q.shape
    hkv = k.shape[2]
    rep = hq // hkv
    blk = half_window
    nb = -(-l // blk)
    lp = nb * blk
    qb = jnp.pad(q, ((0, 0), (0, lp - l), (0, 0), (0, 0))).reshape(b, nb, blk, hkv, rep, dh)

    def windows(t):
        tp = jnp.pad(t, ((0, 0), (blk, lp - l + blk), (0, 0), (0, 0))).reshape(b, nb + 2, blk, hkv, dh)
        return jnp.concatenate([tp[:, :-2], tp[:, 1:-1], tp[:, 2:]], axis=2)

    kw, vw = windows(k), windows(v)
    scores = jnp.einsum('bnqgrd,bnkgd->bngrqk', qb, kw,
                        preferred_element_type=jnp.float32) * (dh ** -0.5)
    qpos = jnp.arange(lp).reshape(nb, blk)
    kpos = jnp.arange(nb)[:, None] * blk + jnp.arange(3 * blk)[None, :] - blk
    valid = ((jnp.abs(qpos[:, :, None] - kpos[:, None, :]) <= half_window)
             & (kpos[:, None, :] >= 0) & (kpos[:, None, :] < l))
    scores = jnp.where(valid[None, :, None, None], scores, NEG_INF)
    lse = jax.nn.logsumexp(scores, axis=-1)
    if sink_logits is not None:
        lse = jnp.logaddexp(lse, sink_logits.astype(jnp.float32).reshape(1, 1, hkv, rep, 1))
    p = jnp.exp(scores - lse[..., None])
    out = jnp.einsum('bngrqk,bnkgd->bnqgrd', p.astype(v.dtype), vw).reshape(b, lp, hq, dh)[:, :l]
    lse = lse.transpose(0, 1, 4, 2, 3).reshape(b, lp, hq)[:, :l]
    return out, lse


def dilated_mixture_attention(q, k, v):
    b, s = q.shape[:2]
    outs, lses = [], []
    for gi, (window, dil) in enumerate(DIL_PATTERNS):
        n_sub = s // dil

        def by_stride(t):
            t = t.reshape(b, n_sub, dil, DIL_HEADS, HEAD_DIM).transpose(0, 2, 1, 3, 4)
            return t.reshape(b * dil, n_sub, DIL_HEADS, HEAD_DIM)

        o, lse = banded_attention(by_stride(q[:, :, gi]), by_stride(k[:, :, gi]),
                                  by_stride(v[:, :, gi]), window // (2 * dil))
        outs.append(o.reshape(b, dil, n_sub, DIL_HEADS, HEAD_DIM)
                    .transpose(0, 2, 1, 3, 4).reshape(b, s, DIL_HEADS, HEAD_DIM))
        lses.append(lse.reshape(b, dil, n_sub, DIL_HEADS).transpose(0, 2, 1, 3).reshape(b, s, DIL_HEADS))
    weights = jax.nn.softmax(jnp.stack(lses, axis=0), axis=0)
    out = jnp.einsum('gbsh,gbshd->bshd', weights, jnp.stack(outs, axis=0).astype(jnp.float32))
    return out.astype(q.dtype).reshape(b, s, DIL_WIDTH)


def ssd_scan(x, dt, a, bm, cm):
    b, l, h, p = x.shape
    g, n = bm.shape[2], bm.shape[3]
    hg = h // g
    t = SSD_CHUNK
    c = l // t
    x = x.reshape(b, c, t, g, hg, p)
    dt = dt.reshape(b, c, t, g, hg)
    bm = bm.reshape(b, c, t, g, n)
    cm = cm.reshape(b, c, t, g, n)
    cs = jnp.cumsum(dt * a.reshape(g, hg), axis=2)
    xdt = x * dt[..., None]
    lower = jnp.tril(jnp.ones((t, t), dtype=bool))[None, None, :, :, None, None]
    seg = cs[:, :, :, None] - cs[:, :, None, :]
    decay = jnp.exp(jnp.where(lower, seg, -jnp.inf))
    cb = jnp.einsum('bclgn,bcsgn->bclsg', cm, bm)
    y_diag = jnp.einsum('bclsg,bclsgh,bcsghp->bclghp', cb, decay, xdt)
    decay_states = jnp.exp(cs[:, :, -1:] - cs)
    states = jnp.einsum('bctgn,bctgh,bctghp->bcghpn', bm, decay_states, xdt)
    chunk_decay = jnp.exp(cs[:, :, -1])

    def step(carry, inp):
        st, dec = inp
        return carry * dec[..., None, None] + st, carry

    init = jnp.zeros((b, g, hg, p, n), dtype=x.dtype)
    _, prev = lax.scan(step, init, (states.transpose(1, 0, 2, 3, 4, 5), chunk_decay.transpose(1, 0, 2, 3)))
    prev = prev.transpose(1, 0, 2, 3, 4, 5)
    y_off = jnp.einsum('bctgn,bcghpn,bctgh->bctghp', cm, prev, jnp.exp(cs))
    return (y_diag + y_off).reshape(b, l, h, p)


def ssd_mixer(z, xbc, dt_raw, conv_w, conv_b, dt_bias, a_log, d_skip, ssd_norm):
    b, s, _ = xbc.shape
    pad = (CONV_WIDTH - 1) // 2
    xbc = lax.conv_general_dilated(xbc, conv_w[:, None, :].astype(xbc.dtype), window_strides=(1,),
                                   padding=[(pad, pad)], dimension_numbers=('NWC', 'WIO', 'NWC'),
                                   feature_group_count=XBC_WIDTH) + conv_b
    xbc = jax.nn.silu(xbc).astype(jnp.float32)
    xs, bm, cm = jnp.split(xbc, [SSD_INNER, SSD_INNER + SSD_GROUPS * SSD_STATE], axis=-1)
    xs = xs.reshape(b, s, SSD_HEADS, SSD_HEAD_DIM)
    bm = bm.reshape(b, s, SSD_GROUPS, SSD_STATE)
    cm = cm.reshape(b, s, SSD_GROUPS, SSD_STATE)
    a = -jnp.exp(a_log.astype(jnp.float32))
    dt = jax.nn.softplus(dt_raw.astype(jnp.float32).reshape(b, s, 2, SSD_HEADS)
                         + dt_bias.astype(jnp.float32))
    flip = lambda t: jnp.flip(t, axis=1)
    y_fwd = ssd_scan(xs, dt[:, :, 0], a[0], bm, cm)
    y_bwd = flip(ssd_scan(flip(xs), flip(dt[:, :, 1]), a[1], flip(bm), flip(cm)))
    y = y_fwd + y_bwd + d_skip.astype(jnp.float32)[:, None] * xs
    y = y.reshape(b, s, SSD_INNER) * jax.nn.silu(z.astype(jnp.float32))
    return rms_norm(y, ssd_norm).astype(z.dtype)


def hybrid_layer(x, cos, sin, g_mix, w_in, conv_w, conv_b, dt_bias, a_log, d_skip, ssd_norm,
                 w_a, w_b, w_c, sink, w_out, g_mlp, w_up, w_down):
    b, s, _ = x.shape
    h = rms_norm(x, g_mix)
    proj = h @ w_in
    z, xbc, dt_raw, qkv_d, q_w, k_w, v_w, gate_logits = jnp.split(proj, SPLIT_POINTS, axis=-1)
    y_a = ssd_mixer(z, xbc, dt_raw, conv_w, conv_b, dt_bias, a_log, d_skip, ssd_norm)
    qkv_d = partial_rope(qkv_d.reshape(b, s, 3 * DIL_GROUPS * DIL_HEADS, HEAD_DIM), cos, sin)
    qkv_d = qkv_d.reshape(b, s, 3, DIL_GROUPS, DIL_HEADS, HEAD_DIM)
    y_b = dilated_mixture_attention(qkv_d[:, :, 0], qkv_d[:, :, 1], qkv_d[:, :, 2])
    q_w = partial_rope(q_w.reshape(b, s, WIN_Q_HEADS, HEAD_DIM), cos, sin)
    k_w = partial_rope(k_w.reshape(b, s, WIN_KV_HEADS, HEAD_DIM), cos, sin)
    v_w = v_w.reshape(b, s, WIN_KV_HEADS, HEAD_DIM)
    y_c, _ = banded_attention(q_w, k_w, v_w, WIN_HALF, sink)
    y_c = y_c.reshape(b, s, WIN_Q_HEADS * HEAD_DIM)
    gates = jax.nn.sigmoid(gate_logits.astype(jnp.float32)).astype(x.dtype).reshape(b, s, N_BRANCH, D_MODEL)
    merged = gates[:, :, 0] * (y_a @ w_a) + gates[:, :, 1] * (y_b @ w_b) + gates[:, :, 2] * (y_c @ w_c)
    x = x + merged @ w_out
    hm = rms_norm(x, g_mlp)
    return x + jnp.square(jax.nn.relu(hm @ w_up)) @ w_down


def _fwd_setup_inputs(seed: int = 0) -> dict:
    key = jax.random.key(seed)
    ks = jax.random.split(key, 18)
    f32 = jnp.float32

    def normal(k, shape, scale):
        return jax.random.normal(k, shape, f32) * scale

    x = normal(ks[0], (BATCH, SEQ, D_MODEL), 1.0)
    g_mix = 1.0 + normal(ks[1], (DEPTH, D_MODEL), 0.02)
    w_in = normal(ks[2], (DEPTH, D_MODEL, N_IN), D_MODEL ** -0.5)
    conv_w = normal(ks[3], (DEPTH, CONV_WIDTH, XBC_WIDTH), CONV_WIDTH ** -0.5)
    conv_b = normal(ks[4], (DEPTH, XBC_WIDTH), 0.01)
    dt0 = jnp.exp(jax.random.uniform(ks[5], (DEPTH, 2, SSD_HEADS), f32, math.log(DT_MIN), math.log(DT_MAX)))
    dt_bias = dt0 + jnp.log(-jnp.expm1(-dt0))
    a_log = jnp.log(jax.random.uniform(ks[6], (DEPTH, 2, SSD_HEADS), f32, 1.0, 16.0))
    d_skip = 1.0 + normal(ks[7], (DEPTH, SSD_HEADS), 0.1)
    ssd_norm = 1.0 + normal(ks[8], (DEPTH, SSD_INNER), 0.02)
    w_a = normal(ks[9], (DEPTH, SSD_INNER, D_MODEL), SSD_INNER ** -0.5)
    w_b = normal(ks[10], (DEPTH, DIL_WIDTH, D_MODEL), DIL_WIDTH ** -0.5)
    w_c = normal(ks[11], (DEPTH, WIN_Q_HEADS * HEAD_DIM, D_MODEL), (WIN_Q_HEADS * HEAD_DIM) ** -0.5)
    sink = normal(ks[12], (DEPTH, WIN_Q_HEADS), 0.5)
    w_out = normal(ks[13], (DEPTH, D_MODEL, D_MODEL), D_MODEL ** -0.5)
    g_mlp = 1.0 + normal(ks[14], (DEPTH, D_MODEL), 0.02)
    w_up = normal(ks[15], (DEPTH, D_MODEL, D_FF), D_MODEL ** -0.5)
    w_down = normal(ks[16], (DEPTH, D_FF, D_MODEL), D_FF ** -0.5)
    g_final = 1.0 + normal(ks[17], (D_MODEL,), 0.02)
    return {"x": x, "g_mix": g_mix, "w_in": w_in, "conv_w": conv_w, "conv_b": conv_b,
            "dt_bias": dt_bias, "a_log": a_log, "d_skip": d_skip, "ssd_norm": ssd_norm,
            "w_a": w_a, "w_b": w_b, "w_c": w_c, "sink": sink, "w_out": w_out,
            "g_mlp": g_mlp, "w_up": w_up, "w_down": w_down, "g_final": g_final}


def _fwd_reference(x, g_mix, w_in, conv_w, conv_b, dt_bias, a_log, d_skip, ssd_norm,
              w_a, w_b, w_c, sink, w_out, g_mlp, w_up, w_down, g_final):
    cos, sin = rope_tables(x.shape[1])
    for i in range(DEPTH):
        x = hybrid_layer(x, cos, sin, g_mix[i], w_in[i], conv_w[i], conv_b[i], dt_bias[i], a_log[i],
                         d_skip[i], ssd_norm[i], w_a[i], w_b[i], w_c[i], sink[i], w_out[i],
                         g_mlp[i], w_up[i], w_down[i])
    return rms_norm(x, g_final)


import jax as _jax
import jax.numpy as _jnp

TWIN_FORMAT = 'train_step'
FWD_PARAMS = ['x', 'g_mix', 'w_in', 'conv_w', 'conv_b', 'dt_bias', 'a_log', 'd_skip', 'ssd_norm', 'w_a', 'w_b', 'w_c', 'sink', 'w_out', 'g_mlp', 'w_up', 'w_down', 'g_final']
TWIN_WEIGHTS = ['g_mix', 'w_in', 'conv_w', 'conv_b', 'dt_bias', 'a_log', 'd_skip', 'ssd_norm', 'w_a', 'w_b', 'w_c', 'sink', 'w_out', 'g_mlp', 'w_up', 'w_down', 'g_final']
TWIN_DIFF_INPUT = 'x'
TWIN_INPUTS = ['x', 'g_mix', 'w_in', 'conv_w', 'conv_b', 'dt_bias', 'a_log', 'd_skip', 'ssd_norm', 'w_a', 'w_b', 'w_c', 'sink', 'w_out', 'g_mlp', 'w_up', 'w_down', 'g_final', 'loss_target', 'm_g_mix', 'm_w_in', 'm_conv_w', 'm_conv_b', 'm_dt_bias', 'm_a_log', 'm_d_skip', 'm_ssd_norm', 'm_w_a', 'm_w_b', 'm_w_c', 'm_sink', 'm_w_out', 'm_g_mlp', 'm_w_up', 'm_w_down', 'm_g_final', 'v_g_mix', 'v_w_in', 'v_conv_w', 'v_conv_b', 'v_dt_bias', 'v_a_log', 'v_d_skip', 'v_ssd_norm', 'v_w_a', 'v_w_b', 'v_w_c', 'v_sink', 'v_w_out', 'v_g_mlp', 'v_w_up', 'v_w_down', 'v_g_final']
TWIN_OUTPUTS = ['loss', 'grad_x', 'grad_g_mix', 'grad_w_in', 'grad_conv_w', 'grad_conv_b', 'grad_dt_bias', 'grad_a_log', 'grad_d_skip', 'grad_ssd_norm', 'grad_w_a', 'grad_w_b', 'grad_w_c', 'grad_sink', 'grad_w_out', 'grad_g_mlp', 'grad_w_up', 'grad_w_down', 'grad_g_final', 'delta_g_mix', 'delta_w_in', 'delta_conv_w', 'delta_conv_b', 'delta_dt_bias', 'delta_a_log', 'delta_d_skip', 'delta_ssd_norm', 'delta_w_a', 'delta_w_b', 'delta_w_c', 'delta_sink', 'delta_w_out', 'delta_g_mlp', 'delta_w_up', 'delta_w_down', 'delta_g_final', 'new_m_g_mix', 'new_m_w_in', 'new_m_conv_w', 'new_m_conv_b', 'new_m_dt_bias', 'new_m_a_log', 'new_m_d_skip', 'new_m_ssd_norm', 'new_m_w_a', 'new_m_w_b', 'new_m_w_c', 'new_m_sink', 'new_m_w_out', 'new_m_g_mlp', 'new_m_w_up', 'new_m_w_down', 'new_m_g_final', 'new_v_g_mix', 'new_v_w_in', 'new_v_conv_w', 'new_v_conv_b', 'new_v_dt_bias', 'new_v_a_log', 'new_v_d_skip', 'new_v_ssd_norm', 'new_v_w_a', 'new_v_w_b', 'new_v_w_c', 'new_v_sink', 'new_v_w_out', 'new_v_g_mlp', 'new_v_w_up', 'new_v_w_down', 'new_v_g_final']
TWIN_LEAF_KINDS = {'loss': 'loss', 'grad_x': 'grad_x', 'grad_g_mix': 'grad_w', 'grad_w_in': 'grad_w', 'grad_conv_w': 'grad_w', 'grad_conv_b': 'grad_w', 'grad_dt_bias': 'grad_w', 'grad_a_log': 'grad_w', 'grad_d_skip': 'grad_w', 'grad_ssd_norm': 'grad_w', 'grad_w_a': 'grad_w', 'grad_w_b': 'grad_w', 'grad_w_c': 'grad_w', 'grad_sink': 'grad_w', 'grad_w_out': 'grad_w', 'grad_g_mlp': 'grad_w', 'grad_w_up': 'grad_w', 'grad_w_down': 'grad_w', 'grad_g_final': 'grad_w', 'delta_g_mix': 'delta_w', 'delta_w_in': 'delta_w', 'delta_conv_w': 'delta_w', 'delta_conv_b': 'delta_w', 'delta_dt_bias': 'delta_w', 'delta_a_log': 'delta_w', 'delta_d_skip': 'delta_w', 'delta_ssd_norm': 'delta_w', 'delta_w_a': 'delta_w', 'delta_w_b': 'delta_w', 'delta_w_c': 'delta_w', 'delta_sink': 'delta_w', 'delta_w_out': 'delta_w', 'delta_g_mlp': 'delta_w', 'delta_w_up': 'delta_w', 'delta_w_down': 'delta_w', 'delta_g_final': 'delta_w', 'new_m_g_mix': 'new_m', 'new_m_w_in': 'new_m', 'new_m_conv_w': 'new_m', 'new_m_conv_b': 'new_m', 'new_m_dt_bias': 'new_m', 'new_m_a_log': 'new_m', 'new_m_d_skip': 'new_m', 'new_m_ssd_norm': 'new_m', 'new_m_w_a': 'new_m', 'new_m_w_b': 'new_m', 'new_m_w_c': 'new_m', 'new_m_sink': 'new_m', 'new_m_w_out': 'new_m', 'new_m_g_mlp': 'new_m', 'new_m_w_up': 'new_m', 'new_m_w_down': 'new_m', 'new_m_g_final': 'new_m', 'new_v_g_mix': 'new_v', 'new_v_w_in': 'new_v', 'new_v_conv_w': 'new_v', 'new_v_conv_b': 'new_v', 'new_v_dt_bias': 'new_v', 'new_v_a_log': 'new_v', 'new_v_d_skip': 'new_v', 'new_v_ssd_norm': 'new_v', 'new_v_w_a': 'new_v', 'new_v_w_b': 'new_v', 'new_v_w_c': 'new_v', 'new_v_sink': 'new_v', 'new_v_w_out': 'new_v', 'new_v_g_mlp': 'new_v', 'new_v_w_up': 'new_v', 'new_v_w_down': 'new_v', 'new_v_g_final': 'new_v'}


def _forward(args):
    return _fwd_reference(*[args[k] for k in FWD_PARAMS])


def _output_shape():
    out = _jax.eval_shape(lambda: _forward(_fwd_setup_inputs(0)))
    return out.shape, out.dtype

N_MICROBATCH = 1
ADAM_LR = 0.001
ADAM_B1 = 0.9
ADAM_B2 = 0.999
ADAM_EPS = 1e-08
ADAM_WD = 0.01
ADAM_STEP = 10
PER_EXAMPLE_BATCH_AXIS = {'x': 0, 'loss_target': 0}
SHARED_INPUTS = []
_WEIGHT_DTYPES = {'g_mix': _jnp.float32, 'w_in': _jnp.float32, 'conv_w': _jnp.float32, 'conv_b': _jnp.float32, 'dt_bias': _jnp.float32, 'a_log': _jnp.float32, 'd_skip': _jnp.float32, 'ssd_norm': _jnp.float32, 'w_a': _jnp.float32, 'w_b': _jnp.float32, 'w_c': _jnp.float32, 'sink': _jnp.float32, 'w_out': _jnp.float32, 'g_mlp': _jnp.float32, 'w_up': _jnp.float32, 'w_down': _jnp.float32, 'g_final': _jnp.float32}
MOMENT_SCALE = {'g_mix': 6.728436e-02, 'w_in': 1.925414e-02, 'conv_w': 3.148844e-02, 'conv_b': 4.897445e-02, 'dt_bias': 8.110552e-02, 'a_log': 7.605237e-02, 'd_skip': 1.707069e-01, 'ssd_norm': 4.169152e-02, 'w_a': 4.206422e-02, 'w_b': 6.395194e-03, 'w_c': 1.034998e-02, 'sink': 3.326211e-04, 'w_out': 4.370840e-02, 'g_mlp': 7.117611e-02, 'w_up': 3.627787e-02, 'w_down': 7.343174e-02, 'g_final': 1.632818e+01}


def _to_microbatches(a, axis):
    t = _jnp.moveaxis(a, axis, 0)
    t = t.reshape((N_MICROBATCH, t.shape[0] // N_MICROBATCH) + t.shape[1:])
    return _jnp.moveaxis(t, 1, axis + 1)


def setup_inputs(seed: int = 0) -> dict:
    inp = _fwd_setup_inputs(seed)
    key = _jax.random.fold_in(_jax.random.key(seed), 7919)
    shape, _ = _output_shape()
    out = dict(inp)
    out["loss_target"] = _jax.random.normal(_jax.random.fold_in(key, 0), shape, _jnp.float32)
    for i, name in enumerate(TWIN_WEIGHTS):
        w = inp[name].astype(_jnp.float32)
        if MOMENT_SCALE is None:
            s = _jnp.sqrt(_jnp.mean(_jnp.square(w)) + 1e-30)
        else:
            s = MOMENT_SCALE[name]
        km, kv = _jax.random.split(_jax.random.fold_in(key, i + 1))
        out[name] = w
        out["m_" + name] = s * _jax.random.normal(km, w.shape, _jnp.float32)
        out["v_" + name] = (s * s) * _jax.random.uniform(kv, w.shape, _jnp.float32, 0.5, 1.5)
    if N_MICROBATCH > 1:
        for name, axis in PER_EXAMPLE_BATCH_AXIS.items():
            out[name] = _to_microbatches(out[name], axis)
    return {'x': out['x'], 'g_mix': out['g_mix'], 'w_in': out['w_in'], 'conv_w': out['conv_w'], 'conv_b': out['conv_b'], 'dt_bias': out['dt_bias'], 'a_log': out['a_log'], 'd_skip': out['d_skip'], 'ssd_norm': out['ssd_norm'], 'w_a': out['w_a'], 'w_b': out['w_b'], 'w_c': out['w_c'], 'sink': out['sink'], 'w_out': out['w_out'], 'g_mlp': out['g_mlp'], 'w_up': out['w_up'], 'w_down': out['w_down'], 'g_final': out['g_final'], 'loss_target': out['loss_target'], 'm_g_mix': out['m_g_mix'], 'm_w_in': out['m_w_in'], 'm_conv_w': out['m_conv_w'], 'm_conv_b': out['m_conv_b'], 'm_dt_bias': out['m_dt_bias'], 'm_a_log': out['m_a_log'], 'm_d_skip': out['m_d_skip'], 'm_ssd_norm': out['m_ssd_norm'], 'm_w_a': out['m_w_a'], 'm_w_b': out['m_w_b'], 'm_w_c': out['m_w_c'], 'm_sink': out['m_sink'], 'm_w_out': out['m_w_out'], 'm_g_mlp': out['m_g_mlp'], 'm_w_up': out['m_w_up'], 'm_w_down': out['m_w_down'], 'm_g_final': out['m_g_final'], 'v_g_mix': out['v_g_mix'], 'v_w_in': out['v_w_in'], 'v_conv_w': out['v_conv_w'], 'v_conv_b': out['v_conv_b'], 'v_dt_bias': out['v_dt_bias'], 'v_a_log': out['v_a_log'], 'v_d_skip': out['v_d_skip'], 'v_ssd_norm': out['v_ssd_norm'], 'v_w_a': out['v_w_a'], 'v_w_b': out['v_w_b'], 'v_w_c': out['v_w_c'], 'v_sink': out['v_sink'], 'v_w_out': out['v_w_out'], 'v_g_mlp': out['v_g_mlp'], 'v_w_up': out['v_w_up'], 'v_w_down': out['v_w_down'], 'v_g_final': out['v_g_final']}


def _loss(weights, diff, rest, loss_target):
    with _jax.named_scope("forward"):
        args = {**rest, TWIN_DIFF_INPUT: diff, **{k: w.astype(_WEIGHT_DTYPES[k]) for k, w in weights.items()}}
        y = _forward(args)
    with _jax.named_scope("loss_head"):
        err = _jnp.square(y.astype(_jnp.float32) - loss_target)
        return 0.5 * _jnp.sum(_jnp.mean(err, axis=-1)) if err.ndim else 0.5 * err


def _adamw(w, g, m, v):
    m = ADAM_B1 * m + (1.0 - ADAM_B1) * g
    v = ADAM_B2 * v + (1.0 - ADAM_B2) * _jnp.square(g)
    m_hat = m / (1.0 - ADAM_B1 ** ADAM_STEP)
    v_hat = v / (1.0 - ADAM_B2 ** ADAM_STEP)
    delta = -ADAM_LR * (m_hat / (_jnp.sqrt(v_hat) + ADAM_EPS) + ADAM_WD * w)
    return delta, m, v


def reference(x, g_mix, w_in, conv_w, conv_b, dt_bias, a_log, d_skip, ssd_norm, w_a, w_b, w_c, sink, w_out, g_mlp, w_up, w_down, g_final, loss_target, m_g_mix, m_w_in, m_conv_w, m_conv_b, m_dt_bias, m_a_log, m_d_skip, m_ssd_norm, m_w_a, m_w_b, m_w_c, m_sink, m_w_out, m_g_mlp, m_w_up, m_w_down, m_g_final, v_g_mix, v_w_in, v_conv_w, v_conv_b, v_dt_bias, v_a_log, v_d_skip, v_ssd_norm, v_w_a, v_w_b, v_w_c, v_sink, v_w_out, v_g_mlp, v_w_up, v_w_down, v_g_final):
    given = dict(x=x, g_mix=g_mix, w_in=w_in, conv_w=conv_w, conv_b=conv_b, dt_bias=dt_bias, a_log=a_log, d_skip=d_skip, ssd_norm=ssd_norm, w_a=w_a, w_b=w_b, w_c=w_c, sink=sink, w_out=w_out, g_mlp=g_mlp, w_up=w_up, w_down=w_down, g_final=g_final, loss_target=loss_target, m_g_mix=m_g_mix, m_w_in=m_w_in, m_conv_w=m_conv_w, m_conv_b=m_conv_b, m_dt_bias=m_dt_bias, m_a_log=m_a_log, m_d_skip=m_d_skip, m_ssd_norm=m_ssd_norm, m_w_a=m_w_a, m_w_b=m_w_b, m_w_c=m_w_c, m_sink=m_sink, m_w_out=m_w_out, m_g_mlp=m_g_mlp, m_w_up=m_w_up, m_w_down=m_w_down, m_g_final=m_g_final, v_g_mix=v_g_mix, v_w_in=v_w_in, v_conv_w=v_conv_w, v_conv_b=v_conv_b, v_dt_bias=v_dt_bias, v_a_log=v_a_log, v_d_skip=v_d_skip, v_ssd_norm=v_ssd_norm, v_w_a=v_w_a, v_w_b=v_w_b, v_w_c=v_w_c, v_sink=v_sink, v_w_out=v_w_out, v_g_mlp=v_g_mlp, v_w_up=v_w_up, v_w_down=v_w_down, v_g_final=v_g_final)
    weights = {n: given[n] for n in TWIN_WEIGHTS}
    shared = {n: given[n] for n in SHARED_INPUTS}
    per_example = {n: given[n] for n in ['x']}
    grad_fn = _jax.value_and_grad(_loss, argnums=(0, 1))

    def one_microbatch(ex, loss_target):
        ex = dict(ex)
        diff = ex.pop(TWIN_DIFF_INPUT)
        return grad_fn(weights, diff, {**shared, **ex}, loss_target)

    if N_MICROBATCH == 1:
        loss, (grad_w, grad_x) = one_microbatch(per_example, given["loss_target"])
    else:
        def body(carry, xs):
            loss_sum, grad_sum = carry
            l_k, (gw_k, gx_k) = one_microbatch(xs[0], xs[1])
            with _jax.named_scope("update"):
                return (loss_sum + l_k, _jax.tree.map(_jnp.add, grad_sum, gw_k)), gx_k

        init = (_jnp.zeros((), _jnp.float32), _jax.tree.map(_jnp.zeros_like, weights))
        (loss, grad_w), grad_x = _jax.lax.scan(body, init, (per_example, given["loss_target"]))
    with _jax.named_scope("update"):
        delta_w, new_m, new_v = {}, {}, {}
        for n in TWIN_WEIGHTS:
            delta_w[n], new_m[n], new_v[n] = _adamw(weights[n], grad_w[n], given["m_" + n], given["v_" + n])
    return (loss, grad_x, *[grad_w[n] for n in TWIN_WEIGHTS], *[delta_w[n] for n in TWIN_WEIGHTS],
            *[new_m[n] for n in TWIN_WEIGHTS], *[new_v[n] for n in TWIN_WEIGHTS])
```

```python
import functools
import math

import jax
import jax.numpy as jnp
from jax import lax
from jax.experimental import pallas as pl
from jax.experimental.pallas import tpu as pltpu

F32 = jnp.float32
BF16 = jnp.bfloat16

SEQ = 2048
SSD_HEADS = 32
SSD_HEAD_DIM = 64
SSD_GROUPS = 8
SSD_STATE = 128
SSD_CHUNK = 128
CONV_WIDTH = 5
HEAD_DIM = 128
ROPE_DIM = 32
ROPE_THETA = 500000.0
DIL_PATTERNS = ((128, 1), (512, 4), (2048, 16))
DIL_HEADS = 8
WIN_Q_HEADS = 16
WIN_KV_HEADS = 4
WIN_HALF = 128
EPS = 1e-6
NEG_BIG = -1e30
ADAM_LR = 0.001
ADAM_B1 = 0.9
ADAM_B2 = 0.999
ADAM_EPS = 1e-08
ADAM_WD = 0.01
ADAM_STEP = 10

LANES = 128
ATT_BLK = 128
VMEM_LIMIT = 48 * 1024 * 1024
MESH = pl.DeviceIdType.MESH
HIGHEST = lax.Precision.HIGHEST
NT = (((1,), (1,)), ((), ()))
TN = (((0,), (0,)), ((), ()))
NN = (((1,), (0,)), ((), ()))


def _dims(d_model):
    hi = SSD_HEADS * SSD_HEAD_DIM
    gn = SSD_GROUPS * SSD_STATE
    ng = len(DIL_PATTERNS)
    dw = DIL_HEADS * HEAD_DIM
    wq = WIN_Q_HEADS * HEAD_DIM
    wk = WIN_KV_HEADS * HEAD_DIM
    d = dict(D=d_model, HI=hi, GN=gn, XBC=hi + 2 * gn, H2=2 * SSD_HEADS, NG=ng, DW=dw, WQ=wq, WK=wk,
             QKVD=3 * ng * dw, QW=3 * ng * dw + wq + 2 * wk, HG=SSD_HEADS // SSD_GROUPS)
    d["OFF_XBC"] = hi
    d["OFF_QKV"] = hi + d["XBC"]
    d["OFF_GATE"] = d["OFF_QKV"] + d["QW"]
    d["NM"] = d["OFF_GATE"] + 3 * d_model
    return d


def _pick(n, prefs):
    for p in prefs:
        if n % p == 0:
            return p
    raise ValueError(f"no tile for {n} in {prefs}")


def _params(sem):
    return pltpu.CompilerParams(dimension_semantics=sem, vmem_limit_bytes=VMEM_LIMIT)


def _sigmoid(x):
    return 1.0 / (1.0 + jnp.exp(-x))


def _mm(a, b, *, ta=False, tb=False, add=None, aux=None, epi=None, out_dtype=BF16, name):
    if ta:
        kdim, m = a.shape
    else:
        m, kdim = a.shape
    if tb:
        n, k2 = b.shape
    else:
        k2, n = b.shape
    assert kdim == k2, (a.shape, b.shape, ta, tb)
    tm = _pick(m, (1024, 512, 256, 128, 64, 32, 16, 8))
    tn = _pick(n, (1024, 512, 256, 128))
    tk = _pick(kdim, (512, 256, 128))
    nk = kdim // tk
    dims = (((0 if ta else 1,), (1 if tb else 0,)), ((), ()))
    n_in = 2 + (add is not None) + (aux is not None)
    n_out = 2 if epi == "relu2" else 1

    def body(*refs):
        a_ref, b_ref = refs[0], refs[1]
        pos = 2
        add_ref = aux_ref = None
        if add is not None:
            add_ref = refs[pos]
            pos += 1
        if aux is not None:
            aux_ref = refs[pos]
            pos += 1
        out_refs = refs[n_in:n_in + n_out]
        acc_ref = refs[n_in + n_out]
        k = pl.program_id(2)

        @pl.when(k == 0)
        def _():
            acc_ref[...] = jnp.zeros_like(acc_ref)

        acc_ref[...] += lax.dot_general(a_ref[...].astype(BF16), b_ref[...].astype(BF16), dims,
                                        preferred_element_type=F32)

        @pl.when(k == nk - 1)
        def _():
            r = acc_ref[...]
            if add_ref is not None:
                r = r + add_ref[...].astype(F32)
            if epi == "relu2":
                out_refs[0][...] = r.astype(out_refs[0].dtype)
                out_refs[1][...] = jnp.square(jnp.maximum(r, 0.0)).astype(out_refs[1].dtype)
            elif epi == "relu2_bwd":
                out_refs[0][...] = (r * 2.0 * jnp.maximum(aux_ref[...].astype(F32), 0.0)).astype(out_refs[0].dtype)
            else:
                out_refs[0][...] = r.astype(out_refs[0].dtype)

    a_spec = pl.BlockSpec((tk, tm), lambda i, j, k: (k, i)) if ta else pl.BlockSpec((tm, tk), lambda i, j, k: (i, k))
    b_spec = pl.BlockSpec((tn, tk), lambda i, j, k: (j, k)) if tb else pl.BlockSpec((tk, tn), lambda i, j, k: (k, j))
    o_spec = pl.BlockSpec((tm, tn), lambda i, j, k: (i, j))
    in_specs = [a_spec, b_spec]
    args = [a, b]
    if add is not None:
        in_specs.append(o_spec)
        args.append(add)
    if aux is not None:
        in_specs.append(o_spec)
        args.append(aux)
    out_shape = [jax.ShapeDtypeStruct((m, n), out_dtype)] * n_out
    res = pl.pallas_call(
        body, name=name, grid=(m // tm, n // tn, nk), in_specs=in_specs, out_specs=[o_spec] * n_out,
        out_shape=out_shape, scratch_shapes=[pltpu.VMEM((tm, tn), F32)],
        compiler_params=_params(("parallel", "parallel", "arbitrary")),
    )(*args)
    return res if n_out == 2 else res[0]


def _rowwise(body, rows, fulls, outs, accs=(), *, tile, name):
    rows = [r if isinstance(r, tuple) else (r, 0, r.shape[1]) for r in rows]
    nrows = rows[0][0].shape[0]
    assert nrows % tile == 0, (nrows, tile)
    in_specs, args = [], []
    for arr, off, width in rows:
        assert arr.shape[0] == nrows and off % width == 0, (arr.shape, off, width)
        in_specs.append(pl.BlockSpec((tile, width), lambda i, o=off // width: (i, o)))
        args.append(arr)
    for arr in fulls:
        in_specs.append(pl.BlockSpec(arr.shape, lambda i, nd=arr.ndim: (0,) * nd))
        args.append(arr)
    out_specs, out_shape = [], []
    for cols, dt in outs:
        out_specs.append(pl.BlockSpec((tile, cols), lambda i: (i, 0)))
        out_shape.append(jax.ShapeDtypeStruct((nrows, cols), dt))
    for shp, dt in accs:
        out_specs.append(pl.BlockSpec(shp, lambda i, nd=len(shp): (0,) * nd))
        out_shape.append(jax.ShapeDtypeStruct(shp, dt))
    n_in, n_out = len(args), len(outs)

    def wrapped(*refs):
        acc_refs = refs[n_in + n_out:]
        if acc_refs:
            @pl.when(pl.program_id(0) == 0)
            def _():
                for r in acc_refs:
                    r[...] = jnp.zeros_like(r)
        body(*refs)

    return pl.pallas_call(
        wrapped, name=name, grid=(nrows // tile,), in_specs=in_specs, out_specs=out_specs, out_shape=out_shape,
        compiler_params=_params(("arbitrary",)),
    )(*args)


def _rms_fwd(x, g, name):
    def body(x_ref, g_ref, h_ref):
        xv = x_ref[...]
        rstd = lax.rsqrt(jnp.mean(xv * xv, axis=-1, keepdims=True) + EPS)
        h_ref[...] = (xv * rstd * g_ref[...]).astype(BF16)

    return _rowwise(body, [x], [g], [(x.shape[1], BF16)], tile=256, name=name)[0]


def _rms_bwd(x, g, dh, dres, name):
    def body(x_ref, dh_ref, dres_ref, g_ref, dx_ref, dg_ref):
        xv = x_ref[...]
        dv = dh_ref[...]
        rstd = lax.rsqrt(jnp.mean(xv * xv, axis=-1, keepdims=True) + EPS)
        xn = xv * rstd
        dg_ref[...] += jnp.sum(dv * xn, axis=0, keepdims=True)
        dn = dv * g_ref[...]
        dx_ref[...] = dres_ref[...] + rstd * (dn - xn * jnp.mean(dn * xn, axis=-1, keepdims=True))

    d = x.shape[1]
    return _rowwise(body, [x, dh, dres], [g], [(d, F32)], [((1, d), F32)], tile=256, name=name)


def _loss_head(x, g, target, name):
    d = x.shape[1]

    def body(x_ref, t_ref, g_ref, dx_ref, loss_ref, dg_ref):
        xv = x_ref[...]
        rstd = lax.rsqrt(jnp.mean(xv * xv, axis=-1, keepdims=True) + EPS)
        xn = xv * rstd
        err = xn * g_ref[...] - t_ref[...]
        loss_ref[...] += jnp.full((1, LANES), 0.5 / d, F32) * jnp.sum(err * err)
        dy = err * (1.0 / d)
        dg_ref[...] += jnp.sum(dy * xn, axis=0, keepdims=True)
        dn = dy * g_ref[...]
        dx_ref[...] = rstd * (dn - xn * jnp.mean(dn * xn, axis=-1, keepdims=True))

    return _rowwise(body, [x, target], [g], [(d, F32)], [((1, LANES), F32), ((1, d), F32)], tile=256, name=name)


def _rope_tables(sign):
    half = ROPE_DIM // 2
    inv = ROPE_THETA ** (-jnp.arange(0, ROPE_DIM, 2, dtype=F32) / ROPE_DIM)
    ang = jnp.arange(SEQ, dtype=F32)[:, None] * inv[None, :]
    cos, sin = jnp.cos(ang), jnp.sin(ang) * sign
    zeros = jnp.zeros((SEQ, HEAD_DIM - ROPE_DIM), F32)
    zh = jnp.zeros((SEQ, half), F32)
    c = jnp.concatenate([cos, cos, zeros + 1.0], axis=1)
    s_up = jnp.concatenate([-sin, zh, zeros], axis=1)
    s_dn = jnp.concatenate([zh, sin, zeros], axis=1)
    return c, s_up, s_dn


def _rope(src, off, nblk, tabs, dm, name):
    t = src.shape[0]
    tq = 512
    half = ROPE_DIM // 2
    win0 = 3 * dm["NG"] * DIL_HEADS
    n_dil_qk = win0
    win1 = win0 + WIN_Q_HEADS + WIN_KV_HEADS
    assert off % HEAD_DIM == 0 and nblk == dm["QW"] // HEAD_DIM
    sb = SEQ // tq

    def body(x_ref, c_ref, up_ref, dn_ref, o_ref):
        j = pl.program_id(1)
        xv = x_ref[...].astype(F32)
        rot = xv * c_ref[...] + pltpu.roll(xv, HEAD_DIM - half, 1) * up_ref[...] + pltpu.roll(xv, half, 1) * dn_ref[...]
        is_rope = (j < n_dil_qk) | ((j >= win0) & (j < win1))
        o_ref[...] = jnp.where(is_rope, rot, xv).astype(BF16)

    tab_spec = pl.BlockSpec((tq, HEAD_DIM), lambda i, j: (i % sb, 0))
    return pl.pallas_call(
        body, name=name, grid=(t // tq, nblk),
        in_specs=[pl.BlockSpec((tq, HEAD_DIM), lambda i, j, o=off // HEAD_DIM: (i, o + j)), tab_spec, tab_spec, tab_spec],
        out_specs=pl.BlockSpec((tq, HEAD_DIM), lambda i, j: (i, j)),
        out_shape=jax.ShapeDtypeStruct((t, nblk * HEAD_DIM), BF16),
        compiler_params=_params(("parallel", "parallel")),
    )(src, *tabs)


def _band_mask(rows_start, cols_start, nrows, ncols, w, n, rows_are_q):
    r = rows_start + lax.broadcasted_iota(jnp.int32, (nrows, ncols), 0)
    c = cols_start + lax.broadcasted_iota(jnp.int32, (nrows, ncols), 1)
    del rows_are_q
    return (jnp.abs(r - c) <= w) & (c >= 0) & (c < n)


def _nbr_specs(make, nb):
    if nb == 1:
        return [make(lambda i: i)]
    return [make(lambda i: jnp.maximum(i - 1, 0)), make(lambda i: i), make(lambda i: jnp.minimum(i + 1, nb - 1))]


def _cat(refs, axis):
    vals = [r[...] for r in refs]
    return vals[0] if len(vals) == 1 else jnp.concatenate(vals, axis=axis)


def _attn_fwd(qa, ka, va, qo, ko, vo, hq, rep, n, w, sink, out_dtype, name):
    bb = qa.shape[0]
    blk = ATT_BLK
    nb = n // blk
    nk = 1 if nb == 1 else 3
    scale = HEAD_DIM ** -0.5
    has_sink = sink is not None

    def body(*refs):
        q_ref = refs[0]
        k_refs = refs[1:1 + nk]
        v_refs = refs[1 + nk:1 + 2 * nk]
        pos = 1 + 2 * nk
        sink_ref = refs[pos] if has_sink else None
        o_ref, lse_ref = refs[pos + has_sink], refs[pos + has_sink + 1]
        i = pl.program_id(2)
        kcat = _cat(k_refs, 0)
        vcat = _cat(v_refs, 0)
        s = lax.dot_general(q_ref[...], kcat, NT, preferred_element_type=F32) * scale
        k0 = (i - 1) * blk if nk == 3 else i * blk
        valid = _band_mask(i * blk, k0, blk, nk * blk, w, n, True)
        s = jnp.where(valid, s, NEG_BIG)
        m = jnp.max(s, axis=1, keepdims=True)
        if has_sink:
            m = jnp.maximum(m, sink_ref[...])
        p = jnp.exp(s - m)
        l = jnp.sum(p, axis=1, keepdims=True)
        if has_sink:
            l = l + jnp.exp(sink_ref[...] - m)
        o = lax.dot_general(p.astype(BF16), vcat, NN, preferred_element_type=F32) / l
        o_ref[...] = o.astype(o_ref.dtype)
        lse_ref[...] = m + jnp.log(l)

    def mk(col0, div):
        return lambda f: pl.BlockSpec((None, blk, HEAD_DIM), lambda b, h, i, f=f: (b, f(i), col0 + h // div))

    in_specs = [pl.BlockSpec((None, blk, HEAD_DIM), lambda b, h, i: (b, i, qo + h))]
    in_specs += _nbr_specs(mk(ko, rep), nb) + _nbr_specs(mk(vo, rep), nb)
    args = [qa] + [ka] * nk + [va] * nk
    if has_sink:
        in_specs.append(pl.BlockSpec((None, 1, 1), lambda b, h, i: (h, 0, 0)))
        args.append(sink)
    return pl.pallas_call(
        body, name=name, grid=(bb, hq, nb), in_specs=in_specs,
        out_specs=[pl.BlockSpec((None, blk, HEAD_DIM), lambda b, h, i: (b, i, h)),
                   pl.BlockSpec((None, None, blk, 1), lambda b, h, i: (b, h, i, 0))],
        out_shape=[jax.ShapeDtypeStruct((bb, n, hq * HEAD_DIM), out_dtype), jax.ShapeDtypeStruct((bb, hq, n, 1), F32)],
        compiler_params=_params(("parallel", "parallel", "parallel")),
    )(*args)


def _attn_dq(qa, ka, va, do, lse, delta, qo, ko, vo, hq, rep, n, w, name):
    bb = qa.shape[0]
    blk = ATT_BLK
    nb = n // blk
    nk = 1 if nb == 1 else 3
    scale = HEAD_DIM ** -0.5

    def body(*refs):
        q_ref = refs[0]
        k_refs = refs[1:1 + nk]
        v_refs = refs[1 + nk:1 + 2 * nk]
        do_ref, lse_ref, dl_ref, dq_ref = refs[1 + 2 * nk:]
        i = pl.program_id(2)
        kcat = _cat(k_refs, 0)
        vcat = _cat(v_refs, 0)
        s = lax.dot_general(q_ref[...], kcat, NT, preferred_element_type=F32) * scale
        k0 = (i - 1) * blk if nk == 3 else i * blk
        valid = _band_mask(i * blk, k0, blk, nk * blk, w, n, True)
        p = jnp.exp(jnp.where(valid, s, NEG_BIG) - lse_ref[...])
        dp = lax.dot_general(do_ref[...].astype(BF16), vcat, NT, preferred_element_type=F32)
        ds = p * (dp - dl_ref[...])
        dq_ref[...] = (lax.dot_general(ds.astype(BF16), kcat, NN, preferred_element_type=F32) * scale).astype(BF16)

    def mk(col0, div):
        return lambda f: pl.BlockSpec((None, blk, HEAD_DIM), lambda b, h, i, f=f: (b, f(i), col0 + h // div))

    col_spec = pl.BlockSpec((None, None, blk, 1), lambda b, h, i: (b, h, i, 0))
    in_specs = [pl.BlockSpec((None, blk, HEAD_DIM), lambda b, h, i: (b, i, qo + h))]
    in_specs += _nbr_specs(mk(ko, rep), nb) + _nbr_specs(mk(vo, rep), nb)
    in_specs += [pl.BlockSpec((None, blk, HEAD_DIM), lambda b, h, i: (b, i, h)), col_spec, col_spec]
    return pl.pallas_call(
        body, name=name, grid=(bb, hq, nb), in_specs=in_specs,
        out_specs=pl.BlockSpec((None, blk, HEAD_DIM), lambda b, h, i: (b, i, h)),
        out_shape=jax.ShapeDtypeStruct((bb, n, hq * HEAD_DIM), BF16),
        compiler_params=_params(("parallel", "parallel", "parallel")),
    )(qa, *([ka] * nk), *([va] * nk), do, lse, delta)


def _attn_dkv(qa, ka, va, do, lse_row, delta_row, qo, ko, vo, hq, rep, n, w, name):
    bb = qa.shape[0]
    blk = ATT_BLK
    nb = n // blk
    nq = 1 if nb == 1 else 3
    hkv = hq // rep
    scale = HEAD_DIM ** -0.5

    def body(*refs):
        k_ref, v_ref = refs[0], refs[1]
        q_refs = refs[2:2 + nq]
        do_refs = refs[2 + nq:2 + 2 * nq]
        lse_refs = refs[2 + 2 * nq:2 + 3 * nq]
        dl_refs = refs[2 + 3 * nq:2 + 4 * nq]
        dk_ref, dv_ref, dk_acc, dv_acc = refs[2 + 4 * nq:]
        j = pl.program_id(2)
        r = pl.program_id(3)

        @pl.when(r == 0)
        def _():
            dk_acc[...] = jnp.zeros_like(dk_acc)
            dv_acc[...] = jnp.zeros_like(dv_acc)

        qcat = _cat(q_refs, 0)
        docat = _cat(do_refs, 0).astype(BF16)
        lse = _cat(lse_refs, 1)
        dl = _cat(dl_refs, 1)
        st = lax.dot_general(k_ref[...], qcat, NT, preferred_element_type=F32) * scale
        q0 = (j - 1) * blk if nq == 3 else j * blk
        valid = _band_mask(j * blk, q0, blk, nq * blk, w, n, False)
        pt = jnp.exp(jnp.where(valid, st, NEG_BIG) - lse)
        dv_acc[...] += lax.dot_general(pt.astype(BF16), docat, NN, preferred_element_type=F32)
        dpt = lax.dot_general(v_ref[...], docat, NT, preferred_element_type=F32)
        dst = pt * (dpt - dl)
        dk_acc[...] += lax.dot_general(dst.astype(BF16), qcat, NN, preferred_element_type=F32) * scale

        @pl.when(r == rep - 1)
        def _():
            dk_ref[...] = dk_acc[...].astype(BF16)
            dv_ref[...] = dv_acc[...].astype(BF16)

    def mkq(col0):
        return lambda f: pl.BlockSpec((None, blk, HEAD_DIM), lambda b, g, j, r, f=f: (b, f(j), col0 + g * rep + r))

    def mkrow(f):
        return pl.BlockSpec((None, None, 1, blk), lambda b, g, j, r, f=f: (b, g * rep + r, 0, f(j)))

    in_specs = [pl.BlockSpec((None, blk, HEAD_DIM), lambda b, g, j, r: (b, j, ko + g)),
                pl.BlockSpec((None, blk, HEAD_DIM), lambda b, g, j, r: (b, j, vo + g))]
    in_specs += _nbr_specs(mkq(qo), nb) + _nbr_specs(mkq(0), nb) + _nbr_specs(mkrow, nb) + _nbr_specs(mkrow, nb)
    o_spec = pl.BlockSpec((None, blk, HEAD_DIM), lambda b, g, j, r: (b, j, g))
    return pl.pallas_call(
        body, name=name, grid=(bb, hkv, nb, rep), in_specs=in_specs, out_specs=[o_spec, o_spec],
        out_shape=[jax.ShapeDtypeStruct((bb, n, hkv * HEAD_DIM), BF16)] * 2,
        scratch_shapes=[pltpu.VMEM((blk, HEAD_DIM), F32), pltpu.VMEM((blk, HEAD_DIM), F32)],
        compiler_params=_params(("parallel", "parallel", "parallel", "arbitrary")),
    )(ka, va, *([qa] * nq), *([do] * nq), *([lse_row] * nq), *([delta_row] * nq))


def _head_expand(v, nh, width):
    lane_head = lax.broadcasted_iota(jnp.int32, (1, nh * width), 1) >> int(math.log2(width))
    out = jnp.zeros((v.shape[0], nh * width), F32)
    for j in range(nh):
        out = jnp.where(lane_head == j, v[:, j:j + 1], out)
    return out


def _head_sums(m, nh, width):
    lane_head = lax.broadcasted_iota(jnp.int32, (1, nh * width), 1) >> int(math.log2(width))
    col = lax.broadcasted_iota(jnp.int32, (1, nh), 1)
    out = jnp.zeros((m.shape[0], nh), F32)
    for j in range(nh):
        sj = jnp.sum(jnp.where(lane_head == j, m, 0.0), axis=1, keepdims=True)
        out = jnp.where(col == j, sj, out)
    return out


def _head_dots(a, b, nh, name):
    def body(a_ref, b_ref, o_ref):
        o_ref[...] = _head_sums(a_ref[...].astype(F32) * b_ref[...].astype(F32), nh, HEAD_DIM)

    return _rowwise(body, [a, b], [], [(nh, F32)], tile=256, name=name)[0]


def _dil_combine(outs, lses, name):
    ng = len(outs)

    def body(*refs):
        o_refs, l_refs = refs[:ng], refs[ng:2 * ng]
        y_ref, lt_ref = refs[2 * ng], refs[2 * ng + 1]
        ls = [r[...] for r in l_refs]
        m = functools.reduce(jnp.maximum, ls)
        es = [jnp.exp(v - m) for v in ls]
        tot = functools.reduce(jnp.add, es)
        acc = jnp.zeros(o_refs[0].shape, F32)
        for o_ref, e in zip(o_refs, es):
            acc = acc + _head_expand(e / tot, DIL_HEADS, HEAD_DIM) * o_ref[...]
        y_ref[...] = acc.astype(BF16)
        lt_ref[...] = m + jnp.log(tot)

    dw = outs[0].shape[1]
    return _rowwise(body, list(outs) + list(lses), [], [(dw, BF16), (DIL_HEADS, F32)], tile=256, name=name)


def _sink_grad(lse, delta, sink, name):
    def body(l_ref, d_ref, s_ref, o_ref):
        o_ref[...] -= jnp.sum(jnp.exp(s_ref[...] - l_ref[...]) * d_ref[...], axis=0, keepdims=True)

    return _rowwise(body, [lse, delta], [sink], [], [((1, lse.shape[1]), F32)], tile=512, name=name)[0]


def _shift_rows(x, d, nrows):
    if d == 0:
        return x
    rolled = pltpu.roll(x, (-d) % nrows, 0)
    row = lax.broadcasted_iota(jnp.int32, x.shape, 0)
    ok = (row + d >= 0) & (row + d < nrows)
    return jnp.where(ok, rolled, 0.0)


def _conv_fwd(proj, off, conv_w, conv_b, xbc, name):
    t = proj.shape[0]
    tc = _pick(xbc, (256, 128))
    assert off % tc == 0
    pad = (CONV_WIDTH - 1) // 2

    def body(x_ref, w_ref, b_ref, c_ref, u_ref):
        xv = x_ref[...].astype(F32)
        acc = jnp.zeros_like(xv) + b_ref[...]
        for k in range(CONV_WIDTH):
            acc = acc + w_ref[k:k + 1, :] * _shift_rows(xv, k - pad, SEQ)
        c_ref[...] = acc.astype(BF16)
        u_ref[...] = (acc * _sigmoid(acc)).astype(BF16)

    o_spec = pl.BlockSpec((SEQ, tc), lambda b, j: (b, j))
    return pl.pallas_call(
        body, name=name, grid=(t // SEQ, xbc // tc),
        in_specs=[pl.BlockSpec((SEQ, tc), lambda b, j, o=off // tc: (b, o + j)),
                  pl.BlockSpec((CONV_WIDTH, tc), lambda b, j: (0, j)), pl.BlockSpec((1, tc), lambda b, j: (0, j))],
        out_specs=[o_spec, o_spec], out_shape=[jax.ShapeDtypeStruct((t, xbc), BF16)] * 2,
        compiler_params=_params(("parallel", "parallel")),
    )(proj, conv_w, conv_b)


def _conv_bwd(du, cpre, proj, off, conv_w, name):
    t, xbc = du.shape
    tc = _pick(xbc, (256, 128))
    pad = (CONV_WIDTH - 1) // 2

    def body(du_ref, c_ref, x_ref, w_ref, dx_ref, dw_ref, db_ref):
        @pl.when(pl.program_id(1) == 0)
        def _():
            dw_ref[...] = jnp.zeros_like(dw_ref)
            db_ref[...] = jnp.zeros_like(db_ref)

        cv = c_ref[...].astype(F32)
        sg = _sigmoid(cv)
        dc = du_ref[...] * (sg * (1.0 + cv * (1.0 - sg)))
        xv = x_ref[...].astype(F32)
        dx = jnp.zeros_like(dc)
        for k in range(CONV_WIDTH):
            dx = dx + w_ref[k:k + 1, :] * _shift_rows(dc, pad - k, SEQ)
            dw_ref[k:k + 1, :] += jnp.sum(dc * _shift_rows(xv, k - pad, SEQ), axis=0, keepdims=True)
        db_ref[...] += jnp.sum(dc, axis=0, keepdims=True)
        dx_ref[...] = dx.astype(BF16)

    blk = pl.BlockSpec((SEQ, tc), lambda j, b: (b, j))
    return pl.pallas_call(
        body, name=name, grid=(xbc // tc, t // SEQ),
        in_specs=[blk, blk, pl.BlockSpec((SEQ, tc), lambda j, b, o=off // tc: (b, o + j)),
                  pl.BlockSpec((CONV_WIDTH, tc), lambda j, b: (0, j))],
        out_specs=[blk, pl.BlockSpec((CONV_WIDTH, tc), lambda j, b: (0, j)), pl.BlockSpec((1, tc), lambda j, b: (0, j))],
        out_shape=[jax.ShapeDtypeStruct((t, xbc), BF16), jax.ShapeDtypeStruct((CONV_WIDTH, xbc), F32),
                   jax.ShapeDtypeStruct((1, xbc), F32)],
        compiler_params=_params(("parallel", "arbitrary")),
    )(du, cpre, proj, conv_w)


def _dt_prep(dtr, bias, name):
    def body(r_ref, b_ref, o_ref):
        v = r_ref[...] + b_ref[...]
        o_ref[...] = jnp.maximum(v, 0.0) + jnp.log1p(jnp.exp(-jnp.abs(v)))

    return _rowwise(body, [dtr], [bias], [(dtr.shape[1], F32)], tile=512, name=name)[0]


def _scan_prelude(d, dt_ref, dtt_ref, al_ref, alt_ref, hg):
    p = SSD_HEAD_DIM
    ch = SSD_CHUNK
    a_row = -jnp.exp(al_ref[...])
    a_col = -jnp.exp(alt_ref[...])
    dtc = dt_ref[...]
    dt_x = _head_expand(dtc, hg, p)
    dta_x = dt_x * _head_expand(a_row, hg, p)
    dta_t = dtt_ref[...] * a_col
    ri = lax.broadcasted_iota(jnp.int32, (ch, ch), 0)
    ci = lax.broadcasted_iota(jnp.int32, (ch, ch), 1)
    mask = (ci <= ri) if d == 0 else (ci >= ri)
    mask_t = (ci >= ri) if d == 0 else (ci <= ri)
    tri = mask.astype(F32)
    phi_x = jnp.dot(tri, dta_x, preferred_element_type=F32, precision=HIGHEST)
    phi_r = lax.dot_general(dta_t, tri, NT, preferred_element_type=F32, precision=HIGHEST)
    tot_x = jnp.sum(dta_x, axis=0, keepdims=True)
    return dtc, dt_x, phi_x, phi_r, tot_x, mask, mask_t


def _scan_specs(d, nc, hg, dm):
    p, n, ch = SSD_HEAD_DIM, SSD_STATE, SSD_CHUNK
    w = hg * p
    b0 = dm["HI"] // n
    c0 = (dm["HI"] + dm["GN"]) // n

    def row(b, c):
        return b * nc + c

    return [
        pl.BlockSpec((ch, w), lambda b, g, c: (row(b, c), g)),
        pl.BlockSpec((ch, n), lambda b, g, c: (row(b, c), b0 + g)),
        pl.BlockSpec((ch, n), lambda b, g, c: (row(b, c), c0 + g)),
        pl.BlockSpec((None, None, ch, hg), lambda b, g, c: (d, g, row(b, c), 0)),
        pl.BlockSpec((None, None, 8, ch), lambda b, g, c: (d, g, 0, row(b, c))),
        pl.BlockSpec((None, None, 1, hg), lambda b, g, c: (d, g, 0, 0)),
        pl.BlockSpec((None, None, 8, 1), lambda b, g, c: (d, g, 0, 0)),
    ]


def _remap(spec, f):
    return pl.BlockSpec(spec.block_shape, lambda b, g, c, im=spec.index_map: im(b, g, f(c)))


def _scan_fwd(u, dtg, dttg, alg, altg, d, dm, name):
    t = u.shape[0]
    p, n, ch, hg = SSD_HEAD_DIM, SSD_STATE, SSD_CHUNK, dm["HG"]
    w = hg * p
    nc = SEQ // ch
    order = (lambda c: c) if d == 0 else (lambda c: nc - 1 - c)

    def body(x_ref, b_ref, c_ref, dt_ref, dtt_ref, al_ref, alt_ref, y_ref, sin_ref, s_ref):
        @pl.when(pl.program_id(2) == 0)
        def _():
            s_ref[...] = jnp.zeros_like(s_ref)

        dtc, dt_x, phi_x, phi_r, tot_x, mask, _ = _scan_prelude(d, dt_ref, dtt_ref, al_ref, alt_ref, hg)
        lane_head = lax.broadcasted_iota(jnp.int32, (1, w), 1) >> int(math.log2(p))
        cm, bm = c_ref[...], b_ref[...]
        cb = lax.dot_general(cm, bm, NT, preferred_element_type=F32)
        xdt = x_ref[...].astype(F32) * dt_x
        xdt_b = xdt.astype(BF16)
        ydiag = jnp.zeros((ch, w), F32)
        for j in range(hg):
            seg = phi_x[:, j * p:j * p + 1] - phi_r[j:j + 1, :]
            mj = (cb * jnp.exp(jnp.where(mask, seg, NEG_BIG))).astype(BF16)
            ydiag = ydiag + jnp.dot(mj, jnp.where(lane_head == j, xdt_b, jnp.zeros_like(xdt_b)), preferred_element_type=F32)
        s = s_ref[...]
        y_ref[...] = ydiag + jnp.dot(cm, s.astype(BF16), preferred_element_type=F32) * jnp.exp(phi_x)
        sin_ref[...] = s
        wm = (xdt * jnp.exp(tot_x - phi_x)).astype(BF16)
        s_ref[...] = s * jnp.exp(tot_x) + lax.dot_general(bm, wm, TN, preferred_element_type=F32)

    specs = [_remap(s, order) for s in _scan_specs(d, nc, hg, dm)]
    return pl.pallas_call(
        body, name=name, grid=(t // SEQ, SSD_GROUPS, nc), in_specs=specs,
        out_specs=[_remap(pl.BlockSpec((ch, w), lambda b, g, c: (b * nc + c, g)), order),
                   _remap(pl.BlockSpec((None, None, n, w), lambda b, g, c: (b * nc + c, g, 0, 0)), order)],
        out_shape=[jax.ShapeDtypeStruct((t, dm["HI"]), F32), jax.ShapeDtypeStruct((t // ch, SSD_GROUPS, n, w), F32)],
        scratch_shapes=[pltpu.VMEM((n, w), F32)],
        compiler_params=_params(("parallel", "parallel", "arbitrary")),
    )(u, u, u, dtg, dttg, alg, altg)


def _scan_bwd(u, dtg, dttg, alg, altg, dy, y, sin, adds, d, dm, name):
    t = u.shape[0]
    p, n, ch, hg = SSD_HEAD_DIM, SSD_STATE, SSD_CHUNK, dm["HG"]
    w = hg * p
    nc = SEQ // ch
    order = (lambda c: nc - 1 - c) if d == 0 else (lambda c: c)
    has_bc_add = adds[1] is not None

    def body(*refs):
        x_ref, b_ref, c_ref, dt_ref, dtt_ref, al_ref, alt_ref, dy_ref, y_ref, sin_ref, ax_ref = refs[:11]
        pos = 11
        ab_ref = ac_ref = None
        if has_bc_add:
            ab_ref, ac_ref = refs[11], refs[12]
            pos = 13
        dxs_ref, db_ref, dc_ref, rq_ref, xdx_ref, ds_ref = refs[pos:]

        @pl.when(pl.program_id(2) == 0)
        def _():
            ds_ref[...] = jnp.zeros_like(ds_ref)

        dtc, dt_x, phi_x, phi_r, tot_x, mask, mask_t = _scan_prelude(d, dt_ref, dtt_ref, al_ref, alt_ref, hg)
        lane_head = lax.broadcasted_iota(jnp.int32, (1, w), 1) >> int(math.log2(p))
        cm, bm = c_ref[...], b_ref[...]
        cb = lax.dot_general(cm, bm, NT, preferred_element_type=F32)
        cb_t = lax.dot_general(bm, cm, NT, preferred_element_type=F32)
        xs = x_ref[...].astype(F32)
        xdt = xs * dt_x
        xdt_b = xdt.astype(BF16)
        dy = dy_ref[...]
        dy_b = dy.astype(BF16)
        zero_b = jnp.zeros_like(dy_b)
        col = lax.broadcasted_iota(jnp.int32, (1, hg), 1)
        dxp = jnp.zeros((ch, w), F32)
        a_ls = jnp.zeros((ch, ch), F32)
        a_sl = jnp.zeros((ch, ch), F32)
        dphi = jnp.zeros((ch, hg), F32)
        for j in range(hg):
            pc = phi_x[:, j * p:j * p + 1]
            pr = phi_r[j:j + 1, :]
            l_ls = jnp.exp(jnp.where(mask, pc - pr, NEG_BIG))
            l_sl = jnp.exp(jnp.where(mask_t, pr - pc, NEG_BIG))
            dy_j = jnp.where(lane_head == j, dy_b, zero_b)
            xdt_j = jnp.where(lane_head == j, xdt_b, zero_b)
            dxp = dxp + jnp.dot((cb_t * l_sl).astype(BF16), dy_j, preferred_element_type=F32)
            g_ls = l_ls * lax.dot_general(dy_j, xdt_b, NT, preferred_element_type=F32)
            g_sl = l_sl * lax.dot_general(xdt_j, dy_b, NT, preferred_element_type=F32)
            a_ls = a_ls + g_ls
            a_sl = a_sl + g_sl
            pair = jnp.sum(g_ls * cb, axis=1, keepdims=True) - jnp.sum(g_sl * cb_t, axis=1, keepdims=True)
            dphi = jnp.where(col == j, pair, dphi)
        ds = ds_ref[...]
        ds_b = ds.astype(BF16)
        sin = sin_ref[...]
        sin_b = sin.astype(BF16)
        e_tp = jnp.exp(tot_x - phi_x)
        e_p = jnp.exp(phi_x)
        dxp_off = e_tp * jnp.dot(bm, ds_b, preferred_element_type=F32)
        dxp = dxp + dxp_off
        dxs_ref[...] = ax_ref[...] + dxp * dt_x
        xdx_ref[...] = _head_sums(xs * dxp, hg, p)
        y_off = jnp.dot(cm, sin_b, preferred_element_type=F32) * e_p
        st_t = _head_sums(xdt * dxp_off, hg, p)
        dphi = dphi + _head_sums(dy * y_off, hg, p) - st_t
        dtot = _head_sums(jnp.sum(ds * sin, axis=0, keepdims=True) * jnp.exp(tot_x), hg, p) + jnp.sum(st_t, axis=0, keepdims=True)
        cum = jnp.dot(mask_t.astype(F32), _head_expand(dphi, hg, p), preferred_element_type=F32, precision=HIGHEST)
        ddta = jnp.zeros((ch, hg), F32)
        for j in range(hg):
            ddta = jnp.where(col == j, cum[:, j * p:j * p + 1], ddta)
        rq_ref[...] = ddta + dtot
        dye = (dy * e_p).astype(BF16)
        dcv = jnp.dot(a_ls.astype(BF16), bm, preferred_element_type=F32)
        dcv = dcv + lax.dot_general(dye, sin_b, NT, preferred_element_type=F32)
        dbv = jnp.dot(a_sl.astype(BF16), cm, preferred_element_type=F32)
        dbv = dbv + lax.dot_general((xdt * e_tp).astype(BF16), ds_b, NT, preferred_element_type=F32)
        if has_bc_add:
            dcv = dcv + ac_ref[...]
            dbv = dbv + ab_ref[...]
        dc_ref[...] = dcv
        db_ref[...] = dbv
        ds_ref[...] = ds * jnp.exp(tot_x) + lax.dot_general(cm, dye, TN, preferred_element_type=F32)

    def sp(spec):
        return _remap(spec, order)

    xw = pl.BlockSpec((ch, w), lambda b, g, c: (b * nc + c, g))
    gn_blk = pl.BlockSpec((ch, n), lambda b, g, c: (b * nc + c, g))
    small = pl.BlockSpec((None, ch, hg), lambda b, g, c: (g, b * nc + c, 0))
    in_specs = [sp(s) for s in _scan_specs(d, nc, hg, dm)]
    in_specs += [sp(xw), sp(xw), sp(pl.BlockSpec((None, None, n, w), lambda b, g, c: (b * nc + c, g, 0, 0))), sp(xw)]
    args = [u, u, u, dtg, dttg, alg, altg, dy, y, sin, adds[0]]
    if has_bc_add:
        in_specs += [sp(gn_blk), sp(gn_blk)]
        args += [adds[1], adds[2]]
    return pl.pallas_call(
        body, name=name, grid=(t // SEQ, SSD_GROUPS, nc), in_specs=in_specs,
        out_specs=[sp(xw), sp(gn_blk), sp(gn_blk), sp(small), sp(small)],
        out_shape=[jax.ShapeDtypeStruct((t, dm["HI"]), F32), jax.ShapeDtypeStruct((t, dm["GN"]), F32),
                   jax.ShapeDtypeStruct((t, dm["GN"]), F32), jax.ShapeDtypeStruct((SSD_GROUPS, t, hg), F32),
                   jax.ShapeDtypeStruct((SSD_GROUPS, t, hg), F32)],
        scratch_shapes=[pltpu.VMEM((n, w), F32)],
        compiler_params=_params(("parallel", "parallel", "arbitrary")),
    )(*args)


def _ssd_param_bwd(rq_f, rq_r, xdx, dtp, dtr, bias, alog, name):
    def body(rf_ref, rr_ref, xdx_ref, dt_ref, dtr_ref, b_ref, al_ref, o_ref, db_ref, da_ref):
        a = -jnp.exp(al_ref[...])
        d_dta = rf_ref[...] + rr_ref[...]
        ddt = a * d_dta + xdx_ref[...]
        ddtr = ddt * _sigmoid(dtr_ref[...] + b_ref[...])
        o_ref[...] = ddtr
        db_ref[...] += jnp.sum(ddtr, axis=0, keepdims=True)
        da_ref[...] += a * jnp.sum(dt_ref[...] * d_dta, axis=0, keepdims=True)

    return _rowwise(body, [rq_f, rq_r, xdx, dtp, dtr], [bias, alog], [(LANES, F32)],
                    [((1, LANES), F32), ((1, LANES), F32)], tile=512, name=name)


def _ssd_out_fwd(y_f, y_b, u, proj, dcols, gn, hi, name):
    def body(yf_ref, yb_ref, x_ref, z_ref, d_ref, g_ref, o_ref):
        ytot = yf_ref[...] + yb_ref[...] + d_ref[...] * x_ref[...].astype(F32)
        zv = z_ref[...].astype(F32)
        yz = ytot * (zv * _sigmoid(zv))
        rstd = lax.rsqrt(jnp.mean(yz * yz, axis=-1, keepdims=True) + EPS)
        o_ref[...] = (yz * rstd * g_ref[...]).astype(BF16)

    return _rowwise(body, [y_f, y_b, (u, 0, hi), (proj, 0, hi)], [dcols, gn], [(hi, BF16)], tile=256, name=name)[0]


def _ssd_out_bwd(dya, y_f, y_b, u, proj, dcols, gn, hi, name):
    def body(dy_ref, yf_ref, yb_ref, x_ref, z_ref, d_ref, g_ref, dyt_ref, dxs_ref, dz_ref, dg_ref, dd_ref):
        xv = x_ref[...].astype(F32)
        ytot = yf_ref[...] + yb_ref[...] + d_ref[...] * xv
        zv = z_ref[...].astype(F32)
        sg = _sigmoid(zv)
        sz = zv * sg
        yz = ytot * sz
        rstd = lax.rsqrt(jnp.mean(yz * yz, axis=-1, keepdims=True) + EPS)
        yn = yz * rstd
        dv = dy_ref[...]
        dg_ref[...] += jnp.sum(dv * yn, axis=0, keepdims=True)
        dn = dv * g_ref[...]
        dyz = rstd * (dn - yn * jnp.mean(dn * yn, axis=-1, keepdims=True))
        dyt = dyz * sz
        dyt_ref[...] = dyt
        dxs_ref[...] = dyt * d_ref[...]
        dz_ref[...] = (dyz * ytot * (sg * (1.0 + zv * (1.0 - sg)))).astype(BF16)
        dd_ref[...] += jnp.sum(dyt * xv, axis=0, keepdims=True)

    return _rowwise(body, [dya, y_f, y_b, (u, 0, hi), (proj, 0, hi)], [dcols, gn],
                    [(hi, F32), (hi, F32), (hi, BF16)], [((1, hi), F32), ((1, hi), F32)], tile=128, name=name)


def _gate_fwd(pa, pb, pc, proj, off, d, name):
    def body(a_ref, b_ref, c_ref, g0_ref, g1_ref, g2_ref, o_ref):
        acc = _sigmoid(g0_ref[...].astype(F32)) * a_ref[...]
        acc = acc + _sigmoid(g1_ref[...].astype(F32)) * b_ref[...]
        acc = acc + _sigmoid(g2_ref[...].astype(F32)) * c_ref[...]
        o_ref[...] = acc.astype(BF16)

    rows = [pa, pb, pc] + [(proj, off + k * d, d) for k in range(3)]
    return _rowwise(body, rows, [], [(d, BF16)], tile=256, name=name)[0]


def _gate_bwd(dm_, pa, pb, pc, proj, off, d, name):
    def body(dm_ref, a_ref, b_ref, c_ref, g0_ref, g1_ref, g2_ref, da_ref, db_ref, dc_ref, dg0_ref, dg1_ref, dg2_ref):
        dmv = dm_ref[...]
        for p_ref, g_ref, dp_ref, dg_ref in ((a_ref, g0_ref, da_ref, dg0_ref), (b_ref, g1_ref, db_ref, dg1_ref),
                                             (c_ref, g2_ref, dc_ref, dg2_ref)):
            sg = _sigmoid(g_ref[...].astype(F32))
            dp_ref[...] = (dmv * sg).astype(BF16)
            dg_ref[...] = (dmv * p_ref[...] * sg * (1.0 - sg)).astype(BF16)

    rows = [dm_, pa, pb, pc] + [(proj, off + k * d, d) for k in range(3)]
    return _rowwise(body, rows, [], [(d, BF16)] * 6, tile=128, name=name)


def _adamw(w, g, m, v, name):
    rows, cols = w.shape
    tile = _pick(rows, (32, 16, 8)) if rows >= 8 else rows
    c1 = 1.0 / (1.0 - ADAM_B1 ** ADAM_STEP)
    c2 = 1.0 / (1.0 - ADAM_B2 ** ADAM_STEP)

    def body(w_ref, g_ref, m_ref, v_ref, d_ref, nm_ref, nv_ref):
        gv = g_ref[...]
        nm = ADAM_B1 * m_ref[...] + (1.0 - ADAM_B1) * gv
        nv = ADAM_B2 * v_ref[...] + (1.0 - ADAM_B2) * (gv * gv)
        nm_ref[...] = nm
        nv_ref[...] = nv
        d_ref[...] = -ADAM_LR * ((nm * c1) / (jnp.sqrt(nv * c2) + ADAM_EPS) + ADAM_WD * w_ref[...])

    return _rowwise(body, [w, g, m, v], [], [(cols, F32)] * 3, tile=tile, name=name)


ANY = pl.BlockSpec(memory_space=pl.ANY)


def _place():
    x, y, c = lax.axis_index("x"), lax.axis_index("y"), lax.axis_index("c")
    chips = [(1 - x, y), (x, 1 - y), (1 - x, 1 - y)]
    return x, y, c, chips


def _gather_chips(arr, name):
    def body(src, out, lsem, ssem, rsem):
        x, y, c, chips = _place()
        k = 2 * x + y
        local = pltpu.make_async_copy(src, out.at[k], lsem)
        local.start()

        def copy(j, kk, layer, to, own=False):
            return pltpu.make_async_remote_copy(
                src_ref=src.at[layer] if own else out.at[kk, layer], dst_ref=out.at[kk, layer],
                send_sem=ssem.at[j], recv_sem=rsem.at[j], device_id=to, device_id_type=MESH)

        first = [copy(j, k, c, (cx, cy, c), own=True) for j, (cx, cy) in enumerate(chips)]
        for cp in first:
            cp.start()
        passed = [copy(3 + j, 2 * cx + cy, c, (x, y, 1 - c)) for j, (cx, cy) in enumerate(chips)]
        for j, (cx, cy) in enumerate(chips):
            copy(j, 2 * cx + cy, c, (x, y, c)).wait_recv()
            passed[j].start()
        for j, (cx, cy) in enumerate(chips):
            copy(3 + j, 2 * cx + cy, 1 - c, (x, y, c)).wait_recv()
        for cp in first + passed:
            cp.wait_send()
        local.wait()

    return pl.pallas_call(
        body, name=name, in_specs=[ANY], out_specs=ANY, out_shape=jax.ShapeDtypeStruct((4,) + arr.shape, arr.dtype),
        scratch_shapes=[pltpu.SemaphoreType.DMA, pltpu.SemaphoreType.DMA((6,)), pltpu.SemaphoreType.DMA((6,))],
    )(arr)


def _pair_swap(g, name):
    def body(src, out, ssem, rsem):
        x, y, c, _ = _place()
        cp = pltpu.make_async_remote_copy(src_ref=src.at[1 - c], dst_ref=out, send_sem=ssem, recv_sem=rsem,
                                          device_id=(x, y, 1 - c), device_id_type=MESH)
        cp.start()
        cp.wait()

    return pl.pallas_call(
        body, name=name, in_specs=[ANY], out_specs=ANY, out_shape=jax.ShapeDtypeStruct(g.shape[1:], g.dtype),
        scratch_shapes=[pltpu.SemaphoreType.DMA, pltpu.SemaphoreType.DMA],
    )(g)


def _chip_exchange(p, name):
    def body(src, out, ssem, rsem):
        x, y, c, chips = _place()
        cps = [pltpu.make_async_remote_copy(src_ref=src.at[2 * cx + cy], dst_ref=out.at[j], send_sem=ssem.at[j],
                                            recv_sem=rsem.at[j], device_id=(cx, cy, c), device_id_type=MESH)
               for j, (cx, cy) in enumerate(chips)]
        for cp in cps:
            cp.start()
        for cp in cps:
            cp.wait()

    return pl.pallas_call(
        body, name=name, in_specs=[ANY], out_specs=ANY, out_shape=jax.ShapeDtypeStruct((3,) + p.shape[1:], p.dtype),
        scratch_shapes=[pltpu.SemaphoreType.DMA((3,)), pltpu.SemaphoreType.DMA((3,))],
    )(p)


def _pair_share(r, name):
    def body(src, out, lsem, ssem, rsem):
        x, y, c, _ = _place()
        local = pltpu.make_async_copy(src, out.at[c], lsem)
        local.start()
        cp = pltpu.make_async_remote_copy(src_ref=src, dst_ref=out.at[c], send_sem=ssem, recv_sem=rsem,
                                          device_id=(x, y, 1 - c), device_id_type=MESH)
        cp.start()
        pltpu.make_async_remote_copy(src_ref=src, dst_ref=out.at[1 - c], send_sem=ssem, recv_sem=rsem,
                                     device_id=(x, y, 1 - c), device_id_type=MESH).wait_recv()
        cp.wait_send()
        local.wait()

    return pl.pallas_call(
        body, name=name, in_specs=[ANY], out_specs=ANY, out_shape=jax.ShapeDtypeStruct((2,) + r.shape, r.dtype),
        scratch_shapes=[pltpu.SemaphoreType.DMA, pltpu.SemaphoreType.DMA, pltpu.SemaphoreType.DMA],
    )(r)


def _sum2(a, b, out_dtype, name):
    def body(a_ref, b_ref, o_ref):
        o_ref[...] = (a_ref[...].astype(F32) + b_ref[...].astype(F32)).astype(out_dtype)

    rows, cols = a.shape
    return _rowwise(body, [a, b], [], [(cols, out_dtype)], tile=_pick(rows, (64, 32, 16, 8)), name=name)[0]


def _sum4(a, b, name):
    rows, cols = a.shape
    tile = _pick(rows, (64, 32, 16, 8))

    def body(a_ref, b0_ref, b1_ref, b2_ref, o_ref):
        acc = a_ref[...].astype(F32) + b0_ref[...].astype(F32)
        acc = acc + b1_ref[...].astype(F32)
        o_ref[...] = acc + b2_ref[...].astype(F32)

    bspec = [pl.BlockSpec((None, tile, cols), lambda i, j=j: (j, i, 0)) for j in range(3)]
    return pl.pallas_call(
        body, name=name, grid=(rows // tile,), in_specs=[pl.BlockSpec((tile, cols), lambda i: (i, 0))] + bspec,
        out_specs=pl.BlockSpec((tile, cols), lambda i: (i, 0)), out_shape=jax.ShapeDtypeStruct((rows, cols), F32),
        compiler_params=_params(("parallel",)),
    )(a, b, b, b)


def _reduce_scatter(g, name):
    _, _, rows, cols = g.shape
    c = lax.axis_index("c")
    k = 2 * lax.axis_index("x") + lax.axis_index("y")
    got = _pair_swap(g, name + "_pair")
    mine = lax.dynamic_index_in_dim(g, c, 0, keepdims=False)
    part = _sum2(mine.reshape(4 * rows, cols), got.reshape(4 * rows, cols), BF16, name + "_add2").reshape(4, rows, cols)
    others = _chip_exchange(part, name + "_chips")
    own = lax.dynamic_index_in_dim(part, k, 0, keepdims=False)
    total = _sum4(own, others, name + "_add4")
    return _pair_share(total, name + "_share")


def _all_reduce_small(buf, name):
    rows = buf.shape[0]

    def body(src, out, slots, ssem, rsem):
        x, y, c, _ = _place()
        me = 4 * x + 2 * y + c
        slots[me] = src[...]
        cps = []
        for j in range(1, 8):
            px, py, pc = x ^ (j >> 2), y ^ ((j >> 1) & 1), c ^ (j & 1)
            cps.append(pltpu.make_async_remote_copy(src_ref=src, dst_ref=slots.at[me], send_sem=ssem.at[j - 1],
                                                    recv_sem=rsem.at[j - 1], device_id=(px, py, pc), device_id_type=MESH))
        for cp in cps:
            cp.start()
        for j in range(1, 8):
            peer = me ^ j
            pltpu.make_async_remote_copy(src_ref=src, dst_ref=slots.at[peer], send_sem=ssem.at[j - 1], recv_sem=rsem.at[j - 1],
                                         device_id=(x, y, c), device_id_type=MESH).wait_recv()
        for cp in cps:
            cp.wait_send()
        acc = slots[0]
        for d in range(1, 8):
            acc = acc + slots[d]
        out[...] = acc

    vm = pl.BlockSpec(memory_space=pltpu.VMEM)
    return pl.pallas_call(
        body, name=name, in_specs=[vm], out_specs=vm, out_shape=jax.ShapeDtypeStruct((rows, LANES), F32),
        scratch_shapes=[pltpu.VMEM((8, rows, LANES), F32), pltpu.SemaphoreType.DMA((7,)), pltpu.SemaphoreType.DMA((7,))],
    )(buf)


def _pack(arrs):
    flat = jnp.concatenate([a.astype(F32).reshape(-1) for a in arrs])
    n = flat.shape[0]
    padded = -(-n // (8 * LANES)) * (8 * LANES)
    return jnp.pad(flat, (0, padded - n)).reshape(padded // LANES, LANES)


def _unpack(buf, like):
    flat = buf.reshape(-1)
    out, pos = [], 0
    for a in like:
        out.append(flat[pos:pos + a.size].reshape(a.shape))
        pos += a.size
    return out


def _stride(t2d, dil):
    t, w = t2d.shape
    b = t // SEQ
    return t2d.reshape(b, SEQ // dil, dil, w).transpose(0, 2, 1, 3).reshape(b * dil, SEQ // dil, w)


def _unstride(t3d, dil):
    bb, n, w = t3d.shape
    b = bb // dil
    return t3d.reshape(b, dil, n, w).transpose(0, 2, 1, 3).reshape(b * SEQ, w)


def _stat_cols(st, dil, heads):
    s3 = _stride(st, dil)
    return s3.transpose(0, 2, 1)[..., None]


def _stat_rows(col):
    bb, h, n, _ = col.shape
    return col.reshape(bb, h, 1, n)


def _scan_params(dtp, alog, dm):
    t = dtp.shape[0]
    g, hg = SSD_GROUPS, dm["HG"]
    dt4 = dtp[:, :dm["H2"]].reshape(t, 2, g, hg)
    dtg = dt4.transpose(1, 2, 0, 3)
    dttg = jnp.pad(dt4.transpose(1, 2, 3, 0), ((0, 0), (0, 0), (0, 8 - hg), (0, 0)))
    al = alog.reshape(2, g, 1, hg)
    alt = jnp.pad(alog.reshape(2, g, hg, 1), ((0, 0), (0, 0), (0, 8 - hg), (0, 0)))
    return dtg, dttg, al, alt


def _layer_fwd(x, wl, tabs, dm, li):
    d = dm["D"]
    nm = f"l{li}_"
    h = _rms_fwd(x, wl["g_mix"], nm + "rms1")
    proj = _mm(h, wl["w_main"], name=nm + "proj")
    dtr = _mm(h, wl["w_dt"], out_dtype=F32, name=nm + "proj_dt")
    cpre, u = _conv_fwd(proj, dm["OFF_XBC"], wl["conv_w"], wl["conv_b"], dm["XBC"], nm + "conv")
    dtp = _dt_prep(dtr, wl["dt_bias"], nm + "dt")
    sp = _scan_params(dtp, wl["a_log"], dm)
    y_f, s_f = _scan_fwd(u, *sp, 0, dm, nm + "scan_f")
    y_b, s_b = _scan_fwd(u, *sp, 1, dm, nm + "scan_b")
    y_a = _ssd_out_fwd(y_f, y_b, u, proj, wl["d_cols"], wl["ssd_norm"], dm["HI"], nm + "ssd_out")
    qkv = _rope(proj, dm["OFF_QKV"], dm["QW"] // HEAD_DIM, tabs[0], dm, nm + "rope")
    ng, dw = dm["NG"], dm["DW"]
    outs, lses, xgs = [], [], []
    for gi, (window, dil) in enumerate(DIL_PATTERNS):
        cols = [qkv[:, s * ng * dw + gi * dw:s * ng * dw + (gi + 1) * dw] for s in range(3)]
        xg = _stride(jnp.concatenate(cols, axis=1), dil)
        o, lse = _attn_fwd(xg, xg, xg, 0, DIL_HEADS, 2 * DIL_HEADS, DIL_HEADS, 1, SEQ // dil, window // (2 * dil),
                           None, F32, nm + f"dil{gi}")
        xgs.append(xg)
        outs.append(_unstride(o, dil))
        lses.append(_unstride(lse[..., 0].transpose(0, 2, 1), dil))
    y_bm, lse_tot = _dil_combine(outs, lses, nm + "dil_mix")
    bsz = x.shape[0] // SEQ
    qkv3 = qkv.reshape(bsz, SEQ, dm["QW"])
    wq0 = 3 * ng * DIL_HEADS
    rep = WIN_Q_HEADS // WIN_KV_HEADS
    y_c3, lse_w = _attn_fwd(qkv3, qkv3, qkv3, wq0, wq0 + WIN_Q_HEADS, wq0 + WIN_Q_HEADS + WIN_KV_HEADS, WIN_Q_HEADS, rep,
                            SEQ, WIN_HALF, wl["sink"].reshape(WIN_Q_HEADS, 1, 1), BF16, nm + "win")
    y_c = y_c3.reshape(x.shape[0], dm["WQ"])
    pa = _mm(y_a, wl["w_a"], out_dtype=F32, name=nm + "pa")
    pb = _mm(y_bm, wl["w_b"], out_dtype=F32, name=nm + "pb")
    pc = _mm(y_c, wl["w_c"], out_dtype=F32, name=nm + "pc")
    merged = _gate_fwd(pa, pb, pc, proj, dm["OFF_GATE"], d, nm + "gate")
    x1 = _mm(merged, wl["w_out"], add=x, out_dtype=F32, name=nm + "out")
    hm = _rms_fwd(x1, wl["g_mlp"], nm + "rms2")
    up, act = _mm(hm, wl["w_up"], epi="relu2", name=nm + "up")
    x2 = _mm(act, wl["w_down"], add=x1, out_dtype=F32, name=nm + "down")
    saved = dict(x=x, h=h, proj=proj, dtr=dtr, cpre=cpre, u=u, dtp=dtp, y_f=y_f, y_b=y_b, s_f=s_f, s_b=s_b, y_a=y_a,
                 qkv=qkv, xgs=xgs, y_bm=y_bm, lse_tot=lse_tot, y_c=y_c, lse_w=lse_w, pa=pa, pb=pb, pc=pc,
                 merged=merged, x1=x1, hm=hm, up=up, act=act)
    return x2, saved


def _layer_bwd(dx2, wl, sv, tabs, dm, li):
    d = dm["D"]
    t = dx2.shape[0]
    bsz = t // SEQ
    nm = f"l{li}b_"
    gr = {}
    dup = _mm(dx2, wl["w_down"], tb=True, aux=sv["up"], epi="relu2_bwd", name=nm + "dup")
    gr["w_down"] = _mm(sv["act"], dx2, ta=True, name=nm + "gw_down")
    dhm = _mm(dup, wl["w_up"], tb=True, out_dtype=F32, name=nm + "dhm")
    gr["w_up"] = _mm(sv["hm"], dup, ta=True, name=nm + "gw_up")
    dx1, gmlp = _rms_bwd(sv["x1"], wl["g_mlp"], dhm, dx2, nm + "rms2")
    gr["g_mlp"] = gmlp[0]
    dmerged = _mm(dx1, wl["w_out"], tb=True, out_dtype=F32, name=nm + "dmerged")
    gr["w_out"] = _mm(sv["merged"], dx1, ta=True, name=nm + "gw_out")
    dpa, dpb, dpc, dg0, dg1, dg2 = _gate_bwd(dmerged, sv["pa"], sv["pb"], sv["pc"], sv["proj"], dm["OFF_GATE"], d, nm + "gate")
    dya = _mm(dpa, wl["w_a"], tb=True, out_dtype=F32, name=nm + "dya")
    gr["w_a"] = _mm(sv["y_a"], dpa, ta=True, name=nm + "gw_a")
    dyb = _mm(dpb, wl["w_b"], tb=True, out_dtype=F32, name=nm + "dyb")
    gr["w_b"] = _mm(sv["y_bm"], dpb, ta=True, name=nm + "gw_b")
    dyc = _mm(dpc, wl["w_c"], tb=True, out_dtype=F32, name=nm + "dyc")
    gr["w_c"] = _mm(sv["y_c"], dpc, ta=True, name=nm + "gw_c")
    ng, dw = dm["NG"], dm["DW"]
    qkv3 = sv["qkv"].reshape(bsz, SEQ, dm["QW"])
    wq0 = 3 * ng * DIL_HEADS
    wk0, wv0 = wq0 + WIN_Q_HEADS, wq0 + WIN_Q_HEADS + WIN_KV_HEADS
    rep = WIN_Q_HEADS // WIN_KV_HEADS
    delta_w = _head_dots(dyc, sv["y_c"], WIN_Q_HEADS, nm + "win_delta")
    dl_col = _stat_cols(delta_w, 1, WIN_Q_HEADS)
    lse_w = sv["lse_w"]
    dyc3 = dyc.reshape(bsz, SEQ, dm["WQ"])
    wargs = (wq0, wk0, wv0, WIN_Q_HEADS, rep, SEQ, WIN_HALF)
    dq_w = _attn_dq(qkv3, qkv3, qkv3, dyc3, lse_w, dl_col, *wargs, nm + "win_dq")
    dk_w, dv_w = _attn_dkv(qkv3, qkv3, qkv3, dyc3, _stat_rows(lse_w), _stat_rows(dl_col), *wargs, nm + "win_dkv")
    lse_w2 = lse_w[..., 0].transpose(0, 2, 1).reshape(t, WIN_Q_HEADS)
    gr["sink"] = _sink_grad(lse_w2, delta_w, wl["sink"], nm + "sink")[0]
    delta_d = _head_dots(dyb, sv["y_bm"], DIL_HEADS, nm + "dil_delta")
    dqs, dks, dvs = [], [], []
    for gi, (window, dil) in enumerate(DIL_PATTERNS):
        xg = sv["xgs"][gi]
        n = SEQ // dil
        do_g = _stride(dyb, dil)
        lse_c = _stat_cols(sv["lse_tot"], dil, DIL_HEADS)
        dl_c = _stat_cols(delta_d, dil, DIL_HEADS)
        dargs = (0, DIL_HEADS, 2 * DIL_HEADS, DIL_HEADS, 1, n, window // (2 * dil))
        dq = _attn_dq(xg, xg, xg, do_g, lse_c, dl_c, *dargs, nm + f"dil{gi}_dq")
        dk, dv = _attn_dkv(xg, xg, xg, do_g, _stat_rows(lse_c), _stat_rows(dl_c), *dargs, nm + f"dil{gi}_dkv")
        dqs.append(_unstride(dq, dil))
        dks.append(_unstride(dk, dil))
        dvs.append(_unstride(dv, dil))
    dqkv_r = jnp.concatenate(dqs + dks + dvs + [dq_w.reshape(t, dm["WQ"]), dk_w.reshape(t, dm["WK"]), dv_w.reshape(t, dm["WK"])],
                             axis=1)
    dqkv = _rope(dqkv_r, 0, dm["QW"] // HEAD_DIM, tabs[1], dm, nm + "rope")
    hi, gn = dm["HI"], dm["GN"]
    dyt, dxs0, dz, gnorm, dd_cols = _ssd_out_bwd(dya, sv["y_f"], sv["y_b"], sv["u"], sv["proj"], wl["d_cols"],
                                                          wl["ssd_norm"], hi, nm + "ssd_out")
    gr["ssd_norm"] = gnorm[0]
    gr["d_skip"] = dd_cols.reshape(SSD_HEADS, SSD_HEAD_DIM).sum(axis=1)
    sp = _scan_params(sv["dtp"], wl["a_log"], dm)
    dxs1, db1, dc1, rq_f, xdx_f = _scan_bwd(sv["u"], *sp, dyt, sv["y_f"], sv["s_f"], (dxs0, None, None), 0, dm, nm + "scan_f")
    dxs2, db2, dc2, rq_r, xdx_r = _scan_bwd(sv["u"], *sp, dyt, sv["y_b"], sv["s_b"], (dxs1, db1, dc1), 1, dm, nm + "scan_b")

    def heads(a):
        return a.transpose(1, 0, 2).reshape(t, SSD_HEADS)

    zpad = jnp.zeros((t, LANES - dm["H2"]), F32)
    zh = jnp.zeros((t, SSD_HEADS), F32)
    rqf_p = jnp.concatenate([heads(rq_f), zh, zpad], axis=1)
    rqr_p = jnp.concatenate([zh, heads(rq_r), zpad], axis=1)
    xdx_p = jnp.concatenate([heads(xdx_f), heads(xdx_r), zpad], axis=1)
    ddtr, dbias, dalog = _ssd_param_bwd(rqf_p, rqr_p, xdx_p, sv["dtp"], sv["dtr"], wl["dt_bias"], wl["a_log_p"], nm + "ssd_par")
    gr["dt_bias"] = dbias[0, :dm["H2"]].reshape(2, SSD_HEADS)
    gr["a_log"] = dalog[0, :dm["H2"]].reshape(2, SSD_HEADS)
    du = jnp.concatenate([dxs2, db2, dc2], axis=1)
    dxbc, gr["conv_w"], gcb = _conv_bwd(du, sv["cpre"], sv["proj"], dm["OFF_XBC"], wl["conv_w"], nm + "conv")
    gr["conv_b"] = gcb[0]
    dproj = jnp.concatenate([dz, dxbc, dqkv, dg0, dg1, dg2], axis=1)
    dh_dt = _mm(ddtr, wl["w_dt"], tb=True, out_dtype=F32, name=nm + "dh_dt")
    dh = _mm(dproj, wl["w_main"], tb=True, add=dh_dt, out_dtype=F32, name=nm + "dh")
    gw_main = _mm(sv["h"], dproj, ta=True, name=nm + "gw_main")
    gw_dt = _mm(sv["h"], ddtr, ta=True, name=nm + "gw_dt")
    o1 = dm["OFF_QKV"]
    gr["w_in"] = jnp.concatenate([gw_main[:, :o1], gw_dt[:, :dm["H2"]], gw_main[:, o1:]], axis=1)
    dx, gmix = _rms_bwd(sv["x"], wl["g_mix"], dh, dx1, nm + "rms1")
    gr["g_mix"] = gmix[0]
    return dx, gr


def _layer_weights(full, li, dm):
    w_in = full["w_in"][li]
    o1 = dm["OFF_QKV"]
    h2 = dm["H2"]
    d = dm["D"]
    wl = dict(
        w_main=jnp.concatenate([w_in[:, :o1], w_in[:, o1 + h2:]], axis=1),
        w_dt=jnp.pad(w_in[:, o1:o1 + h2], ((0, 0), (0, LANES - h2))),
        w_a=full["w_a"][li], w_b=full["w_b"][li], w_c=full["w_c"][li], w_out=full["w_out"][li],
        w_up=full["w_up"][li], w_down=full["w_down"][li],
        conv_w=full["conv_w"][li], conv_b=full["conv_b"][li][None, :],
        g_mix=full["g_mix"][li][None, :], g_mlp=full["g_mlp"][li][None, :], ssd_norm=full["ssd_norm"][li][None, :],
        d_cols=jnp.repeat(full["d_skip"][li], SSD_HEAD_DIM)[None, :],
        sink=full["sink"][li][None, :],
        a_log=full["a_log"][li],
        a_log_p=jnp.pad(full["a_log"][li].reshape(1, h2), ((0, 0), (0, LANES - h2))),
        dt_bias=jnp.pad(full["dt_bias"][li].reshape(1, h2), ((0, 0), (0, LANES - h2))),
    )
    assert wl["w_main"].shape == (d, dm["NM"])
    return wl


def _local_step(x, target, full, depth):
    bsz, seq, d = x.shape
    assert seq == SEQ
    dm = _dims(d)
    assert dm["OFF_GATE"] % d == 0 and dm["HI"] % (dm["HG"] * SSD_HEAD_DIM) == 0 and dm["H2"] <= LANES
    tabs = (_rope_tables(1.0), _rope_tables(-1.0))
    xt = x.reshape(bsz * seq, d)
    wls, saves = [], []
    for li in range(depth):
        wl = _layer_weights(full, li, dm)
        xt, sv = _layer_fwd(xt, wl, tabs, dm, li)
        wls.append(wl)
        saves.append(sv)
    dx, loss, g_final = _loss_head(xt, full["g_final"][None, :], target.reshape(bsz * seq, d), "loss_head")
    grads = [None] * depth
    for li in reversed(range(depth)):
        dx, grads[li] = _layer_bwd(dx, wls[li], saves[li], tabs, dm, li)
    return loss, dx.reshape(bsz, seq, d), grads, g_final[0]


BIG = ("w_in", "w_a", "w_b", "w_c", "w_out", "w_up", "w_down")
COL_SHARDED = ("w_in", "w_b", "w_up")
SMALL = ("g_mix", "conv_w", "conv_b", "dt_bias", "a_log", "d_skip", "ssd_norm", "sink", "g_mlp", "g_final")
ORDER = ("g_mix", "w_in", "conv_w", "conv_b", "dt_bias", "a_log", "d_skip", "ssd_norm", "w_a", "w_b", "w_c", "sink",
         "w_out", "g_mlp", "w_up", "w_down", "g_final")


def _unstack(name, st):
    _, nl, r, c = st.shape
    if name in COL_SHARDED:
        return jnp.moveaxis(st, 0, 2).reshape(nl, r, 4 * c)
    return jnp.moveaxis(st, 0, 1).reshape(nl, 4 * r, c)


def _restack(name, gfull):
    nl, r, c = gfull.shape
    if name in COL_SHARDED:
        return jnp.moveaxis(gfull.reshape(nl, r, 4, c // 4), 2, 1)
    return gfull.reshape(nl, 4, r // 4, c)


def kernel(x, g_mix, w_in, conv_w, conv_b, dt_bias, a_log, d_skip, ssd_norm, w_a, w_b, w_c, sink, w_out, g_mlp, w_up, w_down, g_final, loss_target, m_g_mix, m_w_in, m_conv_w, m_conv_b, m_dt_bias, m_a_log, m_d_skip, m_ssd_norm, m_w_a, m_w_b, m_w_c, m_sink, m_w_out, m_g_mlp, m_w_up, m_w_down, m_g_final, v_g_mix, v_w_in, v_conv_w, v_conv_b, v_dt_bias, v_a_log, v_d_skip, v_ssd_norm, v_w_a, v_w_b, v_w_c, v_sink, v_w_out, v_g_mlp, v_w_up, v_w_down, v_g_final):
    w = dict(g_mix=g_mix, w_in=w_in, conv_w=conv_w, conv_b=conv_b, dt_bias=dt_bias, a_log=a_log, d_skip=d_skip,
             ssd_norm=ssd_norm, w_a=w_a, w_b=w_b, w_c=w_c, sink=sink, w_out=w_out, g_mlp=g_mlp, w_up=w_up, w_down=w_down,
             g_final=g_final)
    m = dict(g_mix=m_g_mix, w_in=m_w_in, conv_w=m_conv_w, conv_b=m_conv_b, dt_bias=m_dt_bias, a_log=m_a_log,
             d_skip=m_d_skip, ssd_norm=m_ssd_norm, w_a=m_w_a, w_b=m_w_b, w_c=m_w_c, sink=m_sink, w_out=m_w_out,
             g_mlp=m_g_mlp, w_up=m_w_up, w_down=m_w_down, g_final=m_g_final)
    v = dict(g_mix=v_g_mix, w_in=v_w_in, conv_w=v_conv_w, conv_b=v_conv_b, dt_bias=v_dt_bias, a_log=v_a_log,
             d_skip=v_d_skip, ssd_norm=v_ssd_norm, w_a=v_w_a, w_b=v_w_b, w_c=v_w_c, sink=v_sink, w_out=v_w_out,
             g_mlp=v_g_mlp, w_up=v_w_up, w_down=v_w_down, g_final=v_g_final)
    depth = w_in.shape[0]
    assert depth == 2
    kchip = 2 * lax.axis_index("x") + lax.axis_index("y")

    full = {n: w[n] for n in SMALL if n != "conv_w"}
    for n in BIG:
        full[n] = _unstack(n, _gather_chips(w[n].astype(BF16), "gather_" + n))
    cw = _gather_chips(conv_w, "gather_conv_w")
    full["conv_w"] = jnp.moveaxis(cw, 0, 2).reshape(depth, CONV_WIDTH, 4 * conv_w.shape[2])

    loss_part, grad_x, grads, gg_final = _local_step(x, loss_target, full, depth)

    gsh = {}
    for n in BIG:
        gfull = jnp.stack([grads[li][n] for li in range(depth)])
        gsh[n] = _reduce_scatter(_restack(n, gfull), "rs_" + n)
    small_names = [n for n in SMALL if n != "g_final"]
    small_g = [jnp.stack([grads[li][n] for li in range(depth)]) for n in small_names] + [gg_final, loss_part[0, :1]]
    red = _unpack(_all_reduce_small(_pack(small_g), "allreduce_small"), small_g)
    for n, a in zip(small_names + ["g_final"], red):
        gsh[n] = a
    loss = red[-1][0]
    cshard = conv_w.shape[2]
    gsh["conv_w"] = lax.dynamic_slice_in_dim(gsh["conv_w"], kchip * cshard, cshard, axis=2)

    delta, new_m, new_v = {}, {}, {}
    for n in BIG:
        shp = w[n].shape
        two = lambda a: a.reshape(shp[0] * shp[1], shp[2])
        dl, nm_, nv_ = _adamw(two(w[n]), two(gsh[n]), two(m[n]), two(v[n]), "adamw_" + n)
        delta[n], new_m[n], new_v[n] = dl.reshape(shp), nm_.reshape(shp), nv_.reshape(shp)
    sm = list(SMALL)
    packed = [_pack([d_[n] for n in sm]) for d_ in (w, gsh, m, v)]
    outs = _adamw(*packed, "adamw_small")
    for dst, buf in zip((delta, new_m, new_v), outs):
        for n, a in zip(sm, _unpack(buf, [w[n] for n in sm])):
            dst[n] = a
    return (loss, grad_x, *[gsh[n] for n in ORDER], *[delta[n] for n in ORDER], *[new_m[n] for n in ORDER],
            *[new_v[n] for n in ORDER])
```

```python
import functools
import math

import jax
import jax.numpy as jnp
from jax import lax
from jax.experimental import pallas as pl
from jax.experimental.pallas import tpu as pltpu

F32 = jnp.float32
BF16 = jnp.bfloat16

SEQ = 2048
SSD_HEADS = 32
SSD_HEAD_DIM = 64
SSD_GROUPS = 8
SSD_STATE = 128
SSD_CHUNK = 128
CONV_WIDTH = 5
HEAD_DIM = 128
ROPE_DIM = 32
ROPE_THETA = 500000.0
DIL_PATTERNS = ((128, 1), (512, 4), (2048, 16))
DIL_HEADS = 8
WIN_Q_HEADS = 16
WIN_KV_HEADS = 4
WIN_HALF = 128
EPS = 1e-6
NEG_BIG = -1e30
ADAM_LR = 0.001
ADAM_B1 = 0.9
ADAM_B2 = 0.999
ADAM_EPS = 1e-08
ADAM_WD = 0.01
ADAM_STEP = 10

LANES = 128
ATT_BLK = 128
VMEM_LIMIT = 48 * 1024 * 1024
MESH = pl.DeviceIdType.MESH
HIGHEST = lax.Precision.HIGHEST
NT = (((1,), (1,)), ((), ()))
TN = (((0,), (0,)), ((), ()))
NN = (((1,), (0,)), ((), ()))


def _dims(d_model):
    hi = SSD_HEADS * SSD_HEAD_DIM
    gn = SSD_GROUPS * SSD_STATE
    ng = len(DIL_PATTERNS)
    dw = DIL_HEADS * HEAD_DIM
    wq = WIN_Q_HEADS * HEAD_DIM
    wk = WIN_KV_HEADS * HEAD_DIM
    d = dict(D=d_model, HI=hi, GN=gn, XBC=hi + 2 * gn, H2=2 * SSD_HEADS, NG=ng, DW=dw, WQ=wq, WK=wk,
             QKVD=3 * ng * dw, QW=3 * ng * dw + wq + 2 * wk, HG=SSD_HEADS // SSD_GROUPS)
    d["OFF_XBC"] = hi
    d["OFF_QKV"] = hi + d["XBC"]
    d["OFF_GATE"] = d["OFF_QKV"] + d["QW"]
    d["NM"] = d["OFF_GATE"] + 3 * d_model
    return d


def _pick(n, prefs):
    for p in prefs:
        if n % p == 0:
            return p
    raise ValueError(f"no tile for {n} in {prefs}")


def _params(sem):
    return pltpu.CompilerParams(dimension_semantics=sem, vmem_limit_bytes=VMEM_LIMIT)


def _sigmoid(x):
    return 1.0 / (1.0 + jnp.exp(-x))


def _mm(a, b, *, ta=False, tb=False, add=None, aux=None, epi=None, out_dtype=BF16, name):
    if ta:
        kdim, m = a.shape
    else:
        m, kdim = a.shape
    if tb:
        n, k2 = b.shape
    else:
        k2, n = b.shape
    assert kdim == k2, (a.shape, b.shape, ta, tb)
    tm = _pick(m, (1024, 512, 256, 128, 64, 32, 16, 8))
    tn = _pick(n, (1024, 512, 256, 128))
    tk = _pick(kdim, (2048, 1024, 512, 256, 128))
    nk = kdim // tk
    dims = (((0 if ta else 1,), (1 if tb else 0,)), ((), ()))
    n_in = 2 + (add is not None) + (aux is not None)
    n_out = 2 if epi == "relu2" else 1

    def body(*refs):
        a_ref, b_ref = refs[0], refs[1]
        pos = 2
        add_ref = aux_ref = None
        if add is not None:
            add_ref = refs[pos]
            pos += 1
        if aux is not None:
            aux_ref = refs[pos]
            pos += 1
        out_refs = refs[n_in:n_in + n_out]
        acc_ref = refs[n_in + n_out]
        k = pl.program_id(2)

        @pl.when(k == 0)
        def _():
            acc_ref[...] = jnp.zeros_like(acc_ref)

        acc_ref[...] += lax.dot_general(a_ref[...].astype(BF16), b_ref[...].astype(BF16), dims,
                                        preferred_element_type=F32)

        @pl.when(k == nk - 1)
        def _():
            r = acc_ref[...]
            if add_ref is not None:
                r = r + add_ref[...].astype(F32)
            if epi == "relu2":
                out_refs[0][...] = r.astype(out_refs[0].dtype)
                out_refs[1][...] = jnp.square(jnp.maximum(r, 0.0)).astype(out_refs[1].dtype)
            elif epi == "relu2_bwd":
                out_refs[0][...] = (r * 2.0 * jnp.maximum(aux_ref[...].astype(F32), 0.0)).astype(out_refs[0].dtype)
            else:
                out_refs[0][...] = r.astype(out_refs[0].dtype)

    a_spec = pl.BlockSpec((tk, tm), lambda i, j, k: (k, i)) if ta else pl.BlockSpec((tm, tk), lambda i, j, k: (i, k))
    b_spec = pl.BlockSpec((tn, tk), lambda i, j, k: (j, k)) if tb else pl.BlockSpec((tk, tn), lambda i, j, k: (k, j))
    o_spec = pl.BlockSpec((tm, tn), lambda i, j, k: (i, j))
    in_specs = [a_spec, b_spec]
    args = [a, b]
    if add is not None:
        in_specs.append(o_spec)
        args.append(add)
    if aux is not None:
        in_specs.append(o_spec)
        args.append(aux)
    out_shape = [jax.ShapeDtypeStruct((m, n), out_dtype)] * n_out
    res = pl.pallas_call(
        body, name=name, grid=(m // tm, n // tn, nk), in_specs=in_specs, out_specs=[o_spec] * n_out,
        out_shape=out_shape, scratch_shapes=[pltpu.VMEM((tm, tn), F32)],
        compiler_params=_params(("parallel", "parallel", "arbitrary")),
    )(*args)
    return res if n_out == 2 else res[0]


def _rowwise(body, rows, fulls, outs, accs=(), *, tile, name):
    rows = [r if isinstance(r, tuple) else (r, 0, r.shape[1]) for r in rows]
    nrows = rows[0][0].shape[0]
    assert nrows % tile == 0, (nrows, tile)
    in_specs, args = [], []
    for arr, off, width in rows:
        assert arr.shape[0] == nrows and off % width == 0, (arr.shape, off, width)
        in_specs.append(pl.BlockSpec((tile, width), lambda i, o=off // width: (i, o)))
        args.append(arr)
    for arr in fulls:
        in_specs.append(pl.BlockSpec(arr.shape, lambda i, nd=arr.ndim: (0,) * nd))
        args.append(arr)
    out_specs, out_shape = [], []
    for cols, dt in outs:
        out_specs.append(pl.BlockSpec((tile, cols), lambda i: (i, 0)))
        out_shape.append(jax.ShapeDtypeStruct((nrows, cols), dt))
    for shp, dt in accs:
        out_specs.append(pl.BlockSpec(shp, lambda i, nd=len(shp): (0,) * nd))
        out_shape.append(jax.ShapeDtypeStruct(shp, dt))
    n_in, n_out = len(args), len(outs)

    def wrapped(*refs):
        acc_refs = refs[n_in + n_out:]
        if acc_refs:
            @pl.when(pl.program_id(0) == 0)
            def _():
                for r in acc_refs:
                    r[...] = jnp.zeros_like(r)
        body(*refs)

    return pl.pallas_call(
        wrapped, name=name, grid=(nrows // tile,), in_specs=in_specs, out_specs=out_specs, out_shape=out_shape,
        compiler_params=_params(("arbitrary",)),
    )(*args)


def _rms_fwd(x, g, name):
    def body(x_ref, g_ref, h_ref):
        xv = x_ref[...]
        rstd = lax.rsqrt(jnp.mean(xv * xv, axis=-1, keepdims=True) + EPS)
        h_ref[...] = (xv * rstd * g_ref[...]).astype(BF16)

    return _rowwise(body, [x], [g], [(x.shape[1], BF16)], tile=256, name=name)[0]


def _rms_bwd(x, g, dh, dres, name):
    def body(x_ref, dh_ref, dres_ref, g_ref, dx_ref, dg_ref):
        xv = x_ref[...]
        dv = dh_ref[...]
        rstd = lax.rsqrt(jnp.mean(xv * xv, axis=-1, keepdims=True) + EPS)
        xn = xv * rstd
        dg_ref[...] += jnp.sum(dv * xn, axis=0, keepdims=True)
        dn = dv * g_ref[...]
        dx_ref[...] = dres_ref[...] + rstd * (dn - xn * jnp.mean(dn * xn, axis=-1, keepdims=True))

    d = x.shape[1]
    return _rowwise(body, [x, dh, dres], [g], [(d, F32)], [((1, d), F32)], tile=256, name=name)


def _loss_head(x, g, target, name):
    d = x.shape[1]

    def body(x_ref, t_ref, g_ref, dx_ref, loss_ref, dg_ref):
        xv = x_ref[...]
        rstd = lax.rsqrt(jnp.mean(xv * xv, axis=-1, keepdims=True) + EPS)
        xn = xv * rstd
        err = xn * g_ref[...] - t_ref[...]
        loss_ref[...] += jnp.full((1, LANES), 0.5 / d, F32) * jnp.sum(err * err)
        dy = err * (1.0 / d)
        dg_ref[...] += jnp.sum(dy * xn, axis=0, keepdims=True)
        dn = dy * g_ref[...]
        dx_ref[...] = rstd * (dn - xn * jnp.mean(dn * xn, axis=-1, keepdims=True))

    return _rowwise(body, [x, target], [g], [(d, F32)], [((1, LANES), F32), ((1, d), F32)], tile=256, name=name)


def _rope_tables(sign):
    half = ROPE_DIM // 2
    inv = ROPE_THETA ** (-jnp.arange(0, ROPE_DIM, 2, dtype=F32) / ROPE_DIM)
    ang = jnp.arange(SEQ, dtype=F32)[:, None] * inv[None, :]
    cos, sin = jnp.cos(ang), jnp.sin(ang) * sign
    zeros = jnp.zeros((SEQ, HEAD_DIM - ROPE_DIM), F32)
    zh = jnp.zeros((SEQ, half), F32)
    c = jnp.concatenate([cos, cos, zeros + 1.0], axis=1)
    s_up = jnp.concatenate([-sin, zh, zeros], axis=1)
    s_dn = jnp.concatenate([zh, sin, zeros], axis=1)
    return c, s_up, s_dn


def _rope(src, off, nblk, tabs, dm, name):
    t = src.shape[0]
    tq = 256
    half = ROPE_DIM // 2
    win0 = 3 * dm["NG"] * DIL_HEADS
    win1 = win0 + WIN_Q_HEADS + WIN_KV_HEADS
    qw = nblk * HEAD_DIM
    assert nblk == dm["QW"] // HEAD_DIM
    wb = next(c for c in (1024, 768, 512, 384, 256, 128) if off % c == 0 and qw % c == 0)
    reps = wb // HEAD_DIM
    sb = SEQ // tq
    flag = (jnp.arange(qw, dtype=jnp.int32) // HEAD_DIM < win1).astype(F32)[None, :]

    def body(x_ref, c_ref, up_ref, dn_ref, f_ref, o_ref):
        xv = x_ref[...].astype(F32)

        def wide(r):
            v = r[...]
            return v if reps == 1 else jnp.concatenate([v] * reps, axis=1)

        rot = xv * wide(c_ref) + pltpu.roll(xv, wb - half, 1) * wide(up_ref) + pltpu.roll(xv, half, 1) * wide(dn_ref)
        o_ref[...] = jnp.where(f_ref[...] > 0.5, rot, xv).astype(BF16)

    tab_spec = pl.BlockSpec((tq, HEAD_DIM), lambda i, j: (i % sb, 0))
    return pl.pallas_call(
        body, name=name, grid=(t // tq, qw // wb),
        in_specs=[pl.BlockSpec((tq, wb), lambda i, j, o=off // wb: (i, o + j)), tab_spec, tab_spec, tab_spec,
                  pl.BlockSpec((1, wb), lambda i, j: (0, j))],
        out_specs=pl.BlockSpec((tq, wb), lambda i, j: (i, j)),
        out_shape=jax.ShapeDtypeStruct((t, qw), BF16),
        compiler_params=_params(("parallel", "parallel")),
    )(src, *tabs, flag)


def _band_mask(rows_start, cols_start, nrows, ncols, w, n, rows_are_q):
    r = rows_start + lax.broadcasted_iota(jnp.int32, (nrows, ncols), 0)
    c = cols_start + lax.broadcasted_iota(jnp.int32, (nrows, ncols), 1)
    del rows_are_q
    return (jnp.abs(r - c) <= w) & (c >= 0) & (c < n)


def _nbr_specs(make, nb):
    if nb == 1:
        return [make(lambda i: i)]
    return [make(lambda i: jnp.maximum(i - 1, 0)), make(lambda i: i), make(lambda i: jnp.minimum(i + 1, nb - 1))]


def _cat(refs, axis):
    vals = [r[...] for r in refs]
    return vals[0] if len(vals) == 1 else jnp.concatenate(vals, axis=axis)


def _head(ref, h):
    return ref[:, h * HEAD_DIM:(h + 1) * HEAD_DIM]


def _head_cat(refs, h, axis=0):
    vals = [_head(r, h) for r in refs]
    return vals[0] if len(vals) == 1 else jnp.concatenate(vals, axis=axis)


def _attn_fwd(qa, ka, va, qb, kb, vb, hq, rep, n, w, sink, out_dtype, name):
    bb = qa.shape[0]
    blk = ATT_BLK
    nb = n // blk
    nk = 1 if nb == 1 else 3
    hkv = hq // rep
    scale = HEAD_DIM ** -0.5
    has_sink = sink is not None

    def body(*refs):
        q_ref = refs[0]
        k_refs = refs[1:1 + nk]
        v_refs = refs[1 + nk:1 + 2 * nk]
        pos = 1 + 2 * nk
        sink_ref = refs[pos] if has_sink else None
        o_ref, lse_ref = refs[pos + has_sink], refs[pos + has_sink + 1]
        i = pl.program_id(1)
        k0 = (i - 1) * blk if nk == 3 else i * blk
        valid = _band_mask(i * blk, k0, blk, nk * blk, w, n, True)
        for g in range(hkv):
            kcat = _head_cat(k_refs, g)
            vcat = _head_cat(v_refs, g)
            for r in range(rep):
                h = g * rep + r
                s = lax.dot_general(_head(q_ref, h), kcat, NT, preferred_element_type=F32) * scale
                s = jnp.where(valid, s, NEG_BIG)
                m = jnp.max(s, axis=1, keepdims=True)
                if has_sink:
                    m = jnp.maximum(m, sink_ref[h])
                p = jnp.exp(s - m)
                l = jnp.sum(p, axis=1, keepdims=True)
                if has_sink:
                    l = l + jnp.exp(sink_ref[h] - m)
                o = lax.dot_general(p.astype(BF16), vcat, NN, preferred_element_type=F32) / l
                o_ref[:, h * HEAD_DIM:(h + 1) * HEAD_DIM] = o.astype(o_ref.dtype)
                lse_ref[h] = m + jnp.log(l)

    def mk(col, width):
        return lambda f: pl.BlockSpec((None, blk, width), lambda b, i, f=f: (b, f(i), col))

    qw, kw = hq * HEAD_DIM, hkv * HEAD_DIM
    in_specs = [pl.BlockSpec((None, blk, qw), lambda b, i: (b, i, qb))]
    in_specs += _nbr_specs(mk(kb, kw), nb) + _nbr_specs(mk(vb, kw), nb)
    args = [qa] + [ka] * nk + [va] * nk
    if has_sink:
        in_specs.append(pl.BlockSpec((hq, 1, 1), lambda b, i: (0, 0, 0)))
        args.append(sink)
    return pl.pallas_call(
        body, name=name, grid=(bb, nb), in_specs=in_specs,
        out_specs=[pl.BlockSpec((None, blk, qw), lambda b, i: (b, i, 0)),
                   pl.BlockSpec((None, hq, blk, 1), lambda b, i: (b, 0, i, 0))],
        out_shape=[jax.ShapeDtypeStruct((bb, n, qw), out_dtype), jax.ShapeDtypeStruct((bb, hq, n, 1), F32)],
        compiler_params=_params(("parallel", "parallel")),
    )(*args)


def _attn_dq(qa, ka, va, do, lse, delta, qb, kb, vb, hq, rep, n, w, name):
    bb = qa.shape[0]
    blk = ATT_BLK
    nb = n // blk
    nk = 1 if nb == 1 else 3
    hkv = hq // rep
    scale = HEAD_DIM ** -0.5

    def body(*refs):
        q_ref = refs[0]
        k_refs = refs[1:1 + nk]
        v_refs = refs[1 + nk:1 + 2 * nk]
        do_ref, lse_ref, dl_ref, dq_ref = refs[1 + 2 * nk:]
        i = pl.program_id(1)
        k0 = (i - 1) * blk if nk == 3 else i * blk
        valid = _band_mask(i * blk, k0, blk, nk * blk, w, n, True)
        for g in range(hkv):
            kcat = _head_cat(k_refs, g)
            vcat = _head_cat(v_refs, g)
            for r in range(rep):
                h = g * rep + r
                s = lax.dot_general(_head(q_ref, h), kcat, NT, preferred_element_type=F32) * scale
                p = jnp.exp(jnp.where(valid, s, NEG_BIG) - lse_ref[h])
                dp = lax.dot_general(_head(do_ref, h).astype(BF16), vcat, NT, preferred_element_type=F32)
                ds = p * (dp - dl_ref[h])
                dq = lax.dot_general(ds.astype(BF16), kcat, NN, preferred_element_type=F32) * scale
                dq_ref[:, h * HEAD_DIM:(h + 1) * HEAD_DIM] = dq.astype(BF16)

    def mk(col, width):
        return lambda f: pl.BlockSpec((None, blk, width), lambda b, i, f=f: (b, f(i), col))

    qw, kw = hq * HEAD_DIM, hkv * HEAD_DIM
    col_spec = pl.BlockSpec((None, hq, blk, 1), lambda b, i: (b, 0, i, 0))
    in_specs = [pl.BlockSpec((None, blk, qw), lambda b, i: (b, i, qb))]
    in_specs += _nbr_specs(mk(kb, kw), nb) + _nbr_specs(mk(vb, kw), nb)
    in_specs += [pl.BlockSpec((None, blk, qw), lambda b, i: (b, i, 0)), col_spec, col_spec]
    return pl.pallas_call(
        body, name=name, grid=(bb, nb), in_specs=in_specs,
        out_specs=pl.BlockSpec((None, blk, qw), lambda b, i: (b, i, 0)),
        out_shape=jax.ShapeDtypeStruct((bb, n, qw), BF16),
        compiler_params=_params(("parallel", "parallel")),
    )(qa, *([ka] * nk), *([va] * nk), do, lse, delta)


def _attn_dkv(qa, ka, va, do, lse_row, delta_row, qb, kb, vb, hq, rep, n, w, name):
    bb = qa.shape[0]
    blk = ATT_BLK
    nb = n // blk
    nq = 1 if nb == 1 else 3
    hkv = hq // rep
    scale = HEAD_DIM ** -0.5

    def body(*refs):
        k_ref, v_ref = refs[0], refs[1]
        q_refs = refs[2:2 + nq]
        do_refs = refs[2 + nq:2 + 2 * nq]
        lse_refs = refs[2 + 2 * nq:2 + 3 * nq]
        dl_refs = refs[2 + 3 * nq:2 + 4 * nq]
        dk_ref, dv_ref = refs[2 + 4 * nq:]
        j = pl.program_id(1)
        q0 = (j - 1) * blk if nq == 3 else j * blk
        valid = _band_mask(j * blk, q0, blk, nq * blk, w, n, False)
        for g in range(hkv):
            kg, vg = _head(k_ref, g), _head(v_ref, g)
            dk = jnp.zeros((blk, HEAD_DIM), F32)
            dv = jnp.zeros((blk, HEAD_DIM), F32)
            for r in range(rep):
                h = g * rep + r
                qcat = _head_cat(q_refs, h)
                docat = _head_cat(do_refs, h).astype(BF16)
                lse = lse_refs[0][h] if nq == 1 else jnp.concatenate([lr[h] for lr in lse_refs], axis=1)
                dl = dl_refs[0][h] if nq == 1 else jnp.concatenate([dr[h] for dr in dl_refs], axis=1)
                st = lax.dot_general(kg, qcat, NT, preferred_element_type=F32) * scale
                pt = jnp.exp(jnp.where(valid, st, NEG_BIG) - lse)
                dv = dv + lax.dot_general(pt.astype(BF16), docat, NN, preferred_element_type=F32)
                dpt = lax.dot_general(vg, docat, NT, preferred_element_type=F32)
                dst = pt * (dpt - dl)
                dk = dk + lax.dot_general(dst.astype(BF16), qcat, NN, preferred_element_type=F32) * scale
            dk_ref[:, g * HEAD_DIM:(g + 1) * HEAD_DIM] = dk.astype(BF16)
            dv_ref[:, g * HEAD_DIM:(g + 1) * HEAD_DIM] = dv.astype(BF16)

    qw, kw = hq * HEAD_DIM, hkv * HEAD_DIM

    def mkq(col):
        return lambda f: pl.BlockSpec((None, blk, qw), lambda b, j, f=f: (b, f(j), col))

    def mkrow(f):
        return pl.BlockSpec((None, hq, 1, blk), lambda b, j, f=f: (b, 0, 0, f(j)))

    in_specs = [pl.BlockSpec((None, blk, kw), lambda b, j: (b, j, kb)), pl.BlockSpec((None, blk, kw), lambda b, j: (b, j, vb))]
    in_specs += _nbr_specs(mkq(qb), nb) + _nbr_specs(mkq(0), nb) + _nbr_specs(mkrow, nb) + _nbr_specs(mkrow, nb)
    o_spec = pl.BlockSpec((None, blk, kw), lambda b, j: (b, j, 0))
    return pl.pallas_call(
        body, name=name, grid=(bb, nb), in_specs=in_specs, out_specs=[o_spec, o_spec],
        out_shape=[jax.ShapeDtypeStruct((bb, n, kw), BF16)] * 2,
        compiler_params=_params(("parallel", "parallel")),
    )(ka, va, *([qa] * nq), *([do] * nq), *([lse_row] * nq), *([delta_row] * nq))


def _head_expand(v, nh, width):
    lane_head = lax.broadcasted_iota(jnp.int32, (1, nh * width), 1) >> int(math.log2(width))
    out = jnp.zeros((v.shape[0], nh * width), F32)
    for j in range(nh):
        out = jnp.where(lane_head == j, v[:, j:j + 1], out)
    return out


def _head_sums(m, nh, width):
    lane_head = lax.broadcasted_iota(jnp.int32, (1, nh * width), 1) >> int(math.log2(width))
    col = lax.broadcasted_iota(jnp.int32, (1, nh), 1)
    out = jnp.zeros((m.shape[0], nh), F32)
    for j in range(nh):
        sj = jnp.sum(jnp.where(lane_head == j, m, 0.0), axis=1, keepdims=True)
        out = jnp.where(col == j, sj, out)
    return out


def _head_dots(a, b, nh, name):
    def body(a_ref, b_ref, o_ref):
        o_ref[...] = _head_sums(a_ref[...].astype(F32) * b_ref[...].astype(F32), nh, HEAD_DIM)

    return _rowwise(body, [a, b], [], [(nh, F32)], tile=256, name=name)[0]


def _dil_combine(outs, lses, name):
    ng = len(outs)

    def body(*refs):
        o_refs, l_refs = refs[:ng], refs[ng:2 * ng]
        y_ref, lt_ref = refs[2 * ng], refs[2 * ng + 1]
        ls = [r[...] for r in l_refs]
        m = functools.reduce(jnp.maximum, ls)
        es = [jnp.exp(v - m) for v in ls]
        tot = functools.reduce(jnp.add, es)
        acc = jnp.zeros(o_refs[0].shape, F32)
        for o_ref, e in zip(o_refs, es):
            acc = acc + _head_expand(e / tot, DIL_HEADS, HEAD_DIM) * o_ref[...]
        y_ref[...] = acc.astype(BF16)
        lt_ref[...] = m + jnp.log(tot)

    dw = outs[0].shape[1]
    return _rowwise(body, list(outs) + list(lses), [], [(dw, BF16), (DIL_HEADS, F32)], tile=256, name=name)


def _sink_grad(lse, delta, sink, name):
    def body(l_ref, d_ref, s_ref, o_ref):
        o_ref[...] -= jnp.sum(jnp.exp(s_ref[...] - l_ref[...]) * d_ref[...], axis=0, keepdims=True)

    return _rowwise(body, [lse, delta], [sink], [], [((1, lse.shape[1]), F32)], tile=512, name=name)[0]


def _shift_rows(x, d, nrows):
    if d == 0:
        return x
    rolled = pltpu.roll(x, (-d) % nrows, 0)
    row = lax.broadcasted_iota(jnp.int32, x.shape, 0)
    ok = (row + d >= 0) & (row + d < nrows)
    return jnp.where(ok, rolled, 0.0)


def _conv_fwd(proj, off, conv_w, conv_b, xbc, name):
    t = proj.shape[0]
    tc = _pick(xbc, (256, 128))
    assert off % tc == 0
    pad = (CONV_WIDTH - 1) // 2

    def body(x_ref, w_ref, b_ref, c_ref, u_ref):
        xv = x_ref[...].astype(F32)
        acc = jnp.zeros_like(xv) + b_ref[...]
        for k in range(CONV_WIDTH):
            acc = acc + w_ref[k:k + 1, :] * _shift_rows(xv, k - pad, SEQ)
        c_ref[...] = acc.astype(BF16)
        u_ref[...] = (acc * _sigmoid(acc)).astype(BF16)

    o_spec = pl.BlockSpec((SEQ, tc), lambda b, j: (b, j))
    return pl.pallas_call(
        body, name=name, grid=(t // SEQ, xbc // tc),
        in_specs=[pl.BlockSpec((SEQ, tc), lambda b, j, o=off // tc: (b, o + j)),
                  pl.BlockSpec((CONV_WIDTH, tc), lambda b, j: (0, j)), pl.BlockSpec((1, tc), lambda b, j: (0, j))],
        out_specs=[o_spec, o_spec], out_shape=[jax.ShapeDtypeStruct((t, xbc), BF16)] * 2,
        compiler_params=_params(("parallel", "parallel")),
    )(proj, conv_w, conv_b)


def _conv_bwd(du, cpre, proj, off, conv_w, name):
    t, xbc = du.shape
    tc = _pick(xbc, (256, 128))
    pad = (CONV_WIDTH - 1) // 2

    def body(du_ref, c_ref, x_ref, w_ref, dx_ref, dw_ref, db_ref):
        @pl.when(pl.program_id(1) == 0)
        def _():
            dw_ref[...] = jnp.zeros_like(dw_ref)
            db_ref[...] = jnp.zeros_like(db_ref)

        cv = c_ref[...].astype(F32)
        sg = _sigmoid(cv)
        dc = du_ref[...] * (sg * (1.0 + cv * (1.0 - sg)))
        xv = x_ref[...].astype(F32)
        dx = jnp.zeros_like(dc)
        for k in range(CONV_WIDTH):
            dx = dx + w_ref[k:k + 1, :] * _shift_rows(dc, pad - k, SEQ)
            dw_ref[k:k + 1, :] += jnp.sum(dc * _shift_rows(xv, k - pad, SEQ), axis=0, keepdims=True)
        db_ref[...] += jnp.sum(dc, axis=0, keepdims=True)
        dx_ref[...] = dx.astype(BF16)

    blk = pl.BlockSpec((SEQ, tc), lambda j, b: (b, j))
    return pl.pallas_call(
        body, name=name, grid=(xbc // tc, t // SEQ),
        in_specs=[blk, blk, pl.BlockSpec((SEQ, tc), lambda j, b, o=off // tc: (b, o + j)),
                  pl.BlockSpec((CONV_WIDTH, tc), lambda j, b: (0, j))],
        out_specs=[blk, pl.BlockSpec((CONV_WIDTH, tc), lambda j, b: (0, j)), pl.BlockSpec((1, tc), lambda j, b: (0, j))],
        out_shape=[jax.ShapeDtypeStruct((t, xbc), BF16), jax.ShapeDtypeStruct((CONV_WIDTH, xbc), F32),
                   jax.ShapeDtypeStruct((1, xbc), F32)],
        compiler_params=_params(("parallel", "arbitrary")),
    )(du, cpre, proj, conv_w)


def _dt_prep(dtr, bias, name):
    def body(r_ref, b_ref, o_ref):
        v = r_ref[...] + b_ref[...]
        o_ref[...] = jnp.maximum(v, 0.0) + jnp.log1p(jnp.exp(-jnp.abs(v)))

    return _rowwise(body, [dtr], [bias], [(dtr.shape[1], F32)], tile=512, name=name)[0]


def _scan_prelude(d, dt_ref, dtt_ref, al_ref, alt_ref, hg):
    p = SSD_HEAD_DIM
    ch = SSD_CHUNK
    a_row = -jnp.exp(al_ref[...])
    a_col = -jnp.exp(alt_ref[...])
    dtc = dt_ref[...]
    dt_x = _head_expand(dtc, hg, p)
    dta_x = dt_x * _head_expand(a_row, hg, p)
    dta_t = dtt_ref[...] * a_col
    ri = lax.broadcasted_iota(jnp.int32, (ch, ch), 0)
    ci = lax.broadcasted_iota(jnp.int32, (ch, ch), 1)
    mask = (ci <= ri) if d == 0 else (ci >= ri)
    mask_t = (ci >= ri) if d == 0 else (ci <= ri)
    tri = mask.astype(F32)
    phi_x = jnp.dot(tri, dta_x, preferred_element_type=F32, precision=HIGHEST)
    phi_r = lax.dot_general(dta_t, tri, NT, preferred_element_type=F32, precision=HIGHEST)
    tot_x = jnp.sum(dta_x, axis=0, keepdims=True)
    return dtc, dt_x, phi_x, phi_r, tot_x, mask, mask_t


def _scan_specs(d, nc, hg, dm):
    p, n, ch = SSD_HEAD_DIM, SSD_STATE, SSD_CHUNK
    w = hg * p
    b0 = dm["HI"] // n
    c0 = (dm["HI"] + dm["GN"]) // n

    def row(b, c):
        return b * nc + c

    return [
        pl.BlockSpec((ch, w), lambda b, g, c: (row(b, c), g)),
        pl.BlockSpec((ch, n), lambda b, g, c: (row(b, c), b0 + g)),
        pl.BlockSpec((ch, n), lambda b, g, c: (row(b, c), c0 + g)),
        pl.BlockSpec((None, None, ch, hg), lambda b, g, c: (d, g, row(b, c), 0)),
        pl.BlockSpec((None, None, 8, ch), lambda b, g, c: (d, g, 0, row(b, c))),
        pl.BlockSpec((None, None, 1, hg), lambda b, g, c: (d, g, 0, 0)),
        pl.BlockSpec((None, None, 8, 1), lambda b, g, c: (d, g, 0, 0)),
    ]


def _remap(spec, f):
    return pl.BlockSpec(spec.block_shape, lambda b, g, c, im=spec.index_map: im(b, g, f(c)))


def _scan_fwd(u, dtg, dttg, alg, altg, d, dm, name):
    t = u.shape[0]
    p, n, ch, hg = SSD_HEAD_DIM, SSD_STATE, SSD_CHUNK, dm["HG"]
    w = hg * p
    nc = SEQ // ch
    order = (lambda c: c) if d == 0 else (lambda c: nc - 1 - c)

    def body(x_ref, b_ref, c_ref, dt_ref, dtt_ref, al_ref, alt_ref, y_ref, sin_ref, s_ref):
        @pl.when(pl.program_id(2) == 0)
        def _():
            s_ref[...] = jnp.zeros_like(s_ref)

        dtc, dt_x, phi_x, phi_r, tot_x, mask, _ = _scan_prelude(d, dt_ref, dtt_ref, al_ref, alt_ref, hg)
        lane_head = lax.broadcasted_iota(jnp.int32, (1, w), 1) >> int(math.log2(p))
        cm, bm = c_ref[...], b_ref[...]
        cb = lax.dot_general(cm, bm, NT, preferred_element_type=F32)
        xdt = x_ref[...].astype(F32) * dt_x
        xdt_b = xdt.astype(BF16)
        ydiag = jnp.zeros((ch, w), F32)
        for j in range(hg):
            seg = phi_x[:, j * p:j * p + 1] - phi_r[j:j + 1, :]
            mj = (cb * jnp.exp(jnp.where(mask, seg, NEG_BIG))).astype(BF16)
            ydiag = ydiag + jnp.dot(mj, jnp.where(lane_head == j, xdt_b, jnp.zeros_like(xdt_b)), preferred_element_type=F32)
        s = s_ref[...]
        y_ref[...] = ydiag + jnp.dot(cm, s.astype(BF16), preferred_element_type=F32) * jnp.exp(phi_x)
        sin_ref[...] = s
        wm = (xdt * jnp.exp(tot_x - phi_x)).astype(BF16)
        s_ref[...] = s * jnp.exp(tot_x) + lax.dot_general(bm, wm, TN, preferred_element_type=F32)

    specs = [_remap(s, order) for s in _scan_specs(d, nc, hg, dm)]
    return pl.pallas_call(
        body, name=name, grid=(t // SEQ, SSD_GROUPS, nc), in_specs=specs,
        out_specs=[_remap(pl.BlockSpec((ch, w), lambda b, g, c: (b * nc + c, g)), order),
                   _remap(pl.BlockSpec((None, None, n, w), lambda b, g, c: (b * nc + c, g, 0, 0)), order)],
        out_shape=[jax.ShapeDtypeStruct((t, dm["HI"]), F32), jax.ShapeDtypeStruct((t // ch, SSD_GROUPS, n, w), F32)],
        scratch_shapes=[pltpu.VMEM((n, w), F32)],
        compiler_params=_params(("parallel", "parallel", "arbitrary")),
    )(u, u, u, dtg, dttg, alg, altg)


def _scan_bwd(u, dtg, dttg, alg, altg, dy, sin, adds, d, dm, name):
    t = u.shape[0]
    p, n, ch, hg = SSD_HEAD_DIM, SSD_STATE, SSD_CHUNK, dm["HG"]
    w = hg * p
    nc = SEQ // ch
    order = (lambda c: nc - 1 - c) if d == 0 else (lambda c: c)
    has_bc_add = adds[1] is not None

    def body(*refs):
        x_ref, b_ref, c_ref, dt_ref, dtt_ref, al_ref, alt_ref, dy_ref, sin_ref, ax_ref = refs[:10]
        pos = 10
        ab_ref = ac_ref = None
        if has_bc_add:
            ab_ref, ac_ref = refs[10], refs[11]
            pos = 12
        dxs_ref, db_ref, dc_ref, rq_ref, xdx_ref, ds_ref = refs[pos:]

        @pl.when(pl.program_id(2) == 0)
        def _():
            ds_ref[...] = jnp.zeros_like(ds_ref)

        dtc, dt_x, phi_x, phi_r, tot_x, mask, mask_t = _scan_prelude(d, dt_ref, dtt_ref, al_ref, alt_ref, hg)
        lane_head = lax.broadcasted_iota(jnp.int32, (1, w), 1) >> int(math.log2(p))
        cm, bm = c_ref[...], b_ref[...]
        cb = lax.dot_general(cm, bm, NT, preferred_element_type=F32)
        cb_t = lax.dot_general(bm, cm, NT, preferred_element_type=F32)
        xs = x_ref[...].astype(F32)
        xdt = xs * dt_x
        xdt_b = xdt.astype(BF16)
        dy = dy_ref[...]
        dy_b = dy.astype(BF16)
        zero_b = jnp.zeros_like(dy_b)
        col = lax.broadcasted_iota(jnp.int32, (1, hg), 1)
        dxp = jnp.zeros((ch, w), F32)
        a_ls = jnp.zeros((ch, ch), F32)
        a_sl = jnp.zeros((ch, ch), F32)
        dphi = jnp.zeros((ch, hg), F32)
        for j in range(hg):
            pc = phi_x[:, j * p:j * p + 1]
            pr = phi_r[j:j + 1, :]
            l_ls = jnp.exp(jnp.where(mask, pc - pr, NEG_BIG))
            l_sl = jnp.exp(jnp.where(mask_t, pr - pc, NEG_BIG))
            dy_j = jnp.where(lane_head == j, dy_b, zero_b)
            xdt_j = jnp.where(lane_head == j, xdt_b, zero_b)
            dxp = dxp + jnp.dot((cb_t * l_sl).astype(BF16), dy_j, preferred_element_type=F32)
            g_ls = l_ls * lax.dot_general(dy_j, xdt_b, NT, preferred_element_type=F32)
            g_sl = l_sl * lax.dot_general(xdt_j, dy_b, NT, preferred_element_type=F32)
            a_ls = a_ls + g_ls
            a_sl = a_sl + g_sl
            pair = jnp.sum(g_ls * cb, axis=1, keepdims=True) - jnp.sum(g_sl * cb_t, axis=1, keepdims=True)
            dphi = jnp.where(col == j, pair, dphi)
        ds = ds_ref[...]
        ds_b = ds.astype(BF16)
        sin = sin_ref[...]
        sin_b = sin.astype(BF16)
        e_tp = jnp.exp(tot_x - phi_x)
        e_p = jnp.exp(phi_x)
        dxp_off = e_tp * jnp.dot(bm, ds_b, preferred_element_type=F32)
        dxp = dxp + dxp_off
        dxs_ref[...] = ax_ref[...] + dxp * dt_x
        xdx_ref[...] = _head_sums(xs * dxp, hg, p)
        y_off = jnp.dot(cm, sin_b, preferred_element_type=F32) * e_p
        st_t = _head_sums(xdt * dxp_off, hg, p)
        dphi = dphi + _head_sums(dy * y_off, hg, p) - st_t
        dtot = _head_sums(jnp.sum(ds * sin, axis=0, keepdims=True) * jnp.exp(tot_x), hg, p) + jnp.sum(st_t, axis=0, keepdims=True)
        cum = jnp.dot(mask_t.astype(F32), _head_expand(dphi, hg, p), preferred_element_type=F32, precision=HIGHEST)
        ddta = jnp.zeros((ch, hg), F32)
        for j in range(hg):
            ddta = jnp.where(col == j, cum[:, j * p:j * p + 1], ddta)
        rq_ref[...] = ddta + dtot
        dye = (dy * e_p).astype(BF16)
        dcv = jnp.dot(a_ls.astype(BF16), bm, preferred_element_type=F32)
        dcv = dcv + lax.dot_general(dye, sin_b, NT, preferred_element_type=F32)
        dbv = jnp.dot(a_sl.astype(BF16), cm, preferred_element_type=F32)
        dbv = dbv + lax.dot_general((xdt * e_tp).astype(BF16), ds_b, NT, preferred_element_type=F32)
        if has_bc_add:
            dcv = dcv + ac_ref[...]
            dbv = dbv + ab_ref[...]
        dc_ref[...] = dcv
        db_ref[...] = dbv
        ds_ref[...] = ds * jnp.exp(tot_x) + lax.dot_general(cm, dye, TN, preferred_element_type=F32)

    def sp(spec):
        return _remap(spec, order)

    xw = pl.BlockSpec((ch, w), lambda b, g, c: (b * nc + c, g))
    gn_blk = pl.BlockSpec((ch, n), lambda b, g, c: (b * nc + c, g))
    small = pl.BlockSpec((None, ch, hg), lambda b, g, c: (g, b * nc + c, 0))
    in_specs = [sp(s) for s in _scan_specs(d, nc, hg, dm)]
    in_specs += [sp(xw), sp(pl.BlockSpec((None, None, n, w), lambda b, g, c: (b * nc + c, g, 0, 0))), sp(xw)]
    args = [u, u, u, dtg, dttg, alg, altg, dy, sin, adds[0]]
    if has_bc_add:
        in_specs += [sp(gn_blk), sp(gn_blk)]
        args += [adds[1], adds[2]]
    return pl.pallas_call(
        body, name=name, grid=(t // SEQ, SSD_GROUPS, nc), in_specs=in_specs,
        out_specs=[sp(xw), sp(gn_blk), sp(gn_blk), sp(small), sp(small)],
        out_shape=[jax.ShapeDtypeStruct((t, dm["HI"]), F32), jax.ShapeDtypeStruct((t, dm["GN"]), F32),
                   jax.ShapeDtypeStruct((t, dm["GN"]), F32), jax.ShapeDtypeStruct((SSD_GROUPS, t, hg), F32),
                   jax.ShapeDtypeStruct((SSD_GROUPS, t, hg), F32)],
        scratch_shapes=[pltpu.VMEM((n, w), F32)],
        compiler_params=_params(("parallel", "parallel", "arbitrary")),
    )(*args)


def _ssd_param_bwd(rq_f, rq_r, xdx, dtp, dtr, bias, alog, name):
    def body(rf_ref, rr_ref, xdx_ref, dt_ref, dtr_ref, b_ref, al_ref, o_ref, db_ref, da_ref):
        a = -jnp.exp(al_ref[...])
        d_dta = rf_ref[...] + rr_ref[...]
        ddt = a * d_dta + xdx_ref[...]
        ddtr = ddt * _sigmoid(dtr_ref[...] + b_ref[...])
        o_ref[...] = ddtr
        db_ref[...] += jnp.sum(ddtr, axis=0, keepdims=True)
        da_ref[...] += a * jnp.sum(dt_ref[...] * d_dta, axis=0, keepdims=True)

    return _rowwise(body, [rq_f, rq_r, xdx, dtp, dtr], [bias, alog], [(LANES, F32)],
                    [((1, LANES), F32), ((1, LANES), F32)], tile=512, name=name)


def _ssd_out_fwd(y_f, y_b, u, proj, dcols, gn, hi, name):
    def body(yf_ref, yb_ref, x_ref, z_ref, d_ref, g_ref, o_ref):
        ytot = yf_ref[...] + yb_ref[...] + d_ref[...] * x_ref[...].astype(F32)
        zv = z_ref[...].astype(F32)
        yz = ytot * (zv * _sigmoid(zv))
        rstd = lax.rsqrt(jnp.mean(yz * yz, axis=-1, keepdims=True) + EPS)
        o_ref[...] = (yz * rstd * g_ref[...]).astype(BF16)

    return _rowwise(body, [y_f, y_b, (u, 0, hi), (proj, 0, hi)], [dcols, gn], [(hi, BF16)], tile=256, name=name)[0]


def _ssd_out_bwd(dya, y_f, y_b, u, proj, dcols, gn, hi, name):
    def body(dy_ref, yf_ref, yb_ref, x_ref, z_ref, d_ref, g_ref, dyt_ref, dxs_ref, dz_ref, dg_ref, dd_ref):
        xv = x_ref[...].astype(F32)
        ytot = yf_ref[...] + yb_ref[...] + d_ref[...] * xv
        zv = z_ref[...].astype(F32)
        sg = _sigmoid(zv)
        sz = zv * sg
        yz = ytot * sz
        rstd = lax.rsqrt(jnp.mean(yz * yz, axis=-1, keepdims=True) + EPS)
        yn = yz * rstd
        dv = dy_ref[...]
        dg_ref[...] += jnp.sum(dv * yn, axis=0, keepdims=True)
        dn = dv * g_ref[...]
        dyz = rstd * (dn - yn * jnp.mean(dn * yn, axis=-1, keepdims=True))
        dyt = dyz * sz
        dyt_ref[...] = dyt
        dxs_ref[...] = dyt * d_ref[...]
        dz_ref[...] = (dyz * ytot * (sg * (1.0 + zv * (1.0 - sg)))).astype(BF16)
        dd_ref[...] += jnp.sum(dyt * xv, axis=0, keepdims=True)

    return _rowwise(body, [dya, y_f, y_b, (u, 0, hi), (proj, 0, hi)], [dcols, gn],
                    [(hi, F32), (hi, F32), (hi, BF16)], [((1, hi), F32), ((1, hi), F32)], tile=128, name=name)


def _gate_fwd(pa, pb, pc, proj, off, d, name):
    def body(a_ref, b_ref, c_ref, g0_ref, g1_ref, g2_ref, o_ref):
        acc = _sigmoid(g0_ref[...].astype(F32)) * a_ref[...]
        acc = acc + _sigmoid(g1_ref[...].astype(F32)) * b_ref[...]
        acc = acc + _sigmoid(g2_ref[...].astype(F32)) * c_ref[...]
        o_ref[...] = acc.astype(BF16)

    rows = [pa, pb, pc] + [(proj, off + k * d, d) for k in range(3)]
    return _rowwise(body, rows, [], [(d, BF16)], tile=256, name=name)[0]


def _gate_bwd(dm_, pa, pb, pc, proj, off, d, name):
    def body(dm_ref, a_ref, b_ref, c_ref, g0_ref, g1_ref, g2_ref, da_ref, db_ref, dc_ref, dg0_ref, dg1_ref, dg2_ref):
        dmv = dm_ref[...]
        for p_ref, g_ref, dp_ref, dg_ref in ((a_ref, g0_ref, da_ref, dg0_ref), (b_ref, g1_ref, db_ref, dg1_ref),
                                             (c_ref, g2_ref, dc_ref, dg2_ref)):
            sg = _sigmoid(g_ref[...].astype(F32))
            dp_ref[...] = (dmv * sg).astype(BF16)
            dg_ref[...] = (dmv * p_ref[...] * sg * (1.0 - sg)).astype(BF16)

    rows = [dm_, pa, pb, pc] + [(proj, off + k * d, d) for k in range(3)]
    return _rowwise(body, rows, [], [(d, BF16)] * 6, tile=128, name=name)


def _adamw(w, g, m, v, name):
    rows, cols = w.shape
    tile = _pick(rows, (32, 16, 8)) if rows >= 8 else rows
    c1 = 1.0 / (1.0 - ADAM_B1 ** ADAM_STEP)
    c2 = 1.0 / (1.0 - ADAM_B2 ** ADAM_STEP)

    def body(w_ref, g_ref, m_ref, v_ref, d_ref, nm_ref, nv_ref):
        gv = g_ref[...]
        nm = ADAM_B1 * m_ref[...] + (1.0 - ADAM_B1) * gv
        nv = ADAM_B2 * v_ref[...] + (1.0 - ADAM_B2) * (gv * gv)
        nm_ref[...] = nm
        nv_ref[...] = nv
        d_ref[...] = -ADAM_LR * ((nm * c1) / (jnp.sqrt(nv * c2) + ADAM_EPS) + ADAM_WD * w_ref[...])

    return _rowwise(body, [w, g, m, v], [], [(cols, F32)] * 3, tile=tile, name=name)


ANY = pl.BlockSpec(memory_space=pl.ANY)


def _place():
    x, y, c = lax.axis_index("x"), lax.axis_index("y"), lax.axis_index("c")
    chips = [(1 - x, y), (x, 1 - y), (1 - x, 1 - y)]
    return x, y, c, chips


def _gather_chips(arr, name):
    def body(src, out, ssem, rsem):
        x, y, c, chips = _place()
        k = 2 * x + y

        def copy(j, kk, layer, to, own=False):
            return pltpu.make_async_remote_copy(
                src_ref=src.at[layer] if own else out.at[kk, layer], dst_ref=out.at[kk, layer],
                send_sem=ssem.at[j], recv_sem=rsem.at[j], device_id=to, device_id_type=MESH)

        first = [copy(j, k, c, (cx, cy, c), own=True) for j, (cx, cy) in enumerate(chips)]
        for cp in first:
            cp.start()
        passed = [copy(3 + j, 2 * cx + cy, c, (x, y, 1 - c)) for j, (cx, cy) in enumerate(chips)]
        for j, (cx, cy) in enumerate(chips):
            copy(j, 2 * cx + cy, c, (x, y, c)).wait_recv()
            passed[j].start()
        for j, (cx, cy) in enumerate(chips):
            copy(3 + j, 2 * cx + cy, 1 - c, (x, y, c)).wait_recv()
        for cp in first + passed:
            cp.wait_send()

    st = pl.pallas_call(
        body, name=name, in_specs=[ANY], out_specs=ANY, out_shape=jax.ShapeDtypeStruct((4,) + arr.shape, arr.dtype),
        scratch_shapes=[pltpu.SemaphoreType.DMA((6,)), pltpu.SemaphoreType.DMA((6,))],
    )(arr)
    kchip = 2 * lax.axis_index("x") + lax.axis_index("y")
    return lax.dynamic_update_slice(st, arr[None], (kchip,) + (0,) * arr.ndim)


def _pair_swap(g, name):
    def body(src, out, ssem, rsem):
        x, y, c, _ = _place()
        cp = pltpu.make_async_remote_copy(src_ref=src.at[1 - c], dst_ref=out, send_sem=ssem, recv_sem=rsem,
                                          device_id=(x, y, 1 - c), device_id_type=MESH)
        cp.start()
        cp.wait()

    return pl.pallas_call(
        body, name=name, in_specs=[ANY], out_specs=ANY, out_shape=jax.ShapeDtypeStruct(g.shape[1:], g.dtype),
        scratch_shapes=[pltpu.SemaphoreType.DMA, pltpu.SemaphoreType.DMA],
    )(g)


def _chip_exchange(p, name):
    def body(src, out, ssem, rsem):
        x, y, c, chips = _place()
        cps = [pltpu.make_async_remote_copy(src_ref=src.at[2 * cx + cy], dst_ref=out.at[j], send_sem=ssem.at[j],
                                            recv_sem=rsem.at[j], device_id=(cx, cy, c), device_id_type=MESH)
               for j, (cx, cy) in enumerate(chips)]
        for cp in cps:
            cp.start()
        for cp in cps:
            cp.wait()

    return pl.pallas_call(
        body, name=name, in_specs=[ANY], out_specs=ANY, out_shape=jax.ShapeDtypeStruct((3,) + p.shape[1:], p.dtype),
        scratch_shapes=[pltpu.SemaphoreType.DMA((3,)), pltpu.SemaphoreType.DMA((3,))],
    )(p)


def _pair_share(r, name):
    def body(src, out, ssem, rsem):
        x, y, c, _ = _place()
        cp = pltpu.make_async_remote_copy(src_ref=src, dst_ref=out, send_sem=ssem, recv_sem=rsem,
                                          device_id=(x, y, 1 - c), device_id_type=MESH)
        cp.start()
        cp.wait()

    theirs = pl.pallas_call(
        body, name=name, in_specs=[ANY], out_specs=ANY, out_shape=jax.ShapeDtypeStruct(r.shape, r.dtype),
        scratch_shapes=[pltpu.SemaphoreType.DMA, pltpu.SemaphoreType.DMA],
    )(r)
    first = lax.axis_index("c") == 0
    return jnp.stack([jnp.where(first, r, theirs), jnp.where(first, theirs, r)])


def _sum2(a, b, out_dtype, name):
    def body(a_ref, b_ref, o_ref):
        o_ref[...] = (a_ref[...].astype(F32) + b_ref[...].astype(F32)).astype(out_dtype)

    rows, cols = a.shape
    return _rowwise(body, [a, b], [], [(cols, out_dtype)], tile=_pick(rows, (64, 32, 16, 8)), name=name)[0]


def _sum4(a, b, name):
    rows, cols = a.shape
    tile = _pick(rows, (64, 32, 16, 8))

    def body(a_ref, b0_ref, b1_ref, b2_ref, o_ref):
        acc = a_ref[...].astype(F32) + b0_ref[...].astype(F32)
        acc = acc + b1_ref[...].astype(F32)
        o_ref[...] = acc + b2_ref[...].astype(F32)

    bspec = [pl.BlockSpec((None, tile, cols), lambda i, j=j: (j, i, 0)) for j in range(3)]
    return pl.pallas_call(
        body, name=name, grid=(rows // tile,), in_specs=[pl.BlockSpec((tile, cols), lambda i: (i, 0))] + bspec,
        out_specs=pl.BlockSpec((tile, cols), lambda i: (i, 0)), out_shape=jax.ShapeDtypeStruct((rows, cols), F32),
        compiler_params=_params(("parallel",)),
    )(a, b, b, b)


def _reduce_scatter(g, name):
    _, _, rows, cols = g.shape
    c = lax.axis_index("c")
    k = 2 * lax.axis_index("x") + lax.axis_index("y")
    got = _pair_swap(g, name + "_pair")
    mine = lax.dynamic_index_in_dim(g, c, 0, keepdims=False)
    part = _sum2(mine.reshape(4 * rows, cols), got.reshape(4 * rows, cols), BF16, name + "_add2").reshape(4, rows, cols)
    others = _chip_exchange(part, name + "_chips")
    own = lax.dynamic_index_in_dim(part, k, 0, keepdims=False)
    total = _sum4(own, others, name + "_add4")
    return _pair_share(total, name + "_share")


def _all_reduce_small(buf, name):
    rows = buf.shape[0]

    def body(src, out, slots, ssem, rsem):
        x, y, c, _ = _place()
        me = 4 * x + 2 * y + c
        slots[me] = src[...]
        cps = []
        for j in range(1, 8):
            px, py, pc = x ^ (j >> 2), y ^ ((j >> 1) & 1), c ^ (j & 1)
            cps.append(pltpu.make_async_remote_copy(src_ref=src, dst_ref=slots.at[me], send_sem=ssem.at[j - 1],
                                                    recv_sem=rsem.at[j - 1], device_id=(px, py, pc), device_id_type=MESH))
        for cp in cps:
            cp.start()
        for j in range(1, 8):
            peer = me ^ j
            pltpu.make_async_remote_copy(src_ref=src, dst_ref=slots.at[peer], send_sem=ssem.at[j - 1], recv_sem=rsem.at[j - 1],
                                         device_id=(x, y, c), device_id_type=MESH).wait_recv()
        for cp in cps:
            cp.wait_send()
        acc = slots[0]
        for d in range(1, 8):
            acc = acc + slots[d]
        out[...] = acc

    vm = pl.BlockSpec(memory_space=pltpu.VMEM)
    return pl.pallas_call(
        body, name=name, in_specs=[vm], out_specs=vm, out_shape=jax.ShapeDtypeStruct((rows, LANES), F32),
        scratch_shapes=[pltpu.VMEM((8, rows, LANES), F32), pltpu.SemaphoreType.DMA((7,)), pltpu.SemaphoreType.DMA((7,))],
    )(buf)


def _pack(arrs):
    flat = jnp.concatenate([a.astype(F32).reshape(-1) for a in arrs])
    n = flat.shape[0]
    padded = -(-n // (8 * LANES)) * (8 * LANES)
    return jnp.pad(flat, (0, padded - n)).reshape(padded // LANES, LANES)


def _unpack(buf, like):
    flat = buf.reshape(-1)
    out, pos = [], 0
    for a in like:
        out.append(flat[pos:pos + a.size].reshape(a.shape))
        pos += a.size
    return out


def _stride(t2d, dil):
    t, w = t2d.shape
    b = t // SEQ
    return t2d.reshape(b, SEQ // dil, dil, w).transpose(0, 2, 1, 3).reshape(b * dil, SEQ // dil, w)


def _unstride(t3d, dil):
    bb, n, w = t3d.shape
    b = bb // dil
    return t3d.reshape(b, dil, n, w).transpose(0, 2, 1, 3).reshape(b * SEQ, w)


def _stat_cols(st, dil, heads):
    s3 = _stride(st, dil)
    return s3.transpose(0, 2, 1)[..., None]


def _stat_rows(col):
    bb, h, n, _ = col.shape
    return col.reshape(bb, h, 1, n)


def _scan_params(dtp, alog, dm):
    t = dtp.shape[0]
    g, hg = SSD_GROUPS, dm["HG"]
    dt4 = dtp[:, :dm["H2"]].reshape(t, 2, g, hg)
    dtg = dt4.transpose(1, 2, 0, 3)
    dttg = jnp.pad(dt4.transpose(1, 2, 3, 0), ((0, 0), (0, 0), (0, 8 - hg), (0, 0)))
    al = alog.reshape(2, g, 1, hg)
    alt = jnp.pad(alog.reshape(2, g, hg, 1), ((0, 0), (0, 0), (0, 8 - hg), (0, 0)))
    return dtg, dttg, al, alt


def _layer_fwd(x, wl, tabs, dm, li):
    d = dm["D"]
    nm = f"l{li}_"
    h = _rms_fwd(x, wl["g_mix"], nm + "rms1")
    proj = _mm(h, wl["w_main"], name=nm + "proj")
    dtr = _mm(h, wl["w_dt"], out_dtype=F32, name=nm + "proj_dt")
    cpre, u = _conv_fwd(proj, dm["OFF_XBC"], wl["conv_w"], wl["conv_b"], dm["XBC"], nm + "conv")
    dtp = _dt_prep(dtr, wl["dt_bias"], nm + "dt")
    sp = _scan_params(dtp, wl["a_log"], dm)
    y_f, s_f = _scan_fwd(u, *sp, 0, dm, nm + "scan_f")
    y_b, s_b = _scan_fwd(u, *sp, 1, dm, nm + "scan_b")
    y_a = _ssd_out_fwd(y_f, y_b, u, proj, wl["d_cols"], wl["ssd_norm"], dm["HI"], nm + "ssd_out")
    qkv = _rope(proj, dm["OFF_QKV"], dm["QW"] // HEAD_DIM, tabs[0], dm, nm + "rope")
    ng, dw = dm["NG"], dm["DW"]
    outs, lses, xgs = [], [], []
    for gi, (window, dil) in enumerate(DIL_PATTERNS):
        cols = [qkv[:, s * ng * dw + gi * dw:s * ng * dw + (gi + 1) * dw] for s in range(3)]
        xg = _stride(jnp.concatenate(cols, axis=1), dil)
        o, lse = _attn_fwd(xg, xg, xg, 0, 1, 2, DIL_HEADS, 1, SEQ // dil, window // (2 * dil), None, F32, nm + f"dil{gi}")
        xgs.append(xg)
        outs.append(_unstride(o, dil))
        lses.append(_unstride(lse[..., 0].transpose(0, 2, 1), dil))
    y_bm, lse_tot = _dil_combine(outs, lses, nm + "dil_mix")
    bsz = x.shape[0] // SEQ
    xw = qkv[:, dm["QKVD"]:].reshape(bsz, SEQ, dm["WQ"] + 2 * dm["WK"])
    rep = WIN_Q_HEADS // WIN_KV_HEADS
    y_c3, lse_w = _attn_fwd(xw, xw, xw, 0, rep, rep + 1, WIN_Q_HEADS, rep, SEQ, WIN_HALF,
                            wl["sink"].reshape(WIN_Q_HEADS, 1, 1), BF16, nm + "win")
    y_c = y_c3.reshape(x.shape[0], dm["WQ"])
    pa = _mm(y_a, wl["w_a"], out_dtype=F32, name=nm + "pa")
    pb = _mm(y_bm, wl["w_b"], out_dtype=F32, name=nm + "pb")
    pc = _mm(y_c, wl["w_c"], out_dtype=F32, name=nm + "pc")
    merged = _gate_fwd(pa, pb, pc, proj, dm["OFF_GATE"], d, nm + "gate")
    x1 = _mm(merged, wl["w_out"], add=x, out_dtype=F32, name=nm + "out")
    hm = _rms_fwd(x1, wl["g_mlp"], nm + "rms2")
    up, act = _mm(hm, wl["w_up"], epi="relu2", name=nm + "up")
    x2 = _mm(act, wl["w_down"], add=x1, out_dtype=F32, name=nm + "down")
    saved = dict(x=x, h=h, proj=proj, dtr=dtr, cpre=cpre, u=u, dtp=dtp, y_f=y_f, y_b=y_b, s_f=s_f, s_b=s_b, y_a=y_a,
                 xw=xw, xgs=xgs, y_bm=y_bm, lse_tot=lse_tot, y_c=y_c, lse_w=lse_w, pa=pa, pb=pb, pc=pc,
                 merged=merged, x1=x1, hm=hm, up=up, act=act)
    return x2, saved


def _layer_bwd(dx2, wl, sv, tabs, dm, li):
    d = dm["D"]
    t = dx2.shape[0]
    bsz = t // SEQ
    nm = f"l{li}b_"
    gr = {}
    dup = _mm(dx2, wl["w_down"], tb=True, aux=sv["up"], epi="relu2_bwd", name=nm + "dup")
    gr["w_down"] = _mm(sv["act"], dx2, ta=True, name=nm + "gw_down")
    dhm = _mm(dup, wl["w_up"], tb=True, out_dtype=F32, name=nm + "dhm")
    gr["w_up"] = _mm(sv["hm"], dup, ta=True, name=nm + "gw_up")
    dx1, gmlp = _rms_bwd(sv["x1"], wl["g_mlp"], dhm, dx2, nm + "rms2")
    gr["g_mlp"] = gmlp[0]
    dmerged = _mm(dx1, wl["w_out"], tb=True, out_dtype=F32, name=nm + "dmerged")
    gr["w_out"] = _mm(sv["merged"], dx1, ta=True, name=nm + "gw_out")
    dpa, dpb, dpc, dg0, dg1, dg2 = _gate_bwd(dmerged, sv["pa"], sv["pb"], sv["pc"], sv["proj"], dm["OFF_GATE"], d, nm + "gate")
    dya = _mm(dpa, wl["w_a"], tb=True, out_dtype=F32, name=nm + "dya")
    gr["w_a"] = _mm(sv["y_a"], dpa, ta=True, name=nm + "gw_a")
    dyb = _mm(dpb, wl["w_b"], tb=True, out_dtype=F32, name=nm + "dyb")
    gr["w_b"] = _mm(sv["y_bm"], dpb, ta=True, name=nm + "gw_b")
    dyc = _mm(dpc, wl["w_c"], tb=True, out_dtype=F32, name=nm + "dyc")
    gr["w_c"] = _mm(sv["y_c"], dpc, ta=True, name=nm + "gw_c")
    ng, dw = dm["NG"], dm["DW"]
    xw = sv["xw"]
    rep = WIN_Q_HEADS // WIN_KV_HEADS
    delta_w = _head_dots(dyc, sv["y_c"], WIN_Q_HEADS, nm + "win_delta")
    dl_col = _stat_cols(delta_w, 1, WIN_Q_HEADS)
    lse_w = sv["lse_w"]
    dyc3 = dyc.reshape(bsz, SEQ, dm["WQ"])
    wargs = (0, rep, rep + 1, WIN_Q_HEADS, rep, SEQ, WIN_HALF)
    dq_w = _attn_dq(xw, xw, xw, dyc3, lse_w, dl_col, *wargs, nm + "win_dq")
    dk_w, dv_w = _attn_dkv(xw, xw, xw, dyc3, _stat_rows(lse_w), _stat_rows(dl_col), *wargs, nm + "win_dkv")
    lse_w2 = lse_w[..., 0].transpose(0, 2, 1).reshape(t, WIN_Q_HEADS)
    gr["sink"] = _sink_grad(lse_w2, delta_w, wl["sink"], nm + "sink")[0]
    delta_d = _head_dots(dyb, sv["y_bm"], DIL_HEADS, nm + "dil_delta")
    dqs, dks, dvs = [], [], []
    for gi, (window, dil) in enumerate(DIL_PATTERNS):
        xg = sv["xgs"][gi]
        n = SEQ // dil
        do_g = _stride(dyb, dil)
        lse_c = _stat_cols(sv["lse_tot"], dil, DIL_HEADS)
        dl_c = _stat_cols(delta_d, dil, DIL_HEADS)
        dargs = (0, 1, 2, DIL_HEADS, 1, n, window // (2 * dil))
        dq = _attn_dq(xg, xg, xg, do_g, lse_c, dl_c, *dargs, nm + f"dil{gi}_dq")
        dk, dv = _attn_dkv(xg, xg, xg, do_g, _stat_rows(lse_c), _stat_rows(dl_c), *dargs, nm + f"dil{gi}_dkv")
        dqs.append(_unstride(dq, dil))
        dks.append(_unstride(dk, dil))
        dvs.append(_unstride(dv, dil))
    dqkv_r = jnp.concatenate(dqs + dks + dvs + [dq_w.reshape(t, dm["WQ"]), dk_w.reshape(t, dm["WK"]), dv_w.reshape(t, dm["WK"])],
                             axis=1)
    dqkv = _rope(dqkv_r, 0, dm["QW"] // HEAD_DIM, tabs[1], dm, nm + "rope")
    hi, gn = dm["HI"], dm["GN"]
    dyt, dxs0, dz, gnorm, dd_cols = _ssd_out_bwd(dya, sv["y_f"], sv["y_b"], sv["u"], sv["proj"], wl["d_cols"],
                                                          wl["ssd_norm"], hi, nm + "ssd_out")
    gr["ssd_norm"] = gnorm[0]
    gr["d_skip"] = dd_cols.reshape(SSD_HEADS, SSD_HEAD_DIM).sum(axis=1)
    sp = _scan_params(sv["dtp"], wl["a_log"], dm)
    dxs1, db1, dc1, rq_f, xdx_f = _scan_bwd(sv["u"], *sp, dyt, sv["s_f"], (dxs0, None, None), 0, dm, nm + "scan_f")
    dxs2, db2, dc2, rq_r, xdx_r = _scan_bwd(sv["u"], *sp, dyt, sv["s_b"], (dxs1, db1, dc1), 1, dm, nm + "scan_b")

    def heads(a):
        return a.transpose(1, 0, 2).reshape(t, SSD_HEADS)

    zpad = jnp.zeros((t, LANES - dm["H2"]), F32)
    zh = jnp.zeros((t, SSD_HEADS), F32)
    rqf_p = jnp.concatenate([heads(rq_f), zh, zpad], axis=1)
    rqr_p = jnp.concatenate([zh, heads(rq_r), zpad], axis=1)
    xdx_p = jnp.concatenate([heads(xdx_f), heads(xdx_r), zpad], axis=1)
    ddtr, dbias, dalog = _ssd_param_bwd(rqf_p, rqr_p, xdx_p, sv["dtp"], sv["dtr"], wl["dt_bias"], wl["a_log_p"], nm + "ssd_par")
    gr["dt_bias"] = dbias[0, :dm["H2"]].reshape(2, SSD_HEADS)
    gr["a_log"] = dalog[0, :dm["H2"]].reshape(2, SSD_HEADS)
    du = jnp.concatenate([dxs2, db2, dc2], axis=1)
    dxbc, gr["conv_w"], gcb = _conv_bwd(du, sv["cpre"], sv["proj"], dm["OFF_XBC"], wl["conv_w"], nm + "conv")
    gr["conv_b"] = gcb[0]
    dproj = jnp.concatenate([dz, dxbc, dqkv, dg0, dg1, dg2], axis=1)
    dh_dt = _mm(ddtr, wl["w_dt"], tb=True, out_dtype=F32, name=nm + "dh_dt")
    dh = _mm(dproj, wl["w_main"], tb=True, add=dh_dt, out_dtype=F32, name=nm + "dh")
    gw_main = _mm(sv["h"], dproj, ta=True, name=nm + "gw_main")
    gw_dt = _mm(sv["h"], ddtr, ta=True, name=nm + "gw_dt")
    o1 = dm["OFF_QKV"]
    gr["w_in"] = jnp.concatenate([gw_main[:, :o1], gw_dt[:, :dm["H2"]], gw_main[:, o1:]], axis=1)
    dx, gmix = _rms_bwd(sv["x"], wl["g_mix"], dh, dx1, nm + "rms1")
    gr["g_mix"] = gmix[0]
    return dx, gr


def _layer_weights(full, li, dm):
    w_in = full["w_in"][li]
    o1 = dm["OFF_QKV"]
    h2 = dm["H2"]
    d = dm["D"]
    wl = dict(
        w_main=jnp.concatenate([w_in[:, :o1], w_in[:, o1 + h2:]], axis=1),
        w_dt=jnp.pad(w_in[:, o1:o1 + h2], ((0, 0), (0, LANES - h2))),
        w_a=full["w_a"][li], w_b=full["w_b"][li], w_c=full["w_c"][li], w_out=full["w_out"][li],
        w_up=full["w_up"][li], w_down=full["w_down"][li],
        conv_w=full["conv_w"][li], conv_b=full["conv_b"][li][None, :],
        g_mix=full["g_mix"][li][None, :], g_mlp=full["g_mlp"][li][None, :], ssd_norm=full["ssd_norm"][li][None, :],
        d_cols=jnp.repeat(full["d_skip"][li], SSD_HEAD_DIM)[None, :],
        sink=full["sink"][li][None, :],
        a_log=full["a_log"][li],
        a_log_p=jnp.pad(full["a_log"][li].reshape(1, h2), ((0, 0), (0, LANES - h2))),
        dt_bias=jnp.pad(full["dt_bias"][li].reshape(1, h2), ((0, 0), (0, LANES - h2))),
    )
    assert wl["w_main"].shape == (d, dm["NM"])
    return wl


def _local_step(x, target, full, depth):
    bsz, seq, d = x.shape
    assert seq == SEQ
    dm = _dims(d)
    assert dm["OFF_GATE"] % d == 0 and dm["HI"] % (dm["HG"] * SSD_HEAD_DIM) == 0 and dm["H2"] <= LANES
    tabs = (_rope_tables(1.0), _rope_tables(-1.0))
    xt = x.reshape(bsz * seq, d)
    wls, saves = [], []
    for li in range(depth):
        wl = _layer_weights(full, li, dm)
        xt, sv = _layer_fwd(xt, wl, tabs, dm, li)
        wls.append(wl)
        saves.append(sv)
    dx, loss, g_final = _loss_head(xt, full["g_final"][None, :], target.reshape(bsz * seq, d), "loss_head")
    grads = [None] * depth
    for li in reversed(range(depth)):
        dx, grads[li] = _layer_bwd(dx, wls[li], saves[li], tabs, dm, li)
    return loss, dx.reshape(bsz, seq, d), grads, g_final[0]


BIG = ("w_in", "w_a", "w_b", "w_c", "w_out", "w_up", "w_down")
COL_SHARDED = ("w_in", "w_b", "w_up")
SMALL = ("g_mix", "conv_w", "conv_b", "dt_bias", "a_log", "d_skip", "ssd_norm", "sink", "g_mlp", "g_final")
ORDER = ("g_mix", "w_in", "conv_w", "conv_b", "dt_bias", "a_log", "d_skip", "ssd_norm", "w_a", "w_b", "w_c", "sink",
         "w_out", "g_mlp", "w_up", "w_down", "g_final")


def _unstack(name, st):
    _, nl, r, c = st.shape
    if name in COL_SHARDED:
        return jnp.moveaxis(st, 0, 2).reshape(nl, r, 4 * c)
    return jnp.moveaxis(st, 0, 1).reshape(nl, 4 * r, c)


def _restack(name, gfull):
    nl, r, c = gfull.shape
    if name in COL_SHARDED:
        return jnp.moveaxis(gfull.reshape(nl, r, 4, c // 4), 2, 1)
    return gfull.reshape(nl, 4, r // 4, c)


def kernel(x, g_mix, w_in, conv_w, conv_b, dt_bias, a_log, d_skip, ssd_norm, w_a, w_b, w_c, sink, w_out, g_mlp, w_up, w_down, g_final, loss_target, m_g_mix, m_w_in, m_conv_w, m_conv_b, m_dt_bias, m_a_log, m_d_skip, m_ssd_norm, m_w_a, m_w_b, m_w_c, m_sink, m_w_out, m_g_mlp, m_w_up, m_w_down, m_g_final, v_g_mix, v_w_in, v_conv_w, v_conv_b, v_dt_bias, v_a_log, v_d_skip, v_ssd_norm, v_w_a, v_w_b, v_w_c, v_sink, v_w_out, v_g_mlp, v_w_up, v_w_down, v_g_final):
    w = dict(g_mix=g_mix, w_in=w_in, conv_w=conv_w, conv_b=conv_b, dt_bias=dt_bias, a_log=a_log, d_skip=d_skip,
             ssd_norm=ssd_norm, w_a=w_a, w_b=w_b, w_c=w_c, sink=sink, w_out=w_out, g_mlp=g_mlp, w_up=w_up, w_down=w_down,
             g_final=g_final)
    m = dict(g_mix=m_g_mix, w_in=m_w_in, conv_w=m_conv_w, conv_b=m_conv_b, dt_bias=m_dt_bias, a_log=m_a_log,
             d_skip=m_d_skip, ssd_norm=m_ssd_norm, w_a=m_w_a, w_b=m_w_b, w_c=m_w_c, sink=m_sink, w_out=m_w_out,
             g_mlp=m_g_mlp, w_up=m_w_up, w_down=m_w_down, g_final=m_g_final)
    v = dict(g_mix=v_g_mix, w_in=v_w_in, conv_w=v_conv_w, conv_b=v_conv_b, dt_bias=v_dt_bias, a_log=v_a_log,
             d_skip=v_d_skip, ssd_norm=v_ssd_norm, w_a=v_w_a, w_b=v_w_b, w_c=v_w_c, sink=v_sink, w_out=v_w_out,
             g_mlp=v_g_mlp, w_up=v_w_up, w_down=v_w_down, g_final=v_g_final)
    depth = w_in.shape[0]
    assert depth == 2
    kchip = 2 * lax.axis_index("x") + lax.axis_index("y")

    full = {n: w[n] for n in SMALL if n != "conv_w"}
    for n in BIG:
        full[n] = _unstack(n, _gather_chips(w[n].astype(BF16), "gather_" + n))
    cw = _gather_chips(conv_w, "gather_conv_w")
    full["conv_w"] = jnp.moveaxis(cw, 0, 2).reshape(depth, CONV_WIDTH, 4 * conv_w.shape[2])

    loss_part, grad_x, grads, gg_final = _local_step(x, loss_target, full, depth)

    gsh = {}
    for n in BIG:
        gfull = jnp.stack([grads[li][n] for li in range(depth)])
        gsh[n] = _reduce_scatter(_restack(n, gfull), "rs_" + n)
    small_names = [n for n in SMALL if n != "g_final"]
    small_g = [jnp.stack([grads[li][n] for li in range(depth)]) for n in small_names] + [gg_final, loss_part[0, :1]]
    red = _unpack(_all_reduce_small(_pack(small_g), "allreduce_small"), small_g)
    for n, a in zip(small_names + ["g_final"], red):
        gsh[n] = a
    loss = red[-1][0]
    cshard = conv_w.shape[2]
    gsh["conv_w"] = lax.dynamic_slice_in_dim(gsh["conv_w"], kchip * cshard, cshard, axis=2)

    delta, new_m, new_v = {}, {}, {}
    for n in BIG:
        shp = w[n].shape
        two = lambda a: a.reshape(shp[0] * shp[1], shp[2])
        dl, nm_, nv_ = _adamw(two(w[n]), two(gsh[n]), two(m[n]), two(v[n]), "adamw_" + n)
        delta[n], new_m[n], new_v[n] = dl.reshape(shp), nm_.reshape(shp), nv_.reshape(shp)
    sm = list(SMALL)
    packed = [_pack([d_[n] for n in sm]) for d_ in (w, gsh, m, v)]
    outs = _adamw(*packed, "adamw_small")
    for dst, buf in zip((delta, new_m, new_v), outs):
        for n, a in zip(sm, _unpack(buf, [w[n] for n in sm])):
            dst[n] = a
    return (loss, grad_x, *[gsh[n] for n in ORDER], *[delta[n] for n in ORDER], *[new_m[n] for n in ORDER],
            *[new_v[n] for n in ORDER])
```

```python
import functools
import math

import jax
import jax.numpy as jnp
from jax import lax
from jax.experimental import pallas as pl
from jax.experimental.pallas import tpu as pltpu

F32 = jnp.float32
BF16 = jnp.bfloat16

SEQ = 2048
SSD_HEADS = 32
SSD_HEAD_DIM = 64
SSD_GROUPS = 8
SSD_STATE = 128
SSD_CHUNK = 128
CONV_WIDTH = 5
HEAD_DIM = 128
ROPE_DIM = 32
ROPE_THETA = 500000.0
DIL_PATTERNS = ((128, 1), (512, 4), (2048, 16))
DIL_HEADS = 8
WIN_Q_HEADS = 16
WIN_KV_HEADS = 4
WIN_HALF = 128
EPS = 1e-6
NEG_BIG = -1e30
ADAM_LR = 0.001
ADAM_B1 = 0.9
ADAM_B2 = 0.999
ADAM_EPS = 1e-08
ADAM_WD = 0.01
ADAM_STEP = 10

LANES = 128
ATT_BLK = 128
VMEM_LIMIT = 48 * 1024 * 1024
MESH = pl.DeviceIdType.MESH
HIGHEST = lax.Precision.HIGHEST
NT = (((1,), (1,)), ((), ()))
TN = (((0,), (0,)), ((), ()))
NN = (((1,), (0,)), ((), ()))


def _dims(d_model):
    hi = SSD_HEADS * SSD_HEAD_DIM
    gn = SSD_GROUPS * SSD_STATE
    ng = len(DIL_PATTERNS)
    dw = DIL_HEADS * HEAD_DIM
    wq = WIN_Q_HEADS * HEAD_DIM
    wk = WIN_KV_HEADS * HEAD_DIM
    d = dict(D=d_model, HI=hi, GN=gn, XBC=hi + 2 * gn, H2=2 * SSD_HEADS, NG=ng, DW=dw, WQ=wq, WK=wk,
             QKVD=3 * ng * dw, QW=3 * ng * dw + wq + 2 * wk, HG=SSD_HEADS // SSD_GROUPS)
    d["OFF_XBC"] = hi
    d["OFF_QKV"] = hi + d["XBC"]
    d["OFF_GATE"] = d["OFF_QKV"] + d["QW"]
    d["NM"] = d["OFF_GATE"] + 3 * d_model
    return d


def _pick(n, prefs):
    for p in prefs:
        if n % p == 0:
            return p
    raise ValueError(f"no tile for {n} in {prefs}")


def _params(sem):
    return pltpu.CompilerParams(dimension_semantics=sem, vmem_limit_bytes=VMEM_LIMIT)


def _sigmoid(x):
    return 1.0 / (1.0 + jnp.exp(-x))


def _mm(a, b, *, ta=False, tb=False, add=None, aux=None, epi=None, out_dtype=BF16, name):
    if ta:
        kdim, m = a.shape
    else:
        m, kdim = a.shape
    if tb:
        n, k2 = b.shape
    else:
        k2, n = b.shape
    assert kdim == k2, (a.shape, b.shape, ta, tb)
    tm = _pick(m, (1024, 512, 256, 128, 64, 32, 16, 8))
    tn = _pick(n, (1024, 512, 256, 128))
    tk = _pick(kdim, (2048, 1024, 512, 256, 128))
    nk = kdim // tk
    dims = (((0 if ta else 1,), (1 if tb else 0,)), ((), ()))
    n_in = 2 + (add is not None) + (aux is not None)
    n_out = 2 if epi == "relu2" else 1

    def body(*refs):
        a_ref, b_ref = refs[0], refs[1]
        pos = 2
        add_ref = aux_ref = None
        if add is not None:
            add_ref = refs[pos]
            pos += 1
        if aux is not None:
            aux_ref = refs[pos]
            pos += 1
        out_refs = refs[n_in:n_in + n_out]
        acc_ref = refs[n_in + n_out]
        k = pl.program_id(2)

        @pl.when(k == 0)
        def _():
            acc_ref[...] = jnp.zeros_like(acc_ref)

        acc_ref[...] += lax.dot_general(a_ref[...].astype(BF16), b_ref[...].astype(BF16), dims,
                                        preferred_element_type=F32)

        @pl.when(k == nk - 1)
        def _():
            r = acc_ref[...]
            if add_ref is not None:
                r = r + add_ref[...].astype(F32)
            if epi == "relu2":
                out_refs[0][...] = r.astype(out_refs[0].dtype)
                out_refs[1][...] = jnp.square(jnp.maximum(r, 0.0)).astype(out_refs[1].dtype)
            elif epi == "relu2_bwd":
                out_refs[0][...] = (r * 2.0 * jnp.maximum(aux_ref[...].astype(F32), 0.0)).astype(out_refs[0].dtype)
            else:
                out_refs[0][...] = r.astype(out_refs[0].dtype)

    a_spec = pl.BlockSpec((tk, tm), lambda i, j, k: (k, i)) if ta else pl.BlockSpec((tm, tk), lambda i, j, k: (i, k))
    b_spec = pl.BlockSpec((tn, tk), lambda i, j, k: (j, k)) if tb else pl.BlockSpec((tk, tn), lambda i, j, k: (k, j))
    o_spec = pl.BlockSpec((tm, tn), lambda i, j, k: (i, j))
    in_specs = [a_spec, b_spec]
    args = [a, b]
    if add is not None:
        in_specs.append(o_spec)
        args.append(add)
    if aux is not None:
        in_specs.append(o_spec)
        args.append(aux)
    out_shape = [jax.ShapeDtypeStruct((m, n), out_dtype)] * n_out
    res = pl.pallas_call(
        body, name=name, grid=(m // tm, n // tn, nk), in_specs=in_specs, out_specs=[o_spec] * n_out,
        out_shape=out_shape, scratch_shapes=[pltpu.VMEM((tm, tn), F32)],
        compiler_params=_params(("parallel", "parallel", "arbitrary")),
    )(*args)
    return res if n_out == 2 else res[0]


def _rowwise(body, rows, fulls, outs, accs=(), *, tile, name):
    rows = [r if isinstance(r, tuple) else (r, 0, r.shape[1]) for r in rows]
    nrows = rows[0][0].shape[0]
    assert nrows % tile == 0, (nrows, tile)
    in_specs, args = [], []
    for arr, off, width in rows:
        assert arr.shape[0] == nrows and off % width == 0, (arr.shape, off, width)
        in_specs.append(pl.BlockSpec((tile, width), lambda i, o=off // width: (i, o)))
        args.append(arr)
    for arr in fulls:
        in_specs.append(pl.BlockSpec(arr.shape, lambda i, nd=arr.ndim: (0,) * nd))
        args.append(arr)
    out_specs, out_shape = [], []
    for cols, dt in outs:
        out_specs.append(pl.BlockSpec((tile, cols), lambda i: (i, 0)))
        out_shape.append(jax.ShapeDtypeStruct((nrows, cols), dt))
    for shp, dt in accs:
        out_specs.append(pl.BlockSpec(shp, lambda i, nd=len(shp): (0,) * nd))
        out_shape.append(jax.ShapeDtypeStruct(shp, dt))
    n_in, n_out = len(args), len(outs)

    def wrapped(*refs):
        acc_refs = refs[n_in + n_out:]
        if acc_refs:
            @pl.when(pl.program_id(0) == 0)
            def _():
                for r in acc_refs:
                    r[...] = jnp.zeros_like(r)
        body(*refs)

    return pl.pallas_call(
        wrapped, name=name, grid=(nrows // tile,), in_specs=in_specs, out_specs=out_specs, out_shape=out_shape,
        compiler_params=_params(("arbitrary",)),
    )(*args)


def _rms_fwd(x, g, name):
    def body(x_ref, g_ref, h_ref):
        xv = x_ref[...]
        rstd = lax.rsqrt(jnp.mean(xv * xv, axis=-1, keepdims=True) + EPS)
        h_ref[...] = (xv * rstd * g_ref[...]).astype(BF16)

    return _rowwise(body, [x], [g], [(x.shape[1], BF16)], tile=256, name=name)[0]


def _rms_bwd(x, g, dh, dres, name):
    def body(x_ref, dh_ref, dres_ref, g_ref, dx_ref, dg_ref):
        xv = x_ref[...]
        dv = dh_ref[...]
        rstd = lax.rsqrt(jnp.mean(xv * xv, axis=-1, keepdims=True) + EPS)
        xn = xv * rstd
        dg_ref[...] += jnp.sum(dv * xn, axis=0, keepdims=True)
        dn = dv * g_ref[...]
        dx_ref[...] = dres_ref[...] + rstd * (dn - xn * jnp.mean(dn * xn, axis=-1, keepdims=True))

    d = x.shape[1]
    return _rowwise(body, [x, dh, dres], [g], [(d, F32)], [((1, d), F32)], tile=256, name=name)


def _loss_head(x, g, target, name):
    d = x.shape[1]

    def body(x_ref, t_ref, g_ref, dx_ref, loss_ref, dg_ref):
        xv = x_ref[...]
        rstd = lax.rsqrt(jnp.mean(xv * xv, axis=-1, keepdims=True) + EPS)
        xn = xv * rstd
        err = xn * g_ref[...] - t_ref[...]
        loss_ref[...] += jnp.full((1, LANES), 0.5 / d, F32) * jnp.sum(err * err)
        dy = err * (1.0 / d)
        dg_ref[...] += jnp.sum(dy * xn, axis=0, keepdims=True)
        dn = dy * g_ref[...]
        dx_ref[...] = rstd * (dn - xn * jnp.mean(dn * xn, axis=-1, keepdims=True))

    return _rowwise(body, [x, target], [g], [(d, F32)], [((1, LANES), F32), ((1, d), F32)], tile=256, name=name)


def _rope_tables(sign):
    half = ROPE_DIM // 2
    inv = ROPE_THETA ** (-jnp.arange(0, ROPE_DIM, 2, dtype=F32) / ROPE_DIM)
    ang = jnp.arange(SEQ, dtype=F32)[:, None] * inv[None, :]
    cos, sin = jnp.cos(ang), jnp.sin(ang) * sign
    zeros = jnp.zeros((SEQ, HEAD_DIM - ROPE_DIM), F32)
    zh = jnp.zeros((SEQ, half), F32)
    c = jnp.concatenate([cos, cos, zeros + 1.0], axis=1)
    s_up = jnp.concatenate([-sin, zh, zeros], axis=1)
    s_dn = jnp.concatenate([zh, sin, zeros], axis=1)
    return c, s_up, s_dn


def _rope(src, off, nblk, tabs, dm, name):
    t = src.shape[0]
    tq = 256
    half = ROPE_DIM // 2
    win0 = 3 * dm["NG"] * DIL_HEADS
    win1 = win0 + WIN_Q_HEADS + WIN_KV_HEADS
    qw = nblk * HEAD_DIM
    assert nblk == dm["QW"] // HEAD_DIM
    wb = next(c for c in (1024, 768, 512, 384, 256, 128) if off % c == 0 and qw % c == 0)
    reps = wb // HEAD_DIM
    sb = SEQ // tq
    flag = (jnp.arange(qw, dtype=jnp.int32) // HEAD_DIM < win1).astype(F32)[None, :]

    def body(x_ref, c_ref, up_ref, dn_ref, f_ref, o_ref):
        xv = x_ref[...].astype(F32)

        def wide(r):
            v = r[...]
            return v if reps == 1 else jnp.concatenate([v] * reps, axis=1)

        rot = xv * wide(c_ref) + pltpu.roll(xv, wb - half, 1) * wide(up_ref) + pltpu.roll(xv, half, 1) * wide(dn_ref)
        o_ref[...] = jnp.where(f_ref[...] > 0.5, rot, xv).astype(BF16)

    tab_spec = pl.BlockSpec((tq, HEAD_DIM), lambda i, j: (i % sb, 0))
    return pl.pallas_call(
        body, name=name, grid=(t // tq, qw // wb),
        in_specs=[pl.BlockSpec((tq, wb), lambda i, j, o=off // wb: (i, o + j)), tab_spec, tab_spec, tab_spec,
                  pl.BlockSpec((1, wb), lambda i, j: (0, j))],
        out_specs=pl.BlockSpec((tq, wb), lambda i, j: (i, j)),
        out_shape=jax.ShapeDtypeStruct((t, qw), BF16),
        compiler_params=_params(("parallel", "parallel")),
    )(src, *tabs, flag)


def _band_mask(rows_start, cols_start, nrows, ncols, w, n, rows_are_q):
    r = rows_start + lax.broadcasted_iota(jnp.int32, (nrows, ncols), 0)
    c = cols_start + lax.broadcasted_iota(jnp.int32, (nrows, ncols), 1)
    del rows_are_q
    return (jnp.abs(r - c) <= w) & (c >= 0) & (c < n)


def _nbr_specs(make, nb):
    if nb == 1:
        return [make(lambda i: i)]
    return [make(lambda i: jnp.maximum(i - 1, 0)), make(lambda i: i), make(lambda i: jnp.minimum(i + 1, nb - 1))]


def _cat(refs, axis):
    vals = [r[...] for r in refs]
    return vals[0] if len(vals) == 1 else jnp.concatenate(vals, axis=axis)


def _head(ref, h):
    return ref[:, h * HEAD_DIM:(h + 1) * HEAD_DIM]


def _head_cat(refs, h, axis=0):
    vals = [_head(r, h) for r in refs]
    return vals[0] if len(vals) == 1 else jnp.concatenate(vals, axis=axis)


def _attn_fwd(qa, ka, va, qb, kb, vb, hq, rep, n, w, sink, out_dtype, name):
    bb = qa.shape[0]
    blk = ATT_BLK
    nb = n // blk
    nk = 1 if nb == 1 else 3
    hkv = hq // rep
    scale = HEAD_DIM ** -0.5
    has_sink = sink is not None

    def body(*refs):
        q_ref = refs[0]
        k_refs = refs[1:1 + nk]
        v_refs = refs[1 + nk:1 + 2 * nk]
        pos = 1 + 2 * nk
        sink_ref = refs[pos] if has_sink else None
        o_ref, lse_ref = refs[pos + has_sink], refs[pos + has_sink + 1]
        i = pl.program_id(1)
        k0 = (i - 1) * blk if nk == 3 else i * blk
        valid = _band_mask(i * blk, k0, blk, nk * blk, w, n, True)
        for g in range(hkv):
            kcat = _head_cat(k_refs, g)
            vcat = _head_cat(v_refs, g)
            for r in range(rep):
                h = g * rep + r
                s = lax.dot_general(_head(q_ref, h), kcat, NT, preferred_element_type=F32) * scale
                s = jnp.where(valid, s, NEG_BIG)
                m = jnp.max(s, axis=1, keepdims=True)
                if has_sink:
                    m = jnp.maximum(m, sink_ref[h])
                p = jnp.exp(s - m)
                l = jnp.sum(p, axis=1, keepdims=True)
                if has_sink:
                    l = l + jnp.exp(sink_ref[h] - m)
                o = lax.dot_general(p.astype(BF16), vcat, NN, preferred_element_type=F32) / l
                o_ref[:, h * HEAD_DIM:(h + 1) * HEAD_DIM] = o.astype(o_ref.dtype)
                lse_ref[h] = m + jnp.log(l)

    def mk(col, width):
        return lambda f: pl.BlockSpec((None, blk, width), lambda b, i, f=f: (b, f(i), col))

    qw, kw = hq * HEAD_DIM, hkv * HEAD_DIM
    in_specs = [pl.BlockSpec((None, blk, qw), lambda b, i: (b, i, qb))]
    in_specs += _nbr_specs(mk(kb, kw), nb) + _nbr_specs(mk(vb, kw), nb)
    args = [qa] + [ka] * nk + [va] * nk
    if has_sink:
        in_specs.append(pl.BlockSpec((hq, 1, 1), lambda b, i: (0, 0, 0)))
        args.append(sink)
    return pl.pallas_call(
        body, name=name, grid=(bb, nb), in_specs=in_specs,
        out_specs=[pl.BlockSpec((None, blk, qw), lambda b, i: (b, i, 0)),
                   pl.BlockSpec((None, hq, blk, 1), lambda b, i: (b, 0, i, 0))],
        out_shape=[jax.ShapeDtypeStruct((bb, n, qw), out_dtype), jax.ShapeDtypeStruct((bb, hq, n, 1), F32)],
        compiler_params=_params(("parallel", "parallel")),
    )(*args)


def _attn_dq(qa, ka, va, do, lse, delta, qb, kb, vb, hq, rep, n, w, name):
    bb = qa.shape[0]
    blk = ATT_BLK
    nb = n // blk
    nk = 1 if nb == 1 else 3
    hkv = hq // rep
    scale = HEAD_DIM ** -0.5

    def body(*refs):
        q_ref = refs[0]
        k_refs = refs[1:1 + nk]
        v_refs = refs[1 + nk:1 + 2 * nk]
        do_ref, lse_ref, dl_ref, dq_ref = refs[1 + 2 * nk:]
        i = pl.program_id(1)
        k0 = (i - 1) * blk if nk == 3 else i * blk
        valid = _band_mask(i * blk, k0, blk, nk * blk, w, n, True)
        for g in range(hkv):
            kcat = _head_cat(k_refs, g)
            vcat = _head_cat(v_refs, g)
            for r in range(rep):
                h = g * rep + r
                s = lax.dot_general(_head(q_ref, h), kcat, NT, preferred_element_type=F32) * scale
                p = jnp.exp(jnp.where(valid, s, NEG_BIG) - lse_ref[h])
                dp = lax.dot_general(_head(do_ref, h).astype(BF16), vcat, NT, preferred_element_type=F32)
                ds = p * (dp - dl_ref[h])
                dq = lax.dot_general(ds.astype(BF16), kcat, NN, preferred_element_type=F32) * scale
                dq_ref[:, h * HEAD_DIM:(h + 1) * HEAD_DIM] = dq.astype(BF16)

    def mk(col, width):
        return lambda f: pl.BlockSpec((None, blk, width), lambda b, i, f=f: (b, f(i), col))

    qw, kw = hq * HEAD_DIM, hkv * HEAD_DIM
    col_spec = pl.BlockSpec((None, hq, blk, 1), lambda b, i: (b, 0, i, 0))
    in_specs = [pl.BlockSpec((None, blk, qw), lambda b, i: (b, i, qb))]
    in_specs += _nbr_specs(mk(kb, kw), nb) + _nbr_specs(mk(vb, kw), nb)
    in_specs += [pl.BlockSpec((None, blk, qw), lambda b, i: (b, i, 0)), col_spec, col_spec]
    return pl.pallas_call(
        body, name=name, grid=(bb, nb), in_specs=in_specs,
        out_specs=pl.BlockSpec((None, blk, qw), lambda b, i: (b, i, 0)),
        out_shape=jax.ShapeDtypeStruct((bb, n, qw), BF16),
        compiler_params=_params(("parallel", "parallel")),
    )(qa, *([ka] * nk), *([va] * nk), do, lse, delta)


def _attn_dkv(qa, ka, va, do, lse_row, delta_row, qb, kb, vb, hq, rep, n, w, name):
    bb = qa.shape[0]
    blk = ATT_BLK
    nb = n // blk
    nq = 1 if nb == 1 else 3
    hkv = hq // rep
    scale = HEAD_DIM ** -0.5

    def body(*refs):
        k_ref, v_ref = refs[0], refs[1]
        q_refs = refs[2:2 + nq]
        do_refs = refs[2 + nq:2 + 2 * nq]
        lse_refs = refs[2 + 2 * nq:2 + 3 * nq]
        dl_refs = refs[2 + 3 * nq:2 + 4 * nq]
        dk_ref, dv_ref = refs[2 + 4 * nq:]
        j = pl.program_id(1)
        q0 = (j - 1) * blk if nq == 3 else j * blk
        valid = _band_mask(j * blk, q0, blk, nq * blk, w, n, False)
        for g in range(hkv):
            kg, vg = _head(k_ref, g), _head(v_ref, g)
            dk = jnp.zeros((blk, HEAD_DIM), F32)
            dv = jnp.zeros((blk, HEAD_DIM), F32)
            for r in range(rep):
                h = g * rep + r
                qcat = _head_cat(q_refs, h)
                docat = _head_cat(do_refs, h).astype(BF16)
                lse = lse_refs[0][h] if nq == 1 else jnp.concatenate([lr[h] for lr in lse_refs], axis=1)
                dl = dl_refs[0][h] if nq == 1 else jnp.concatenate([dr[h] for dr in dl_refs], axis=1)
                st = lax.dot_general(kg, qcat, NT, preferred_element_type=F32) * scale
                pt = jnp.exp(jnp.where(valid, st, NEG_BIG) - lse)
                dv = dv + lax.dot_general(pt.astype(BF16), docat, NN, preferred_element_type=F32)
                dpt = lax.dot_general(vg, docat, NT, preferred_element_type=F32)
                dst = pt * (dpt - dl)
                dk = dk + lax.dot_general(dst.astype(BF16), qcat, NN, preferred_element_type=F32) * scale
            dk_ref[:, g * HEAD_DIM:(g + 1) * HEAD_DIM] = dk.astype(BF16)
            dv_ref[:, g * HEAD_DIM:(g + 1) * HEAD_DIM] = dv.astype(BF16)

    qw, kw = hq * HEAD_DIM, hkv * HEAD_DIM

    def mkq(col):
        return lambda f: pl.BlockSpec((None, blk, qw), lambda b, j, f=f: (b, f(j), col))

    def mkrow(f):
        return pl.BlockSpec((None, hq, 1, blk), lambda b, j, f=f: (b, 0, 0, f(j)))

    in_specs = [pl.BlockSpec((None, blk, kw), lambda b, j: (b, j, kb)), pl.BlockSpec((None, blk, kw), lambda b, j: (b, j, vb))]
    in_specs += _nbr_specs(mkq(qb), nb) + _nbr_specs(mkq(0), nb) + _nbr_specs(mkrow, nb) + _nbr_specs(mkrow, nb)
    o_spec = pl.BlockSpec((None, blk, kw), lambda b, j: (b, j, 0))
    return pl.pallas_call(
        body, name=name, grid=(bb, nb), in_specs=in_specs, out_specs=[o_spec, o_spec],
        out_shape=[jax.ShapeDtypeStruct((bb, n, kw), BF16)] * 2,
        compiler_params=_params(("parallel", "parallel")),
    )(ka, va, *([qa] * nq), *([do] * nq), *([lse_row] * nq), *([delta_row] * nq))


def _head_expand(v, nh, width):
    lane_head = lax.broadcasted_iota(jnp.int32, (1, nh * width), 1) >> int(math.log2(width))
    out = jnp.zeros((v.shape[0], nh * width), F32)
    for j in range(nh):
        out = jnp.where(lane_head == j, v[:, j:j + 1], out)
    return out


def _head_sums(m, nh, width):
    lane_head = lax.broadcasted_iota(jnp.int32, (1, nh * width), 1) >> int(math.log2(width))
    col = lax.broadcasted_iota(jnp.int32, (1, nh), 1)
    out = jnp.zeros((m.shape[0], nh), F32)
    for j in range(nh):
        sj = jnp.sum(jnp.where(lane_head == j, m, 0.0), axis=1, keepdims=True)
        out = jnp.where(col == j, sj, out)
    return out


def _head_dots(a, b, nh, name):
    def body(a_ref, b_ref, o_ref):
        o_ref[...] = _head_sums(a_ref[...].astype(F32) * b_ref[...].astype(F32), nh, HEAD_DIM)

    return _rowwise(body, [a, b], [], [(nh, F32)], tile=256, name=name)[0]


def _dil_combine(outs, lses, name):
    ng = len(outs)

    def body(*refs):
        o_refs, l_refs = refs[:ng], refs[ng:2 * ng]
        y_ref, lt_ref = refs[2 * ng], refs[2 * ng + 1]
        ls = [r[...] for r in l_refs]
        m = functools.reduce(jnp.maximum, ls)
        es = [jnp.exp(v - m) for v in ls]
        tot = functools.reduce(jnp.add, es)
        acc = jnp.zeros(o_refs[0].shape, F32)
        for o_ref, e in zip(o_refs, es):
            acc = acc + _head_expand(e / tot, DIL_HEADS, HEAD_DIM) * o_ref[...]
        y_ref[...] = acc.astype(BF16)
        lt_ref[...] = m + jnp.log(tot)

    dw = outs[0].shape[1]
    return _rowwise(body, list(outs) + list(lses), [], [(dw, BF16), (DIL_HEADS, F32)], tile=256, name=name)


def _sink_grad(lse, delta, sink, name):
    def body(l_ref, d_ref, s_ref, o_ref):
        o_ref[...] -= jnp.sum(jnp.exp(s_ref[...] - l_ref[...]) * d_ref[...], axis=0, keepdims=True)

    return _rowwise(body, [lse, delta], [sink], [], [((1, lse.shape[1]), F32)], tile=512, name=name)[0]


def _shift_rows(x, d, nrows):
    if d == 0:
        return x
    rolled = pltpu.roll(x, (-d) % nrows, 0)
    row = lax.broadcasted_iota(jnp.int32, x.shape, 0)
    ok = (row + d >= 0) & (row + d < nrows)
    return jnp.where(ok, rolled, 0.0)


def _conv_fwd(proj, off, conv_w, conv_b, xbc, name):
    t = proj.shape[0]
    tc = _pick(xbc, (256, 128))
    assert off % tc == 0
    pad = (CONV_WIDTH - 1) // 2

    def body(x_ref, w_ref, b_ref, c_ref, u_ref):
        xv = x_ref[...].astype(F32)
        acc = jnp.zeros_like(xv) + b_ref[...]
        for k in range(CONV_WIDTH):
            acc = acc + w_ref[k:k + 1, :] * _shift_rows(xv, k - pad, SEQ)
        c_ref[...] = acc.astype(BF16)
        u_ref[...] = (acc * _sigmoid(acc)).astype(BF16)

    o_spec = pl.BlockSpec((SEQ, tc), lambda b, j: (b, j))
    return pl.pallas_call(
        body, name=name, grid=(t // SEQ, xbc // tc),
        in_specs=[pl.BlockSpec((SEQ, tc), lambda b, j, o=off // tc: (b, o + j)),
                  pl.BlockSpec((CONV_WIDTH, tc), lambda b, j: (0, j)), pl.BlockSpec((1, tc), lambda b, j: (0, j))],
        out_specs=[o_spec, o_spec], out_shape=[jax.ShapeDtypeStruct((t, xbc), BF16)] * 2,
        compiler_params=_params(("parallel", "parallel")),
    )(proj, conv_w, conv_b)


def _conv_bwd(du, cpre, proj, off, conv_w, name):
    t, xbc = du.shape
    tc = _pick(xbc, (256, 128))
    pad = (CONV_WIDTH - 1) // 2

    def body(du_ref, c_ref, x_ref, w_ref, dx_ref, dw_ref, db_ref):
        @pl.when(pl.program_id(1) == 0)
        def _():
            dw_ref[...] = jnp.zeros_like(dw_ref)
            db_ref[...] = jnp.zeros_like(db_ref)

        cv = c_ref[...].astype(F32)
        sg = _sigmoid(cv)
        dc = du_ref[...] * (sg * (1.0 + cv * (1.0 - sg)))
        xv = x_ref[...].astype(F32)
        dx = jnp.zeros_like(dc)
        for k in range(CONV_WIDTH):
            dx = dx + w_ref[k:k + 1, :] * _shift_rows(dc, pad - k, SEQ)
            dw_ref[k:k + 1, :] += jnp.sum(dc * _shift_rows(xv, k - pad, SEQ), axis=0, keepdims=True)
        db_ref[...] += jnp.sum(dc, axis=0, keepdims=True)
        dx_ref[...] = dx.astype(BF16)

    blk = pl.BlockSpec((SEQ, tc), lambda j, b: (b, j))
    return pl.pallas_call(
        body, name=name, grid=(xbc // tc, t // SEQ),
        in_specs=[blk, blk, pl.BlockSpec((SEQ, tc), lambda j, b, o=off // tc: (b, o + j)),
                  pl.BlockSpec((CONV_WIDTH, tc), lambda j, b: (0, j))],
        out_specs=[blk, pl.BlockSpec((CONV_WIDTH, tc), lambda j, b: (0, j)), pl.BlockSpec((1, tc), lambda j, b: (0, j))],
        out_shape=[jax.ShapeDtypeStruct((t, xbc), BF16), jax.ShapeDtypeStruct((CONV_WIDTH, xbc), F32),
                   jax.ShapeDtypeStruct((1, xbc), F32)],
        compiler_params=_params(("parallel", "arbitrary")),
    )(du, cpre, proj, conv_w)


def _dt_prep(dtr, bias, name):
    def body(r_ref, b_ref, o_ref):
        v = r_ref[...] + b_ref[...]
        o_ref[...] = jnp.maximum(v, 0.0) + jnp.log1p(jnp.exp(-jnp.abs(v)))

    return _rowwise(body, [dtr], [bias], [(dtr.shape[1], F32)], tile=512, name=name)[0]


def _scan_prelude(d, dt_ref, dtt_ref, al_ref, alt_ref, hg):
    p = SSD_HEAD_DIM
    ch = SSD_CHUNK
    a_row = -jnp.exp(al_ref[...])
    a_col = -jnp.exp(alt_ref[...])
    dtc = dt_ref[...]
    dt_x = _head_expand(dtc, hg, p)
    dta_x = dt_x * _head_expand(a_row, hg, p)
    dta_t = dtt_ref[...] * a_col
    ri = lax.broadcasted_iota(jnp.int32, (ch, ch), 0)
    ci = lax.broadcasted_iota(jnp.int32, (ch, ch), 1)
    mask = (ci <= ri) if d == 0 else (ci >= ri)
    mask_t = (ci >= ri) if d == 0 else (ci <= ri)
    tri = mask.astype(F32)
    phi_x = jnp.dot(tri, dta_x, preferred_element_type=F32, precision=HIGHEST)
    phi_r = lax.dot_general(dta_t, tri, NT, preferred_element_type=F32, precision=HIGHEST)
    tot_x = jnp.sum(dta_x, axis=0, keepdims=True)
    return dtc, dt_x, phi_x, phi_r, tot_x, mask, mask_t


def _scan_specs(d, nc, hg, dm):
    p, n, ch = SSD_HEAD_DIM, SSD_STATE, SSD_CHUNK
    w = hg * p
    b0 = dm["HI"] // n
    c0 = (dm["HI"] + dm["GN"]) // n

    def row(b, c):
        return b * nc + c

    return [
        pl.BlockSpec((ch, w), lambda b, g, c: (row(b, c), g)),
        pl.BlockSpec((ch, n), lambda b, g, c: (row(b, c), b0 + g)),
        pl.BlockSpec((ch, n), lambda b, g, c: (row(b, c), c0 + g)),
        pl.BlockSpec((None, None, ch, hg), lambda b, g, c: (d, g, row(b, c), 0)),
        pl.BlockSpec((None, None, 8, ch), lambda b, g, c: (d, g, 0, row(b, c))),
        pl.BlockSpec((None, None, 1, hg), lambda b, g, c: (d, g, 0, 0)),
        pl.BlockSpec((None, None, 8, 1), lambda b, g, c: (d, g, 0, 0)),
    ]


def _remap(spec, f):
    return pl.BlockSpec(spec.block_shape, lambda b, g, c, im=spec.index_map: im(b, g, f(c)))


def _scan_fwd(u, dtg, dttg, alg, altg, d, dm, name):
    t = u.shape[0]
    p, n, ch, hg = SSD_HEAD_DIM, SSD_STATE, SSD_CHUNK, dm["HG"]
    w = hg * p
    nc = SEQ // ch
    order = (lambda c: c) if d == 0 else (lambda c: nc - 1 - c)

    def body(x_ref, b_ref, c_ref, dt_ref, dtt_ref, al_ref, alt_ref, y_ref, sin_ref, s_ref):
        @pl.when(pl.program_id(2) == 0)
        def _():
            s_ref[...] = jnp.zeros_like(s_ref)

        dtc, dt_x, phi_x, phi_r, tot_x, mask, _ = _scan_prelude(d, dt_ref, dtt_ref, al_ref, alt_ref, hg)
        lane_head = lax.broadcasted_iota(jnp.int32, (1, w), 1) >> int(math.log2(p))
        cm, bm = c_ref[...], b_ref[...]
        cb = lax.dot_general(cm, bm, NT, preferred_element_type=F32)
        xdt = x_ref[...].astype(F32) * dt_x
        xdt_b = xdt.astype(BF16)
        ydiag = jnp.zeros((ch, w), F32)
        for j in range(hg):
            seg = phi_x[:, j * p:j * p + 1] - phi_r[j:j + 1, :]
            mj = (cb * jnp.exp(jnp.where(mask, seg, NEG_BIG))).astype(BF16)
            ydiag = ydiag + jnp.dot(mj, jnp.where(lane_head == j, xdt_b, jnp.zeros_like(xdt_b)), preferred_element_type=F32)
        s = s_ref[...]
        y_ref[...] = ydiag + jnp.dot(cm, s.astype(BF16), preferred_element_type=F32) * jnp.exp(phi_x)
        sin_ref[...] = s
        wm = (xdt * jnp.exp(tot_x - phi_x)).astype(BF16)
        s_ref[...] = s * jnp.exp(tot_x) + lax.dot_general(bm, wm, TN, preferred_element_type=F32)

    specs = [_remap(s, order) for s in _scan_specs(d, nc, hg, dm)]
    return pl.pallas_call(
        body, name=name, grid=(t // SEQ, SSD_GROUPS, nc), in_specs=specs,
        out_specs=[_remap(pl.BlockSpec((ch, w), lambda b, g, c: (b * nc + c, g)), order),
                   _remap(pl.BlockSpec((None, None, n, w), lambda b, g, c: (b * nc + c, g, 0, 0)), order)],
        out_shape=[jax.ShapeDtypeStruct((t, dm["HI"]), F32), jax.ShapeDtypeStruct((t // ch, SSD_GROUPS, n, w), F32)],
        scratch_shapes=[pltpu.VMEM((n, w), F32)],
        compiler_params=_params(("parallel", "parallel", "arbitrary")),
    )(u, u, u, dtg, dttg, alg, altg)


def _scan_bwd(u, dtg, dttg, alg, altg, dy, sin, adds, d, dm, name):
    t = u.shape[0]
    p, n, ch, hg = SSD_HEAD_DIM, SSD_STATE, SSD_CHUNK, dm["HG"]
    w = hg * p
    nc = SEQ // ch
    order = (lambda c: nc - 1 - c) if d == 0 else (lambda c: c)
    has_bc_add = adds[1] is not None

    def body(*refs):
        x_ref, b_ref, c_ref, dt_ref, dtt_ref, al_ref, alt_ref, dy_ref, sin_ref, ax_ref = refs[:10]
        pos = 10
        ab_ref = ac_ref = None
        if has_bc_add:
            ab_ref, ac_ref = refs[10], refs[11]
            pos = 12
        dxs_ref, db_ref, dc_ref, rq_ref, xdx_ref, ds_ref = refs[pos:]

        @pl.when(pl.program_id(2) == 0)
        def _():
            ds_ref[...] = jnp.zeros_like(ds_ref)

        dtc, dt_x, phi_x, phi_r, tot_x, mask, mask_t = _scan_prelude(d, dt_ref, dtt_ref, al_ref, alt_ref, hg)
        lane_head = lax.broadcasted_iota(jnp.int32, (1, w), 1) >> int(math.log2(p))
        cm, bm = c_ref[...], b_ref[...]
        cb = lax.dot_general(cm, bm, NT, preferred_element_type=F32)
        cb_t = lax.dot_general(bm, cm, NT, preferred_element_type=F32)
        xs = x_ref[...].astype(F32)
        xdt = xs * dt_x
        xdt_b = xdt.astype(BF16)
        dy = dy_ref[...]
        dy_b = dy.astype(BF16)
        zero_b = jnp.zeros_like(dy_b)
        col = lax.broadcasted_iota(jnp.int32, (1, hg), 1)
        dxp = jnp.zeros((ch, w), F32)
        a_ls = jnp.zeros((ch, ch), F32)
        a_sl = jnp.zeros((ch, ch), F32)
        dphi = jnp.zeros((ch, hg), F32)
        for j in range(hg):
            pc = phi_x[:, j * p:j * p + 1]
            pr = phi_r[j:j + 1, :]
            l_ls = jnp.exp(jnp.where(mask, pc - pr, NEG_BIG))
            l_sl = jnp.exp(jnp.where(mask_t, pr - pc, NEG_BIG))
            dy_j = jnp.where(lane_head == j, dy_b, zero_b)
            xdt_j = jnp.where(lane_head == j, xdt_b, zero_b)
            dxp = dxp + jnp.dot((cb_t * l_sl).astype(BF16), dy_j, preferred_element_type=F32)
            g_ls = l_ls * lax.dot_general(dy_j, xdt_b, NT, preferred_element_type=F32)
            g_sl = l_sl * lax.dot_general(xdt_j, dy_b, NT, preferred_element_type=F32)
            a_ls = a_ls + g_ls
            a_sl = a_sl + g_sl
            pair = jnp.sum(g_ls * cb, axis=1, keepdims=True) - jnp.sum(g_sl * cb_t, axis=1, keepdims=True)
            dphi = jnp.where(col == j, pair, dphi)
        ds = ds_ref[...]
        ds_b = ds.astype(BF16)
        sin = sin_ref[...]
        sin_b = sin.astype(BF16)
        e_tp = jnp.exp(tot_x - phi_x)
        e_p = jnp.exp(phi_x)
        dxp_off = e_tp * jnp.dot(bm, ds_b, preferred_element_type=F32)
        dxp = dxp + dxp_off
        dxs_ref[...] = ax_ref[...] + dxp * dt_x
        xdx_ref[...] = _head_sums(xs * dxp, hg, p)
        y_off = jnp.dot(cm, sin_b, preferred_element_type=F32) * e_p
        st_t = _head_sums(xdt * dxp_off, hg, p)
        dphi = dphi + _head_sums(dy * y_off, hg, p) - st_t
        dtot = _head_sums(jnp.sum(ds * sin, axis=0, keepdims=True) * jnp.exp(tot_x), hg, p) + jnp.sum(st_t, axis=0, keepdims=True)
        cum = jnp.dot(mask_t.astype(F32), _head_expand(dphi, hg, p), preferred_element_type=F32, precision=HIGHEST)
        ddta = jnp.zeros((ch, hg), F32)
        for j in range(hg):
            ddta = jnp.where(col == j, cum[:, j * p:j * p + 1], ddta)
        rq_ref[...] = ddta + dtot
        dye = (dy * e_p).astype(BF16)
        dcv = jnp.dot(a_ls.astype(BF16), bm, preferred_element_type=F32)
        dcv = dcv + lax.dot_general(dye, sin_b, NT, preferred_element_type=F32)
        dbv = jnp.dot(a_sl.astype(BF16), cm, preferred_element_type=F32)
        dbv = dbv + lax.dot_general((xdt * e_tp).astype(BF16), ds_b, NT, preferred_element_type=F32)
        if has_bc_add:
            dcv = dcv + ac_ref[...]
            dbv = dbv + ab_ref[...]
        dc_ref[...] = dcv
        db_ref[...] = dbv
        ds_ref[...] = ds * jnp.exp(tot_x) + lax.dot_general(cm, dye, TN, preferred_element_type=F32)

    def sp(spec):
        return _remap(spec, order)

    xw = pl.BlockSpec((ch, w), lambda b, g, c: (b * nc + c, g))
    gn_blk = pl.BlockSpec((ch, n), lambda b, g, c: (b * nc + c, g))
    small = pl.BlockSpec((None, ch, hg), lambda b, g, c: (g, b * nc + c, 0))
    in_specs = [sp(s) for s in _scan_specs(d, nc, hg, dm)]
    in_specs += [sp(xw), sp(pl.BlockSpec((None, None, n, w), lambda b, g, c: (b * nc + c, g, 0, 0))), sp(xw)]
    args = [u, u, u, dtg, dttg, alg, altg, dy, sin, adds[0]]
    if has_bc_add:
        in_specs += [sp(gn_blk), sp(gn_blk)]
        args += [adds[1], adds[2]]
    return pl.pallas_call(
        body, name=name, grid=(t // SEQ, SSD_GROUPS, nc), in_specs=in_specs,
        out_specs=[sp(xw), sp(gn_blk), sp(gn_blk), sp(small), sp(small)],
        out_shape=[jax.ShapeDtypeStruct((t, dm["HI"]), F32), jax.ShapeDtypeStruct((t, dm["GN"]), F32),
                   jax.ShapeDtypeStruct((t, dm["GN"]), F32), jax.ShapeDtypeStruct((SSD_GROUPS, t, hg), F32),
                   jax.ShapeDtypeStruct((SSD_GROUPS, t, hg), F32)],
        scratch_shapes=[pltpu.VMEM((n, w), F32)],
        compiler_params=_params(("parallel", "parallel", "arbitrary")),
    )(*args)


def _ssd_param_bwd(rq_f, rq_r, xdx, dtp, dtr, bias, alog, name):
    def body(rf_ref, rr_ref, xdx_ref, dt_ref, dtr_ref, b_ref, al_ref, o_ref, db_ref, da_ref):
        a = -jnp.exp(al_ref[...])
        d_dta = rf_ref[...] + rr_ref[...]
        ddt = a * d_dta + xdx_ref[...]
        ddtr = ddt * _sigmoid(dtr_ref[...] + b_ref[...])
        o_ref[...] = ddtr
        db_ref[...] += jnp.sum(ddtr, axis=0, keepdims=True)
        da_ref[...] += a * jnp.sum(dt_ref[...] * d_dta, axis=0, keepdims=True)

    return _rowwise(body, [rq_f, rq_r, xdx, dtp, dtr], [bias, alog], [(LANES, F32)],
                    [((1, LANES), F32), ((1, LANES), F32)], tile=512, name=name)


def _ssd_out_fwd(y_f, y_b, u, proj, dcols, gn, hi, name):
    def body(yf_ref, yb_ref, x_ref, z_ref, d_ref, g_ref, o_ref):
        ytot = yf_ref[...] + yb_ref[...] + d_ref[...] * x_ref[...].astype(F32)
        zv = z_ref[...].astype(F32)
        yz = ytot * (zv * _sigmoid(zv))
        rstd = lax.rsqrt(jnp.mean(yz * yz, axis=-1, keepdims=True) + EPS)
        o_ref[...] = (yz * rstd * g_ref[...]).astype(BF16)

    return _rowwise(body, [y_f, y_b, (u, 0, hi), (proj, 0, hi)], [dcols, gn], [(hi, BF16)], tile=256, name=name)[0]


def _ssd_out_bwd(dya, y_f, y_b, u, proj, dcols, gn, hi, name):
    def body(dy_ref, yf_ref, yb_ref, x_ref, z_ref, d_ref, g_ref, dyt_ref, dxs_ref, dz_ref, dg_ref, dd_ref):
        xv = x_ref[...].astype(F32)
        ytot = yf_ref[...] + yb_ref[...] + d_ref[...] * xv
        zv = z_ref[...].astype(F32)
        sg = _sigmoid(zv)
        sz = zv * sg
        yz = ytot * sz
        rstd = lax.rsqrt(jnp.mean(yz * yz, axis=-1, keepdims=True) + EPS)
        yn = yz * rstd
        dv = dy_ref[...]
        dg_ref[...] += jnp.sum(dv * yn, axis=0, keepdims=True)
        dn = dv * g_ref[...]
        dyz = rstd * (dn - yn * jnp.mean(dn * yn, axis=-1, keepdims=True))
        dyt = dyz * sz
        dyt_ref[...] = dyt
        dxs_ref[...] = dyt * d_ref[...]
        dz_ref[...] = (dyz * ytot * (sg * (1.0 + zv * (1.0 - sg)))).astype(BF16)
        dd_ref[...] += jnp.sum(dyt * xv, axis=0, keepdims=True)

    return _rowwise(body, [dya, y_f, y_b, (u, 0, hi), (proj, 0, hi)], [dcols, gn],
                    [(hi, F32), (hi, F32), (hi, BF16)], [((1, hi), F32), ((1, hi), F32)], tile=128, name=name)


def _gate_fwd(pa, pb, pc, proj, off, d, name):
    def body(a_ref, b_ref, c_ref, g0_ref, g1_ref, g2_ref, o_ref):
        acc = _sigmoid(g0_ref[...].astype(F32)) * a_ref[...]
        acc = acc + _sigmoid(g1_ref[...].astype(F32)) * b_ref[...]
        acc = acc + _sigmoid(g2_ref[...].astype(F32)) * c_ref[...]
        o_ref[...] = acc.astype(BF16)

    rows = [pa, pb, pc] + [(proj, off + k * d, d) for k in range(3)]
    return _rowwise(body, rows, [], [(d, BF16)], tile=256, name=name)[0]


def _gate_bwd(dm_, pa, pb, pc, proj, off, d, name):
    def body(dm_ref, a_ref, b_ref, c_ref, g0_ref, g1_ref, g2_ref, da_ref, db_ref, dc_ref, dg0_ref, dg1_ref, dg2_ref):
        dmv = dm_ref[...]
        for p_ref, g_ref, dp_ref, dg_ref in ((a_ref, g0_ref, da_ref, dg0_ref), (b_ref, g1_ref, db_ref, dg1_ref),
                                             (c_ref, g2_ref, dc_ref, dg2_ref)):
            sg = _sigmoid(g_ref[...].astype(F32))
            dp_ref[...] = (dmv * sg).astype(BF16)
            dg_ref[...] = (dmv * p_ref[...] * sg * (1.0 - sg)).astype(BF16)

    rows = [dm_, pa, pb, pc] + [(proj, off + k * d, d) for k in range(3)]
    return _rowwise(body, rows, [], [(d, BF16)] * 6, tile=128, name=name)


def _adamw(w, g, m, v, name):
    nl, rows, cols = w.shape
    tile = _pick(rows, (64, 32, 16, 8))
    c1 = 1.0 / (1.0 - ADAM_B1 ** ADAM_STEP)
    c2 = 1.0 / (1.0 - ADAM_B2 ** ADAM_STEP)

    def body(w_ref, g_ref, m_ref, v_ref, d_ref, nm_ref, nv_ref):
        gv = g_ref[...]
        nm = ADAM_B1 * m_ref[...] + (1.0 - ADAM_B1) * gv
        nv = ADAM_B2 * v_ref[...] + (1.0 - ADAM_B2) * (gv * gv)
        nm_ref[...] = nm
        nv_ref[...] = nv
        d_ref[...] = -ADAM_LR * ((nm * c1) / (jnp.sqrt(nv * c2) + ADAM_EPS) + ADAM_WD * w_ref[...])

    blk = pl.BlockSpec((None, tile, cols), lambda l, i: (l, i, 0))
    return pl.pallas_call(
        body, name=name, grid=(nl, rows // tile), in_specs=[blk] * 4, out_specs=[blk] * 3,
        out_shape=[jax.ShapeDtypeStruct(w.shape, F32)] * 3, compiler_params=_params(("parallel", "parallel")),
    )(w, g, m, v)


ANY = pl.BlockSpec(memory_space=pl.ANY)


def _place():
    x, y, c = lax.axis_index("x"), lax.axis_index("y"), lax.axis_index("c")
    chips = [(1 - x, y), (x, 1 - y), (1 - x, 1 - y)]
    return x, y, c, chips


def _gather_chips(arr, name):
    def body(src, out, ssem, rsem):
        x, y, c, chips = _place()
        k = 2 * x + y

        def copy(j, kk, layer, to, own=False):
            return pltpu.make_async_remote_copy(
                src_ref=src.at[layer] if own else out.at[kk, layer], dst_ref=out.at[kk, layer],
                send_sem=ssem.at[j], recv_sem=rsem.at[j], device_id=to, device_id_type=MESH)

        first = [copy(j, k, c, (cx, cy, c), own=True) for j, (cx, cy) in enumerate(chips)]
        for cp in first:
            cp.start()
        passed = [copy(3 + j, 2 * cx + cy, c, (x, y, 1 - c)) for j, (cx, cy) in enumerate(chips)]
        for j, (cx, cy) in enumerate(chips):
            copy(j, 2 * cx + cy, c, (x, y, c)).wait_recv()
            passed[j].start()
        for j, (cx, cy) in enumerate(chips):
            copy(3 + j, 2 * cx + cy, 1 - c, (x, y, c)).wait_recv()
        for cp in first + passed:
            cp.wait_send()

    st = pl.pallas_call(
        body, name=name, in_specs=[ANY], out_specs=ANY, out_shape=jax.ShapeDtypeStruct((4,) + arr.shape, arr.dtype),
        scratch_shapes=[pltpu.SemaphoreType.DMA((6,)), pltpu.SemaphoreType.DMA((6,))],
    )(arr)
    kchip = 2 * lax.axis_index("x") + lax.axis_index("y")
    return lax.dynamic_update_slice(st, arr[None], (kchip,) + (0,) * arr.ndim)


def _pair_swap(g, name):
    def body(src, out, ssem, rsem):
        x, y, c, _ = _place()
        cp = pltpu.make_async_remote_copy(src_ref=src.at[1 - c], dst_ref=out, send_sem=ssem, recv_sem=rsem,
                                          device_id=(x, y, 1 - c), device_id_type=MESH)
        cp.start()
        cp.wait()

    return pl.pallas_call(
        body, name=name, in_specs=[ANY], out_specs=ANY, out_shape=jax.ShapeDtypeStruct(g.shape[1:], g.dtype),
        scratch_shapes=[pltpu.SemaphoreType.DMA, pltpu.SemaphoreType.DMA],
    )(g)


def _chip_exchange(p, name):
    def body(src, out, ssem, rsem):
        x, y, c, chips = _place()
        cps = [pltpu.make_async_remote_copy(src_ref=src.at[2 * cx + cy], dst_ref=out.at[j], send_sem=ssem.at[j],
                                            recv_sem=rsem.at[j], device_id=(cx, cy, c), device_id_type=MESH)
               for j, (cx, cy) in enumerate(chips)]
        for cp in cps:
            cp.start()
        for cp in cps:
            cp.wait()

    return pl.pallas_call(
        body, name=name, in_specs=[ANY], out_specs=ANY, out_shape=jax.ShapeDtypeStruct((3,) + p.shape[1:], p.dtype),
        scratch_shapes=[pltpu.SemaphoreType.DMA((3,)), pltpu.SemaphoreType.DMA((3,))],
    )(p)


def _pair_share(r, name):
    def body(src, out, ssem, rsem):
        x, y, c, _ = _place()
        cp = pltpu.make_async_remote_copy(src_ref=src, dst_ref=out, send_sem=ssem, recv_sem=rsem,
                                          device_id=(x, y, 1 - c), device_id_type=MESH)
        cp.start()
        cp.wait()

    theirs = pl.pallas_call(
        body, name=name, in_specs=[ANY], out_specs=ANY, out_shape=jax.ShapeDtypeStruct(r.shape, r.dtype),
        scratch_shapes=[pltpu.SemaphoreType.DMA, pltpu.SemaphoreType.DMA],
    )(r)
    first = lax.axis_index("c") == 0
    return jnp.stack([jnp.where(first, r, theirs), jnp.where(first, theirs, r)])


def _sum_pair(g, got, sel, name):
    _, four, rows, cols = g.shape
    tile = _pick(rows, (64, 32, 16, 8))

    def body(sel_ref, a_ref, b_ref, o_ref):
        o_ref[...] = (a_ref[...].astype(F32) + b_ref[...].astype(F32)).astype(BF16)

    blk = pl.BlockSpec((None, tile, cols), lambda k, i, s: (k, i, 0))
    return pl.pallas_call(
        body, name=name, out_shape=jax.ShapeDtypeStruct((four, rows, cols), BF16),
        grid_spec=pltpu.PrefetchScalarGridSpec(
            num_scalar_prefetch=1, grid=(four, rows // tile),
            in_specs=[pl.BlockSpec((None, None, tile, cols), lambda k, i, s: (s[0], k, i, 0)), blk], out_specs=blk),
        compiler_params=_params(("parallel", "parallel")),
    )(sel, g, got)


def _sum4(a, b, sel, name):
    _, rows, cols = a.shape
    tile = _pick(rows, (64, 32, 16, 8))

    def body(sel_ref, a_ref, b0_ref, b1_ref, b2_ref, o_ref):
        acc = a_ref[...].astype(F32) + b0_ref[...].astype(F32)
        acc = acc + b1_ref[...].astype(F32)
        o_ref[...] = acc + b2_ref[...].astype(F32)

    bspec = [pl.BlockSpec((None, tile, cols), lambda i, s, j=j: (j, i, 0)) for j in range(3)]
    return pl.pallas_call(
        body, name=name, out_shape=jax.ShapeDtypeStruct((rows, cols), F32),
        grid_spec=pltpu.PrefetchScalarGridSpec(
            num_scalar_prefetch=1, grid=(rows // tile,),
            in_specs=[pl.BlockSpec((None, tile, cols), lambda i, s: (s[0], i, 0))] + bspec,
            out_specs=pl.BlockSpec((tile, cols), lambda i, s: (i, 0))),
        compiler_params=_params(("parallel",)),
    )(sel, a, b, b, b)


def _reduce_scatter(g, name):
    c = lax.axis_index("c").astype(jnp.int32).reshape(1)
    k = (2 * lax.axis_index("x") + lax.axis_index("y")).astype(jnp.int32).reshape(1)
    got = _pair_swap(g, name + "_pair")
    part = _sum_pair(g, got, c, name + "_add2")
    others = _chip_exchange(part, name + "_chips")
    total = _sum4(part, others, k, name + "_add4")
    return _pair_share(total, name + "_share")


def _all_reduce_small(buf, name):
    rows = buf.shape[0]

    def body(src, out, slots, ssem, rsem):
        x, y, c, _ = _place()
        me = 4 * x + 2 * y + c
        slots[me] = src[...]
        cps = []
        for j in range(1, 8):
            px, py, pc = x ^ (j >> 2), y ^ ((j >> 1) & 1), c ^ (j & 1)
            cps.append(pltpu.make_async_remote_copy(src_ref=src, dst_ref=slots.at[me], send_sem=ssem.at[j - 1],
                                                    recv_sem=rsem.at[j - 1], device_id=(px, py, pc), device_id_type=MESH))
        for cp in cps:
            cp.start()
        for j in range(1, 8):
            peer = me ^ j
            pltpu.make_async_remote_copy(src_ref=src, dst_ref=slots.at[peer], send_sem=ssem.at[j - 1], recv_sem=rsem.at[j - 1],
                                         device_id=(x, y, c), device_id_type=MESH).wait_recv()
        for cp in cps:
            cp.wait_send()
        acc = slots[0]
        for d in range(1, 8):
            acc = acc + slots[d]
        out[...] = acc

    vm = pl.BlockSpec(memory_space=pltpu.VMEM)
    return pl.pallas_call(
        body, name=name, in_specs=[vm], out_specs=vm, out_shape=jax.ShapeDtypeStruct((rows, LANES), F32),
        scratch_shapes=[pltpu.VMEM((8, rows, LANES), F32), pltpu.SemaphoreType.DMA((7,)), pltpu.SemaphoreType.DMA((7,))],
    )(buf)


def _pack(arrs):
    flat = jnp.concatenate([a.astype(F32).reshape(-1) for a in arrs])
    n = flat.shape[0]
    padded = -(-n // (8 * LANES)) * (8 * LANES)
    return jnp.pad(flat, (0, padded - n)).reshape(padded // LANES, LANES)


def _unpack(buf, like):
    flat = buf.reshape(-1)
    out, pos = [], 0
    for a in like:
        out.append(flat[pos:pos + a.size].reshape(a.shape))
        pos += a.size
    return out


def _stride(t2d, dil):
    t, w = t2d.shape
    b = t // SEQ
    return t2d.reshape(b, SEQ // dil, dil, w).transpose(0, 2, 1, 3).reshape(b * dil, SEQ // dil, w)


def _unstride(t3d, dil):
    bb, n, w = t3d.shape
    b = bb // dil
    return t3d.reshape(b, dil, n, w).transpose(0, 2, 1, 3).reshape(b * SEQ, w)


def _stat_cols(st, dil, heads):
    s3 = _stride(st, dil)
    return s3.transpose(0, 2, 1)[..., None]


def _stat_rows(col):
    bb, h, n, _ = col.shape
    return col.reshape(bb, h, 1, n)


def _scan_params(dtp, alog, dm):
    t = dtp.shape[0]
    g, hg = SSD_GROUPS, dm["HG"]
    dt4 = dtp[:, :dm["H2"]].reshape(t, 2, g, hg)
    dtg = dt4.transpose(1, 2, 0, 3)
    dttg = jnp.pad(dt4.transpose(1, 2, 3, 0), ((0, 0), (0, 0), (0, 8 - hg), (0, 0)))
    al = alog.reshape(2, g, 1, hg)
    alt = jnp.pad(alog.reshape(2, g, hg, 1), ((0, 0), (0, 0), (0, 8 - hg), (0, 0)))
    return dtg, dttg, al, alt


def _layer_fwd(x, wl, tabs, dm, li):
    d = dm["D"]
    nm = f"l{li}_"
    h = _rms_fwd(x, wl["g_mix"], nm + "rms1")
    proj = _mm(h, wl["w_main"], name=nm + "proj")
    dtr = _mm(h, wl["w_dt"], out_dtype=F32, name=nm + "proj_dt")
    cpre, u = _conv_fwd(proj, dm["OFF_XBC"], wl["conv_w"], wl["conv_b"], dm["XBC"], nm + "conv")
    dtp = _dt_prep(dtr, wl["dt_bias"], nm + "dt")
    sp = _scan_params(dtp, wl["a_log"], dm)
    y_f, s_f = _scan_fwd(u, *sp, 0, dm, nm + "scan_f")
    y_b, s_b = _scan_fwd(u, *sp, 1, dm, nm + "scan_b")
    y_a = _ssd_out_fwd(y_f, y_b, u, proj, wl["d_cols"], wl["ssd_norm"], dm["HI"], nm + "ssd_out")
    qkv = _rope(proj, dm["OFF_QKV"], dm["QW"] // HEAD_DIM, tabs[0], dm, nm + "rope")
    ng, dw = dm["NG"], dm["DW"]
    outs, lses, xgs = [], [], []
    for gi, (window, dil) in enumerate(DIL_PATTERNS):
        cols = [qkv[:, s * ng * dw + gi * dw:s * ng * dw + (gi + 1) * dw] for s in range(3)]
        xg = _stride(jnp.concatenate(cols, axis=1), dil)
        o, lse = _attn_fwd(xg, xg, xg, 0, 1, 2, DIL_HEADS, 1, SEQ // dil, window // (2 * dil), None, F32, nm + f"dil{gi}")
        xgs.append(xg)
        outs.append(_unstride(o, dil))
        lses.append(_unstride(lse[..., 0].transpose(0, 2, 1), dil))
    y_bm, lse_tot = _dil_combine(outs, lses, nm + "dil_mix")
    bsz = x.shape[0] // SEQ
    xw = qkv[:, dm["QKVD"]:].reshape(bsz, SEQ, dm["WQ"] + 2 * dm["WK"])
    rep = WIN_Q_HEADS // WIN_KV_HEADS
    y_c3, lse_w = _attn_fwd(xw, xw, xw, 0, rep, rep + 1, WIN_Q_HEADS, rep, SEQ, WIN_HALF,
                            wl["sink"].reshape(WIN_Q_HEADS, 1, 1), BF16, nm + "win")
    y_c = y_c3.reshape(x.shape[0], dm["WQ"])
    pa = _mm(y_a, wl["w_a"], out_dtype=F32, name=nm + "pa")
    pb = _mm(y_bm, wl["w_b"], out_dtype=F32, name=nm + "pb")
    pc = _mm(y_c, wl["w_c"], out_dtype=F32, name=nm + "pc")
    merged = _gate_fwd(pa, pb, pc, proj, dm["OFF_GATE"], d, nm + "gate")
    x1 = _mm(merged, wl["w_out"], add=x, out_dtype=F32, name=nm + "out")
    hm = _rms_fwd(x1, wl["g_mlp"], nm + "rms2")
    up, act = _mm(hm, wl["w_up"], epi="relu2", name=nm + "up")
    x2 = _mm(act, wl["w_down"], add=x1, out_dtype=F32, name=nm + "down")
    saved = dict(x=x, h=h, proj=proj, dtr=dtr, cpre=cpre, u=u, dtp=dtp, y_f=y_f, y_b=y_b, s_f=s_f, s_b=s_b, y_a=y_a,
                 xw=xw, xgs=xgs, y_bm=y_bm, lse_tot=lse_tot, y_c=y_c, lse_w=lse_w, pa=pa, pb=pb, pc=pc,
                 merged=merged, x1=x1, hm=hm, up=up, act=act)
    return x2, saved


def _layer_bwd(dx2, wl, sv, tabs, dm, li):
    d = dm["D"]
    t = dx2.shape[0]
    bsz = t // SEQ
    nm = f"l{li}b_"
    gr = {}
    dup = _mm(dx2, wl["w_down"], tb=True, aux=sv["up"], epi="relu2_bwd", name=nm + "dup")
    gr["w_down"] = _mm(sv["act"], dx2, ta=True, name=nm + "gw_down")
    dhm = _mm(dup, wl["w_up"], tb=True, out_dtype=F32, name=nm + "dhm")
    gr["w_up"] = _mm(sv["hm"], dup, ta=True, name=nm + "gw_up")
    dx1, gmlp = _rms_bwd(sv["x1"], wl["g_mlp"], dhm, dx2, nm + "rms2")
    gr["g_mlp"] = gmlp[0]
    dmerged = _mm(dx1, wl["w_out"], tb=True, out_dtype=F32, name=nm + "dmerged")
    gr["w_out"] = _mm(sv["merged"], dx1, ta=True, name=nm + "gw_out")
    dpa, dpb, dpc, dg0, dg1, dg2 = _gate_bwd(dmerged, sv["pa"], sv["pb"], sv["pc"], sv["proj"], dm["OFF_GATE"], d, nm + "gate")
    dya = _mm(dpa, wl["w_a"], tb=True, out_dtype=F32, name=nm + "dya")
    gr["w_a"] = _mm(sv["y_a"], dpa, ta=True, name=nm + "gw_a")
    dyb = _mm(dpb, wl["w_b"], tb=True, out_dtype=F32, name=nm + "dyb")
    gr["w_b"] = _mm(sv["y_bm"], dpb, ta=True, name=nm + "gw_b")
    dyc = _mm(dpc, wl["w_c"], tb=True, out_dtype=F32, name=nm + "dyc")
    gr["w_c"] = _mm(sv["y_c"], dpc, ta=True, name=nm + "gw_c")
    ng, dw = dm["NG"], dm["DW"]
    xw = sv["xw"]
    rep = WIN_Q_HEADS // WIN_KV_HEADS
    delta_w = _head_dots(dyc, sv["y_c"], WIN_Q_HEADS, nm + "win_delta")
    dl_col = _stat_cols(delta_w, 1, WIN_Q_HEADS)
    lse_w = sv["lse_w"]
    dyc3 = dyc.reshape(bsz, SEQ, dm["WQ"])
    wargs = (0, rep, rep + 1, WIN_Q_HEADS, rep, SEQ, WIN_HALF)
    dq_w = _attn_dq(xw, xw, xw, dyc3, lse_w, dl_col, *wargs, nm + "win_dq")
    dk_w, dv_w = _attn_dkv(xw, xw, xw, dyc3, _stat_rows(lse_w), _stat_rows(dl_col), *wargs, nm + "win_dkv")
    lse_w2 = lse_w[..., 0].transpose(0, 2, 1).reshape(t, WIN_Q_HEADS)
    gr["sink"] = _sink_grad(lse_w2, delta_w, wl["sink"], nm + "sink")[0]
    delta_d = _head_dots(dyb, sv["y_bm"], DIL_HEADS, nm + "dil_delta")
    dqs, dks, dvs = [], [], []
    for gi, (window, dil) in enumerate(DIL_PATTERNS):
        xg = sv["xgs"][gi]
        n = SEQ // dil
        do_g = _stride(dyb, dil)
        lse_c = _stat_cols(sv["lse_tot"], dil, DIL_HEADS)
        dl_c = _stat_cols(delta_d, dil, DIL_HEADS)
        dargs = (0, 1, 2, DIL_HEADS, 1, n, window // (2 * dil))
        dq = _attn_dq(xg, xg, xg, do_g, lse_c, dl_c, *dargs, nm + f"dil{gi}_dq")
        dk, dv = _attn_dkv(xg, xg, xg, do_g, _stat_rows(lse_c), _stat_rows(dl_c), *dargs, nm + f"dil{gi}_dkv")
        dqs.append(_unstride(dq, dil))
        dks.append(_unstride(dk, dil))
        dvs.append(_unstride(dv, dil))
    dqkv_r = jnp.concatenate(dqs + dks + dvs + [dq_w.reshape(t, dm["WQ"]), dk_w.reshape(t, dm["WK"]), dv_w.reshape(t, dm["WK"])],
                             axis=1)
    dqkv = _rope(dqkv_r, 0, dm["QW"] // HEAD_DIM, tabs[1], dm, nm + "rope")
    hi, gn = dm["HI"], dm["GN"]
    dyt, dxs0, dz, gnorm, dd_cols = _ssd_out_bwd(dya, sv["y_f"], sv["y_b"], sv["u"], sv["proj"], wl["d_cols"],
                                                          wl["ssd_norm"], hi, nm + "ssd_out")
    gr["ssd_norm"] = gnorm[0]
    gr["d_skip"] = dd_cols.reshape(SSD_HEADS, SSD_HEAD_DIM).sum(axis=1)
    sp = _scan_params(sv["dtp"], wl["a_log"], dm)
    dxs1, db1, dc1, rq_f, xdx_f = _scan_bwd(sv["u"], *sp, dyt, sv["s_f"], (dxs0, None, None), 0, dm, nm + "scan_f")
    dxs2, db2, dc2, rq_r, xdx_r = _scan_bwd(sv["u"], *sp, dyt, sv["s_b"], (dxs1, db1, dc1), 1, dm, nm + "scan_b")

    def heads(a):
        return a.transpose(1, 0, 2).reshape(t, SSD_HEADS)

    zpad = jnp.zeros((t, LANES - dm["H2"]), F32)
    zh = jnp.zeros((t, SSD_HEADS), F32)
    rqf_p = jnp.concatenate([heads(rq_f), zh, zpad], axis=1)
    rqr_p = jnp.concatenate([zh, heads(rq_r), zpad], axis=1)
    xdx_p = jnp.concatenate([heads(xdx_f), heads(xdx_r), zpad], axis=1)
    ddtr, dbias, dalog = _ssd_param_bwd(rqf_p, rqr_p, xdx_p, sv["dtp"], sv["dtr"], wl["dt_bias"], wl["a_log_p"], nm + "ssd_par")
    gr["dt_bias"] = dbias[0, :dm["H2"]].reshape(2, SSD_HEADS)
    gr["a_log"] = dalog[0, :dm["H2"]].reshape(2, SSD_HEADS)
    du = jnp.concatenate([dxs2, db2, dc2], axis=1)
    dxbc, gr["conv_w"], gcb = _conv_bwd(du, sv["cpre"], sv["proj"], dm["OFF_XBC"], wl["conv_w"], nm + "conv")
    gr["conv_b"] = gcb[0]
    dproj = jnp.concatenate([dz, dxbc, dqkv, dg0, dg1, dg2], axis=1)
    dh_dt = _mm(ddtr, wl["w_dt"], tb=True, out_dtype=F32, name=nm + "dh_dt")
    dh = _mm(dproj, wl["w_main"], tb=True, add=dh_dt, out_dtype=F32, name=nm + "dh")
    gw_main = _mm(sv["h"], dproj, ta=True, name=nm + "gw_main")
    gw_dt = _mm(sv["h"], ddtr, ta=True, name=nm + "gw_dt")
    o1, h2 = dm["OFF_QKV"], dm["H2"]
    n_in = dm["NM"] + h2
    cs = n_in // 4
    shards = []
    for k in range(4):
        pieces = []
        for lo, hi, src, shift in ((0, o1, gw_main, 0), (o1, o1 + h2, gw_dt, o1), (o1 + h2, n_in, gw_main, h2)):
            a, b = max(lo, k * cs), min(hi, (k + 1) * cs)
            if a < b:
                pieces.append(src[:, a - shift:b - shift])
        shards.append(jnp.concatenate(pieces, axis=1))
    gr["w_in"] = jnp.stack(shards)
    dx, gmix = _rms_bwd(sv["x"], wl["g_mix"], dh, dx1, nm + "rms1")
    gr["g_mix"] = gmix[0]
    return dx, gr


def _layer_weights(full, li, dm):
    st = full["w_in"]
    o1 = dm["OFF_QKV"]
    h2 = dm["H2"]
    d = dm["D"]
    cs = st.shape[-1]

    def cols(lo, hi):
        out = []
        for k in range(4):
            a, b = max(lo, k * cs), min(hi, (k + 1) * cs)
            if a < b:
                out.append(st[k, li][:, a - k * cs:b - k * cs])
        return out

    wl = dict(
        w_main=jnp.concatenate(cols(0, o1) + cols(o1 + h2, 4 * cs), axis=1),
        w_dt=jnp.pad(jnp.concatenate(cols(o1, o1 + h2), axis=1), ((0, 0), (0, LANES - h2))),
        w_a=full["w_a"][li], w_b=full["w_b"][li], w_c=full["w_c"][li], w_out=full["w_out"][li],
        w_up=full["w_up"][li], w_down=full["w_down"][li],
        conv_w=full["conv_w"][li], conv_b=full["conv_b"][li][None, :],
        g_mix=full["g_mix"][li][None, :], g_mlp=full["g_mlp"][li][None, :], ssd_norm=full["ssd_norm"][li][None, :],
        d_cols=jnp.repeat(full["d_skip"][li], SSD_HEAD_DIM)[None, :],
        sink=full["sink"][li][None, :],
        a_log=full["a_log"][li],
        a_log_p=jnp.pad(full["a_log"][li].reshape(1, h2), ((0, 0), (0, LANES - h2))),
        dt_bias=jnp.pad(full["dt_bias"][li].reshape(1, h2), ((0, 0), (0, LANES - h2))),
    )
    assert wl["w_main"].shape == (d, dm["NM"])
    return wl


def _local_step(x, target, full, depth):
    bsz, seq, d = x.shape
    assert seq == SEQ
    dm = _dims(d)
    assert dm["OFF_GATE"] % d == 0 and dm["HI"] % (dm["HG"] * SSD_HEAD_DIM) == 0 and dm["H2"] <= LANES
    tabs = (_rope_tables(1.0), _rope_tables(-1.0))
    xt = x.reshape(bsz * seq, d)
    wls, saves = [], []
    for li in range(depth):
        wl = _layer_weights(full, li, dm)
        xt, sv = _layer_fwd(xt, wl, tabs, dm, li)
        wls.append(wl)
        saves.append(sv)
    dx, loss, g_final = _loss_head(xt, full["g_final"][None, :], target.reshape(bsz * seq, d), "loss_head")
    grads = [None] * depth
    for li in reversed(range(depth)):
        dx, grads[li] = _layer_bwd(dx, wls[li], saves[li], tabs, dm, li)
    return loss, dx.reshape(bsz, seq, d), grads, g_final[0]


BIG = ("w_in", "w_a", "w_b", "w_c", "w_out", "w_up", "w_down")
COL_SHARDED = ("w_in", "w_b", "w_up")
SMALL = ("g_mix", "conv_w", "conv_b", "dt_bias", "a_log", "d_skip", "ssd_norm", "sink", "g_mlp", "g_final")
ORDER = ("g_mix", "w_in", "conv_w", "conv_b", "dt_bias", "a_log", "d_skip", "ssd_norm", "w_a", "w_b", "w_c", "sink",
         "w_out", "g_mlp", "w_up", "w_down", "g_final")


def _unstack(name, st):
    _, nl, r, c = st.shape
    if name in COL_SHARDED:
        return jnp.moveaxis(st, 0, 2).reshape(nl, r, 4 * c)
    return jnp.moveaxis(st, 0, 1).reshape(nl, 4 * r, c)


def _restack(name, gfull):
    nl, r, c = gfull.shape
    if name in COL_SHARDED:
        return jnp.moveaxis(gfull.reshape(nl, r, 4, c // 4), 2, 1)
    return gfull.reshape(nl, 4, r // 4, c)


def kernel(x, g_mix, w_in, conv_w, conv_b, dt_bias, a_log, d_skip, ssd_norm, w_a, w_b, w_c, sink, w_out, g_mlp, w_up, w_down, g_final, loss_target, m_g_mix, m_w_in, m_conv_w, m_conv_b, m_dt_bias, m_a_log, m_d_skip, m_ssd_norm, m_w_a, m_w_b, m_w_c, m_sink, m_w_out, m_g_mlp, m_w_up, m_w_down, m_g_final, v_g_mix, v_w_in, v_conv_w, v_conv_b, v_dt_bias, v_a_log, v_d_skip, v_ssd_norm, v_w_a, v_w_b, v_w_c, v_sink, v_w_out, v_g_mlp, v_w_up, v_w_down, v_g_final):
    w = dict(g_mix=g_mix, w_in=w_in, conv_w=conv_w, conv_b=conv_b, dt_bias=dt_bias, a_log=a_log, d_skip=d_skip,
             ssd_norm=ssd_norm, w_a=w_a, w_b=w_b, w_c=w_c, sink=sink, w_out=w_out, g_mlp=g_mlp, w_up=w_up, w_down=w_down,
             g_final=g_final)
    m = dict(g_mix=m_g_mix, w_in=m_w_in, conv_w=m_conv_w, conv_b=m_conv_b, dt_bias=m_dt_bias, a_log=m_a_log,
             d_skip=m_d_skip, ssd_norm=m_ssd_norm, w_a=m_w_a, w_b=m_w_b, w_c=m_w_c, sink=m_sink, w_out=m_w_out,
             g_mlp=m_g_mlp, w_up=m_w_up, w_down=m_w_down, g_final=m_g_final)
    v = dict(g_mix=v_g_mix, w_in=v_w_in, conv_w=v_conv_w, conv_b=v_conv_b, dt_bias=v_dt_bias, a_log=v_a_log,
             d_skip=v_d_skip, ssd_norm=v_ssd_norm, w_a=v_w_a, w_b=v_w_b, w_c=v_w_c, sink=v_sink, w_out=v_w_out,
             g_mlp=v_g_mlp, w_up=v_w_up, w_down=v_w_down, g_final=v_g_final)
    depth = w_in.shape[0]
    assert depth == 2
    kchip = 2 * lax.axis_index("x") + lax.axis_index("y")

    full = {n: w[n] for n in SMALL if n != "conv_w"}
    for n in BIG:
        st = _gather_chips(w[n].astype(BF16), "gather_" + n)
        full[n] = st if n == "w_in" else _unstack(n, st)
    cw = _gather_chips(conv_w, "gather_conv_w")
    full["conv_w"] = jnp.moveaxis(cw, 0, 2).reshape(depth, CONV_WIDTH, 4 * conv_w.shape[2])

    loss_part, grad_x, grads, gg_final = _local_step(x, loss_target, full, depth)

    gsh = {}
    for n in BIG:
        gfull = jnp.stack([grads[li][n] for li in range(depth)])
        gsh[n] = _reduce_scatter(gfull if n == "w_in" else _restack(n, gfull), "rs_" + n)
    small_names = [n for n in SMALL if n != "g_final"]
    small_g = [jnp.stack([grads[li][n] for li in range(depth)]) for n in small_names] + [gg_final, loss_part[0, :1]]
    red = _unpack(_all_reduce_small(_pack(small_g), "allreduce_small"), small_g)
    for n, a in zip(small_names + ["g_final"], red):
        gsh[n] = a
    loss = red[-1][0]
    cshard = conv_w.shape[2]
    gsh["conv_w"] = lax.dynamic_slice_in_dim(gsh["conv_w"], kchip * cshard, cshard, axis=2)

    delta, new_m, new_v = {}, {}, {}
    for n in BIG:
        delta[n], new_m[n], new_v[n] = _adamw(w[n], gsh[n], m[n], v[n], "adamw_" + n)
    sm = list(SMALL)
    packed = [_pack([d_[n] for n in sm])[None] for d_ in (w, gsh, m, v)]
    outs = [o[0] for o in _adamw(*packed, "adamw_small")]
    for dst, buf in zip((delta, new_m, new_v), outs):
        for n, a in zip(sm, _unpack(buf, [w[n] for n in sm])):
            dst[n] = a
    return (loss, grad_x, *[gsh[n] for n in ORDER], *[delta[n] for n in ORDER], *[new_m[n] for n in ORDER],
            *[new_v[n] for n in ORDER])
```

```python
import functools
import math

import jax
import jax.numpy as jnp
from jax import lax
from jax.experimental import pallas as pl
from jax.experimental.pallas import tpu as pltpu

F32 = jnp.float32
BF16 = jnp.bfloat16

SEQ = 2048
SSD_HEADS = 32
SSD_HEAD_DIM = 64
SSD_GROUPS = 8
SSD_STATE = 128
SSD_CHUNK = 128
CONV_WIDTH = 5
HEAD_DIM = 128
ROPE_DIM = 32
ROPE_THETA = 500000.0
DIL_PATTERNS = ((128, 1), (512, 4), (2048, 16))
DIL_HEADS = 8
WIN_Q_HEADS = 16
WIN_KV_HEADS = 4
WIN_HALF = 128
EPS = 1e-6
NEG_BIG = -1e30
ADAM_LR = 0.001
ADAM_B1 = 0.9
ADAM_B2 = 0.999
ADAM_EPS = 1e-08
ADAM_WD = 0.01
ADAM_STEP = 10

LANES = 128
ATT_BLK = 128
ROW_TILES = (128, 80, 64, 32, 16, 8)
VMEM_LIMIT = 48 * 1024 * 1024
MESH = pl.DeviceIdType.MESH
HIGHEST = lax.Precision.HIGHEST
NT = (((1,), (1,)), ((), ()))
TN = (((0,), (0,)), ((), ()))
NN = (((1,), (0,)), ((), ()))


def _dims(d_model):
    hi = SSD_HEADS * SSD_HEAD_DIM
    gn = SSD_GROUPS * SSD_STATE
    ng = len(DIL_PATTERNS)
    dw = DIL_HEADS * HEAD_DIM
    wq = WIN_Q_HEADS * HEAD_DIM
    wk = WIN_KV_HEADS * HEAD_DIM
    d = dict(D=d_model, HI=hi, GN=gn, XBC=hi + 2 * gn, H2=2 * SSD_HEADS, NG=ng, DW=dw, WQ=wq, WK=wk,
             QKVD=3 * ng * dw, QW=3 * ng * dw + wq + 2 * wk, HG=SSD_HEADS // SSD_GROUPS)
    d["OFF_XBC"] = hi
    d["OFF_QKV"] = hi + d["XBC"]
    d["OFF_GATE"] = d["OFF_QKV"] + d["QW"]
    d["NM"] = d["OFF_GATE"] + 3 * d_model
    return d


def _pick(n, prefs):
    for p in prefs:
        if n % p == 0:
            return p
    return n


def _params(sem):
    return pltpu.CompilerParams(dimension_semantics=sem, vmem_limit_bytes=VMEM_LIMIT)


def _sigmoid(x):
    return 1.0 / (1.0 + jnp.exp(-x))


def _mm(a, b, *, ta=False, tb=False, add=None, aux=None, epi=None, out_dtype=BF16, name):
    if ta:
        kdim, m = a.shape
    else:
        m, kdim = a.shape
    if tb:
        n, k2 = b.shape
    else:
        k2, n = b.shape
    assert kdim == k2, (a.shape, b.shape, ta, tb)
    tm = _pick(m, (1024, 512, 256, 128, 64, 32, 16, 8))
    tn = _pick(n, (1024, 512, 256, 128))
    tk = _pick(kdim, (2048, 1024, 512, 256, 128))
    nk = kdim // tk
    dims = (((0 if ta else 1,), (1 if tb else 0,)), ((), ()))
    n_in = 2 + (add is not None) + (aux is not None)
    n_out = 2 if epi == "relu2" else 1

    def body(*refs):
        a_ref, b_ref = refs[0], refs[1]
        pos = 2
        add_ref = aux_ref = None
        if add is not None:
            add_ref = refs[pos]
            pos += 1
        if aux is not None:
            aux_ref = refs[pos]
            pos += 1
        out_refs = refs[n_in:n_in + n_out]
        acc_ref = refs[n_in + n_out]
        k = pl.program_id(2)

        @pl.when(k == 0)
        def _():
            acc_ref[...] = jnp.zeros_like(acc_ref)

        acc_ref[...] += lax.dot_general(a_ref[...].astype(BF16), b_ref[...].astype(BF16), dims,
                                        preferred_element_type=F32)

        @pl.when(k == nk - 1)
        def _():
            r = acc_ref[...]
            if add_ref is not None:
                r = r + add_ref[...].astype(F32)
            if epi == "relu2":
                out_refs[0][...] = r.astype(out_refs[0].dtype)
                out_refs[1][...] = jnp.square(jnp.maximum(r, 0.0)).astype(out_refs[1].dtype)
            elif epi == "relu2_bwd":
                out_refs[0][...] = (r * 2.0 * jnp.maximum(aux_ref[...].astype(F32), 0.0)).astype(out_refs[0].dtype)
            else:
                out_refs[0][...] = r.astype(out_refs[0].dtype)

    a_spec = pl.BlockSpec((tk, tm), lambda i, j, k: (k, i)) if ta else pl.BlockSpec((tm, tk), lambda i, j, k: (i, k))
    b_spec = pl.BlockSpec((tn, tk), lambda i, j, k: (j, k)) if tb else pl.BlockSpec((tk, tn), lambda i, j, k: (k, j))
    o_spec = pl.BlockSpec((tm, tn), lambda i, j, k: (i, j))
    in_specs = [a_spec, b_spec]
    args = [a, b]
    if add is not None:
        in_specs.append(o_spec)
        args.append(add)
    if aux is not None:
        in_specs.append(o_spec)
        args.append(aux)
    out_shape = [jax.ShapeDtypeStruct((m, n), out_dtype)] * n_out
    res = pl.pallas_call(
        body, name=name, grid=(m // tm, n // tn, nk), in_specs=in_specs, out_specs=[o_spec] * n_out,
        out_shape=out_shape, scratch_shapes=[pltpu.VMEM((tm, tn), F32)],
        compiler_params=_params(("parallel", "parallel", "arbitrary")),
    )(*args)
    return res if n_out == 2 else res[0]


def _rowwise(body, rows, fulls, outs, accs=(), *, tile, name):
    rows = [r if isinstance(r, tuple) else (r, 0, r.shape[1]) for r in rows]
    nrows = rows[0][0].shape[0]
    assert nrows % tile == 0, (nrows, tile)
    in_specs, args = [], []
    for arr, off, width in rows:
        assert arr.shape[0] == nrows and off % width == 0, (arr.shape, off, width)
        in_specs.append(pl.BlockSpec((tile, width), lambda i, o=off // width: (i, o)))
        args.append(arr)
    for arr in fulls:
        in_specs.append(pl.BlockSpec(arr.shape, lambda i, nd=arr.ndim: (0,) * nd))
        args.append(arr)
    out_specs, out_shape = [], []
    for cols, dt in outs:
        out_specs.append(pl.BlockSpec((tile, cols), lambda i: (i, 0)))
        out_shape.append(jax.ShapeDtypeStruct((nrows, cols), dt))
    for shp, dt in accs:
        out_specs.append(pl.BlockSpec(shp, lambda i, nd=len(shp): (0,) * nd))
        out_shape.append(jax.ShapeDtypeStruct(shp, dt))
    n_in, n_out = len(args), len(outs)

    def wrapped(*refs):
        acc_refs = refs[n_in + n_out:]
        if acc_refs:
            @pl.when(pl.program_id(0) == 0)
            def _():
                for r in acc_refs:
                    r[...] = jnp.zeros_like(r)
        body(*refs)

    return pl.pallas_call(
        wrapped, name=name, grid=(nrows // tile,), in_specs=in_specs, out_specs=out_specs, out_shape=out_shape,
        compiler_params=_params(("arbitrary",)),
    )(*args)


def _rms_fwd(x, g, name):
    def body(x_ref, g_ref, h_ref):
        xv = x_ref[...]
        rstd = lax.rsqrt(jnp.mean(xv * xv, axis=-1, keepdims=True) + EPS)
        h_ref[...] = (xv * rstd * g_ref[...]).astype(BF16)

    return _rowwise(body, [x], [g], [(x.shape[1], BF16)], tile=256, name=name)[0]


def _rms_bwd(x, g, dh, dres, name):
    def body(x_ref, dh_ref, dres_ref, g_ref, dx_ref, dg_ref):
        xv = x_ref[...]
        dv = dh_ref[...]
        rstd = lax.rsqrt(jnp.mean(xv * xv, axis=-1, keepdims=True) + EPS)
        xn = xv * rstd
        dg_ref[...] += jnp.sum(dv * xn, axis=0, keepdims=True)
        dn = dv * g_ref[...]
        dx_ref[...] = dres_ref[...] + rstd * (dn - xn * jnp.mean(dn * xn, axis=-1, keepdims=True))

    d = x.shape[1]
    return _rowwise(body, [x, dh, dres], [g], [(d, F32)], [((1, d), F32)], tile=256, name=name)


def _loss_head(x, g, target, name):
    d = x.shape[1]

    def body(x_ref, t_ref, g_ref, dx_ref, loss_ref, dg_ref):
        xv = x_ref[...]
        rstd = lax.rsqrt(jnp.mean(xv * xv, axis=-1, keepdims=True) + EPS)
        xn = xv * rstd
        err = xn * g_ref[...] - t_ref[...]
        loss_ref[...] += jnp.full((1, LANES), 0.5 / d, F32) * jnp.sum(err * err)
        dy = err * (1.0 / d)
        dg_ref[...] += jnp.sum(dy * xn, axis=0, keepdims=True)
        dn = dy * g_ref[...]
        dx_ref[...] = rstd * (dn - xn * jnp.mean(dn * xn, axis=-1, keepdims=True))

    return _rowwise(body, [x, target], [g], [(d, F32)], [((1, LANES), F32), ((1, d), F32)], tile=256, name=name)


def _rope_tables(sign):
    half = ROPE_DIM // 2
    inv = ROPE_THETA ** (-jnp.arange(0, ROPE_DIM, 2, dtype=F32) / ROPE_DIM)
    ang = jnp.arange(SEQ, dtype=F32)[:, None] * inv[None, :]
    cos, sin = jnp.cos(ang), jnp.sin(ang) * sign
    zeros = jnp.zeros((SEQ, HEAD_DIM - ROPE_DIM), F32)
    zh = jnp.zeros((SEQ, half), F32)
    c = jnp.concatenate([cos, cos, zeros + 1.0], axis=1)
    s_up = jnp.concatenate([-sin, zh, zeros], axis=1)
    s_dn = jnp.concatenate([zh, sin, zeros], axis=1)
    return c, s_up, s_dn


def _rope(src, off, nblk, tabs, dm, name):
    t = src.shape[0]
    tq = 256
    half = ROPE_DIM // 2
    win0 = 3 * dm["NG"] * DIL_HEADS
    win1 = win0 + WIN_Q_HEADS + WIN_KV_HEADS
    qw = nblk * HEAD_DIM
    assert nblk == dm["QW"] // HEAD_DIM
    wb = next(c for c in (1024, 768, 512, 384, 256, 128) if off % c == 0 and qw % c == 0)
    reps = wb // HEAD_DIM
    sb = SEQ // tq
    flag = (jnp.arange(qw, dtype=jnp.int32) // HEAD_DIM < win1).astype(F32)[None, :]

    def body(x_ref, c_ref, up_ref, dn_ref, f_ref, o_ref):
        xv = x_ref[...].astype(F32)

        def wide(r):
            v = r[...]
            return v if reps == 1 else jnp.concatenate([v] * reps, axis=1)

        rot = xv * wide(c_ref) + pltpu.roll(xv, wb - half, 1) * wide(up_ref) + pltpu.roll(xv, half, 1) * wide(dn_ref)
        o_ref[...] = jnp.where(f_ref[...] > 0.5, rot, xv).astype(BF16)

    tab_spec = pl.BlockSpec((tq, HEAD_DIM), lambda i, j: (i % sb, 0))
    return pl.pallas_call(
        body, name=name, grid=(t // tq, qw // wb),
        in_specs=[pl.BlockSpec((tq, wb), lambda i, j, o=off // wb: (i, o + j)), tab_spec, tab_spec, tab_spec,
                  pl.BlockSpec((1, wb), lambda i, j: (0, j))],
        out_specs=pl.BlockSpec((tq, wb), lambda i, j: (i, j)),
        out_shape=jax.ShapeDtypeStruct((t, qw), BF16),
        compiler_params=_params(("parallel", "parallel")),
    )(src, *tabs, flag)


def _band_mask(rows_start, cols_start, nrows, ncols, w, n, rows_are_q):
    r = rows_start + lax.broadcasted_iota(jnp.int32, (nrows, ncols), 0)
    c = cols_start + lax.broadcasted_iota(jnp.int32, (nrows, ncols), 1)
    del rows_are_q
    return (jnp.abs(r - c) <= w) & (c >= 0) & (c < n)


def _nbr_specs(make, nb):
    if nb == 1:
        return [make(lambda i: i)]
    return [make(lambda i: jnp.maximum(i - 1, 0)), make(lambda i: i), make(lambda i: jnp.minimum(i + 1, nb - 1))]


def _cat(refs, axis):
    vals = [r[...] for r in refs]
    return vals[0] if len(vals) == 1 else jnp.concatenate(vals, axis=axis)


def _head(ref, h):
    return ref[:, h * HEAD_DIM:(h + 1) * HEAD_DIM]


def _head_cat(refs, h, axis=0):
    vals = [_head(r, h) for r in refs]
    return vals[0] if len(vals) == 1 else jnp.concatenate(vals, axis=axis)


def _attn_fwd(qa, ka, va, qb, kb, vb, hq, rep, n, w, sink, out_dtype, name):
    bb = qa.shape[0]
    blk = ATT_BLK
    nb = n // blk
    nk = 1 if nb == 1 else 3
    hkv = hq // rep
    scale = HEAD_DIM ** -0.5
    has_sink = sink is not None

    def body(*refs):
        q_ref = refs[0]
        k_refs = refs[1:1 + nk]
        v_refs = refs[1 + nk:1 + 2 * nk]
        pos = 1 + 2 * nk
        sink_ref = refs[pos] if has_sink else None
        o_ref, lse_ref = refs[pos + has_sink], refs[pos + has_sink + 1]
        i = pl.program_id(1)
        k0 = (i - 1) * blk if nk == 3 else i * blk
        valid = _band_mask(i * blk, k0, blk, nk * blk, w, n, True)
        for g in range(hkv):
            kcat = _head_cat(k_refs, g)
            vcat = _head_cat(v_refs, g)
            for r in range(rep):
                h = g * rep + r
                s = lax.dot_general(_head(q_ref, h), kcat, NT, preferred_element_type=F32) * scale
                s = jnp.where(valid, s, NEG_BIG)
                m = jnp.max(s, axis=1, keepdims=True)
                if has_sink:
                    m = jnp.maximum(m, sink_ref[h])
                p = jnp.exp(s - m)
                l = jnp.sum(p, axis=1, keepdims=True)
                if has_sink:
                    l = l + jnp.exp(sink_ref[h] - m)
                o = lax.dot_general(p.astype(BF16), vcat, NN, preferred_element_type=F32) / l
                o_ref[:, h * HEAD_DIM:(h + 1) * HEAD_DIM] = o.astype(o_ref.dtype)
                lse_ref[h] = m + jnp.log(l)

    def mk(col, width):
        return lambda f: pl.BlockSpec((None, blk, width), lambda b, i, f=f: (b, f(i), col))

    qw, kw = hq * HEAD_DIM, hkv * HEAD_DIM
    in_specs = [pl.BlockSpec((None, blk, qw), lambda b, i: (b, i, qb))]
    in_specs += _nbr_specs(mk(kb, kw), nb) + _nbr_specs(mk(vb, kw), nb)
    args = [qa] + [ka] * nk + [va] * nk
    if has_sink:
        in_specs.append(pl.BlockSpec((hq, 1, 1), lambda b, i: (0, 0, 0)))
        args.append(sink)
    return pl.pallas_call(
        body, name=name, grid=(bb, nb), in_specs=in_specs,
        out_specs=[pl.BlockSpec((None, blk, qw), lambda b, i: (b, i, 0)),
                   pl.BlockSpec((None, hq, blk, 1), lambda b, i: (b, 0, i, 0))],
        out_shape=[jax.ShapeDtypeStruct((bb, n, qw), out_dtype), jax.ShapeDtypeStruct((bb, hq, n, 1), F32)],
        compiler_params=_params(("parallel", "parallel")),
    )(*args)


def _attn_dq(qa, ka, va, do, lse, delta, qb, kb, vb, hq, rep, n, w, name):
    bb = qa.shape[0]
    blk = ATT_BLK
    nb = n // blk
    nk = 1 if nb == 1 else 3
    hkv = hq // rep
    scale = HEAD_DIM ** -0.5

    def body(*refs):
        q_ref = refs[0]
        k_refs = refs[1:1 + nk]
        v_refs = refs[1 + nk:1 + 2 * nk]
        do_ref, lse_ref, dl_ref, dq_ref = refs[1 + 2 * nk:]
        i = pl.program_id(1)
        k0 = (i - 1) * blk if nk == 3 else i * blk
        valid = _band_mask(i * blk, k0, blk, nk * blk, w, n, True)
        for g in range(hkv):
            kcat = _head_cat(k_refs, g)
            vcat = _head_cat(v_refs, g)
            for r in range(rep):
                h = g * rep + r
                s = lax.dot_general(_head(q_ref, h), kcat, NT, preferred_element_type=F32) * scale
                p = jnp.exp(jnp.where(valid, s, NEG_BIG) - lse_ref[h])
                dp = lax.dot_general(_head(do_ref, h).astype(BF16), vcat, NT, preferred_element_type=F32)
                ds = p * (dp - dl_ref[h])
                dq = lax.dot_general(ds.astype(BF16), kcat, NN, preferred_element_type=F32) * scale
                dq_ref[:, h * HEAD_DIM:(h + 1) * HEAD_DIM] = dq.astype(BF16)

    def mk(col, width):
        return lambda f: pl.BlockSpec((None, blk, width), lambda b, i, f=f: (b, f(i), col))

    qw, kw = hq * HEAD_DIM, hkv * HEAD_DIM
    col_spec = pl.BlockSpec((None, hq, blk, 1), lambda b, i: (b, 0, i, 0))
    in_specs = [pl.BlockSpec((None, blk, qw), lambda b, i: (b, i, qb))]
    in_specs += _nbr_specs(mk(kb, kw), nb) + _nbr_specs(mk(vb, kw), nb)
    in_specs += [pl.BlockSpec((None, blk, qw), lambda b, i: (b, i, 0)), col_spec, col_spec]
    return pl.pallas_call(
        body, name=name, grid=(bb, nb), in_specs=in_specs,
        out_specs=pl.BlockSpec((None, blk, qw), lambda b, i: (b, i, 0)),
        out_shape=jax.ShapeDtypeStruct((bb, n, qw), BF16),
        compiler_params=_params(("parallel", "parallel")),
    )(qa, *([ka] * nk), *([va] * nk), do, lse, delta)


def _attn_dkv(qa, ka, va, do, lse_row, delta_row, qb, kb, vb, hq, rep, n, w, name):
    bb = qa.shape[0]
    blk = ATT_BLK
    nb = n // blk
    nq = 1 if nb == 1 else 3
    hkv = hq // rep
    scale = HEAD_DIM ** -0.5

    def body(*refs):
        k_ref, v_ref = refs[0], refs[1]
        q_refs = refs[2:2 + nq]
        do_refs = refs[2 + nq:2 + 2 * nq]
        lse_refs = refs[2 + 2 * nq:2 + 3 * nq]
        dl_refs = refs[2 + 3 * nq:2 + 4 * nq]
        dk_ref, dv_ref = refs[2 + 4 * nq:]
        j = pl.program_id(1)
        q0 = (j - 1) * blk if nq == 3 else j * blk
        valid = _band_mask(j * blk, q0, blk, nq * blk, w, n, False)
        for g in range(hkv):
            kg, vg = _head(k_ref, g), _head(v_ref, g)
            dk = jnp.zeros((blk, HEAD_DIM), F32)
            dv = jnp.zeros((blk, HEAD_DIM), F32)
            for r in range(rep):
                h = g * rep + r
                qcat = _head_cat(q_refs, h)
                docat = _head_cat(do_refs, h).astype(BF16)
                lse = lse_refs[0][h] if nq == 1 else jnp.concatenate([lr[h] for lr in lse_refs], axis=1)
                dl = dl_refs[0][h] if nq == 1 else jnp.concatenate([dr[h] for dr in dl_refs], axis=1)
                st = lax.dot_general(kg, qcat, NT, preferred_element_type=F32) * scale
                pt = jnp.exp(jnp.where(valid, st, NEG_BIG) - lse)
                dv = dv + lax.dot_general(pt.astype(BF16), docat, NN, preferred_element_type=F32)
                dpt = lax.dot_general(vg, docat, NT, preferred_element_type=F32)
                dst = pt * (dpt - dl)
                dk = dk + lax.dot_general(dst.astype(BF16), qcat, NN, preferred_element_type=F32) * scale
            dk_ref[:, g * HEAD_DIM:(g + 1) * HEAD_DIM] = dk.astype(BF16)
            dv_ref[:, g * HEAD_DIM:(g + 1) * HEAD_DIM] = dv.astype(BF16)

    qw, kw = hq * HEAD_DIM, hkv * HEAD_DIM

    def mkq(col):
        return lambda f: pl.BlockSpec((None, blk, qw), lambda b, j, f=f: (b, f(j), col))

    def mkrow(f):
        return pl.BlockSpec((None, hq, 1, blk), lambda b, j, f=f: (b, 0, 0, f(j)))

    in_specs = [pl.BlockSpec((None, blk, kw), lambda b, j: (b, j, kb)), pl.BlockSpec((None, blk, kw), lambda b, j: (b, j, vb))]
    in_specs += _nbr_specs(mkq(qb), nb) + _nbr_specs(mkq(0), nb) + _nbr_specs(mkrow, nb) + _nbr_specs(mkrow, nb)
    o_spec = pl.BlockSpec((None, blk, kw), lambda b, j: (b, j, 0))
    return pl.pallas_call(
        body, name=name, grid=(bb, nb), in_specs=in_specs, out_specs=[o_spec, o_spec],
        out_shape=[jax.ShapeDtypeStruct((bb, n, kw), BF16)] * 2,
        compiler_params=_params(("parallel", "parallel")),
    )(ka, va, *([qa] * nq), *([do] * nq), *([lse_row] * nq), *([delta_row] * nq))


def _head_expand(v, nh, width):
    lane_head = lax.broadcasted_iota(jnp.int32, (1, nh * width), 1) >> int(math.log2(width))
    out = jnp.zeros((v.shape[0], nh * width), F32)
    for j in range(nh):
        out = jnp.where(lane_head == j, v[:, j:j + 1], out)
    return out


def _head_sums(m, nh, width):
    lane_head = lax.broadcasted_iota(jnp.int32, (1, nh * width), 1) >> int(math.log2(width))
    col = lax.broadcasted_iota(jnp.int32, (1, nh), 1)
    out = jnp.zeros((m.shape[0], nh), F32)
    for j in range(nh):
        sj = jnp.sum(jnp.where(lane_head == j, m, 0.0), axis=1, keepdims=True)
        out = jnp.where(col == j, sj, out)
    return out


def _head_dots(a, b, nh, name):
    def body(a_ref, b_ref, o_ref):
        o_ref[...] = _head_sums(a_ref[...].astype(F32) * b_ref[...].astype(F32), nh, HEAD_DIM)

    return _rowwise(body, [a, b], [], [(nh, F32)], tile=256, name=name)[0]


def _dil_combine(outs, lses, name):
    ng = len(outs)

    def body(*refs):
        o_refs, l_refs = refs[:ng], refs[ng:2 * ng]
        y_ref, lt_ref = refs[2 * ng], refs[2 * ng + 1]
        ls = [r[...] for r in l_refs]
        m = functools.reduce(jnp.maximum, ls)
        es = [jnp.exp(v - m) for v in ls]
        tot = functools.reduce(jnp.add, es)
        acc = jnp.zeros(o_refs[0].shape, F32)
        for o_ref, e in zip(o_refs, es):
            acc = acc + _head_expand(e / tot, DIL_HEADS, HEAD_DIM) * o_ref[...]
        y_ref[...] = acc.astype(BF16)
        lt_ref[...] = m + jnp.log(tot)

    dw = outs[0].shape[1]
    return _rowwise(body, list(outs) + list(lses), [], [(dw, BF16), (DIL_HEADS, F32)], tile=256, name=name)


def _sink_grad(lse, delta, sink, name):
    def body(l_ref, d_ref, s_ref, o_ref):
        o_ref[...] -= jnp.sum(jnp.exp(s_ref[...] - l_ref[...]) * d_ref[...], axis=0, keepdims=True)

    return _rowwise(body, [lse, delta], [sink], [], [((1, lse.shape[1]), F32)], tile=512, name=name)[0]


def _shift_rows(x, d, nrows):
    if d == 0:
        return x
    rolled = pltpu.roll(x, (-d) % nrows, 0)
    row = lax.broadcasted_iota(jnp.int32, x.shape, 0)
    ok = (row + d >= 0) & (row + d < nrows)
    return jnp.where(ok, rolled, 0.0)


def _conv_fwd(proj, off, conv_w, conv_b, xbc, name):
    t = proj.shape[0]
    tc = _pick(xbc, (256, 128))
    assert off % tc == 0
    pad = (CONV_WIDTH - 1) // 2

    def body(x_ref, w_ref, b_ref, c_ref, u_ref):
        xv = x_ref[...].astype(F32)
        acc = jnp.zeros_like(xv) + b_ref[...]
        for k in range(CONV_WIDTH):
            acc = acc + w_ref[k:k + 1, :] * _shift_rows(xv, k - pad, SEQ)
        c_ref[...] = acc.astype(BF16)
        u_ref[...] = (acc * _sigmoid(acc)).astype(BF16)

    o_spec = pl.BlockSpec((SEQ, tc), lambda b, j: (b, j))
    return pl.pallas_call(
        body, name=name, grid=(t // SEQ, xbc // tc),
        in_specs=[pl.BlockSpec((SEQ, tc), lambda b, j, o=off // tc: (b, o + j)),
                  pl.BlockSpec((CONV_WIDTH, tc), lambda b, j: (0, j)), pl.BlockSpec((1, tc), lambda b, j: (0, j))],
        out_specs=[o_spec, o_spec], out_shape=[jax.ShapeDtypeStruct((t, xbc), BF16)] * 2,
        compiler_params=_params(("parallel", "parallel")),
    )(proj, conv_w, conv_b)


def _conv_bwd(du, cpre, proj, off, conv_w, name):
    t, xbc = du.shape
    tc = _pick(xbc, (256, 128))
    pad = (CONV_WIDTH - 1) // 2

    def body(du_ref, c_ref, x_ref, w_ref, dx_ref, dw_ref, db_ref):
        @pl.when(pl.program_id(1) == 0)
        def _():
            dw_ref[...] = jnp.zeros_like(dw_ref)
            db_ref[...] = jnp.zeros_like(db_ref)

        cv = c_ref[...].astype(F32)
        sg = _sigmoid(cv)
        dc = du_ref[...] * (sg * (1.0 + cv * (1.0 - sg)))
        xv = x_ref[...].astype(F32)
        dx = jnp.zeros_like(dc)
        for k in range(CONV_WIDTH):
            dx = dx + w_ref[k:k + 1, :] * _shift_rows(dc, pad - k, SEQ)
            dw_ref[k:k + 1, :] += jnp.sum(dc * _shift_rows(xv, k - pad, SEQ), axis=0, keepdims=True)
        db_ref[...] += jnp.sum(dc, axis=0, keepdims=True)
        dx_ref[...] = dx.astype(BF16)

    blk = pl.BlockSpec((SEQ, tc), lambda j, b: (b, j))
    return pl.pallas_call(
        body, name=name, grid=(xbc // tc, t // SEQ),
        in_specs=[blk, blk, pl.BlockSpec((SEQ, tc), lambda j, b, o=off // tc: (b, o + j)),
                  pl.BlockSpec((CONV_WIDTH, tc), lambda j, b: (0, j))],
        out_specs=[blk, pl.BlockSpec((CONV_WIDTH, tc), lambda j, b: (0, j)), pl.BlockSpec((1, tc), lambda j, b: (0, j))],
        out_shape=[jax.ShapeDtypeStruct((t, xbc), BF16), jax.ShapeDtypeStruct((CONV_WIDTH, xbc), F32),
                   jax.ShapeDtypeStruct((1, xbc), F32)],
        compiler_params=_params(("parallel", "arbitrary")),
    )(du, cpre, proj, conv_w)


def _dt_prep(dtr, bias, name):
    def body(r_ref, b_ref, o_ref):
        v = r_ref[...] + b_ref[...]
        o_ref[...] = jnp.maximum(v, 0.0) + jnp.log1p(jnp.exp(-jnp.abs(v)))

    return _rowwise(body, [dtr], [bias], [(dtr.shape[1], F32)], tile=512, name=name)[0]


def _scan_prelude(d, dt_ref, dtt_ref, al_ref, alt_ref, hg):
    p = SSD_HEAD_DIM
    ch = SSD_CHUNK
    a_row = -jnp.exp(al_ref[...])
    a_col = -jnp.exp(alt_ref[...])
    dtc = dt_ref[...]
    dt_x = _head_expand(dtc, hg, p)
    dta_x = dt_x * _head_expand(a_row, hg, p)
    dta_t = dtt_ref[...] * a_col
    ri = lax.broadcasted_iota(jnp.int32, (ch, ch), 0)
    ci = lax.broadcasted_iota(jnp.int32, (ch, ch), 1)
    mask = (ci <= ri) if d == 0 else (ci >= ri)
    mask_t = (ci >= ri) if d == 0 else (ci <= ri)
    tri = mask.astype(F32)
    phi_x = jnp.dot(tri, dta_x, preferred_element_type=F32, precision=HIGHEST)
    phi_r = lax.dot_general(dta_t, tri, NT, preferred_element_type=F32, precision=HIGHEST)
    tot_x = jnp.sum(dta_x, axis=0, keepdims=True)
    return dtc, dt_x, phi_x, phi_r, tot_x, mask, mask_t


def _scan_specs(d, nc, hg, dm):
    p, n, ch = SSD_HEAD_DIM, SSD_STATE, SSD_CHUNK
    w = hg * p
    b0 = dm["HI"] // n
    c0 = (dm["HI"] + dm["GN"]) // n

    def row(b, c):
        return b * nc + c

    return [
        pl.BlockSpec((ch, w), lambda b, g, c: (row(b, c), g)),
        pl.BlockSpec((ch, n), lambda b, g, c: (row(b, c), b0 + g)),
        pl.BlockSpec((ch, n), lambda b, g, c: (row(b, c), c0 + g)),
        pl.BlockSpec((None, None, ch, hg), lambda b, g, c: (d, g, row(b, c), 0)),
        pl.BlockSpec((None, None, 8, ch), lambda b, g, c: (d, g, 0, row(b, c))),
        pl.BlockSpec((None, None, 1, hg), lambda b, g, c: (d, g, 0, 0)),
        pl.BlockSpec((None, None, 8, 1), lambda b, g, c: (d, g, 0, 0)),
    ]


def _remap(spec, f):
    return pl.BlockSpec(spec.block_shape, lambda b, g, c, im=spec.index_map: im(b, g, f(c)))


def _scan_fwd(u, dtg, dttg, alg, altg, d, dm, name):
    t = u.shape[0]
    p, n, ch, hg = SSD_HEAD_DIM, SSD_STATE, SSD_CHUNK, dm["HG"]
    w = hg * p
    nc = SEQ // ch
    order = (lambda c: c) if d == 0 else (lambda c: nc - 1 - c)

    def body(x_ref, b_ref, c_ref, dt_ref, dtt_ref, al_ref, alt_ref, y_ref, sin_ref, s_ref):
        @pl.when(pl.program_id(2) == 0)
        def _():
            s_ref[...] = jnp.zeros_like(s_ref)

        dtc, dt_x, phi_x, phi_r, tot_x, mask, _ = _scan_prelude(d, dt_ref, dtt_ref, al_ref, alt_ref, hg)
        lane_head = lax.broadcasted_iota(jnp.int32, (1, w), 1) >> int(math.log2(p))
        cm, bm = c_ref[...], b_ref[...]
        cb = lax.dot_general(cm, bm, NT, preferred_element_type=F32)
        xdt = x_ref[...].astype(F32) * dt_x
        xdt_b = xdt.astype(BF16)
        ydiag = jnp.zeros((ch, w), F32)
        for j in range(hg):
            seg = phi_x[:, j * p:j * p + 1] - phi_r[j:j + 1, :]
            mj = (cb * jnp.exp(jnp.where(mask, seg, NEG_BIG))).astype(BF16)
            ydiag = ydiag + jnp.dot(mj, jnp.where(lane_head == j, xdt_b, jnp.zeros_like(xdt_b)), preferred_element_type=F32)
        s = s_ref[...]
        y_ref[...] = ydiag + jnp.dot(cm, s.astype(BF16), preferred_element_type=F32) * jnp.exp(phi_x)
        sin_ref[...] = s
        wm = (xdt * jnp.exp(tot_x - phi_x)).astype(BF16)
        s_ref[...] = s * jnp.exp(tot_x) + lax.dot_general(bm, wm, TN, preferred_element_type=F32)

    specs = [_remap(s, order) for s in _scan_specs(d, nc, hg, dm)]
    return pl.pallas_call(
        body, name=name, grid=(t // SEQ, SSD_GROUPS, nc), in_specs=specs,
        out_specs=[_remap(pl.BlockSpec((ch, w), lambda b, g, c: (b * nc + c, g)), order),
                   _remap(pl.BlockSpec((None, None, n, w), lambda b, g, c: (b * nc + c, g, 0, 0)), order)],
        out_shape=[jax.ShapeDtypeStruct((t, dm["HI"]), F32), jax.ShapeDtypeStruct((t // ch, SSD_GROUPS, n, w), F32)],
        scratch_shapes=[pltpu.VMEM((n, w), F32)],
        compiler_params=_params(("parallel", "parallel", "arbitrary")),
    )(u, u, u, dtg, dttg, alg, altg)


def _scan_bwd(u, dtg, dttg, alg, altg, dy, sin, adds, d, dm, name):
    t = u.shape[0]
    p, n, ch, hg = SSD_HEAD_DIM, SSD_STATE, SSD_CHUNK, dm["HG"]
    w = hg * p
    nc = SEQ // ch
    order = (lambda c: nc - 1 - c) if d == 0 else (lambda c: c)
    has_bc_add = adds[1] is not None

    def body(*refs):
        x_ref, b_ref, c_ref, dt_ref, dtt_ref, al_ref, alt_ref, dy_ref, sin_ref, ax_ref = refs[:10]
        pos = 10
        ab_ref = ac_ref = None
        if has_bc_add:
            ab_ref, ac_ref = refs[10], refs[11]
            pos = 12
        dxs_ref, db_ref, dc_ref, rq_ref, xdx_ref, ds_ref = refs[pos:]

        @pl.when(pl.program_id(2) == 0)
        def _():
            ds_ref[...] = jnp.zeros_like(ds_ref)

        dtc, dt_x, phi_x, phi_r, tot_x, mask, mask_t = _scan_prelude(d, dt_ref, dtt_ref, al_ref, alt_ref, hg)
        lane_head = lax.broadcasted_iota(jnp.int32, (1, w), 1) >> int(math.log2(p))
        cm, bm = c_ref[...], b_ref[...]
        cb = lax.dot_general(cm, bm, NT, preferred_element_type=F32)
        cb_t = lax.dot_general(bm, cm, NT, preferred_element_type=F32)
        xs = x_ref[...].astype(F32)
        xdt = xs * dt_x
        xdt_b = xdt.astype(BF16)
        dy = dy_ref[...]
        dy_b = dy.astype(BF16)
        zero_b = jnp.zeros_like(dy_b)
        col = lax.broadcasted_iota(jnp.int32, (1, hg), 1)
        dxp = jnp.zeros((ch, w), F32)
        a_ls = jnp.zeros((ch, ch), F32)
        a_sl = jnp.zeros((ch, ch), F32)
        dphi = jnp.zeros((ch, hg), F32)
        for j in range(hg):
            pc = phi_x[:, j * p:j * p + 1]
            pr = phi_r[j:j + 1, :]
            l_ls = jnp.exp(jnp.where(mask, pc - pr, NEG_BIG))
            l_sl = jnp.exp(jnp.where(mask_t, pr - pc, NEG_BIG))
            dy_j = jnp.where(lane_head == j, dy_b, zero_b)
            xdt_j = jnp.where(lane_head == j, xdt_b, zero_b)
            dxp = dxp + jnp.dot((cb_t * l_sl).astype(BF16), dy_j, preferred_element_type=F32)
            g_ls = l_ls * lax.dot_general(dy_j, xdt_b, NT, preferred_element_type=F32)
            g_sl = l_sl * lax.dot_general(xdt_j, dy_b, NT, preferred_element_type=F32)
            a_ls = a_ls + g_ls
            a_sl = a_sl + g_sl
            pair = jnp.sum(g_ls * cb, axis=1, keepdims=True) - jnp.sum(g_sl * cb_t, axis=1, keepdims=True)
            dphi = jnp.where(col == j, pair, dphi)
        ds = ds_ref[...]
        ds_b = ds.astype(BF16)
        sin = sin_ref[...]
        sin_b = sin.astype(BF16)
        e_tp = jnp.exp(tot_x - phi_x)
        e_p = jnp.exp(phi_x)
        dxp_off = e_tp * jnp.dot(bm, ds_b, preferred_element_type=F32)
        dxp = dxp + dxp_off
        dxs_ref[...] = ax_ref[...] + dxp * dt_x
        xdx_ref[...] = _head_sums(xs * dxp, hg, p)
        y_off = jnp.dot(cm, sin_b, preferred_element_type=F32) * e_p
        st_t = _head_sums(xdt * dxp_off, hg, p)
        dphi = dphi + _head_sums(dy * y_off, hg, p) - st_t
        dtot = _head_sums(jnp.sum(ds * sin, axis=0, keepdims=True) * jnp.exp(tot_x), hg, p) + jnp.sum(st_t, axis=0, keepdims=True)
        cum = jnp.dot(mask_t.astype(F32), _head_expand(dphi, hg, p), preferred_element_type=F32, precision=HIGHEST)
        ddta = jnp.zeros((ch, hg), F32)
        for j in range(hg):
            ddta = jnp.where(col == j, cum[:, j * p:j * p + 1], ddta)
        rq_ref[...] = ddta + dtot
        dye = (dy * e_p).astype(BF16)
        dcv = jnp.dot(a_ls.astype(BF16), bm, preferred_element_type=F32)
        dcv = dcv + lax.dot_general(dye, sin_b, NT, preferred_element_type=F32)
        dbv = jnp.dot(a_sl.astype(BF16), cm, preferred_element_type=F32)
        dbv = dbv + lax.dot_general((xdt * e_tp).astype(BF16), ds_b, NT, preferred_element_type=F32)
        if has_bc_add:
            dcv = dcv + ac_ref[...]
            dbv = dbv + ab_ref[...]
        dc_ref[...] = dcv
        db_ref[...] = dbv
        ds_ref[...] = ds * jnp.exp(tot_x) + lax.dot_general(cm, dye, TN, preferred_element_type=F32)

    def sp(spec):
        return _remap(spec, order)

    xw = pl.BlockSpec((ch, w), lambda b, g, c: (b * nc + c, g))
    gn_blk = pl.BlockSpec((ch, n), lambda b, g, c: (b * nc + c, g))
    small = pl.BlockSpec((None, ch, hg), lambda b, g, c: (g, b * nc + c, 0))
    in_specs = [sp(s) for s in _scan_specs(d, nc, hg, dm)]
    in_specs += [sp(xw), sp(pl.BlockSpec((None, None, n, w), lambda b, g, c: (b * nc + c, g, 0, 0))), sp(xw)]
    args = [u, u, u, dtg, dttg, alg, altg, dy, sin, adds[0]]
    if has_bc_add:
        in_specs += [sp(gn_blk), sp(gn_blk)]
        args += [adds[1], adds[2]]
    return pl.pallas_call(
        body, name=name, grid=(t // SEQ, SSD_GROUPS, nc), in_specs=in_specs,
        out_specs=[sp(xw), sp(gn_blk), sp(gn_blk), sp(small), sp(small)],
        out_shape=[jax.ShapeDtypeStruct((t, dm["HI"]), F32), jax.ShapeDtypeStruct((t, dm["GN"]), F32),
                   jax.ShapeDtypeStruct((t, dm["GN"]), F32), jax.ShapeDtypeStruct((SSD_GROUPS, t, hg), F32),
                   jax.ShapeDtypeStruct((SSD_GROUPS, t, hg), F32)],
        scratch_shapes=[pltpu.VMEM((n, w), F32)],
        compiler_params=_params(("parallel", "parallel", "arbitrary")),
    )(*args)


def _ssd_param_bwd(rq_f, rq_r, xdx, dtp, dtr, bias, alog, name):
    def body(rf_ref, rr_ref, xdx_ref, dt_ref, dtr_ref, b_ref, al_ref, o_ref, db_ref, da_ref):
        a = -jnp.exp(al_ref[...])
        d_dta = rf_ref[...] + rr_ref[...]
        ddt = a * d_dta + xdx_ref[...]
        ddtr = ddt * _sigmoid(dtr_ref[...] + b_ref[...])
        o_ref[...] = ddtr
        db_ref[...] += jnp.sum(ddtr, axis=0, keepdims=True)
        da_ref[...] += a * jnp.sum(dt_ref[...] * d_dta, axis=0, keepdims=True)

    return _rowwise(body, [rq_f, rq_r, xdx, dtp, dtr], [bias, alog], [(LANES, F32)],
                    [((1, LANES), F32), ((1, LANES), F32)], tile=512, name=name)


def _ssd_out_fwd(y_f, y_b, u, proj, dcols, gn, hi, name):
    def body(yf_ref, yb_ref, x_ref, z_ref, d_ref, g_ref, o_ref):
        ytot = yf_ref[...] + yb_ref[...] + d_ref[...] * x_ref[...].astype(F32)
        zv = z_ref[...].astype(F32)
        yz = ytot * (zv * _sigmoid(zv))
        rstd = lax.rsqrt(jnp.mean(yz * yz, axis=-1, keepdims=True) + EPS)
        o_ref[...] = (yz * rstd * g_ref[...]).astype(BF16)

    return _rowwise(body, [y_f, y_b, (u, 0, hi), (proj, 0, hi)], [dcols, gn], [(hi, BF16)], tile=256, name=name)[0]


def _ssd_out_bwd(dya, y_f, y_b, u, proj, dcols, gn, hi, name):
    def body(dy_ref, yf_ref, yb_ref, x_ref, z_ref, d_ref, g_ref, dyt_ref, dxs_ref, dz_ref, dg_ref, dd_ref):
        xv = x_ref[...].astype(F32)
        ytot = yf_ref[...] + yb_ref[...] + d_ref[...] * xv
        zv = z_ref[...].astype(F32)
        sg = _sigmoid(zv)
        sz = zv * sg
        yz = ytot * sz
        rstd = lax.rsqrt(jnp.mean(yz * yz, axis=-1, keepdims=True) + EPS)
        yn = yz * rstd
        dv = dy_ref[...]
        dg_ref[...] += jnp.sum(dv * yn, axis=0, keepdims=True)
        dn = dv * g_ref[...]
        dyz = rstd * (dn - yn * jnp.mean(dn * yn, axis=-1, keepdims=True))
        dyt = dyz * sz
        dyt_ref[...] = dyt
        dxs_ref[...] = dyt * d_ref[...]
        dz_ref[...] = (dyz * ytot * (sg * (1.0 + zv * (1.0 - sg)))).astype(BF16)
        dd_ref[...] += jnp.sum(dyt * xv, axis=0, keepdims=True)

    return _rowwise(body, [dya, y_f, y_b, (u, 0, hi), (proj, 0, hi)], [dcols, gn],
                    [(hi, F32), (hi, F32), (hi, BF16)], [((1, hi), F32), ((1, hi), F32)], tile=128, name=name)


def _gate_fwd(pa, pb, pc, proj, off, d, name):
    def body(a_ref, b_ref, c_ref, g0_ref, g1_ref, g2_ref, o_ref):
        acc = _sigmoid(g0_ref[...].astype(F32)) * a_ref[...]
        acc = acc + _sigmoid(g1_ref[...].astype(F32)) * b_ref[...]
        acc = acc + _sigmoid(g2_ref[...].astype(F32)) * c_ref[...]
        o_ref[...] = acc.astype(BF16)

    rows = [pa, pb, pc] + [(proj, off + k * d, d) for k in range(3)]
    return _rowwise(body, rows, [], [(d, BF16)], tile=256, name=name)[0]


def _gate_bwd(dm_, pa, pb, pc, proj, off, d, name):
    def body(dm_ref, a_ref, b_ref, c_ref, g0_ref, g1_ref, g2_ref, da_ref, db_ref, dc_ref, dg0_ref, dg1_ref, dg2_ref):
        dmv = dm_ref[...]
        for p_ref, g_ref, dp_ref, dg_ref in ((a_ref, g0_ref, da_ref, dg0_ref), (b_ref, g1_ref, db_ref, dg1_ref),
                                             (c_ref, g2_ref, dc_ref, dg2_ref)):
            sg = _sigmoid(g_ref[...].astype(F32))
            dp_ref[...] = (dmv * sg).astype(BF16)
            dg_ref[...] = (dmv * p_ref[...] * sg * (1.0 - sg)).astype(BF16)

    rows = [dm_, pa, pb, pc] + [(proj, off + k * d, d) for k in range(3)]
    return _rowwise(body, rows, [], [(d, BF16)] * 6, tile=128, name=name)


def _adamw(w, g, m, v, name):
    nl, rows, cols = w.shape
    tile = _pick(rows, ROW_TILES)
    c1 = 1.0 / (1.0 - ADAM_B1 ** ADAM_STEP)
    c2 = 1.0 / (1.0 - ADAM_B2 ** ADAM_STEP)

    def body(w_ref, g_ref, m_ref, v_ref, d_ref, nm_ref, nv_ref):
        gv = g_ref[...]
        nm = ADAM_B1 * m_ref[...] + (1.0 - ADAM_B1) * gv
        nv = ADAM_B2 * v_ref[...] + (1.0 - ADAM_B2) * (gv * gv)
        nm_ref[...] = nm
        nv_ref[...] = nv
        d_ref[...] = -ADAM_LR * ((nm * c1) / (jnp.sqrt(nv * c2) + ADAM_EPS) + ADAM_WD * w_ref[...])

    blk = pl.BlockSpec((None, tile, cols), lambda l, i: (l, i, 0))
    return pl.pallas_call(
        body, name=name, grid=(nl, rows // tile), in_specs=[blk] * 4, out_specs=[blk] * 3,
        out_shape=[jax.ShapeDtypeStruct(w.shape, F32)] * 3, compiler_params=_params(("parallel", "parallel")),
    )(w, g, m, v)


ANY = pl.BlockSpec(memory_space=pl.ANY)


def _place():
    x, y, c = lax.axis_index("x"), lax.axis_index("y"), lax.axis_index("c")
    chips = [(1 - x, y), (x, 1 - y), (1 - x, 1 - y)]
    return x, y, c, chips


def _gather_chips(arr, name):
    def body(src, out, ssem, rsem):
        x, y, c, chips = _place()
        k = 2 * x + y

        def copy(j, kk, layer, to, own=False):
            return pltpu.make_async_remote_copy(
                src_ref=src.at[layer] if own else out.at[layer, kk], dst_ref=out.at[layer, kk],
                send_sem=ssem.at[j], recv_sem=rsem.at[j], device_id=to, device_id_type=MESH)

        first = [copy(j, k, c, (cx, cy, c), own=True) for j, (cx, cy) in enumerate(chips)]
        for cp in first:
            cp.start()
        passed = [copy(3 + j, 2 * cx + cy, c, (x, y, 1 - c)) for j, (cx, cy) in enumerate(chips)]
        for j, (cx, cy) in enumerate(chips):
            copy(j, 2 * cx + cy, c, (x, y, c)).wait_recv()
            passed[j].start()
        for j, (cx, cy) in enumerate(chips):
            copy(3 + j, 2 * cx + cy, 1 - c, (x, y, c)).wait_recv()
        for cp in first + passed:
            cp.wait_send()

    st = pl.pallas_call(
        body, name=name, in_specs=[ANY], out_specs=ANY,
        out_shape=jax.ShapeDtypeStruct((arr.shape[0], 4) + arr.shape[1:], arr.dtype),
        scratch_shapes=[pltpu.SemaphoreType.DMA((6,)), pltpu.SemaphoreType.DMA((6,))],
    )(arr)
    kchip = 2 * lax.axis_index("x") + lax.axis_index("y")
    return lax.dynamic_update_slice(st, arr[:, None], (0, kchip) + (0,) * (arr.ndim - 1))


def _pair_swap(g, name):
    def body(src, out, ssem, rsem):
        x, y, c, _ = _place()
        cp = pltpu.make_async_remote_copy(src_ref=src.at[1 - c], dst_ref=out, send_sem=ssem, recv_sem=rsem,
                                          device_id=(x, y, 1 - c), device_id_type=MESH)
        cp.start()
        cp.wait()

    return pl.pallas_call(
        body, name=name, in_specs=[ANY], out_specs=ANY, out_shape=jax.ShapeDtypeStruct(g.shape[1:], g.dtype),
        scratch_shapes=[pltpu.SemaphoreType.DMA, pltpu.SemaphoreType.DMA],
    )(g)


def _chip_exchange(p, name):
    def body(src, out, ssem, rsem):
        x, y, c, chips = _place()
        cps = [pltpu.make_async_remote_copy(src_ref=src.at[2 * cx + cy], dst_ref=out.at[j], send_sem=ssem.at[j],
                                            recv_sem=rsem.at[j], device_id=(cx, cy, c), device_id_type=MESH)
               for j, (cx, cy) in enumerate(chips)]
        for cp in cps:
            cp.start()
        for cp in cps:
            cp.wait()

    return pl.pallas_call(
        body, name=name, in_specs=[ANY], out_specs=ANY, out_shape=jax.ShapeDtypeStruct((3,) + p.shape[1:], p.dtype),
        scratch_shapes=[pltpu.SemaphoreType.DMA((3,)), pltpu.SemaphoreType.DMA((3,))],
    )(p)


def _pair_share(r, name):
    def body(src, out, ssem, rsem):
        x, y, c, _ = _place()
        cp = pltpu.make_async_remote_copy(src_ref=src, dst_ref=out, send_sem=ssem, recv_sem=rsem,
                                          device_id=(x, y, 1 - c), device_id_type=MESH)
        cp.start()
        cp.wait()

    theirs = pl.pallas_call(
        body, name=name, in_specs=[ANY], out_specs=ANY, out_shape=jax.ShapeDtypeStruct(r.shape, r.dtype),
        scratch_shapes=[pltpu.SemaphoreType.DMA, pltpu.SemaphoreType.DMA],
    )(r)
    first = lax.axis_index("c") == 0
    return jnp.stack([jnp.where(first, r, theirs), jnp.where(first, theirs, r)])


def _sum_pair(g, got, sel, name):
    two, four, rows, cols = g.shape
    flat = four * rows
    tile = _pick(flat, ROW_TILES)

    def body(sel_ref, a_ref, b_ref, o_ref):
        o_ref[...] = (a_ref[...].astype(F32) + b_ref[...].astype(F32)).astype(BF16)

    blk = pl.BlockSpec((tile, cols), lambda i, s: (i, 0))
    return pl.pallas_call(
        body, name=name, out_shape=jax.ShapeDtypeStruct((flat, cols), BF16),
        grid_spec=pltpu.PrefetchScalarGridSpec(
            num_scalar_prefetch=1, grid=(flat // tile,),
            in_specs=[pl.BlockSpec((None, tile, cols), lambda i, s: (s[0], i, 0)), blk], out_specs=blk),
        compiler_params=_params(("parallel",)),
    )(sel, g.reshape(two, flat, cols), got.reshape(flat, cols)).reshape(four, rows, cols)


def _sum4(a, b, sel, name):
    _, rows, cols = a.shape
    tile = _pick(rows, ROW_TILES)

    def body(sel_ref, a_ref, b0_ref, b1_ref, b2_ref, o_ref):
        acc = a_ref[...].astype(F32) + b0_ref[...].astype(F32)
        acc = acc + b1_ref[...].astype(F32)
        o_ref[...] = acc + b2_ref[...].astype(F32)

    bspec = [pl.BlockSpec((None, tile, cols), lambda i, s, j=j: (j, i, 0)) for j in range(3)]
    return pl.pallas_call(
        body, name=name, out_shape=jax.ShapeDtypeStruct((rows, cols), F32),
        grid_spec=pltpu.PrefetchScalarGridSpec(
            num_scalar_prefetch=1, grid=(rows // tile,),
            in_specs=[pl.BlockSpec((None, tile, cols), lambda i, s: (s[0], i, 0))] + bspec,
            out_specs=pl.BlockSpec((tile, cols), lambda i, s: (i, 0))),
        compiler_params=_params(("parallel",)),
    )(sel, a, b, b, b)


def _reduce_scatter(g, name):
    c = lax.axis_index("c").astype(jnp.int32).reshape(1)
    k = (2 * lax.axis_index("x") + lax.axis_index("y")).astype(jnp.int32).reshape(1)
    got = _pair_swap(g, name + "_pair")
    part = _sum_pair(g, got, c, name + "_add2")
    others = _chip_exchange(part, name + "_chips")
    total = _sum4(part, others, k, name + "_add4")
    return _pair_share(total, name + "_share")


def _all_reduce_small(buf, name):
    rows = buf.shape[0]

    def body(src, out, slots, ssem, rsem):
        x, y, c, _ = _place()
        me = 4 * x + 2 * y + c
        slots[me] = src[...]
        cps = []
        for j in range(1, 8):
            px, py, pc = x ^ (j >> 2), y ^ ((j >> 1) & 1), c ^ (j & 1)
            cps.append(pltpu.make_async_remote_copy(src_ref=src, dst_ref=slots.at[me], send_sem=ssem.at[j - 1],
                                                    recv_sem=rsem.at[j - 1], device_id=(px, py, pc), device_id_type=MESH))
        for cp in cps:
            cp.start()
        for j in range(1, 8):
            peer = me ^ j
            pltpu.make_async_remote_copy(src_ref=src, dst_ref=slots.at[peer], send_sem=ssem.at[j - 1], recv_sem=rsem.at[j - 1],
                                         device_id=(x, y, c), device_id_type=MESH).wait_recv()
        for cp in cps:
            cp.wait_send()
        acc = slots[0]
        for d in range(1, 8):
            acc = acc + slots[d]
        out[...] = acc

    vm = pl.BlockSpec(memory_space=pltpu.VMEM)
    return pl.pallas_call(
        body, name=name, in_specs=[vm], out_specs=vm, out_shape=jax.ShapeDtypeStruct((rows, LANES), F32),
        scratch_shapes=[pltpu.VMEM((8, rows, LANES), F32), pltpu.SemaphoreType.DMA((7,)), pltpu.SemaphoreType.DMA((7,))],
    )(buf)


def _pack(arrs):
    flat = jnp.concatenate([a.astype(F32).reshape(-1) for a in arrs])
    n = flat.shape[0]
    padded = -(-n // (8 * LANES)) * (8 * LANES)
    return jnp.pad(flat, (0, padded - n)).reshape(padded // LANES, LANES)


def _unpack(buf, like):
    flat = buf.reshape(-1)
    out, pos = [], 0
    for a in like:
        out.append(flat[pos:pos + a.size].reshape(a.shape))
        pos += a.size
    return out


def _stride(t2d, dil):
    t, w = t2d.shape
    b = t // SEQ
    return t2d.reshape(b, SEQ // dil, dil, w).transpose(0, 2, 1, 3).reshape(b * dil, SEQ // dil, w)


def _unstride(t3d, dil):
    bb, n, w = t3d.shape
    b = bb // dil
    return t3d.reshape(b, dil, n, w).transpose(0, 2, 1, 3).reshape(b * SEQ, w)


def _stat_cols(st, dil, heads):
    s3 = _stride(st, dil)
    return s3.transpose(0, 2, 1)[..., None]


def _stat_rows(col):
    bb, h, n, _ = col.shape
    return col.reshape(bb, h, 1, n)


def _scan_params(dtp, alog, dm):
    t = dtp.shape[0]
    g, hg = SSD_GROUPS, dm["HG"]
    dt4 = dtp[:, :dm["H2"]].reshape(t, 2, g, hg)
    dtg = dt4.transpose(1, 2, 0, 3)
    dttg = jnp.pad(dt4.transpose(1, 2, 3, 0), ((0, 0), (0, 0), (0, 8 - hg), (0, 0)))
    al = alog.reshape(2, g, 1, hg)
    alt = jnp.pad(alog.reshape(2, g, hg, 1), ((0, 0), (0, 0), (0, 8 - hg), (0, 0)))
    return dtg, dttg, al, alt


def _layer_fwd(x, wl, tabs, dm, li):
    d = dm["D"]
    nm = f"l{li}_"
    h = _rms_fwd(x, wl["g_mix"], nm + "rms1")
    proj = _mm(h, wl["w_main"], tb=True, name=nm + "proj")
    dtr = _mm(h, wl["w_dt"], tb=True, out_dtype=F32, name=nm + "proj_dt")
    cpre, u = _conv_fwd(proj, dm["OFF_XBC"], wl["conv_w"], wl["conv_b"], dm["XBC"], nm + "conv")
    dtp = _dt_prep(dtr, wl["dt_bias"], nm + "dt")
    sp = _scan_params(dtp, wl["a_log"], dm)
    y_f, s_f = _scan_fwd(u, *sp, 0, dm, nm + "scan_f")
    y_b, s_b = _scan_fwd(u, *sp, 1, dm, nm + "scan_b")
    y_a = _ssd_out_fwd(y_f, y_b, u, proj, wl["d_cols"], wl["ssd_norm"], dm["HI"], nm + "ssd_out")
    qkv = _rope(proj, dm["OFF_QKV"], dm["QW"] // HEAD_DIM, tabs[0], dm, nm + "rope")
    ng, dw = dm["NG"], dm["DW"]
    outs, lses, xgs = [], [], []
    for gi, (window, dil) in enumerate(DIL_PATTERNS):
        cols = [qkv[:, s * ng * dw + gi * dw:s * ng * dw + (gi + 1) * dw] for s in range(3)]
        xg = _stride(jnp.concatenate(cols, axis=1), dil)
        o, lse = _attn_fwd(xg, xg, xg, 0, 1, 2, DIL_HEADS, 1, SEQ // dil, window // (2 * dil), None, F32, nm + f"dil{gi}")
        xgs.append(xg)
        outs.append(_unstride(o, dil))
        lses.append(_unstride(lse[..., 0].transpose(0, 2, 1), dil))
    y_bm, lse_tot = _dil_combine(outs, lses, nm + "dil_mix")
    bsz = x.shape[0] // SEQ
    xw = qkv[:, dm["QKVD"]:].reshape(bsz, SEQ, dm["WQ"] + 2 * dm["WK"])
    rep = WIN_Q_HEADS // WIN_KV_HEADS
    y_c3, lse_w = _attn_fwd(xw, xw, xw, 0, rep, rep + 1, WIN_Q_HEADS, rep, SEQ, WIN_HALF,
                            wl["sink"].reshape(WIN_Q_HEADS, 1, 1), BF16, nm + "win")
    y_c = y_c3.reshape(x.shape[0], dm["WQ"])
    pa = _mm(y_a, wl["w_a"], out_dtype=F32, name=nm + "pa")
    pb = _mm(y_bm, wl["w_b"], out_dtype=F32, name=nm + "pb")
    pc = _mm(y_c, wl["w_c"], out_dtype=F32, name=nm + "pc")
    merged = _gate_fwd(pa, pb, pc, proj, dm["OFF_GATE"], d, nm + "gate")
    x1 = _mm(merged, wl["w_out"], add=x, out_dtype=F32, name=nm + "out")
    hm = _rms_fwd(x1, wl["g_mlp"], nm + "rms2")
    up, act = _mm(hm, wl["w_up"], epi="relu2", name=nm + "up")
    x2 = _mm(act, wl["w_down"], add=x1, out_dtype=F32, name=nm + "down")
    saved = dict(x=x, h=h, proj=proj, dtr=dtr, cpre=cpre, u=u, dtp=dtp, y_f=y_f, y_b=y_b, s_f=s_f, s_b=s_b, y_a=y_a,
                 xw=xw, xgs=xgs, y_bm=y_bm, lse_tot=lse_tot, y_c=y_c, lse_w=lse_w, pa=pa, pb=pb, pc=pc,
                 merged=merged, x1=x1, hm=hm, up=up, act=act)
    return x2, saved


def _layer_bwd(dx2, wl, sv, tabs, dm, li):
    d = dm["D"]
    t = dx2.shape[0]
    bsz = t // SEQ
    nm = f"l{li}b_"
    gr = {}
    dup = _mm(dx2, wl["w_down"], tb=True, aux=sv["up"], epi="relu2_bwd", name=nm + "dup")
    gr["w_down"] = _mm(sv["act"], dx2, ta=True, name=nm + "gw_down")
    dhm = _mm(dup, wl["w_up"], tb=True, out_dtype=F32, name=nm + "dhm")
    gr["w_up"] = _mm(sv["hm"], dup, ta=True, name=nm + "gw_up")
    dx1, gmlp = _rms_bwd(sv["x1"], wl["g_mlp"], dhm, dx2, nm + "rms2")
    gr["g_mlp"] = gmlp[0]
    dmerged = _mm(dx1, wl["w_out"], tb=True, out_dtype=F32, name=nm + "dmerged")
    gr["w_out"] = _mm(sv["merged"], dx1, ta=True, name=nm + "gw_out")
    dpa, dpb, dpc, dg0, dg1, dg2 = _gate_bwd(dmerged, sv["pa"], sv["pb"], sv["pc"], sv["proj"], dm["OFF_GATE"], d, nm + "gate")
    dya = _mm(dpa, wl["w_a"], tb=True, out_dtype=F32, name=nm + "dya")
    gr["w_a"] = _mm(sv["y_a"], dpa, ta=True, name=nm + "gw_a")
    dyb = _mm(dpb, wl["w_b"], tb=True, out_dtype=F32, name=nm + "dyb")
    gr["w_b"] = _mm(sv["y_bm"], dpb, ta=True, name=nm + "gw_b")
    dyc = _mm(dpc, wl["w_c"], tb=True, out_dtype=F32, name=nm + "dyc")
    gr["w_c"] = _mm(sv["y_c"], dpc, ta=True, name=nm + "gw_c")
    ng, dw = dm["NG"], dm["DW"]
    xw = sv["xw"]
    rep = WIN_Q_HEADS // WIN_KV_HEADS
    delta_w = _head_dots(dyc, sv["y_c"], WIN_Q_HEADS, nm + "win_delta")
    dl_col = _stat_cols(delta_w, 1, WIN_Q_HEADS)
    lse_w = sv["lse_w"]
    dyc3 = dyc.reshape(bsz, SEQ, dm["WQ"])
    wargs = (0, rep, rep + 1, WIN_Q_HEADS, rep, SEQ, WIN_HALF)
    dq_w = _attn_dq(xw, xw, xw, dyc3, lse_w, dl_col, *wargs, nm + "win_dq")
    dk_w, dv_w = _attn_dkv(xw, xw, xw, dyc3, _stat_rows(lse_w), _stat_rows(dl_col), *wargs, nm + "win_dkv")
    lse_w2 = lse_w[..., 0].transpose(0, 2, 1).reshape(t, WIN_Q_HEADS)
    gr["sink"] = _sink_grad(lse_w2, delta_w, wl["sink"], nm + "sink")[0]
    delta_d = _head_dots(dyb, sv["y_bm"], DIL_HEADS, nm + "dil_delta")
    dqs, dks, dvs = [], [], []
    for gi, (window, dil) in enumerate(DIL_PATTERNS):
        xg = sv["xgs"][gi]
        n = SEQ // dil
        do_g = _stride(dyb, dil)
        lse_c = _stat_cols(sv["lse_tot"], dil, DIL_HEADS)
        dl_c = _stat_cols(delta_d, dil, DIL_HEADS)
        dargs = (0, 1, 2, DIL_HEADS, 1, n, window // (2 * dil))
        dq = _attn_dq(xg, xg, xg, do_g, lse_c, dl_c, *dargs, nm + f"dil{gi}_dq")
        dk, dv = _attn_dkv(xg, xg, xg, do_g, _stat_rows(lse_c), _stat_rows(dl_c), *dargs, nm + f"dil{gi}_dkv")
        dqs.append(_unstride(dq, dil))
        dks.append(_unstride(dk, dil))
        dvs.append(_unstride(dv, dil))
    dqkv_r = jnp.concatenate(dqs + dks + dvs + [dq_w.reshape(t, dm["WQ"]), dk_w.reshape(t, dm["WK"]), dv_w.reshape(t, dm["WK"])],
                             axis=1)
    dqkv = _rope(dqkv_r, 0, dm["QW"] // HEAD_DIM, tabs[1], dm, nm + "rope")
    hi, gn = dm["HI"], dm["GN"]
    dyt, dxs0, dz, gnorm, dd_cols = _ssd_out_bwd(dya, sv["y_f"], sv["y_b"], sv["u"], sv["proj"], wl["d_cols"],
                                                          wl["ssd_norm"], hi, nm + "ssd_out")
    gr["ssd_norm"] = gnorm[0]
    gr["d_skip"] = dd_cols.reshape(SSD_HEADS, SSD_HEAD_DIM).sum(axis=1)
    sp = _scan_params(sv["dtp"], wl["a_log"], dm)
    dxs1, db1, dc1, rq_f, xdx_f = _scan_bwd(sv["u"], *sp, dyt, sv["s_f"], (dxs0, None, None), 0, dm, nm + "scan_f")
    dxs2, db2, dc2, rq_r, xdx_r = _scan_bwd(sv["u"], *sp, dyt, sv["s_b"], (dxs1, db1, dc1), 1, dm, nm + "scan_b")

    def heads(a):
        return a.transpose(1, 0, 2).reshape(t, SSD_HEADS)

    zpad = jnp.zeros((t, LANES - dm["H2"]), F32)
    zh = jnp.zeros((t, SSD_HEADS), F32)
    rqf_p = jnp.concatenate([heads(rq_f), zh, zpad], axis=1)
    rqr_p = jnp.concatenate([zh, heads(rq_r), zpad], axis=1)
    xdx_p = jnp.concatenate([heads(xdx_f), heads(xdx_r), zpad], axis=1)
    ddtr, dbias, dalog = _ssd_param_bwd(rqf_p, rqr_p, xdx_p, sv["dtp"], sv["dtr"], wl["dt_bias"], wl["a_log_p"], nm + "ssd_par")
    gr["dt_bias"] = dbias[0, :dm["H2"]].reshape(2, SSD_HEADS)
    gr["a_log"] = dalog[0, :dm["H2"]].reshape(2, SSD_HEADS)
    du = jnp.concatenate([dxs2, db2, dc2], axis=1)
    dxbc, gr["conv_w"], gcb = _conv_bwd(du, sv["cpre"], sv["proj"], dm["OFF_XBC"], wl["conv_w"], nm + "conv")
    gr["conv_b"] = gcb[0]
    dproj = jnp.concatenate([dz, dxbc, dqkv, dg0, dg1, dg2], axis=1)
    dh_dt = _mm(ddtr, wl["w_dt"], out_dtype=F32, name=nm + "dh_dt")
    dh = _mm(dproj, wl["w_main"], add=dh_dt, out_dtype=F32, name=nm + "dh")
    gw_main = _mm(dproj, sv["h"], ta=True, name=nm + "gw_main")
    gw_dt = _mm(ddtr, sv["h"], ta=True, name=nm + "gw_dt")
    o1, h2 = dm["OFF_QKV"], dm["H2"]
    gw_in_t = jnp.concatenate([gw_main[:o1], gw_dt[:h2], gw_main[o1:]], axis=0)
    gr["w_in"] = gw_in_t.reshape(4, (dm["NM"] + h2) // 4, d)
    dx, gmix = _rms_bwd(sv["x"], wl["g_mix"], dh, dx1, nm + "rms1")
    gr["g_mix"] = gmix[0]
    return dx, gr


def _layer_weights(full, li, dm):
    st = full["w_in"]
    o1 = dm["OFF_QKV"]
    h2 = dm["H2"]
    d = dm["D"]
    w_in_t = st[li].reshape(4 * st.shape[2], d)
    wl = dict(
        w_main=jnp.concatenate([w_in_t[:o1], w_in_t[o1 + h2:]], axis=0),
        w_dt=jnp.pad(w_in_t[o1:o1 + h2], ((0, LANES - h2), (0, 0))),
        w_a=full["w_a"][li], w_b=full["w_b"][li], w_c=full["w_c"][li], w_out=full["w_out"][li],
        w_up=full["w_up"][li], w_down=full["w_down"][li],
        conv_w=full["conv_w"][li], conv_b=full["conv_b"][li][None, :],
        g_mix=full["g_mix"][li][None, :], g_mlp=full["g_mlp"][li][None, :], ssd_norm=full["ssd_norm"][li][None, :],
        d_cols=jnp.repeat(full["d_skip"][li], SSD_HEAD_DIM)[None, :],
        sink=full["sink"][li][None, :],
        a_log=full["a_log"][li],
        a_log_p=jnp.pad(full["a_log"][li].reshape(1, h2), ((0, 0), (0, LANES - h2))),
        dt_bias=jnp.pad(full["dt_bias"][li].reshape(1, h2), ((0, 0), (0, LANES - h2))),
    )
    assert wl["w_main"].shape == (dm["NM"], d)
    return wl


def _local_step(x, target, full, depth):
    bsz, seq, d = x.shape
    assert seq == SEQ
    dm = _dims(d)
    assert dm["OFF_GATE"] % d == 0 and dm["HI"] % (dm["HG"] * SSD_HEAD_DIM) == 0 and dm["H2"] <= LANES
    tabs = (_rope_tables(1.0), _rope_tables(-1.0))
    xt = x.reshape(bsz * seq, d)
    wls, saves = [], []
    for li in range(depth):
        wl = _layer_weights(full, li, dm)
        xt, sv = _layer_fwd(xt, wl, tabs, dm, li)
        wls.append(wl)
        saves.append(sv)
    dx, loss, g_final = _loss_head(xt, full["g_final"][None, :], target.reshape(bsz * seq, d), "loss_head")
    grads = [None] * depth
    for li in reversed(range(depth)):
        dx, grads[li] = _layer_bwd(dx, wls[li], saves[li], tabs, dm, li)
    return loss, dx.reshape(bsz, seq, d), grads, g_final[0]


BIG = ("w_in", "w_a", "w_b", "w_c", "w_out", "w_up", "w_down")
COL_SHARDED = ("w_in", "w_b", "w_up")
SMALL = ("g_mix", "conv_w", "conv_b", "dt_bias", "a_log", "d_skip", "ssd_norm", "sink", "g_mlp", "g_final")
ORDER = ("g_mix", "w_in", "conv_w", "conv_b", "dt_bias", "a_log", "d_skip", "ssd_norm", "w_a", "w_b", "w_c", "sink",
         "w_out", "g_mlp", "w_up", "w_down", "g_final")


def _unstack(name, st):
    nl, _, r, c = st.shape
    if name in COL_SHARDED:
        return jnp.moveaxis(st, 1, 2).reshape(nl, r, 4 * c)
    return st.reshape(nl, 4 * r, c)


def _restack(name, gfull):
    nl, r, c = gfull.shape
    if name in COL_SHARDED:
        return jnp.moveaxis(gfull.reshape(nl, r, 4, c // 4), 2, 1)
    return gfull.reshape(nl, 4, r // 4, c)


def kernel(x, g_mix, w_in, conv_w, conv_b, dt_bias, a_log, d_skip, ssd_norm, w_a, w_b, w_c, sink, w_out, g_mlp, w_up, w_down, g_final, loss_target, m_g_mix, m_w_in, m_conv_w, m_conv_b, m_dt_bias, m_a_log, m_d_skip, m_ssd_norm, m_w_a, m_w_b, m_w_c, m_sink, m_w_out, m_g_mlp, m_w_up, m_w_down, m_g_final, v_g_mix, v_w_in, v_conv_w, v_conv_b, v_dt_bias, v_a_log, v_d_skip, v_ssd_norm, v_w_a, v_w_b, v_w_c, v_sink, v_w_out, v_g_mlp, v_w_up, v_w_down, v_g_final):
    w = dict(g_mix=g_mix, w_in=w_in, conv_w=conv_w, conv_b=conv_b, dt_bias=dt_bias, a_log=a_log, d_skip=d_skip,
             ssd_norm=ssd_norm, w_a=w_a, w_b=w_b, w_c=w_c, sink=sink, w_out=w_out, g_mlp=g_mlp, w_up=w_up, w_down=w_down,
             g_final=g_final)
    m = dict(g_mix=m_g_mix, w_in=m_w_in, conv_w=m_conv_w, conv_b=m_conv_b, dt_bias=m_dt_bias, a_log=m_a_log,
             d_skip=m_d_skip, ssd_norm=m_ssd_norm, w_a=m_w_a, w_b=m_w_b, w_c=m_w_c, sink=m_sink, w_out=m_w_out,
             g_mlp=m_g_mlp, w_up=m_w_up, w_down=m_w_down, g_final=m_g_final)
    v = dict(g_mix=v_g_mix, w_in=v_w_in, conv_w=v_conv_w, conv_b=v_conv_b, dt_bias=v_dt_bias, a_log=v_a_log,
             d_skip=v_d_skip, ssd_norm=v_ssd_norm, w_a=v_w_a, w_b=v_w_b, w_c=v_w_c, sink=v_sink, w_out=v_w_out,
             g_mlp=v_g_mlp, w_up=v_w_up, w_down=v_w_down, g_final=v_g_final)
    depth = w_in.shape[0]
    assert depth == 2
    kchip = 2 * lax.axis_index("x") + lax.axis_index("y")

    full = {n: w[n] for n in SMALL if n != "conv_w"}
    tr = lambda a: jnp.swapaxes(a, 1, 2)
    for n in BIG:
        st = _gather_chips((tr(w[n]) if n == "w_in" else w[n]).astype(BF16), "gather_" + n)
        full[n] = st if n == "w_in" else _unstack(n, st)
    cw = _gather_chips(conv_w, "gather_conv_w")
    full["conv_w"] = jnp.moveaxis(cw, 1, 2).reshape(depth, CONV_WIDTH, 4 * conv_w.shape[2])

    loss_part, grad_x, grads, gg_final = _local_step(x, loss_target, full, depth)

    gsh = {}
    for n in BIG:
        gfull = jnp.stack([grads[li][n] for li in range(depth)])
        gsh[n] = _reduce_scatter(gfull if n == "w_in" else _restack(n, gfull), "rs_" + n)
    small_names = [n for n in SMALL if n != "g_final"]
    small_g = [jnp.stack([grads[li][n] for li in range(depth)]) for n in small_names] + [gg_final, loss_part[0, :1]]
    red = _unpack(_all_reduce_small(_pack(small_g), "allreduce_small"), small_g)
    for n, a in zip(small_names + ["g_final"], red):
        gsh[n] = a
    loss = red[-1][0]
    cshard = conv_w.shape[2]
    gsh["conv_w"] = lax.dynamic_slice_in_dim(gsh["conv_w"], kchip * cshard, cshard, axis=2)

    delta, new_m, new_v = {}, {}, {}
    for n in BIG:
        if n == "w_in":
            outs_t = _adamw(tr(w[n]), gsh[n], tr(m[n]), tr(v[n]), "adamw_" + n)
            delta[n], new_m[n], new_v[n] = [tr(o) for o in outs_t]
            gsh[n] = tr(gsh[n])
        else:
            delta[n], new_m[n], new_v[n] = _adamw(w[n], gsh[n], m[n], v[n], "adamw_" + n)
    sm = list(SMALL)
    packed = [_pack([d_[n] for n in sm])[None] for d_ in (w, gsh, m, v)]
    outs = [o[0] for o in _adamw(*packed, "adamw_small")]
    for dst, buf in zip((delta, new_m, new_v), outs):
        for n, a in zip(sm, _unpack(buf, [w[n] for n in sm])):
            dst[n] = a
    return (loss, grad_x, *[gsh[n] for n in ORDER], *[delta[n] for n in ORDER], *[new_m[n] for n in ORDER],
            *[new_v[n] for n in ORDER])
```

```python
import functools
import math

import jax
import jax.numpy as jnp
from jax import lax
from jax.experimental import pallas as pl
from jax.experimental.pallas import tpu as pltpu

F32 = jnp.float32
BF16 = jnp.bfloat16

SEQ = 2048
SSD_HEADS = 32
SSD_HEAD_DIM = 64
SSD_GROUPS = 8
SSD_STATE = 128
SSD_CHUNK = 128
CONV_WIDTH = 5
HEAD_DIM = 128
ROPE_DIM = 32
ROPE_THETA = 500000.0
DIL_PATTERNS = ((128, 1), (512, 4), (2048, 16))
DIL_HEADS = 8
WIN_Q_HEADS = 16
WIN_KV_HEADS = 4
WIN_HALF = 128
EPS = 1e-6
NEG_BIG = -1e30
ADAM_LR = 0.001
ADAM_B1 = 0.9
ADAM_B2 = 0.999
ADAM_EPS = 1e-08
ADAM_WD = 0.01
ADAM_STEP = 10

LANES = 128
ATT_BLK = 128
ROW_TILES = (128, 80, 64, 32, 16, 8)
VMEM_LIMIT = 48 * 1024 * 1024
MESH = pl.DeviceIdType.MESH
HIGHEST = lax.Precision.HIGHEST
NT = (((1,), (1,)), ((), ()))
TN = (((0,), (0,)), ((), ()))
NN = (((1,), (0,)), ((), ()))


def _dims(d_model):
    hi = SSD_HEADS * SSD_HEAD_DIM
    gn = SSD_GROUPS * SSD_STATE
    ng = len(DIL_PATTERNS)
    dw = DIL_HEADS * HEAD_DIM
    wq = WIN_Q_HEADS * HEAD_DIM
    wk = WIN_KV_HEADS * HEAD_DIM
    d = dict(D=d_model, HI=hi, GN=gn, XBC=hi + 2 * gn, H2=2 * SSD_HEADS, NG=ng, DW=dw, WQ=wq, WK=wk,
             QKVD=3 * ng * dw, QW=3 * ng * dw + wq + 2 * wk, HG=SSD_HEADS // SSD_GROUPS)
    d["OFF_XBC"] = hi
    d["OFF_QKV"] = hi + d["XBC"]
    d["OFF_GATE"] = d["OFF_QKV"] + d["QW"]
    d["NM"] = d["OFF_GATE"] + 3 * d_model
    return d


def _pick(n, prefs):
    for p in prefs:
        if n % p == 0:
            return p
    return n


def _params(sem):
    return pltpu.CompilerParams(dimension_semantics=sem, vmem_limit_bytes=VMEM_LIMIT)


def _sigmoid(x):
    return 1.0 / (1.0 + jnp.exp(-x))


class _Side:
    def __init__(self, args, outs, sems, start, finish):
        self.args, self.outs, self.sems, self.start, self.finish = args, outs, sems, start, finish


def _mm(a, b, *, ta=False, tb=False, add=None, aux=None, epi=None, out_dtype=BF16, side=None, name):
    if ta:
        kdim, m = a.shape
    else:
        m, kdim = a.shape
    if tb:
        n, k2 = b.shape
    else:
        k2, n = b.shape
    assert kdim == k2, (a.shape, b.shape, ta, tb)
    tm = _pick(m, (1024, 512, 256, 128, 64, 32, 16, 8))
    tn = _pick(n, (1024, 512, 256, 128))
    tk = _pick(kdim, (2048, 1024, 512, 256, 128))
    nk = kdim // tk
    dims = (((0 if ta else 1,), (1 if tb else 0,)), ((), ()))
    n_in = 2 + (add is not None) + (aux is not None)
    n_out = 2 if epi == "relu2" else 1
    n_sin = len(side.args) if side else 0
    n_sout = len(side.outs) if side else 0
    grid = (m // tm, n // tn, nk)

    def body(*refs):
        a_ref, b_ref = refs[0], refs[1]
        pos = 2
        add_ref = aux_ref = None
        if add is not None:
            add_ref = refs[pos]
            pos += 1
        if aux is not None:
            aux_ref = refs[pos]
            pos += 1
        out_refs = refs[n_in + n_sin:n_in + n_sin + n_out]
        acc_ref = refs[n_in + n_sin + n_out + n_sout]
        k = pl.program_id(2)
        if side is not None:
            s_in = refs[n_in:n_in + n_sin]
            s_out = refs[n_in + n_sin + n_out:n_in + n_sin + n_out + n_sout]
            s_sem = refs[n_in + n_sin + n_out + n_sout + 1:]
            i, j = pl.program_id(0), pl.program_id(1)

            @pl.when((i == 0) & (j == 0) & (k == 0))
            def _():
                side.start(s_in, s_out, s_sem)

        @pl.when(k == 0)
        def _():
            acc_ref[...] = jnp.zeros_like(acc_ref)

        acc_ref[...] += lax.dot_general(a_ref[...].astype(BF16), b_ref[...].astype(BF16), dims,
                                        preferred_element_type=F32)

        @pl.when(k == nk - 1)
        def _():
            r = acc_ref[...]
            if add_ref is not None:
                r = r + add_ref[...].astype(F32)
            if epi == "relu2":
                out_refs[0][...] = r.astype(out_refs[0].dtype)
                out_refs[1][...] = jnp.square(jnp.maximum(r, 0.0)).astype(out_refs[1].dtype)
            elif epi == "relu2_bwd":
                out_refs[0][...] = (r * 2.0 * jnp.maximum(aux_ref[...].astype(F32), 0.0)).astype(out_refs[0].dtype)
            else:
                out_refs[0][...] = r.astype(out_refs[0].dtype)

        if side is not None:
            @pl.when((i == grid[0] - 1) & (j == grid[1] - 1) & (k == nk - 1))
            def _():
                side.finish(s_in, s_out, s_sem)

    a_spec = pl.BlockSpec((tk, tm), lambda i, j, k: (k, i)) if ta else pl.BlockSpec((tm, tk), lambda i, j, k: (i, k))
    b_spec = pl.BlockSpec((tn, tk), lambda i, j, k: (j, k)) if tb else pl.BlockSpec((tk, tn), lambda i, j, k: (k, j))
    o_spec = pl.BlockSpec((tm, tn), lambda i, j, k: (i, j))
    in_specs = [a_spec, b_spec]
    args = [a, b]
    if add is not None:
        in_specs.append(o_spec)
        args.append(add)
    if aux is not None:
        in_specs.append(o_spec)
        args.append(aux)
    out_shape = [jax.ShapeDtypeStruct((m, n), out_dtype)] * n_out
    out_specs = [o_spec] * n_out
    scratch = [pltpu.VMEM((tm, tn), F32)]
    sem = ("parallel", "parallel", "arbitrary")
    if side is not None:
        any_spec = pl.BlockSpec(memory_space=pl.ANY)
        in_specs += [any_spec] * n_sin
        args += list(side.args)
        out_shape += list(side.outs)
        out_specs += [any_spec] * n_sout
        scratch += list(side.sems)
        sem = ("arbitrary", "arbitrary", "arbitrary")
    res = pl.pallas_call(
        body, name=name, grid=grid, in_specs=in_specs, out_specs=out_specs, out_shape=out_shape, scratch_shapes=scratch,
        compiler_params=_params(sem),
    )(*args)
    if side is not None:
        return (res[0] if n_out == 1 else tuple(res[:n_out])), list(res[n_out:])
    return res if n_out == 2 else res[0]


def _rowwise(body, rows, fulls, outs, accs=(), *, tile, name):
    rows = [r if isinstance(r, tuple) else (r, 0, r.shape[1]) for r in rows]
    nrows = rows[0][0].shape[0]
    assert nrows % tile == 0, (nrows, tile)
    in_specs, args = [], []
    for arr, off, width in rows:
        assert arr.shape[0] == nrows and off % width == 0, (arr.shape, off, width)
        in_specs.append(pl.BlockSpec((tile, width), lambda i, o=off // width: (i, o)))
        args.append(arr)
    for arr in fulls:
        in_specs.append(pl.BlockSpec(arr.shape, lambda i, nd=arr.ndim: (0,) * nd))
        args.append(arr)
    out_specs, out_shape = [], []
    for cols, dt in outs:
        out_specs.append(pl.BlockSpec((tile, cols), lambda i: (i, 0)))
        out_shape.append(jax.ShapeDtypeStruct((nrows, cols), dt))
    for shp, dt in accs:
        out_specs.append(pl.BlockSpec(shp, lambda i, nd=len(shp): (0,) * nd))
        out_shape.append(jax.ShapeDtypeStruct(shp, dt))
    n_in, n_out = len(args), len(outs)

    def wrapped(*refs):
        acc_refs = refs[n_in + n_out:]
        if acc_refs:
            @pl.when(pl.program_id(0) == 0)
            def _():
                for r in acc_refs:
                    r[...] = jnp.zeros_like(r)
        body(*refs)

    return pl.pallas_call(
        wrapped, name=name, grid=(nrows // tile,), in_specs=in_specs, out_specs=out_specs, out_shape=out_shape,
        compiler_params=_params(("arbitrary",)),
    )(*args)


def _rms_fwd(x, g, name):
    def body(x_ref, g_ref, h_ref):
        xv = x_ref[...]
        rstd = lax.rsqrt(jnp.mean(xv * xv, axis=-1, keepdims=True) + EPS)
        h_ref[...] = (xv * rstd * g_ref[...]).astype(BF16)

    return _rowwise(body, [x], [g], [(x.shape[1], BF16)], tile=256, name=name)[0]


def _rms_bwd(x, g, dh, dres, name):
    def body(x_ref, dh_ref, dres_ref, g_ref, dx_ref, dg_ref):
        xv = x_ref[...]
        dv = dh_ref[...]
        rstd = lax.rsqrt(jnp.mean(xv * xv, axis=-1, keepdims=True) + EPS)
        xn = xv * rstd
        dg_ref[...] += jnp.sum(dv * xn, axis=0, keepdims=True)
        dn = dv * g_ref[...]
        dx_ref[...] = dres_ref[...] + rstd * (dn - xn * jnp.mean(dn * xn, axis=-1, keepdims=True))

    d = x.shape[1]
    return _rowwise(body, [x, dh, dres], [g], [(d, F32)], [((1, d), F32)], tile=256, name=name)


def _loss_head(x, g, target, name):
    d = x.shape[1]

    def body(x_ref, t_ref, g_ref, dx_ref, loss_ref, dg_ref):
        xv = x_ref[...]
        rstd = lax.rsqrt(jnp.mean(xv * xv, axis=-1, keepdims=True) + EPS)
        xn = xv * rstd
        err = xn * g_ref[...] - t_ref[...]
        loss_ref[...] += jnp.full((1, LANES), 0.5 / d, F32) * jnp.sum(err * err)
        dy = err * (1.0 / d)
        dg_ref[...] += jnp.sum(dy * xn, axis=0, keepdims=True)
        dn = dy * g_ref[...]
        dx_ref[...] = rstd * (dn - xn * jnp.mean(dn * xn, axis=-1, keepdims=True))

    return _rowwise(body, [x, target], [g], [(d, F32)], [((1, LANES), F32), ((1, d), F32)], tile=256, name=name)


def _rope_tables(sign):
    half = ROPE_DIM // 2
    inv = ROPE_THETA ** (-jnp.arange(0, ROPE_DIM, 2, dtype=F32) / ROPE_DIM)
    ang = jnp.arange(SEQ, dtype=F32)[:, None] * inv[None, :]
    cos, sin = jnp.cos(ang), jnp.sin(ang) * sign
    zeros = jnp.zeros((SEQ, HEAD_DIM - ROPE_DIM), F32)
    zh = jnp.zeros((SEQ, half), F32)
    c = jnp.concatenate([cos, cos, zeros + 1.0], axis=1)
    s_up = jnp.concatenate([-sin, zh, zeros], axis=1)
    s_dn = jnp.concatenate([zh, sin, zeros], axis=1)
    return c, s_up, s_dn


def _rope(src, off, nblk, tabs, dm, name):
    t = src.shape[0]
    tq = 256
    half = ROPE_DIM // 2
    win0 = 3 * dm["NG"] * DIL_HEADS
    win1 = win0 + WIN_Q_HEADS + WIN_KV_HEADS
    qw = nblk * HEAD_DIM
    assert nblk == dm["QW"] // HEAD_DIM
    wb = next(c for c in (1024, 768, 512, 384, 256, 128) if off % c == 0 and qw % c == 0)
    reps = wb // HEAD_DIM
    sb = SEQ // tq
    flag = (jnp.arange(qw, dtype=jnp.int32) // HEAD_DIM < win1).astype(F32)[None, :]

    def body(x_ref, c_ref, up_ref, dn_ref, f_ref, o_ref):
        xv = x_ref[...].astype(F32)

        def wide(r):
            v = r[...]
            return v if reps == 1 else jnp.concatenate([v] * reps, axis=1)

        rot = xv * wide(c_ref) + pltpu.roll(xv, wb - half, 1) * wide(up_ref) + pltpu.roll(xv, half, 1) * wide(dn_ref)
        o_ref[...] = jnp.where(f_ref[...] > 0.5, rot, xv).astype(BF16)

    tab_spec = pl.BlockSpec((tq, HEAD_DIM), lambda i, j: (i % sb, 0))
    return pl.pallas_call(
        body, name=name, grid=(t // tq, qw // wb),
        in_specs=[pl.BlockSpec((tq, wb), lambda i, j, o=off // wb: (i, o + j)), tab_spec, tab_spec, tab_spec,
                  pl.BlockSpec((1, wb), lambda i, j: (0, j))],
        out_specs=pl.BlockSpec((tq, wb), lambda i, j: (i, j)),
        out_shape=jax.ShapeDtypeStruct((t, qw), BF16),
        compiler_params=_params(("parallel", "parallel")),
    )(src, *tabs, flag)


def _band_mask(rows_start, cols_start, nrows, ncols, w, n, rows_are_q):
    r = rows_start + lax.broadcasted_iota(jnp.int32, (nrows, ncols), 0)
    c = cols_start + lax.broadcasted_iota(jnp.int32, (nrows, ncols), 1)
    del rows_are_q
    return (jnp.abs(r - c) <= w) & (c >= 0) & (c < n)


def _nbr_specs(make, nb):
    if nb == 1:
        return [make(lambda i: i)]
    return [make(lambda i: jnp.maximum(i - 1, 0)), make(lambda i: i), make(lambda i: jnp.minimum(i + 1, nb - 1))]


def _cat(refs, axis):
    vals = [r[...] for r in refs]
    return vals[0] if len(vals) == 1 else jnp.concatenate(vals, axis=axis)


def _head(ref, h):
    return ref[:, h * HEAD_DIM:(h + 1) * HEAD_DIM]


def _head_cat(refs, h, axis=0):
    vals = [_head(r, h) for r in refs]
    return vals[0] if len(vals) == 1 else jnp.concatenate(vals, axis=axis)


def _attn_fwd(qa, ka, va, qb, kb, vb, hq, rep, n, w, sink, out_dtype, name):
    bb = qa.shape[0]
    blk = ATT_BLK
    nb = n // blk
    nk = 1 if nb == 1 else 3
    hkv = hq // rep
    scale = HEAD_DIM ** -0.5
    has_sink = sink is not None

    def body(*refs):
        q_ref = refs[0]
        k_refs = refs[1:1 + nk]
        v_refs = refs[1 + nk:1 + 2 * nk]
        pos = 1 + 2 * nk
        sink_ref = refs[pos] if has_sink else None
        o_ref, lse_ref = refs[pos + has_sink], refs[pos + has_sink + 1]
        i = pl.program_id(1)
        k0 = (i - 1) * blk if nk == 3 else i * blk
        valid = _band_mask(i * blk, k0, blk, nk * blk, w, n, True)
        for g in range(hkv):
            kcat = _head_cat(k_refs, g)
            vcat = _head_cat(v_refs, g)
            for r in range(rep):
                h = g * rep + r
                s = lax.dot_general(_head(q_ref, h), kcat, NT, preferred_element_type=F32) * scale
                s = jnp.where(valid, s, NEG_BIG)
                m = jnp.max(s, axis=1, keepdims=True)
                if has_sink:
                    m = jnp.maximum(m, sink_ref[h])
                p = jnp.exp(s - m)
                l = jnp.sum(p, axis=1, keepdims=True)
                if has_sink:
                    l = l + jnp.exp(sink_ref[h] - m)
                o = lax.dot_general(p.astype(BF16), vcat, NN, preferred_element_type=F32) / l
                o_ref[:, h * HEAD_DIM:(h + 1) * HEAD_DIM] = o.astype(o_ref.dtype)
                lse_ref[h] = m + jnp.log(l)

    def mk(col, width):
        return lambda f: pl.BlockSpec((None, blk, width), lambda b, i, f=f: (b, f(i), col))

    qw, kw = hq * HEAD_DIM, hkv * HEAD_DIM
    in_specs = [pl.BlockSpec((None, blk, qw), lambda b, i: (b, i, qb))]
    in_specs += _nbr_specs(mk(kb, kw), nb) + _nbr_specs(mk(vb, kw), nb)
    args = [qa] + [ka] * nk + [va] * nk
    if has_sink:
        in_specs.append(pl.BlockSpec((hq, 1, 1), lambda b, i: (0, 0, 0)))
        args.append(sink)
    return pl.pallas_call(
        body, name=name, grid=(bb, nb), in_specs=in_specs,
        out_specs=[pl.BlockSpec((None, blk, qw), lambda b, i: (b, i, 0)),
                   pl.BlockSpec((None, hq, blk, 1), lambda b, i: (b, 0, i, 0))],
        out_shape=[jax.ShapeDtypeStruct((bb, n, qw), out_dtype), jax.ShapeDtypeStruct((bb, hq, n, 1), F32)],
        compiler_params=_params(("parallel", "parallel")),
    )(*args)


def _attn_dq(qa, ka, va, do, lse, delta, qb, kb, vb, hq, rep, n, w, name):
    bb = qa.shape[0]
    blk = ATT_BLK
    nb = n // blk
    nk = 1 if nb == 1 else 3
    hkv = hq // rep
    scale = HEAD_DIM ** -0.5

    def body(*refs):
        q_ref = refs[0]
        k_refs = refs[1:1 + nk]
        v_refs = refs[1 + nk:1 + 2 * nk]
        do_ref, lse_ref, dl_ref, dq_ref = refs[1 + 2 * nk:]
        i = pl.program_id(1)
        k0 = (i - 1) * blk if nk == 3 else i * blk
        valid = _band_mask(i * blk, k0, blk, nk * blk, w, n, True)
        for g in range(hkv):
            kcat = _head_cat(k_refs, g)
            vcat = _head_cat(v_refs, g)
            for r in range(rep):
                h = g * rep + r
                s = lax.dot_general(_head(q_ref, h), kcat, NT, preferred_element_type=F32) * scale
                p = jnp.exp(jnp.where(valid, s, NEG_BIG) - lse_ref[h])
                dp = lax.dot_general(_head(do_ref, h).astype(BF16), vcat, NT, preferred_element_type=F32)
                ds = p * (dp - dl_ref[h])
                dq = lax.dot_general(ds.astype(BF16), kcat, NN, preferred_element_type=F32) * scale
                dq_ref[:, h * HEAD_DIM:(h + 1) * HEAD_DIM] = dq.astype(BF16)

    def mk(col, width):
        return lambda f: pl.BlockSpec((None, blk, width), lambda b, i, f=f: (b, f(i), col))

    qw, kw = hq * HEAD_DIM, hkv * HEAD_DIM
    col_spec = pl.BlockSpec((None, hq, blk, 1), lambda b, i: (b, 0, i, 0))
    in_specs = [pl.BlockSpec((None, blk, qw), lambda b, i: (b, i, qb))]
    in_specs += _nbr_specs(mk(kb, kw), nb) + _nbr_specs(mk(vb, kw), nb)
    in_specs += [pl.BlockSpec((None, blk, qw), lambda b, i: (b, i, 0)), col_spec, col_spec]
    return pl.pallas_call(
        body, name=name, grid=(bb, nb), in_specs=in_specs,
        out_specs=pl.BlockSpec((None, blk, qw), lambda b, i: (b, i, 0)),
        out_shape=jax.ShapeDtypeStruct((bb, n, qw), BF16),
        compiler_params=_params(("parallel", "parallel")),
    )(qa, *([ka] * nk), *([va] * nk), do, lse, delta)


def _attn_dkv(qa, ka, va, do, lse_row, delta_row, qb, kb, vb, hq, rep, n, w, name):
    bb = qa.shape[0]
    blk = ATT_BLK
    nb = n // blk
    nq = 1 if nb == 1 else 3
    hkv = hq // rep
    scale = HEAD_DIM ** -0.5

    def body(*refs):
        k_ref, v_ref = refs[0], refs[1]
        q_refs = refs[2:2 + nq]
        do_refs = refs[2 + nq:2 + 2 * nq]
        lse_refs = refs[2 + 2 * nq:2 + 3 * nq]
        dl_refs = refs[2 + 3 * nq:2 + 4 * nq]
        dk_ref, dv_ref = refs[2 + 4 * nq:]
        j = pl.program_id(1)
        q0 = (j - 1) * blk if nq == 3 else j * blk
        valid = _band_mask(j * blk, q0, blk, nq * blk, w, n, False)
        for g in range(hkv):
            kg, vg = _head(k_ref, g), _head(v_ref, g)
            dk = jnp.zeros((blk, HEAD_DIM), F32)
            dv = jnp.zeros((blk, HEAD_DIM), F32)
            for r in range(rep):
                h = g * rep + r
                qcat = _head_cat(q_refs, h)
                docat = _head_cat(do_refs, h).astype(BF16)
                lse = lse_refs[0][h] if nq == 1 else jnp.concatenate([lr[h] for lr in lse_refs], axis=1)
                dl = dl_refs[0][h] if nq == 1 else jnp.concatenate([dr[h] for dr in dl_refs], axis=1)
                st = lax.dot_general(kg, qcat, NT, preferred_element_type=F32) * scale
                pt = jnp.exp(jnp.where(valid, st, NEG_BIG) - lse)
                dv = dv + lax.dot_general(pt.astype(BF16), docat, NN, preferred_element_type=F32)
                dpt = lax.dot_general(vg, docat, NT, preferred_element_type=F32)
                dst = pt * (dpt - dl)
                dk = dk + lax.dot_general(dst.astype(BF16), qcat, NN, preferred_element_type=F32) * scale
            dk_ref[:, g * HEAD_DIM:(g + 1) * HEAD_DIM] = dk.astype(BF16)
            dv_ref[:, g * HEAD_DIM:(g + 1) * HEAD_DIM] = dv.astype(BF16)

    qw, kw = hq * HEAD_DIM, hkv * HEAD_DIM

    def mkq(col):
        return lambda f: pl.BlockSpec((None, blk, qw), lambda b, j, f=f: (b, f(j), col))

    def mkrow(f):
        return pl.BlockSpec((None, hq, 1, blk), lambda b, j, f=f: (b, 0, 0, f(j)))

    in_specs = [pl.BlockSpec((None, blk, kw), lambda b, j: (b, j, kb)), pl.BlockSpec((None, blk, kw), lambda b, j: (b, j, vb))]
    in_specs += _nbr_specs(mkq(qb), nb) + _nbr_specs(mkq(0), nb) + _nbr_specs(mkrow, nb) + _nbr_specs(mkrow, nb)
    o_spec = pl.BlockSpec((None, blk, kw), lambda b, j: (b, j, 0))
    return pl.pallas_call(
        body, name=name, grid=(bb, nb), in_specs=in_specs, out_specs=[o_spec, o_spec],
        out_shape=[jax.ShapeDtypeStruct((bb, n, kw), BF16)] * 2,
        compiler_params=_params(("parallel", "parallel")),
    )(ka, va, *([qa] * nq), *([do] * nq), *([lse_row] * nq), *([delta_row] * nq))


def _head_expand(v, nh, width):
    lane_head = lax.broadcasted_iota(jnp.int32, (1, nh * width), 1) >> int(math.log2(width))
    out = jnp.zeros((v.shape[0], nh * width), F32)
    for j in range(nh):
        out = jnp.where(lane_head == j, v[:, j:j + 1], out)
    return out


def _head_sums(m, nh, width):
    lane_head = lax.broadcasted_iota(jnp.int32, (1, nh * width), 1) >> int(math.log2(width))
    col = lax.broadcasted_iota(jnp.int32, (1, nh), 1)
    out = jnp.zeros((m.shape[0], nh), F32)
    for j in range(nh):
        sj = jnp.sum(jnp.where(lane_head == j, m, 0.0), axis=1, keepdims=True)
        out = jnp.where(col == j, sj, out)
    return out


def _head_dots(a, b, nh, name):
    def body(a_ref, b_ref, o_ref):
        o_ref[...] = _head_sums(a_ref[...].astype(F32) * b_ref[...].astype(F32), nh, HEAD_DIM)

    return _rowwise(body, [a, b], [], [(nh, F32)], tile=256, name=name)[0]


def _dil_combine(outs, lses, name):
    ng = len(outs)

    def body(*refs):
        o_refs, l_refs = refs[:ng], refs[ng:2 * ng]
        y_ref, lt_ref = refs[2 * ng], refs[2 * ng + 1]
        ls = [r[...] for r in l_refs]
        m = functools.reduce(jnp.maximum, ls)
        es = [jnp.exp(v - m) for v in ls]
        tot = functools.reduce(jnp.add, es)
        acc = jnp.zeros(o_refs[0].shape, F32)
        for o_ref, e in zip(o_refs, es):
            acc = acc + _head_expand(e / tot, DIL_HEADS, HEAD_DIM) * o_ref[...]
        y_ref[...] = acc.astype(BF16)
        lt_ref[...] = m + jnp.log(tot)

    dw = outs[0].shape[1]
    return _rowwise(body, list(outs) + list(lses), [], [(dw, BF16), (DIL_HEADS, F32)], tile=256, name=name)


def _sink_grad(lse, delta, sink, name):
    def body(l_ref, d_ref, s_ref, o_ref):
        o_ref[...] -= jnp.sum(jnp.exp(s_ref[...] - l_ref[...]) * d_ref[...], axis=0, keepdims=True)

    return _rowwise(body, [lse, delta], [sink], [], [((1, lse.shape[1]), F32)], tile=512, name=name)[0]


def _shift_rows(x, d, nrows):
    if d == 0:
        return x
    rolled = pltpu.roll(x, (-d) % nrows, 0)
    row = lax.broadcasted_iota(jnp.int32, x.shape, 0)
    ok = (row + d >= 0) & (row + d < nrows)
    return jnp.where(ok, rolled, 0.0)


def _conv_fwd(proj, off, conv_w, conv_b, xbc, name):
    t = proj.shape[0]
    tc = _pick(xbc, (256, 128))
    assert off % tc == 0
    pad = (CONV_WIDTH - 1) // 2

    def body(x_ref, w_ref, b_ref, c_ref, u_ref):
        xv = x_ref[...].astype(F32)
        acc = jnp.zeros_like(xv) + b_ref[...]
        for k in range(CONV_WIDTH):
            acc = acc + w_ref[k:k + 1, :] * _shift_rows(xv, k - pad, SEQ)
        c_ref[...] = acc.astype(BF16)
        u_ref[...] = (acc * _sigmoid(acc)).astype(BF16)

    o_spec = pl.BlockSpec((SEQ, tc), lambda b, j: (b, j))
    return pl.pallas_call(
        body, name=name, grid=(t // SEQ, xbc // tc),
        in_specs=[pl.BlockSpec((SEQ, tc), lambda b, j, o=off // tc: (b, o + j)),
                  pl.BlockSpec((CONV_WIDTH, tc), lambda b, j: (0, j)), pl.BlockSpec((1, tc), lambda b, j: (0, j))],
        out_specs=[o_spec, o_spec], out_shape=[jax.ShapeDtypeStruct((t, xbc), BF16)] * 2,
        compiler_params=_params(("parallel", "parallel")),
    )(proj, conv_w, conv_b)


def _conv_bwd(du, cpre, proj, off, conv_w, name):
    t, xbc = du.shape
    tc = _pick(xbc, (256, 128))
    pad = (CONV_WIDTH - 1) // 2

    def body(du_ref, c_ref, x_ref, w_ref, dx_ref, dw_ref, db_ref):
        @pl.when(pl.program_id(1) == 0)
        def _():
            dw_ref[...] = jnp.zeros_like(dw_ref)
            db_ref[...] = jnp.zeros_like(db_ref)

        cv = c_ref[...].astype(F32)
        sg = _sigmoid(cv)
        dc = du_ref[...] * (sg * (1.0 + cv * (1.0 - sg)))
        xv = x_ref[...].astype(F32)
        dx = jnp.zeros_like(dc)
        for k in range(CONV_WIDTH):
            dx = dx + w_ref[k:k + 1, :] * _shift_rows(dc, pad - k, SEQ)
            dw_ref[k:k + 1, :] += jnp.sum(dc * _shift_rows(xv, k - pad, SEQ), axis=0, keepdims=True)
        db_ref[...] += jnp.sum(dc, axis=0, keepdims=True)
        dx_ref[...] = dx.astype(BF16)

    blk = pl.BlockSpec((SEQ, tc), lambda j, b: (b, j))
    return pl.pallas_call(
        body, name=name, grid=(xbc // tc, t // SEQ),
        in_specs=[blk, blk, pl.BlockSpec((SEQ, tc), lambda j, b, o=off // tc: (b, o + j)),
                  pl.BlockSpec((CONV_WIDTH, tc), lambda j, b: (0, j))],
        out_specs=[blk, pl.BlockSpec((CONV_WIDTH, tc), lambda j, b: (0, j)), pl.BlockSpec((1, tc), lambda j, b: (0, j))],
        out_shape=[jax.ShapeDtypeStruct((t, xbc), BF16), jax.ShapeDtypeStruct((CONV_WIDTH, xbc), F32),
                   jax.ShapeDtypeStruct((1, xbc), F32)],
        compiler_params=_params(("parallel", "arbitrary")),
    )(du, cpre, proj, conv_w)


def _dt_prep(dtr, bias, name):
    def body(r_ref, b_ref, o_ref):
        v = r_ref[...] + b_ref[...]
        o_ref[...] = jnp.maximum(v, 0.0) + jnp.log1p(jnp.exp(-jnp.abs(v)))

    return _rowwise(body, [dtr], [bias], [(dtr.shape[1], F32)], tile=512, name=name)[0]


def _scan_prelude(d, dt_ref, dtt_ref, al_ref, alt_ref, hg):
    p = SSD_HEAD_DIM
    ch = SSD_CHUNK
    a_row = -jnp.exp(al_ref[...])
    a_col = -jnp.exp(alt_ref[...])
    dtc = dt_ref[...]
    dt_x = _head_expand(dtc, hg, p)
    dta_x = dt_x * _head_expand(a_row, hg, p)
    dta_t = dtt_ref[...] * a_col
    ri = lax.broadcasted_iota(jnp.int32, (ch, ch), 0)
    ci = lax.broadcasted_iota(jnp.int32, (ch, ch), 1)
    mask = (ci <= ri) if d == 0 else (ci >= ri)
    mask_t = (ci >= ri) if d == 0 else (ci <= ri)
    tri = mask.astype(F32)
    phi_x = jnp.dot(tri, dta_x, preferred_element_type=F32, precision=HIGHEST)
    phi_r = lax.dot_general(dta_t, tri, NT, preferred_element_type=F32, precision=HIGHEST)
    tot_x = jnp.sum(dta_x, axis=0, keepdims=True)
    return dtc, dt_x, phi_x, phi_r, tot_x, mask, mask_t


def _scan_specs(d, nc, hg, dm):
    p, n, ch = SSD_HEAD_DIM, SSD_STATE, SSD_CHUNK
    w = hg * p
    b0 = dm["HI"] // n
    c0 = (dm["HI"] + dm["GN"]) // n

    def row(b, c):
        return b * nc + c

    return [
        pl.BlockSpec((ch, w), lambda b, g, c: (row(b, c), g)),
        pl.BlockSpec((ch, n), lambda b, g, c: (row(b, c), b0 + g)),
        pl.BlockSpec((ch, n), lambda b, g, c: (row(b, c), c0 + g)),
        pl.BlockSpec((None, None, ch, hg), lambda b, g, c: (d, g, row(b, c), 0)),
        pl.BlockSpec((None, None, 8, ch), lambda b, g, c: (d, g, 0, row(b, c))),
        pl.BlockSpec((None, None, 1, hg), lambda b, g, c: (d, g, 0, 0)),
        pl.BlockSpec((None, None, 8, 1), lambda b, g, c: (d, g, 0, 0)),
    ]


def _remap(spec, f):
    return pl.BlockSpec(spec.block_shape, lambda b, g, c, im=spec.index_map: im(b, g, f(c)))


def _scan_fwd(u, dtg, dttg, alg, altg, d, dm, name):
    t = u.shape[0]
    p, n, ch, hg = SSD_HEAD_DIM, SSD_STATE, SSD_CHUNK, dm["HG"]
    w = hg * p
    nc = SEQ // ch
    order = (lambda c: c) if d == 0 else (lambda c: nc - 1 - c)

    def body(x_ref, b_ref, c_ref, dt_ref, dtt_ref, al_ref, alt_ref, y_ref, sin_ref, s_ref):
        @pl.when(pl.program_id(2) == 0)
        def _():
            s_ref[...] = jnp.zeros_like(s_ref)

        dtc, dt_x, phi_x, phi_r, tot_x, mask, _ = _scan_prelude(d, dt_ref, dtt_ref, al_ref, alt_ref, hg)
        lane_head = lax.broadcasted_iota(jnp.int32, (1, w), 1) >> int(math.log2(p))
        cm, bm = c_ref[...], b_ref[...]
        cb = lax.dot_general(cm, bm, NT, preferred_element_type=F32)
        xdt = x_ref[...].astype(F32) * dt_x
        xdt_b = xdt.astype(BF16)
        ydiag = jnp.zeros((ch, w), F32)
        for j in range(hg):
            seg = phi_x[:, j * p:j * p + 1] - phi_r[j:j + 1, :]
            mj = (cb * jnp.exp(jnp.where(mask, seg, NEG_BIG))).astype(BF16)
            ydiag = ydiag + jnp.dot(mj, jnp.where(lane_head == j, xdt_b, jnp.zeros_like(xdt_b)), preferred_element_type=F32)
        s = s_ref[...]
        y_ref[...] = ydiag + jnp.dot(cm, s.astype(BF16), preferred_element_type=F32) * jnp.exp(phi_x)
        sin_ref[...] = s
        wm = (xdt * jnp.exp(tot_x - phi_x)).astype(BF16)
        s_ref[...] = s * jnp.exp(tot_x) + lax.dot_general(bm, wm, TN, preferred_element_type=F32)

    specs = [_remap(s, order) for s in _scan_specs(d, nc, hg, dm)]
    return pl.pallas_call(
        body, name=name, grid=(t // SEQ, SSD_GROUPS, nc), in_specs=specs,
        out_specs=[_remap(pl.BlockSpec((ch, w), lambda b, g, c: (b * nc + c, g)), order),
                   _remap(pl.BlockSpec((None, None, n, w), lambda b, g, c: (b * nc + c, g, 0, 0)), order)],
        out_shape=[jax.ShapeDtypeStruct((t, dm["HI"]), F32), jax.ShapeDtypeStruct((t // ch, SSD_GROUPS, n, w), F32)],
        scratch_shapes=[pltpu.VMEM((n, w), F32)],
        compiler_params=_params(("parallel", "parallel", "arbitrary")),
    )(u, u, u, dtg, dttg, alg, altg)


def _scan_bwd(u, dtg, dttg, alg, altg, dy, sin, adds, d, dm, name):
    t = u.shape[0]
    p, n, ch, hg = SSD_HEAD_DIM, SSD_STATE, SSD_CHUNK, dm["HG"]
    w = hg * p
    nc = SEQ // ch
    order = (lambda c: nc - 1 - c) if d == 0 else (lambda c: c)
    has_bc_add = adds[1] is not None

    def body(*refs):
        x_ref, b_ref, c_ref, dt_ref, dtt_ref, al_ref, alt_ref, dy_ref, sin_ref, ax_ref = refs[:10]
        pos = 10
        ab_ref = ac_ref = None
        if has_bc_add:
            ab_ref, ac_ref = refs[10], refs[11]
            pos = 12
        dxs_ref, db_ref, dc_ref, rq_ref, xdx_ref, ds_ref = refs[pos:]

        @pl.when(pl.program_id(2) == 0)
        def _():
            ds_ref[...] = jnp.zeros_like(ds_ref)

        dtc, dt_x, phi_x, phi_r, tot_x, mask, mask_t = _scan_prelude(d, dt_ref, dtt_ref, al_ref, alt_ref, hg)
        lane_head = lax.broadcasted_iota(jnp.int32, (1, w), 1) >> int(math.log2(p))
        cm, bm = c_ref[...], b_ref[...]
        cb = lax.dot_general(cm, bm, NT, preferred_element_type=F32)
        cb_t = lax.dot_general(bm, cm, NT, preferred_element_type=F32)
        xs = x_ref[...].astype(F32)
        xdt = xs * dt_x
        xdt_b = xdt.astype(BF16)
        dy = dy_ref[...]
        dy_b = dy.astype(BF16)
        zero_b = jnp.zeros_like(dy_b)
        col = lax.broadcasted_iota(jnp.int32, (1, hg), 1)
        dxp = jnp.zeros((ch, w), F32)
        a_ls = jnp.zeros((ch, ch), F32)
        a_sl = jnp.zeros((ch, ch), F32)
        dphi = jnp.zeros((ch, hg), F32)
        for j in range(hg):
            pc = phi_x[:, j * p:j * p + 1]
            pr = phi_r[j:j + 1, :]
            l_ls = jnp.exp(jnp.where(mask, pc - pr, NEG_BIG))
            l_sl = jnp.exp(jnp.where(mask_t, pr - pc, NEG_BIG))
            dy_j = jnp.where(lane_head == j, dy_b, zero_b)
            xdt_j = jnp.where(lane_head == j, xdt_b, zero_b)
            dxp = dxp + jnp.dot((cb_t * l_sl).astype(BF16), dy_j, preferred_element_type=F32)
            g_ls = l_ls * lax.dot_general(dy_j, xdt_b, NT, preferred_element_type=F32)
            g_sl = l_sl * lax.dot_general(xdt_j, dy_b, NT, preferred_element_type=F32)
            a_ls = a_ls + g_ls
            a_sl = a_sl + g_sl
            pair = jnp.sum(g_ls * cb, axis=1, keepdims=True) - jnp.sum(g_sl * cb_t, axis=1, keepdims=True)
            dphi = jnp.where(col == j, pair, dphi)
        ds = ds_ref[...]
        ds_b = ds.astype(BF16)
        sin = sin_ref[...]
        sin_b = sin.astype(BF16)
        e_tp = jnp.exp(tot_x - phi_x)
        e_p = jnp.exp(phi_x)
        dxp_off = e_tp * jnp.dot(bm, ds_b, preferred_element_type=F32)
        dxp = dxp + dxp_off
        dxs_ref[...] = ax_ref[...] + dxp * dt_x
        xdx_ref[...] = _head_sums(xs * dxp, hg, p)
        y_off = jnp.dot(cm, sin_b, preferred_element_type=F32) * e_p
        st_t = _head_sums(xdt * dxp_off, hg, p)
        dphi = dphi + _head_sums(dy * y_off, hg, p) - st_t
        dtot = _head_sums(jnp.sum(ds * sin, axis=0, keepdims=True) * jnp.exp(tot_x), hg, p) + jnp.sum(st_t, axis=0, keepdims=True)
        cum = jnp.dot(mask_t.astype(F32), _head_expand(dphi, hg, p), preferred_element_type=F32, precision=HIGHEST)
        ddta = jnp.zeros((ch, hg), F32)
        for j in range(hg):
            ddta = jnp.where(col == j, cum[:, j * p:j * p + 1], ddta)
        rq_ref[...] = ddta + dtot
        dye = (dy * e_p).astype(BF16)
        dcv = jnp.dot(a_ls.astype(BF16), bm, preferred_element_type=F32)
        dcv = dcv + lax.dot_general(dye, sin_b, NT, preferred_element_type=F32)
        dbv = jnp.dot(a_sl.astype(BF16), cm, preferred_element_type=F32)
        dbv = dbv + lax.dot_general((xdt * e_tp).astype(BF16), ds_b, NT, preferred_element_type=F32)
        if has_bc_add:
            dcv = dcv + ac_ref[...]
            dbv = dbv + ab_ref[...]
        dc_ref[...] = dcv
        db_ref[...] = dbv
        ds_ref[...] = ds * jnp.exp(tot_x) + lax.dot_general(cm, dye, TN, preferred_element_type=F32)

    def sp(spec):
        return _remap(spec, order)

    xw = pl.BlockSpec((ch, w), lambda b, g, c: (b * nc + c, g))
    gn_blk = pl.BlockSpec((ch, n), lambda b, g, c: (b * nc + c, g))
    small = pl.BlockSpec((None, ch, hg), lambda b, g, c: (g, b * nc + c, 0))
    in_specs = [sp(s) for s in _scan_specs(d, nc, hg, dm)]
    in_specs += [sp(xw), sp(pl.BlockSpec((None, None, n, w), lambda b, g, c: (b * nc + c, g, 0, 0))), sp(xw)]
    args = [u, u, u, dtg, dttg, alg, altg, dy, sin, adds[0]]
    if has_bc_add:
        in_specs += [sp(gn_blk), sp(gn_blk)]
        args += [adds[1], adds[2]]
    return pl.pallas_call(
        body, name=name, grid=(t // SEQ, SSD_GROUPS, nc), in_specs=in_specs,
        out_specs=[sp(xw), sp(gn_blk), sp(gn_blk), sp(small), sp(small)],
        out_shape=[jax.ShapeDtypeStruct((t, dm["HI"]), F32), jax.ShapeDtypeStruct((t, dm["GN"]), F32),
                   jax.ShapeDtypeStruct((t, dm["GN"]), F32), jax.ShapeDtypeStruct((SSD_GROUPS, t, hg), F32),
                   jax.ShapeDtypeStruct((SSD_GROUPS, t, hg), F32)],
        scratch_shapes=[pltpu.VMEM((n, w), F32)],
        compiler_params=_params(("parallel", "parallel", "arbitrary")),
    )(*args)


def _ssd_param_bwd(rq_f, rq_r, xdx, dtp, dtr, bias, alog, name):
    def body(rf_ref, rr_ref, xdx_ref, dt_ref, dtr_ref, b_ref, al_ref, o_ref, db_ref, da_ref):
        a = -jnp.exp(al_ref[...])
        d_dta = rf_ref[...] + rr_ref[...]
        ddt = a * d_dta + xdx_ref[...]
        ddtr = ddt * _sigmoid(dtr_ref[...] + b_ref[...])
        o_ref[...] = ddtr
        db_ref[...] += jnp.sum(ddtr, axis=0, keepdims=True)
        da_ref[...] += a * jnp.sum(dt_ref[...] * d_dta, axis=0, keepdims=True)

    return _rowwise(body, [rq_f, rq_r, xdx, dtp, dtr], [bias, alog], [(LANES, F32)],
                    [((1, LANES), F32), ((1, LANES), F32)], tile=512, name=name)


def _ssd_out_fwd(y_f, y_b, u, proj, dcols, gn, hi, name):
    def body(yf_ref, yb_ref, x_ref, z_ref, d_ref, g_ref, o_ref):
        ytot = yf_ref[...] + yb_ref[...] + d_ref[...] * x_ref[...].astype(F32)
        zv = z_ref[...].astype(F32)
        yz = ytot * (zv * _sigmoid(zv))
        rstd = lax.rsqrt(jnp.mean(yz * yz, axis=-1, keepdims=True) + EPS)
        o_ref[...] = (yz * rstd * g_ref[...]).astype(BF16)

    return _rowwise(body, [y_f, y_b, (u, 0, hi), (proj, 0, hi)], [dcols, gn], [(hi, BF16)], tile=256, name=name)[0]


def _ssd_out_bwd(dya, y_f, y_b, u, proj, dcols, gn, hi, name):
    def body(dy_ref, yf_ref, yb_ref, x_ref, z_ref, d_ref, g_ref, dyt_ref, dxs_ref, dz_ref, dg_ref, dd_ref):
        xv = x_ref[...].astype(F32)
        ytot = yf_ref[...] + yb_ref[...] + d_ref[...] * xv
        zv = z_ref[...].astype(F32)
        sg = _sigmoid(zv)
        sz = zv * sg
        yz = ytot * sz
        rstd = lax.rsqrt(jnp.mean(yz * yz, axis=-1, keepdims=True) + EPS)
        yn = yz * rstd
        dv = dy_ref[...]
        dg_ref[...] += jnp.sum(dv * yn, axis=0, keepdims=True)
        dn = dv * g_ref[...]
        dyz = rstd * (dn - yn * jnp.mean(dn * yn, axis=-1, keepdims=True))
        dyt = dyz * sz
        dyt_ref[...] = dyt
        dxs_ref[...] = dyt * d_ref[...]
        dz_ref[...] = (dyz * ytot * (sg * (1.0 + zv * (1.0 - sg)))).astype(BF16)
        dd_ref[...] += jnp.sum(dyt * xv, axis=0, keepdims=True)

    return _rowwise(body, [dya, y_f, y_b, (u, 0, hi), (proj, 0, hi)], [dcols, gn],
                    [(hi, F32), (hi, F32), (hi, BF16)], [((1, hi), F32), ((1, hi), F32)], tile=128, name=name)


def _gate_fwd(pa, pb, pc, proj, off, d, name):
    def body(a_ref, b_ref, c_ref, g0_ref, g1_ref, g2_ref, o_ref):
        acc = _sigmoid(g0_ref[...].astype(F32)) * a_ref[...]
        acc = acc + _sigmoid(g1_ref[...].astype(F32)) * b_ref[...]
        acc = acc + _sigmoid(g2_ref[...].astype(F32)) * c_ref[...]
        o_ref[...] = acc.astype(BF16)

    rows = [pa, pb, pc] + [(proj, off + k * d, d) for k in range(3)]
    return _rowwise(body, rows, [], [(d, BF16)], tile=256, name=name)[0]


def _gate_bwd(dm_, pa, pb, pc, proj, off, d, name):
    def body(dm_ref, a_ref, b_ref, c_ref, g0_ref, g1_ref, g2_ref, da_ref, db_ref, dc_ref, dg0_ref, dg1_ref, dg2_ref):
        dmv = dm_ref[...]
        for p_ref, g_ref, dp_ref, dg_ref in ((a_ref, g0_ref, da_ref, dg0_ref), (b_ref, g1_ref, db_ref, dg1_ref),
                                             (c_ref, g2_ref, dc_ref, dg2_ref)):
            sg = _sigmoid(g_ref[...].astype(F32))
            dp_ref[...] = (dmv * sg).astype(BF16)
            dg_ref[...] = (dmv * p_ref[...] * sg * (1.0 - sg)).astype(BF16)

    rows = [dm_, pa, pb, pc] + [(proj, off + k * d, d) for k in range(3)]
    return _rowwise(body, rows, [], [(d, BF16)] * 6, tile=128, name=name)


def _adamw(w, g, m, v, name):
    nl, rows, cols = w.shape
    tile = _pick(rows, ROW_TILES)
    c1 = 1.0 / (1.0 - ADAM_B1 ** ADAM_STEP)
    c2 = 1.0 / (1.0 - ADAM_B2 ** ADAM_STEP)

    def body(w_ref, g_ref, m_ref, v_ref, d_ref, nm_ref, nv_ref):
        gv = g_ref[...]
        nm = ADAM_B1 * m_ref[...] + (1.0 - ADAM_B1) * gv
        nv = ADAM_B2 * v_ref[...] + (1.0 - ADAM_B2) * (gv * gv)
        nm_ref[...] = nm
        nv_ref[...] = nv
        d_ref[...] = -ADAM_LR * ((nm * c1) / (jnp.sqrt(nv * c2) + ADAM_EPS) + ADAM_WD * w_ref[...])

    blk = pl.BlockSpec((None, tile, cols), lambda l, i: (l, i, 0))
    return pl.pallas_call(
        body, name=name, grid=(nl, rows // tile), in_specs=[blk] * 4, out_specs=[blk] * 3,
        out_shape=[jax.ShapeDtypeStruct(w.shape, F32)] * 3, compiler_params=_params(("parallel", "parallel")),
    )(w, g, m, v)


ANY = pl.BlockSpec(memory_space=pl.ANY)


def _place():
    x, y, c = lax.axis_index("x"), lax.axis_index("y"), lax.axis_index("c")
    chips = [(1 - x, y), (x, 1 - y), (1 - x, 1 - y)]
    return x, y, c, chips


def _gather_copies(src, out, ssem, rsem, base):
    x, y, c, chips = _place()
    k = 2 * x + y

    def copy(j, kk, layer, to, own=False):
        return pltpu.make_async_remote_copy(
            src_ref=src.at[layer] if own else out.at[layer, kk], dst_ref=out.at[layer, kk],
            send_sem=ssem.at[base + j], recv_sem=rsem.at[base + j], device_id=to, device_id_type=MESH)

    first = [copy(j, k, c, (cx, cy, c), own=True) for j, (cx, cy) in enumerate(chips)]
    passed = [copy(3 + j, 2 * cx + cy, c, (x, y, 1 - c)) for j, (cx, cy) in enumerate(chips)]
    landed = [copy(j, 2 * cx + cy, c, (x, y, c)) for j, (cx, cy) in enumerate(chips)]
    handed = [copy(3 + j, 2 * cx + cy, 1 - c, (x, y, c)) for j, (cx, cy) in enumerate(chips)]
    return first, passed, landed, handed


def _gather_start(src, out, ssem, rsem, base):
    for cp in _gather_copies(src, out, ssem, rsem, base)[0]:
        cp.start()


def _gather_finish(src, out, ssem, rsem, base):
    first, passed, landed, handed = _gather_copies(src, out, ssem, rsem, base)
    for arrived, onward in zip(landed, passed):
        arrived.wait_recv()
        onward.start()
    for arrived in handed:
        arrived.wait_recv()
    for cp in first + passed:
        cp.wait_send()


def _gather_side(arrs):
    n = len(arrs)

    def start(srcs, outs, sems):
        for a in range(n):
            _gather_start(srcs[a], outs[a], sems[0], sems[1], 6 * a)

    def finish(srcs, outs, sems):
        for a in range(n):
            _gather_finish(srcs[a], outs[a], sems[0], sems[1], 6 * a)

    outs = [jax.ShapeDtypeStruct((a.shape[0], 4) + a.shape[1:], a.dtype) for a in arrs]
    return _Side(list(arrs), outs, [pltpu.SemaphoreType.DMA((6 * n,)), pltpu.SemaphoreType.DMA((6 * n,))], start, finish)


def _put_own(st, arr):
    kchip = 2 * lax.axis_index("x") + lax.axis_index("y")
    return lax.dynamic_update_slice(st, arr[:, None], (0, kchip) + (0,) * (arr.ndim - 1))


def _chip_copies(src, out, ssem, rsem):
    x, y, c, chips = _place()
    return [pltpu.make_async_remote_copy(src_ref=src.at[2 * cx + cy], dst_ref=out.at[j], send_sem=ssem.at[j],
                                         recv_sem=rsem.at[j], device_id=(cx, cy, c), device_id_type=MESH)
            for j, (cx, cy) in enumerate(chips)]


def _chips_side(p):
    def start(srcs, outs, sems):
        for cp in _chip_copies(srcs[0], outs[0], sems[0], sems[1]):
            cp.start()

    def finish(srcs, outs, sems):
        for cp in _chip_copies(srcs[0], outs[0], sems[0], sems[1]):
            cp.wait()

    return _Side([p], [jax.ShapeDtypeStruct((3,) + p.shape[1:], p.dtype)],
                 [pltpu.SemaphoreType.DMA((3,)), pltpu.SemaphoreType.DMA((3,))], start, finish)


def _gather_chips(arr, name):
    def body(src, out, ssem, rsem):
        _gather_start(src, out, ssem, rsem, 0)
        _gather_finish(src, out, ssem, rsem, 0)

    st = pl.pallas_call(
        body, name=name, in_specs=[ANY], out_specs=ANY,
        out_shape=jax.ShapeDtypeStruct((arr.shape[0], 4) + arr.shape[1:], arr.dtype),
        scratch_shapes=[pltpu.SemaphoreType.DMA((6,)), pltpu.SemaphoreType.DMA((6,))],
    )(arr)
    return _put_own(st, arr)


def _pair_swap(g, name):
    def body(src, out, ssem, rsem):
        x, y, c, _ = _place()
        cp = pltpu.make_async_remote_copy(src_ref=src.at[1 - c], dst_ref=out, send_sem=ssem, recv_sem=rsem,
                                          device_id=(x, y, 1 - c), device_id_type=MESH)
        cp.start()
        cp.wait()

    return pl.pallas_call(
        body, name=name, in_specs=[ANY], out_specs=ANY, out_shape=jax.ShapeDtypeStruct(g.shape[1:], g.dtype),
        scratch_shapes=[pltpu.SemaphoreType.DMA, pltpu.SemaphoreType.DMA],
    )(g)


def _chip_exchange(p, name):
    def body(src, out, ssem, rsem):
        cps = _chip_copies(src, out, ssem, rsem)
        for cp in cps:
            cp.start()
        for cp in cps:
            cp.wait()

    return pl.pallas_call(
        body, name=name, in_specs=[ANY], out_specs=ANY, out_shape=jax.ShapeDtypeStruct((3,) + p.shape[1:], p.dtype),
        scratch_shapes=[pltpu.SemaphoreType.DMA((3,)), pltpu.SemaphoreType.DMA((3,))],
    )(p)


def _pair_share(r, name):
    def body(src, out, ssem, rsem):
        x, y, c, _ = _place()
        cp = pltpu.make_async_remote_copy(src_ref=src, dst_ref=out, send_sem=ssem, recv_sem=rsem,
                                          device_id=(x, y, 1 - c), device_id_type=MESH)
        cp.start()
        cp.wait()

    theirs = pl.pallas_call(
        body, name=name, in_specs=[ANY], out_specs=ANY, out_shape=jax.ShapeDtypeStruct(r.shape, r.dtype),
        scratch_shapes=[pltpu.SemaphoreType.DMA, pltpu.SemaphoreType.DMA],
    )(r)
    first = lax.axis_index("c") == 0
    return jnp.stack([jnp.where(first, r, theirs), jnp.where(first, theirs, r)])


def _sum_pair(g, got, sel, name):
    two, four, rows, cols = g.shape
    flat = four * rows
    tile = _pick(flat, ROW_TILES)

    def body(sel_ref, a_ref, b_ref, o_ref):
        o_ref[...] = (a_ref[...].astype(F32) + b_ref[...].astype(F32)).astype(BF16)

    blk = pl.BlockSpec((tile, cols), lambda i, s: (i, 0))
    return pl.pallas_call(
        body, name=name, out_shape=jax.ShapeDtypeStruct((flat, cols), BF16),
        grid_spec=pltpu.PrefetchScalarGridSpec(
            num_scalar_prefetch=1, grid=(flat // tile,),
            in_specs=[pl.BlockSpec((None, tile, cols), lambda i, s: (s[0], i, 0)), blk], out_specs=blk),
        compiler_params=_params(("parallel",)),
    )(sel, g.reshape(two, flat, cols), got.reshape(flat, cols)).reshape(four, rows, cols)


def _sum4(a, b, sel, name):
    _, rows, cols = a.shape
    tile = _pick(rows, ROW_TILES)

    def body(sel_ref, a_ref, b0_ref, b1_ref, b2_ref, o_ref):
        acc = a_ref[...].astype(F32) + b0_ref[...].astype(F32)
        acc = acc + b1_ref[...].astype(F32)
        o_ref[...] = acc + b2_ref[...].astype(F32)

    bspec = [pl.BlockSpec((None, tile, cols), lambda i, s, j=j: (j, i, 0)) for j in range(3)]
    return pl.pallas_call(
        body, name=name, out_shape=jax.ShapeDtypeStruct((rows, cols), F32),
        grid_spec=pltpu.PrefetchScalarGridSpec(
            num_scalar_prefetch=1, grid=(rows // tile,),
            in_specs=[pl.BlockSpec((None, tile, cols), lambda i, s: (s[0], i, 0))] + bspec,
            out_specs=pl.BlockSpec((tile, cols), lambda i, s: (i, 0))),
        compiler_params=_params(("parallel",)),
    )(sel, a, b, b, b)


def _reduce_scatter(g, name):
    part = _reduce_pair(g, name)
    return _reduce_chips(part, _chip_exchange(part, name + "_chips"), name)


def _reduce_pair(g, name):
    c = lax.axis_index("c").astype(jnp.int32).reshape(1)
    got = _pair_swap(g, name + "_pair")
    return _sum_pair(g, got, c, name + "_add2")


def _reduce_chips(part, others, name):
    k = (2 * lax.axis_index("x") + lax.axis_index("y")).astype(jnp.int32).reshape(1)
    total = _sum4(part, others, k, name + "_add4")
    return _pair_share(total, name + "_share")


class _EarlyReduce:
    def __init__(self, names):
        self.names, self.grads, self.part, self.others = names, None, {}, {}

    def early(self, n, g0):
        if n in self.names:
            self.part[n] = _reduce_pair(_restack(n, jnp.stack([g0, self.grads[1][n]])), "rs_" + n)

    def side(self, n):
        return _chips_side(self.part[n])

    def finish(self, n):
        return _reduce_chips(self.part[n], self.others[n], "rs_" + n)


def _all_reduce_small(buf, name):
    rows = buf.shape[0]

    def body(src, out, slots, ssem, rsem):
        x, y, c, _ = _place()
        me = 4 * x + 2 * y + c
        slots[me] = src[...]
        cps = []
        for j in range(1, 8):
            px, py, pc = x ^ (j >> 2), y ^ ((j >> 1) & 1), c ^ (j & 1)
            cps.append(pltpu.make_async_remote_copy(src_ref=src, dst_ref=slots.at[me], send_sem=ssem.at[j - 1],
                                                    recv_sem=rsem.at[j - 1], device_id=(px, py, pc), device_id_type=MESH))
        for cp in cps:
            cp.start()
        for j in range(1, 8):
            peer = me ^ j
            pltpu.make_async_remote_copy(src_ref=src, dst_ref=slots.at[peer], send_sem=ssem.at[j - 1], recv_sem=rsem.at[j - 1],
                                         device_id=(x, y, c), device_id_type=MESH).wait_recv()
        for cp in cps:
            cp.wait_send()
        acc = slots[0]
        for d in range(1, 8):
            acc = acc + slots[d]
        out[...] = acc

    vm = pl.BlockSpec(memory_space=pltpu.VMEM)
    return pl.pallas_call(
        body, name=name, in_specs=[vm], out_specs=vm, out_shape=jax.ShapeDtypeStruct((rows, LANES), F32),
        scratch_shapes=[pltpu.VMEM((8, rows, LANES), F32), pltpu.SemaphoreType.DMA((7,)), pltpu.SemaphoreType.DMA((7,))],
    )(buf)


def _pack(arrs):
    flat = jnp.concatenate([a.astype(F32).reshape(-1) for a in arrs])
    n = flat.shape[0]
    padded = -(-n // (8 * LANES)) * (8 * LANES)
    return jnp.pad(flat, (0, padded - n)).reshape(padded // LANES, LANES)


def _unpack(buf, like):
    flat = buf.reshape(-1)
    out, pos = [], 0
    for a in like:
        out.append(flat[pos:pos + a.size].reshape(a.shape))
        pos += a.size
    return out


def _stride(t2d, dil):
    t, w = t2d.shape
    b = t // SEQ
    return t2d.reshape(b, SEQ // dil, dil, w).transpose(0, 2, 1, 3).reshape(b * dil, SEQ // dil, w)


def _unstride(t3d, dil):
    bb, n, w = t3d.shape
    b = bb // dil
    return t3d.reshape(b, dil, n, w).transpose(0, 2, 1, 3).reshape(b * SEQ, w)


def _stat_cols(st, dil, heads):
    s3 = _stride(st, dil)
    return s3.transpose(0, 2, 1)[..., None]


def _stat_rows(col):
    bb, h, n, _ = col.shape
    return col.reshape(bb, h, 1, n)


def _scan_params(dtp, alog, dm):
    t = dtp.shape[0]
    g, hg = SSD_GROUPS, dm["HG"]
    dt4 = dtp[:, :dm["H2"]].reshape(t, 2, g, hg)
    dtg = dt4.transpose(1, 2, 0, 3)
    dttg = jnp.pad(dt4.transpose(1, 2, 3, 0), ((0, 0), (0, 0), (0, 8 - hg), (0, 0)))
    al = alog.reshape(2, g, 1, hg)
    alt = jnp.pad(alog.reshape(2, g, hg, 1), ((0, 0), (0, 0), (0, 8 - hg), (0, 0)))
    return dtg, dttg, al, alt


def _layer_fwd(x, wl, tabs, dm, li, late=None):
    d = dm["D"]
    nm = f"l{li}_"
    h = _rms_fwd(x, wl["g_mix"], nm + "rms1")
    if late is None:
        proj = _mm(h, wl["w_main"], tb=True, name=nm + "proj")
    else:
        proj, brought = _mm(h, wl["w_main"], tb=True, side=late[0], name=nm + "proj")
        wl.update(late[1](brought))
    dtr = _mm(h, wl["w_dt"], tb=True, out_dtype=F32, name=nm + "proj_dt")
    cpre, u = _conv_fwd(proj, dm["OFF_XBC"], wl["conv_w"], wl["conv_b"], dm["XBC"], nm + "conv")
    dtp = _dt_prep(dtr, wl["dt_bias"], nm + "dt")
    sp = _scan_params(dtp, wl["a_log"], dm)
    y_f, s_f = _scan_fwd(u, *sp, 0, dm, nm + "scan_f")
    y_b, s_b = _scan_fwd(u, *sp, 1, dm, nm + "scan_b")
    y_a = _ssd_out_fwd(y_f, y_b, u, proj, wl["d_cols"], wl["ssd_norm"], dm["HI"], nm + "ssd_out")
    qkv = _rope(proj, dm["OFF_QKV"], dm["QW"] // HEAD_DIM, tabs[0], dm, nm + "rope")
    ng, dw = dm["NG"], dm["DW"]
    outs, lses, xgs = [], [], []
    for gi, (window, dil) in enumerate(DIL_PATTERNS):
        cols = [qkv[:, s * ng * dw + gi * dw:s * ng * dw + (gi + 1) * dw] for s in range(3)]
        xg = _stride(jnp.concatenate(cols, axis=1), dil)
        o, lse = _attn_fwd(xg, xg, xg, 0, 1, 2, DIL_HEADS, 1, SEQ // dil, window // (2 * dil), None, F32, nm + f"dil{gi}")
        xgs.append(xg)
        outs.append(_unstride(o, dil))
        lses.append(_unstride(lse[..., 0].transpose(0, 2, 1), dil))
    y_bm, lse_tot = _dil_combine(outs, lses, nm + "dil_mix")
    bsz = x.shape[0] // SEQ
    xw = qkv[:, dm["QKVD"]:].reshape(bsz, SEQ, dm["WQ"] + 2 * dm["WK"])
    rep = WIN_Q_HEADS // WIN_KV_HEADS
    y_c3, lse_w = _attn_fwd(xw, xw, xw, 0, rep, rep + 1, WIN_Q_HEADS, rep, SEQ, WIN_HALF,
                            wl["sink"].reshape(WIN_Q_HEADS, 1, 1), BF16, nm + "win")
    y_c = y_c3.reshape(x.shape[0], dm["WQ"])
    pa = _mm(y_a, wl["w_a"], out_dtype=F32, name=nm + "pa")
    pb = _mm(y_bm, wl["w_b"], out_dtype=F32, name=nm + "pb")
    pc = _mm(y_c, wl["w_c"], out_dtype=F32, name=nm + "pc")
    merged = _gate_fwd(pa, pb, pc, proj, dm["OFF_GATE"], d, nm + "gate")
    x1 = _mm(merged, wl["w_out"], add=x, out_dtype=F32, name=nm + "out")
    hm = _rms_fwd(x1, wl["g_mlp"], nm + "rms2")
    up, act = _mm(hm, wl["w_up"], epi="relu2", name=nm + "up")
    x2 = _mm(act, wl["w_down"], add=x1, out_dtype=F32, name=nm + "down")
    saved = dict(x=x, h=h, proj=proj, dtr=dtr, cpre=cpre, u=u, dtp=dtp, y_f=y_f, y_b=y_b, s_f=s_f, s_b=s_b, y_a=y_a,
                 xw=xw, xgs=xgs, y_bm=y_bm, lse_tot=lse_tot, y_c=y_c, lse_w=lse_w, pa=pa, pb=pb, pc=pc,
                 merged=merged, x1=x1, hm=hm, up=up, act=act)
    return x2, saved


def _layer_bwd(dx2, wl, sv, tabs, dm, li, early=None):
    d = dm["D"]
    t = dx2.shape[0]
    bsz = t // SEQ
    nm = f"l{li}b_"
    gr = {}
    dup = _mm(dx2, wl["w_down"], tb=True, aux=sv["up"], epi="relu2_bwd", name=nm + "dup")
    gr["w_down"] = _mm(sv["act"], dx2, ta=True, name=nm + "gw_down")
    dhm = _mm(dup, wl["w_up"], tb=True, out_dtype=F32, name=nm + "dhm")
    gr["w_up"] = _mm(sv["hm"], dup, ta=True, name=nm + "gw_up")
    if early is not None:
        early.early("w_down", gr["w_down"])
        early.early("w_up", gr["w_up"])
    dx1, gmlp = _rms_bwd(sv["x1"], wl["g_mlp"], dhm, dx2, nm + "rms2")
    gr["g_mlp"] = gmlp[0]
    dmerged = _mm(dx1, wl["w_out"], tb=True, out_dtype=F32, name=nm + "dmerged")
    gr["w_out"] = _mm(sv["merged"], dx1, ta=True, name=nm + "gw_out")
    dpa, dpb, dpc, dg0, dg1, dg2 = _gate_bwd(dmerged, sv["pa"], sv["pb"], sv["pc"], sv["proj"], dm["OFF_GATE"], d, nm + "gate")
    dya = _mm(dpa, wl["w_a"], tb=True, out_dtype=F32, name=nm + "dya")
    gr["w_a"] = _mm(sv["y_a"], dpa, ta=True, name=nm + "gw_a")
    dyb = _mm(dpb, wl["w_b"], tb=True, out_dtype=F32, name=nm + "dyb")
    gr["w_b"] = _mm(sv["y_bm"], dpb, ta=True, name=nm + "gw_b")
    dyc = _mm(dpc, wl["w_c"], tb=True, out_dtype=F32, name=nm + "dyc")
    gr["w_c"] = _mm(sv["y_c"], dpc, ta=True, name=nm + "gw_c")
    ng, dw = dm["NG"], dm["DW"]
    xw = sv["xw"]
    rep = WIN_Q_HEADS // WIN_KV_HEADS
    delta_w = _head_dots(dyc, sv["y_c"], WIN_Q_HEADS, nm + "win_delta")
    dl_col = _stat_cols(delta_w, 1, WIN_Q_HEADS)
    lse_w = sv["lse_w"]
    dyc3 = dyc.reshape(bsz, SEQ, dm["WQ"])
    wargs = (0, rep, rep + 1, WIN_Q_HEADS, rep, SEQ, WIN_HALF)
    dq_w = _attn_dq(xw, xw, xw, dyc3, lse_w, dl_col, *wargs, nm + "win_dq")
    dk_w, dv_w = _attn_dkv(xw, xw, xw, dyc3, _stat_rows(lse_w), _stat_rows(dl_col), *wargs, nm + "win_dkv")
    lse_w2 = lse_w[..., 0].transpose(0, 2, 1).reshape(t, WIN_Q_HEADS)
    gr["sink"] = _sink_grad(lse_w2, delta_w, wl["sink"], nm + "sink")[0]
    delta_d = _head_dots(dyb, sv["y_bm"], DIL_HEADS, nm + "dil_delta")
    dqs, dks, dvs = [], [], []
    for gi, (window, dil) in enumerate(DIL_PATTERNS):
        xg = sv["xgs"][gi]
        n = SEQ // dil
        do_g = _stride(dyb, dil)
        lse_c = _stat_cols(sv["lse_tot"], dil, DIL_HEADS)
        dl_c = _stat_cols(delta_d, dil, DIL_HEADS)
        dargs = (0, 1, 2, DIL_HEADS, 1, n, window // (2 * dil))
        dq = _attn_dq(xg, xg, xg, do_g, lse_c, dl_c, *dargs, nm + f"dil{gi}_dq")
        dk, dv = _attn_dkv(xg, xg, xg, do_g, _stat_rows(lse_c), _stat_rows(dl_c), *dargs, nm + f"dil{gi}_dkv")
        dqs.append(_unstride(dq, dil))
        dks.append(_unstride(dk, dil))
        dvs.append(_unstride(dv, dil))
    dqkv_r = jnp.concatenate(dqs + dks + dvs + [dq_w.reshape(t, dm["WQ"]), dk_w.reshape(t, dm["WK"]), dv_w.reshape(t, dm["WK"])],
                             axis=1)
    dqkv = _rope(dqkv_r, 0, dm["QW"] // HEAD_DIM, tabs[1], dm, nm + "rope")
    hi, gn = dm["HI"], dm["GN"]
    dyt, dxs0, dz, gnorm, dd_cols = _ssd_out_bwd(dya, sv["y_f"], sv["y_b"], sv["u"], sv["proj"], wl["d_cols"],
                                                          wl["ssd_norm"], hi, nm + "ssd_out")
    gr["ssd_norm"] = gnorm[0]
    gr["d_skip"] = dd_cols.reshape(SSD_HEADS, SSD_HEAD_DIM).sum(axis=1)
    sp = _scan_params(sv["dtp"], wl["a_log"], dm)
    dxs1, db1, dc1, rq_f, xdx_f = _scan_bwd(sv["u"], *sp, dyt, sv["s_f"], (dxs0, None, None), 0, dm, nm + "scan_f")
    dxs2, db2, dc2, rq_r, xdx_r = _scan_bwd(sv["u"], *sp, dyt, sv["s_b"], (dxs1, db1, dc1), 1, dm, nm + "scan_b")

    def heads(a):
        return a.transpose(1, 0, 2).reshape(t, SSD_HEADS)

    zpad = jnp.zeros((t, LANES - dm["H2"]), F32)
    zh = jnp.zeros((t, SSD_HEADS), F32)
    rqf_p = jnp.concatenate([heads(rq_f), zh, zpad], axis=1)
    rqr_p = jnp.concatenate([zh, heads(rq_r), zpad], axis=1)
    xdx_p = jnp.concatenate([heads(xdx_f), heads(xdx_r), zpad], axis=1)
    ddtr, dbias, dalog = _ssd_param_bwd(rqf_p, rqr_p, xdx_p, sv["dtp"], sv["dtr"], wl["dt_bias"], wl["a_log_p"], nm + "ssd_par")
    gr["dt_bias"] = dbias[0, :dm["H2"]].reshape(2, SSD_HEADS)
    gr["a_log"] = dalog[0, :dm["H2"]].reshape(2, SSD_HEADS)
    du = jnp.concatenate([dxs2, db2, dc2], axis=1)
    dxbc, gr["conv_w"], gcb = _conv_bwd(du, sv["cpre"], sv["proj"], dm["OFF_XBC"], wl["conv_w"], nm + "conv")
    gr["conv_b"] = gcb[0]
    dproj = jnp.concatenate([dz, dxbc, dqkv, dg0, dg1, dg2], axis=1)
    dh_dt = _mm(ddtr, wl["w_dt"], out_dtype=F32, name=nm + "dh_dt")
    if early is None:
        dh = _mm(dproj, wl["w_main"], add=dh_dt, out_dtype=F32, name=nm + "dh")
        gw_main = _mm(dproj, sv["h"], ta=True, name=nm + "gw_main")
    else:
        dh, got = _mm(dproj, wl["w_main"], add=dh_dt, out_dtype=F32, side=early.side("w_up"), name=nm + "dh")
        early.others["w_up"] = got[0]
        gw_main, got = _mm(dproj, sv["h"], ta=True, side=early.side("w_down"), name=nm + "gw_main")
        early.others["w_down"] = got[0]
    gw_dt = _mm(ddtr, sv["h"], ta=True, name=nm + "gw_dt")
    o1, h2 = dm["OFF_QKV"], dm["H2"]
    gw_in_t = jnp.concatenate([gw_main[:o1], gw_dt[:h2], gw_main[o1:]], axis=0)
    gr["w_in"] = gw_in_t.reshape(4, (dm["NM"] + h2) // 4, d)
    dx, gmix = _rms_bwd(sv["x"], wl["g_mix"], dh, dx1, nm + "rms1")
    gr["g_mix"] = gmix[0]
    return dx, gr


def _layer_weights(full, li, dm):
    st = full["w_in"]
    o1 = dm["OFF_QKV"]
    h2 = dm["H2"]
    d = dm["D"]
    w_in_t = st[li].reshape(4 * st.shape[2], d)
    wl = dict(
        w_main=jnp.concatenate([w_in_t[:o1], w_in_t[o1 + h2:]], axis=0),
        w_dt=jnp.pad(w_in_t[o1:o1 + h2], ((0, LANES - h2), (0, 0))),
        w_a=full["w_a"][li], w_b=full["w_b"][li], w_c=full["w_c"][li], w_out=full["w_out"][li],
        w_up=full["w_up"][li] if "w_up" in full else None,
        w_down=full["w_down"][li] if "w_down" in full else None,
        conv_w=full["conv_w"][li], conv_b=full["conv_b"][li][None, :],
        g_mix=full["g_mix"][li][None, :], g_mlp=full["g_mlp"][li][None, :], ssd_norm=full["ssd_norm"][li][None, :],
        d_cols=jnp.repeat(full["d_skip"][li], SSD_HEAD_DIM)[None, :],
        sink=full["sink"][li][None, :],
        a_log=full["a_log"][li],
        a_log_p=jnp.pad(full["a_log"][li].reshape(1, h2), ((0, 0), (0, LANES - h2))),
        dt_bias=jnp.pad(full["dt_bias"][li].reshape(1, h2), ((0, 0), (0, LANES - h2))),
    )
    assert wl["w_main"].shape == (dm["NM"], d)
    return wl


def _local_step(x, target, full, depth, late=None, early=None):
    bsz, seq, d = x.shape
    assert seq == SEQ
    dm = _dims(d)
    assert dm["OFF_GATE"] % d == 0 and dm["HI"] % (dm["HG"] * SSD_HEAD_DIM) == 0 and dm["H2"] <= LANES
    tabs = (_rope_tables(1.0), _rope_tables(-1.0))
    xt = x.reshape(bsz * seq, d)
    wls, saves = [], []
    for li in range(depth):
        wl = _layer_weights(full, li, dm)
        if li == 0 and late is not None:
            def done(brought, li=li):
                late[1](brought)
                return {n: full[n][li] for n in ("w_up", "w_down")}
            xt, sv = _layer_fwd(xt, wl, tabs, dm, li, late=(late[0], done))
        else:
            xt, sv = _layer_fwd(xt, wl, tabs, dm, li)
        wls.append(wl)
        saves.append(sv)
    dx, loss, g_final = _loss_head(xt, full["g_final"][None, :], target.reshape(bsz * seq, d), "loss_head")
    grads = [None] * depth
    if early is not None:
        early.grads = grads
    for li in reversed(range(depth)):
        dx, grads[li] = _layer_bwd(dx, wls[li], saves[li], tabs, dm, li, early=early if li == 0 else None)
    return loss, dx.reshape(bsz, seq, d), grads, g_final[0]


BIG = ("w_in", "w_a", "w_b", "w_c", "w_out", "w_up", "w_down")
COL_SHARDED = ("w_in", "w_b", "w_up")
SMALL = ("g_mix", "conv_w", "conv_b", "dt_bias", "a_log", "d_skip", "ssd_norm", "sink", "g_mlp", "g_final")
ORDER = ("g_mix", "w_in", "conv_w", "conv_b", "dt_bias", "a_log", "d_skip", "ssd_norm", "w_a", "w_b", "w_c", "sink",
         "w_out", "g_mlp", "w_up", "w_down", "g_final")


def _unstack(name, st):
    nl, _, r, c = st.shape
    if name in COL_SHARDED:
        return jnp.moveaxis(st, 1, 2).reshape(nl, r, 4 * c)
    return st.reshape(nl, 4 * r, c)


def _restack(name, gfull):
    nl, r, c = gfull.shape
    if name in COL_SHARDED:
        return jnp.moveaxis(gfull.reshape(nl, r, 4, c // 4), 2, 1)
    return gfull.reshape(nl, 4, r // 4, c)


def kernel(x, g_mix, w_in, conv_w, conv_b, dt_bias, a_log, d_skip, ssd_norm, w_a, w_b, w_c, sink, w_out, g_mlp, w_up, w_down, g_final, loss_target, m_g_mix, m_w_in, m_conv_w, m_conv_b, m_dt_bias, m_a_log, m_d_skip, m_ssd_norm, m_w_a, m_w_b, m_w_c, m_sink, m_w_out, m_g_mlp, m_w_up, m_w_down, m_g_final, v_g_mix, v_w_in, v_conv_w, v_conv_b, v_dt_bias, v_a_log, v_d_skip, v_ssd_norm, v_w_a, v_w_b, v_w_c, v_sink, v_w_out, v_g_mlp, v_w_up, v_w_down, v_g_final):
    w = dict(g_mix=g_mix, w_in=w_in, conv_w=conv_w, conv_b=conv_b, dt_bias=dt_bias, a_log=a_log, d_skip=d_skip,
             ssd_norm=ssd_norm, w_a=w_a, w_b=w_b, w_c=w_c, sink=sink, w_out=w_out, g_mlp=g_mlp, w_up=w_up, w_down=w_down,
             g_final=g_final)
    m = dict(g_mix=m_g_mix, w_in=m_w_in, conv_w=m_conv_w, conv_b=m_conv_b, dt_bias=m_dt_bias, a_log=m_a_log,
             d_skip=m_d_skip, ssd_norm=m_ssd_norm, w_a=m_w_a, w_b=m_w_b, w_c=m_w_c, sink=m_sink, w_out=m_w_out,
             g_mlp=m_g_mlp, w_up=m_w_up, w_down=m_w_down, g_final=m_g_final)
    v = dict(g_mix=v_g_mix, w_in=v_w_in, conv_w=v_conv_w, conv_b=v_conv_b, dt_bias=v_dt_bias, a_log=v_a_log,
             d_skip=v_d_skip, ssd_norm=v_ssd_norm, w_a=v_w_a, w_b=v_w_b, w_c=v_w_c, sink=v_sink, w_out=v_w_out,
             g_mlp=v_g_mlp, w_up=v_w_up, w_down=v_w_down, g_final=v_g_final)
    depth = w_in.shape[0]
    assert depth == 2
    kchip = 2 * lax.axis_index("x") + lax.axis_index("y")

    full = {n: w[n] for n in SMALL if n != "conv_w"}
    tr = lambda a: jnp.swapaxes(a, 1, 2)
    mlp = ("w_up", "w_down")
    for n in BIG:
        if n not in mlp:
            st = _gather_chips((tr(w[n]) if n == "w_in" else w[n]).astype(BF16), "gather_" + n)
            full[n] = st if n == "w_in" else _unstack(n, st)
    cw = _gather_chips(conv_w, "gather_conv_w")
    full["conv_w"] = jnp.moveaxis(cw, 1, 2).reshape(depth, CONV_WIDTH, 4 * conv_w.shape[2])
    mlp_shards = [w[n].astype(BF16) for n in mlp]

    def mlp_arrived(brought):
        for n, shard, st in zip(mlp, mlp_shards, brought):
            full[n] = _unstack(n, _put_own(st, shard))

    early = _EarlyReduce(mlp)
    loss_part, grad_x, grads, gg_final = _local_step(x, loss_target, full, depth, late=(_gather_side(mlp_shards), mlp_arrived),
                                                     early=early)

    gsh = {}
    for n in BIG:
        if n in mlp:
            gsh[n] = early.finish(n)
            continue
        gfull = jnp.stack([grads[li][n] for li in range(depth)])
        gsh[n] = _reduce_scatter(gfull if n == "w_in" else _restack(n, gfull), "rs_" + n)
    small_names = [n for n in SMALL if n != "g_final"]
    small_g = [jnp.stack([grads[li][n] for li in range(depth)]) for n in small_names] + [gg_final, loss_part[0, :1]]
    red = _unpack(_all_reduce_small(_pack(small_g), "allreduce_small"), small_g)
    for n, a in zip(small_names + ["g_final"], red):
        gsh[n] = a
    loss = red[-1][0]
    cshard = conv_w.shape[2]
    gsh["conv_w"] = lax.dynamic_slice_in_dim(gsh["conv_w"], kchip * cshard, cshard, axis=2)

    delta, new_m, new_v = {}, {}, {}
    for n in BIG:
        if n == "w_in":
            outs_t = _adamw(tr(w[n]), gsh[n], tr(m[n]), tr(v[n]), "adamw_" + n)
            delta[n], new_m[n], new_v[n] = [tr(o) for o in outs_t]
            gsh[n] = tr(gsh[n])
        else:
            delta[n], new_m[n], new_v[n] = _adamw(w[n], gsh[n], m[n], v[n], "adamw_" + n)
    sm = list(SMALL)
    packed = [_pack([d_[n] for n in sm])[None] for d_ in (w, gsh, m, v)]
    outs = [o[0] for o in _adamw(*packed, "adamw_small")]
    for dst, buf in zip((delta, new_m, new_v), outs):
        for n, a in zip(sm, _unpack(buf, [w[n] for n in sm])):
            dst[n] = a
    return (loss, grad_x, *[gsh[n] for n in ORDER], *[delta[n] for n in ORDER], *[new_m[n] for n in ORDER],
            *[new_v[n] for n in ORDER])
```

```python
import functools
import math

import jax
import jax.numpy as jnp
from jax import lax
from jax.experimental import pallas as pl
from jax.experimental.pallas import tpu as pltpu

F32 = jnp.float32
BF16 = jnp.bfloat16

SEQ = 2048
SSD_HEADS = 32
SSD_HEAD_DIM = 64
SSD_GROUPS = 8
SSD_STATE = 128
SSD_CHUNK = 128
CONV_WIDTH = 5
HEAD_DIM = 128
ROPE_DIM = 32
ROPE_THETA = 500000.0
DIL_PATTERNS = ((128, 1), (512, 4), (2048, 16))
DIL_HEADS = 8
WIN_Q_HEADS = 16
WIN_KV_HEADS = 4
WIN_HALF = 128
EPS = 1e-6
NEG_BIG = -1e30
ADAM_LR = 0.001
ADAM_B1 = 0.9
ADAM_B2 = 0.999
ADAM_EPS = 1e-08
ADAM_WD = 0.01
ADAM_STEP = 10

LANES = 128
ATT_BLK = 128
ROW_TILES = (128, 80, 64, 32, 16, 8)
VMEM_LIMIT = 48 * 1024 * 1024
MESH = pl.DeviceIdType.MESH
HIGHEST = lax.Precision.HIGHEST
NT = (((1,), (1,)), ((), ()))
TN = (((0,), (0,)), ((), ()))
NN = (((1,), (0,)), ((), ()))


def _dims(d_model):
    hi = SSD_HEADS * SSD_HEAD_DIM
    gn = SSD_GROUPS * SSD_STATE
    ng = len(DIL_PATTERNS)
    dw = DIL_HEADS * HEAD_DIM
    wq = WIN_Q_HEADS * HEAD_DIM
    wk = WIN_KV_HEADS * HEAD_DIM
    d = dict(D=d_model, HI=hi, GN=gn, XBC=hi + 2 * gn, H2=2 * SSD_HEADS, NG=ng, DW=dw, WQ=wq, WK=wk,
             QKVD=3 * ng * dw, QW=3 * ng * dw + wq + 2 * wk, HG=SSD_HEADS // SSD_GROUPS)
    d["OFF_XBC"] = hi
    d["OFF_QKV"] = hi + d["XBC"]
    d["OFF_GATE"] = d["OFF_QKV"] + d["QW"]
    d["NM"] = d["OFF_GATE"] + 3 * d_model
    return d


def _pick(n, prefs):
    for p in prefs:
        if n % p == 0:
            return p
    return n


def _params(sem):
    return pltpu.CompilerParams(dimension_semantics=sem, vmem_limit_bytes=VMEM_LIMIT)


def _sigmoid(x):
    return 1.0 / (1.0 + jnp.exp(-x))


class _Side:
    def __init__(self, args, outs, sems, start, finish):
        self.args, self.outs, self.sems, self.start, self.finish = args, outs, sems, start, finish


def _carry(side, body, n_in, n_out, n_scratch, grid):
    if side is None:
        return body
    n_sin, n_sout = len(side.args), len(side.outs)

    def wrapped(*refs):
        o0 = n_in + n_sin
        s0 = o0 + n_out + n_sout
        s_in, s_out, s_sem = refs[n_in:o0], refs[o0 + n_out:s0], refs[s0 + n_scratch:]
        ids = [pl.program_id(ax) for ax in range(len(grid))]
        first = functools.reduce(jnp.logical_and, [i == 0 for i in ids])
        last = functools.reduce(jnp.logical_and, [i == g - 1 for i, g in zip(ids, grid)])

        @pl.when(first)
        def _():
            side.start(s_in, s_out, s_sem)

        body(*refs[:n_in], *refs[o0:o0 + n_out], *refs[s0:s0 + n_scratch])

        @pl.when(last)
        def _():
            side.finish(s_in, s_out, s_sem)

    return wrapped


def _carry_call(side, body, *, name, grid, in_specs, out_specs, out_shape, scratch_shapes, sem, args):
    n_in, n_out = len(in_specs), len(out_specs)
    in_specs, out_specs, out_shape, scratch_shapes, args = (list(v) for v in (in_specs, out_specs, out_shape,
                                                                              scratch_shapes, args))
    body = _carry(side, body, n_in, n_out, len(scratch_shapes), grid)
    if side is not None:
        any_spec = pl.BlockSpec(memory_space=pl.ANY)
        in_specs += [any_spec] * len(side.args)
        args += list(side.args)
        out_specs += [any_spec] * len(side.outs)
        out_shape += list(side.outs)
        scratch_shapes += list(side.sems)
        sem = ("arbitrary",) * len(grid)
    res = pl.pallas_call(body, name=name, grid=grid, in_specs=in_specs, out_specs=out_specs, out_shape=out_shape,
                         scratch_shapes=scratch_shapes, compiler_params=_params(sem))(*args)
    return list(res[:n_out]), list(res[n_out:])


def _mm(a, b, *, ta=False, tb=False, add=None, aux=None, epi=None, out_dtype=BF16, side=None, name):
    if ta:
        kdim, m = a.shape
    else:
        m, kdim = a.shape
    if tb:
        n, k2 = b.shape
    else:
        k2, n = b.shape
    assert kdim == k2, (a.shape, b.shape, ta, tb)
    tm = _pick(m, (1024, 512, 256, 128, 64, 32, 16, 8))
    tn = _pick(n, (1024, 512, 256, 128))
    tk = _pick(kdim, (2048, 1024, 512, 256, 128))
    nk = kdim // tk
    dims = (((0 if ta else 1,), (1 if tb else 0,)), ((), ()))
    n_in = 2 + (add is not None) + (aux is not None)
    n_out = 2 if epi == "relu2" else 1
    n_sin = len(side.args) if side else 0
    n_sout = len(side.outs) if side else 0
    grid = (m // tm, n // tn, nk)

    def body(*refs):
        a_ref, b_ref = refs[0], refs[1]
        pos = 2
        add_ref = aux_ref = None
        if add is not None:
            add_ref = refs[pos]
            pos += 1
        if aux is not None:
            aux_ref = refs[pos]
            pos += 1
        out_refs = refs[n_in + n_sin:n_in + n_sin + n_out]
        acc_ref = refs[n_in + n_sin + n_out + n_sout]
        k = pl.program_id(2)
        if side is not None:
            s_in = refs[n_in:n_in + n_sin]
            s_out = refs[n_in + n_sin + n_out:n_in + n_sin + n_out + n_sout]
            s_sem = refs[n_in + n_sin + n_out + n_sout + 1:]
            i, j = pl.program_id(0), pl.program_id(1)

            @pl.when((i == 0) & (j == 0) & (k == 0))
            def _():
                side.start(s_in, s_out, s_sem)

        @pl.when(k == 0)
        def _():
            acc_ref[...] = jnp.zeros_like(acc_ref)

        acc_ref[...] += lax.dot_general(a_ref[...].astype(BF16), b_ref[...].astype(BF16), dims,
                                        preferred_element_type=F32)

        @pl.when(k == nk - 1)
        def _():
            r = acc_ref[...]
            if add_ref is not None:
                r = r + add_ref[...].astype(F32)
            if epi == "relu2":
                out_refs[0][...] = r.astype(out_refs[0].dtype)
                out_refs[1][...] = jnp.square(jnp.maximum(r, 0.0)).astype(out_refs[1].dtype)
            elif epi == "relu2_bwd":
                out_refs[0][...] = (r * 2.0 * jnp.maximum(aux_ref[...].astype(F32), 0.0)).astype(out_refs[0].dtype)
            else:
                out_refs[0][...] = r.astype(out_refs[0].dtype)

        if side is not None:
            @pl.when((i == grid[0] - 1) & (j == grid[1] - 1) & (k == nk - 1))
            def _():
                side.finish(s_in, s_out, s_sem)

    a_spec = pl.BlockSpec((tk, tm), lambda i, j, k: (k, i)) if ta else pl.BlockSpec((tm, tk), lambda i, j, k: (i, k))
    b_spec = pl.BlockSpec((tn, tk), lambda i, j, k: (j, k)) if tb else pl.BlockSpec((tk, tn), lambda i, j, k: (k, j))
    o_spec = pl.BlockSpec((tm, tn), lambda i, j, k: (i, j))
    in_specs = [a_spec, b_spec]
    args = [a, b]
    if add is not None:
        in_specs.append(o_spec)
        args.append(add)
    if aux is not None:
        in_specs.append(o_spec)
        args.append(aux)
    out_shape = [jax.ShapeDtypeStruct((m, n), out_dtype)] * n_out
    out_specs = [o_spec] * n_out
    scratch = [pltpu.VMEM((tm, tn), F32)]
    sem = ("parallel", "parallel", "arbitrary")
    if side is not None:
        any_spec = pl.BlockSpec(memory_space=pl.ANY)
        in_specs += [any_spec] * n_sin
        args += list(side.args)
        out_shape += list(side.outs)
        out_specs += [any_spec] * n_sout
        scratch += list(side.sems)
        sem = ("arbitrary", "arbitrary", "arbitrary")
    res = pl.pallas_call(
        body, name=name, grid=grid, in_specs=in_specs, out_specs=out_specs, out_shape=out_shape, scratch_shapes=scratch,
        compiler_params=_params(sem),
    )(*args)
    if side is not None:
        return (res[0] if n_out == 1 else tuple(res[:n_out])), list(res[n_out:])
    return res if n_out == 2 else res[0]


def _rowwise(body, rows, fulls, outs, accs=(), *, tile, name):
    rows = [r if isinstance(r, tuple) else (r, 0, r.shape[1]) for r in rows]
    nrows = rows[0][0].shape[0]
    assert nrows % tile == 0, (nrows, tile)
    in_specs, args = [], []
    for arr, off, width in rows:
        assert arr.shape[0] == nrows and off % width == 0, (arr.shape, off, width)
        in_specs.append(pl.BlockSpec((tile, width), lambda i, o=off // width: (i, o)))
        args.append(arr)
    for arr in fulls:
        in_specs.append(pl.BlockSpec(arr.shape, lambda i, nd=arr.ndim: (0,) * nd))
        args.append(arr)
    out_specs, out_shape = [], []
    for cols, dt in outs:
        out_specs.append(pl.BlockSpec((tile, cols), lambda i: (i, 0)))
        out_shape.append(jax.ShapeDtypeStruct((nrows, cols), dt))
    for shp, dt in accs:
        out_specs.append(pl.BlockSpec(shp, lambda i, nd=len(shp): (0,) * nd))
        out_shape.append(jax.ShapeDtypeStruct(shp, dt))
    n_in, n_out = len(args), len(outs)

    def wrapped(*refs):
        acc_refs = refs[n_in + n_out:]
        if acc_refs:
            @pl.when(pl.program_id(0) == 0)
            def _():
                for r in acc_refs:
                    r[...] = jnp.zeros_like(r)
        body(*refs)

    return pl.pallas_call(
        wrapped, name=name, grid=(nrows // tile,), in_specs=in_specs, out_specs=out_specs, out_shape=out_shape,
        compiler_params=_params(("arbitrary",)),
    )(*args)


def _rms_fwd(x, g, name):
    def body(x_ref, g_ref, h_ref):
        xv = x_ref[...]
        rstd = lax.rsqrt(jnp.mean(xv * xv, axis=-1, keepdims=True) + EPS)
        h_ref[...] = (xv * rstd * g_ref[...]).astype(BF16)

    return _rowwise(body, [x], [g], [(x.shape[1], BF16)], tile=256, name=name)[0]


def _rms_bwd(x, g, dh, dres, name):
    def body(x_ref, dh_ref, dres_ref, g_ref, dx_ref, dg_ref):
        xv = x_ref[...]
        dv = dh_ref[...]
        rstd = lax.rsqrt(jnp.mean(xv * xv, axis=-1, keepdims=True) + EPS)
        xn = xv * rstd
        dg_ref[...] += jnp.sum(dv * xn, axis=0, keepdims=True)
        dn = dv * g_ref[...]
        dx_ref[...] = dres_ref[...] + rstd * (dn - xn * jnp.mean(dn * xn, axis=-1, keepdims=True))

    d = x.shape[1]
    return _rowwise(body, [x, dh, dres], [g], [(d, F32)], [((1, d), F32)], tile=256, name=name)


def _loss_head(x, g, target, name):
    d = x.shape[1]

    def body(x_ref, t_ref, g_ref, dx_ref, loss_ref, dg_ref):
        xv = x_ref[...]
        rstd = lax.rsqrt(jnp.mean(xv * xv, axis=-1, keepdims=True) + EPS)
        xn = xv * rstd
        err = xn * g_ref[...] - t_ref[...]
        loss_ref[...] += jnp.full((1, LANES), 0.5 / d, F32) * jnp.sum(err * err)
        dy = err * (1.0 / d)
        dg_ref[...] += jnp.sum(dy * xn, axis=0, keepdims=True)
        dn = dy * g_ref[...]
        dx_ref[...] = rstd * (dn - xn * jnp.mean(dn * xn, axis=-1, keepdims=True))

    return _rowwise(body, [x, target], [g], [(d, F32)], [((1, LANES), F32), ((1, d), F32)], tile=256, name=name)


def _rope_tables(sign):
    half = ROPE_DIM // 2
    inv = ROPE_THETA ** (-jnp.arange(0, ROPE_DIM, 2, dtype=F32) / ROPE_DIM)
    ang = jnp.arange(SEQ, dtype=F32)[:, None] * inv[None, :]
    cos, sin = jnp.cos(ang), jnp.sin(ang) * sign
    zeros = jnp.zeros((SEQ, HEAD_DIM - ROPE_DIM), F32)
    zh = jnp.zeros((SEQ, half), F32)
    c = jnp.concatenate([cos, cos, zeros + 1.0], axis=1)
    s_up = jnp.concatenate([-sin, zh, zeros], axis=1)
    s_dn = jnp.concatenate([zh, sin, zeros], axis=1)
    return c, s_up, s_dn


def _rope(src, off, nblk, tabs, dm, name):
    t = src.shape[0]
    tq = 256
    half = ROPE_DIM // 2
    win0 = 3 * dm["NG"] * DIL_HEADS
    win1 = win0 + WIN_Q_HEADS + WIN_KV_HEADS
    qw = nblk * HEAD_DIM
    assert nblk == dm["QW"] // HEAD_DIM
    wb = next(c for c in (1024, 768, 512, 384, 256, 128) if off % c == 0 and qw % c == 0)
    reps = wb // HEAD_DIM
    sb = SEQ // tq
    flag = (jnp.arange(qw, dtype=jnp.int32) // HEAD_DIM < win1).astype(F32)[None, :]

    def body(x_ref, c_ref, up_ref, dn_ref, f_ref, o_ref):
        xv = x_ref[...].astype(F32)

        def wide(r):
            v = r[...]
            return v if reps == 1 else jnp.concatenate([v] * reps, axis=1)

        rot = xv * wide(c_ref) + pltpu.roll(xv, wb - half, 1) * wide(up_ref) + pltpu.roll(xv, half, 1) * wide(dn_ref)
        o_ref[...] = jnp.where(f_ref[...] > 0.5, rot, xv).astype(BF16)

    tab_spec = pl.BlockSpec((tq, HEAD_DIM), lambda i, j: (i % sb, 0))
    return pl.pallas_call(
        body, name=name, grid=(t // tq, qw // wb),
        in_specs=[pl.BlockSpec((tq, wb), lambda i, j, o=off // wb: (i, o + j)), tab_spec, tab_spec, tab_spec,
                  pl.BlockSpec((1, wb), lambda i, j: (0, j))],
        out_specs=pl.BlockSpec((tq, wb), lambda i, j: (i, j)),
        out_shape=jax.ShapeDtypeStruct((t, qw), BF16),
        compiler_params=_params(("parallel", "parallel")),
    )(src, *tabs, flag)


def _band_mask(rows_start, cols_start, nrows, ncols, w, n, rows_are_q):
    r = rows_start + lax.broadcasted_iota(jnp.int32, (nrows, ncols), 0)
    c = cols_start + lax.broadcasted_iota(jnp.int32, (nrows, ncols), 1)
    del rows_are_q
    return (jnp.abs(r - c) <= w) & (c >= 0) & (c < n)


def _nbr_specs(make, nb):
    if nb == 1:
        return [make(lambda i: i)]
    return [make(lambda i: jnp.maximum(i - 1, 0)), make(lambda i: i), make(lambda i: jnp.minimum(i + 1, nb - 1))]


def _cat(refs, axis):
    vals = [r[...] for r in refs]
    return vals[0] if len(vals) == 1 else jnp.concatenate(vals, axis=axis)


def _head(ref, h):
    return ref[:, h * HEAD_DIM:(h + 1) * HEAD_DIM]


def _head_cat(refs, h, axis=0):
    vals = [_head(r, h) for r in refs]
    return vals[0] if len(vals) == 1 else jnp.concatenate(vals, axis=axis)


def _attn_fwd(qa, ka, va, qb, kb, vb, hq, rep, n, w, sink, out_dtype, name):
    bb = qa.shape[0]
    blk = ATT_BLK
    nb = n // blk
    nk = 1 if nb == 1 else 3
    hkv = hq // rep
    scale = HEAD_DIM ** -0.5
    has_sink = sink is not None

    def body(*refs):
        q_ref = refs[0]
        k_refs = refs[1:1 + nk]
        v_refs = refs[1 + nk:1 + 2 * nk]
        pos = 1 + 2 * nk
        sink_ref = refs[pos] if has_sink else None
        o_ref, lse_ref = refs[pos + has_sink], refs[pos + has_sink + 1]
        i = pl.program_id(1)
        k0 = (i - 1) * blk if nk == 3 else i * blk
        valid = _band_mask(i * blk, k0, blk, nk * blk, w, n, True)
        for g in range(hkv):
            kcat = _head_cat(k_refs, g)
            vcat = _head_cat(v_refs, g)
            for r in range(rep):
                h = g * rep + r
                s = lax.dot_general(_head(q_ref, h), kcat, NT, preferred_element_type=F32) * scale
                s = jnp.where(valid, s, NEG_BIG)
                m = jnp.max(s, axis=1, keepdims=True)
                if has_sink:
                    m = jnp.maximum(m, sink_ref[h])
                p = jnp.exp(s - m)
                l = jnp.sum(p, axis=1, keepdims=True)
                if has_sink:
                    l = l + jnp.exp(sink_ref[h] - m)
                o = lax.dot_general(p.astype(BF16), vcat, NN, preferred_element_type=F32) / l
                o_ref[:, h * HEAD_DIM:(h + 1) * HEAD_DIM] = o.astype(o_ref.dtype)
                lse_ref[h] = m + jnp.log(l)

    def mk(col, width):
        return lambda f: pl.BlockSpec((None, blk, width), lambda b, i, f=f: (b, f(i), col))

    qw, kw = hq * HEAD_DIM, hkv * HEAD_DIM
    in_specs = [pl.BlockSpec((None, blk, qw), lambda b, i: (b, i, qb))]
    in_specs += _nbr_specs(mk(kb, kw), nb) + _nbr_specs(mk(vb, kw), nb)
    args = [qa] + [ka] * nk + [va] * nk
    if has_sink:
        in_specs.append(pl.BlockSpec((hq, 1, 1), lambda b, i: (0, 0, 0)))
        args.append(sink)
    return pl.pallas_call(
        body, name=name, grid=(bb, nb), in_specs=in_specs,
        out_specs=[pl.BlockSpec((None, blk, qw), lambda b, i: (b, i, 0)),
                   pl.BlockSpec((None, hq, blk, 1), lambda b, i: (b, 0, i, 0))],
        out_shape=[jax.ShapeDtypeStruct((bb, n, qw), out_dtype), jax.ShapeDtypeStruct((bb, hq, n, 1), F32)],
        compiler_params=_params(("parallel", "parallel")),
    )(*args)


def _attn_dq(qa, ka, va, do, lse, delta, qb, kb, vb, hq, rep, n, w, name):
    bb = qa.shape[0]
    blk = ATT_BLK
    nb = n // blk
    nk = 1 if nb == 1 else 3
    hkv = hq // rep
    scale = HEAD_DIM ** -0.5

    def body(*refs):
        q_ref = refs[0]
        k_refs = refs[1:1 + nk]
        v_refs = refs[1 + nk:1 + 2 * nk]
        do_ref, lse_ref, dl_ref, dq_ref = refs[1 + 2 * nk:]
        i = pl.program_id(1)
        k0 = (i - 1) * blk if nk == 3 else i * blk
        valid = _band_mask(i * blk, k0, blk, nk * blk, w, n, True)
        for g in range(hkv):
            kcat = _head_cat(k_refs, g)
            vcat = _head_cat(v_refs, g)
            for r in range(rep):
                h = g * rep + r
                s = lax.dot_general(_head(q_ref, h), kcat, NT, preferred_element_type=F32) * scale
                p = jnp.exp(jnp.where(valid, s, NEG_BIG) - lse_ref[h])
                dp = lax.dot_general(_head(do_ref, h).astype(BF16), vcat, NT, preferred_element_type=F32)
                ds = p * (dp - dl_ref[h])
                dq = lax.dot_general(ds.astype(BF16), kcat, NN, preferred_element_type=F32) * scale
                dq_ref[:, h * HEAD_DIM:(h + 1) * HEAD_DIM] = dq.astype(BF16)

    def mk(col, width):
        return lambda f: pl.BlockSpec((None, blk, width), lambda b, i, f=f: (b, f(i), col))

    qw, kw = hq * HEAD_DIM, hkv * HEAD_DIM
    col_spec = pl.BlockSpec((None, hq, blk, 1), lambda b, i: (b, 0, i, 0))
    in_specs = [pl.BlockSpec((None, blk, qw), lambda b, i: (b, i, qb))]
    in_specs += _nbr_specs(mk(kb, kw), nb) + _nbr_specs(mk(vb, kw), nb)
    in_specs += [pl.BlockSpec((None, blk, qw), lambda b, i: (b, i, 0)), col_spec, col_spec]
    return pl.pallas_call(
        body, name=name, grid=(bb, nb), in_specs=in_specs,
        out_specs=pl.BlockSpec((None, blk, qw), lambda b, i: (b, i, 0)),
        out_shape=jax.ShapeDtypeStruct((bb, n, qw), BF16),
        compiler_params=_params(("parallel", "parallel")),
    )(qa, *([ka] * nk), *([va] * nk), do, lse, delta)


def _attn_dkv(qa, ka, va, do, lse_row, delta_row, qb, kb, vb, hq, rep, n, w, name):
    bb = qa.shape[0]
    blk = ATT_BLK
    nb = n // blk
    nq = 1 if nb == 1 else 3
    hkv = hq // rep
    scale = HEAD_DIM ** -0.5

    def body(*refs):
        k_ref, v_ref = refs[0], refs[1]
        q_refs = refs[2:2 + nq]
        do_refs = refs[2 + nq:2 + 2 * nq]
        lse_refs = refs[2 + 2 * nq:2 + 3 * nq]
        dl_refs = refs[2 + 3 * nq:2 + 4 * nq]
        dk_ref, dv_ref = refs[2 + 4 * nq:]
        j = pl.program_id(1)
        q0 = (j - 1) * blk if nq == 3 else j * blk
        valid = _band_mask(j * blk, q0, blk, nq * blk, w, n, False)
        for g in range(hkv):
            kg, vg = _head(k_ref, g), _head(v_ref, g)
            dk = jnp.zeros((blk, HEAD_DIM), F32)
            dv = jnp.zeros((blk, HEAD_DIM), F32)
            for r in range(rep):
                h = g * rep + r
                qcat = _head_cat(q_refs, h)
                docat = _head_cat(do_refs, h).astype(BF16)
                lse = lse_refs[0][h] if nq == 1 else jnp.concatenate([lr[h] for lr in lse_refs], axis=1)
                dl = dl_refs[0][h] if nq == 1 else jnp.concatenate([dr[h] for dr in dl_refs], axis=1)
                st = lax.dot_general(kg, qcat, NT, preferred_element_type=F32) * scale
                pt = jnp.exp(jnp.where(valid, st, NEG_BIG) - lse)
                dv = dv + lax.dot_general(pt.astype(BF16), docat, NN, preferred_element_type=F32)
                dpt = lax.dot_general(vg, docat, NT, preferred_element_type=F32)
                dst = pt * (dpt - dl)
                dk = dk + lax.dot_general(dst.astype(BF16), qcat, NN, preferred_element_type=F32) * scale
            dk_ref[:, g * HEAD_DIM:(g + 1) * HEAD_DIM] = dk.astype(BF16)
            dv_ref[:, g * HEAD_DIM:(g + 1) * HEAD_DIM] = dv.astype(BF16)

    qw, kw = hq * HEAD_DIM, hkv * HEAD_DIM

    def mkq(col):
        return lambda f: pl.BlockSpec((None, blk, qw), lambda b, j, f=f: (b, f(j), col))

    def mkrow(f):
        return pl.BlockSpec((None, hq, 1, blk), lambda b, j, f=f: (b, 0, 0, f(j)))

    in_specs = [pl.BlockSpec((None, blk, kw), lambda b, j: (b, j, kb)), pl.BlockSpec((None, blk, kw), lambda b, j: (b, j, vb))]
    in_specs += _nbr_specs(mkq(qb), nb) + _nbr_specs(mkq(0), nb) + _nbr_specs(mkrow, nb) + _nbr_specs(mkrow, nb)
    o_spec = pl.BlockSpec((None, blk, kw), lambda b, j: (b, j, 0))
    return pl.pallas_call(
        body, name=name, grid=(bb, nb), in_specs=in_specs, out_specs=[o_spec, o_spec],
        out_shape=[jax.ShapeDtypeStruct((bb, n, kw), BF16)] * 2,
        compiler_params=_params(("parallel", "parallel")),
    )(ka, va, *([qa] * nq), *([do] * nq), *([lse_row] * nq), *([delta_row] * nq))


def _head_expand(v, nh, width):
    lane_head = lax.broadcasted_iota(jnp.int32, (1, nh * width), 1) >> int(math.log2(width))
    out = jnp.zeros((v.shape[0], nh * width), F32)
    for j in range(nh):
        out = jnp.where(lane_head == j, v[:, j:j + 1], out)
    return out


def _head_sums(m, nh, width):
    lane_head = lax.broadcasted_iota(jnp.int32, (1, nh * width), 1) >> int(math.log2(width))
    col = lax.broadcasted_iota(jnp.int32, (1, nh), 1)
    out = jnp.zeros((m.shape[0], nh), F32)
    for j in range(nh):
        sj = jnp.sum(jnp.where(lane_head == j, m, 0.0), axis=1, keepdims=True)
        out = jnp.where(col == j, sj, out)
    return out


def _head_dots(a, b, nh, name):
    def body(a_ref, b_ref, o_ref):
        o_ref[...] = _head_sums(a_ref[...].astype(F32) * b_ref[...].astype(F32), nh, HEAD_DIM)

    return _rowwise(body, [a, b], [], [(nh, F32)], tile=256, name=name)[0]


def _dil_combine(outs, lses, name):
    ng = len(outs)

    def body(*refs):
        o_refs, l_refs = refs[:ng], refs[ng:2 * ng]
        y_ref, lt_ref = refs[2 * ng], refs[2 * ng + 1]
        ls = [r[...] for r in l_refs]
        m = functools.reduce(jnp.maximum, ls)
        es = [jnp.exp(v - m) for v in ls]
        tot = functools.reduce(jnp.add, es)
        acc = jnp.zeros(o_refs[0].shape, F32)
        for o_ref, e in zip(o_refs, es):
            acc = acc + _head_expand(e / tot, DIL_HEADS, HEAD_DIM) * o_ref[...]
        y_ref[...] = acc.astype(BF16)
        lt_ref[...] = m + jnp.log(tot)

    dw = outs[0].shape[1]
    return _rowwise(body, list(outs) + list(lses), [], [(dw, BF16), (DIL_HEADS, F32)], tile=256, name=name)


def _sink_grad(lse, delta, sink, name):
    def body(l_ref, d_ref, s_ref, o_ref):
        o_ref[...] -= jnp.sum(jnp.exp(s_ref[...] - l_ref[...]) * d_ref[...], axis=0, keepdims=True)

    return _rowwise(body, [lse, delta], [sink], [], [((1, lse.shape[1]), F32)], tile=512, name=name)[0]


def _shift_rows(x, d, nrows):
    if d == 0:
        return x
    rolled = pltpu.roll(x, (-d) % nrows, 0)
    row = lax.broadcasted_iota(jnp.int32, x.shape, 0)
    ok = (row + d >= 0) & (row + d < nrows)
    return jnp.where(ok, rolled, 0.0)


def _conv_fwd(proj, off, conv_w, conv_b, xbc, name):
    t = proj.shape[0]
    tc = _pick(xbc, (256, 128))
    assert off % tc == 0
    pad = (CONV_WIDTH - 1) // 2

    def body(x_ref, w_ref, b_ref, c_ref, u_ref):
        xv = x_ref[...].astype(F32)
        acc = jnp.zeros_like(xv) + b_ref[...]
        for k in range(CONV_WIDTH):
            acc = acc + w_ref[k:k + 1, :] * _shift_rows(xv, k - pad, SEQ)
        c_ref[...] = acc.astype(BF16)
        u_ref[...] = (acc * _sigmoid(acc)).astype(BF16)

    o_spec = pl.BlockSpec((SEQ, tc), lambda b, j: (b, j))
    return pl.pallas_call(
        body, name=name, grid=(t // SEQ, xbc // tc),
        in_specs=[pl.BlockSpec((SEQ, tc), lambda b, j, o=off // tc: (b, o + j)),
                  pl.BlockSpec((CONV_WIDTH, tc), lambda b, j: (0, j)), pl.BlockSpec((1, tc), lambda b, j: (0, j))],
        out_specs=[o_spec, o_spec], out_shape=[jax.ShapeDtypeStruct((t, xbc), BF16)] * 2,
        compiler_params=_params(("parallel", "parallel")),
    )(proj, conv_w, conv_b)


def _conv_bwd(du, cpre, proj, off, conv_w, name):
    t, xbc = du.shape
    tc = _pick(xbc, (256, 128))
    pad = (CONV_WIDTH - 1) // 2

    def body(du_ref, c_ref, x_ref, w_ref, dx_ref, dw_ref, db_ref):
        @pl.when(pl.program_id(1) == 0)
        def _():
            dw_ref[...] = jnp.zeros_like(dw_ref)
            db_ref[...] = jnp.zeros_like(db_ref)

        cv = c_ref[...].astype(F32)
        sg = _sigmoid(cv)
        dc = du_ref[...] * (sg * (1.0 + cv * (1.0 - sg)))
        xv = x_ref[...].astype(F32)
        dx = jnp.zeros_like(dc)
        for k in range(CONV_WIDTH):
            dx = dx + w_ref[k:k + 1, :] * _shift_rows(dc, pad - k, SEQ)
            dw_ref[k:k + 1, :] += jnp.sum(dc * _shift_rows(xv, k - pad, SEQ), axis=0, keepdims=True)
        db_ref[...] += jnp.sum(dc, axis=0, keepdims=True)
        dx_ref[...] = dx.astype(BF16)

    blk = pl.BlockSpec((SEQ, tc), lambda j, b: (b, j))
    return pl.pallas_call(
        body, name=name, grid=(xbc // tc, t // SEQ),
        in_specs=[blk, blk, pl.BlockSpec((SEQ, tc), lambda j, b, o=off // tc: (b, o + j)),
                  pl.BlockSpec((CONV_WIDTH, tc), lambda j, b: (0, j))],
        out_specs=[blk, pl.BlockSpec((CONV_WIDTH, tc), lambda j, b: (0, j)), pl.BlockSpec((1, tc), lambda j, b: (0, j))],
        out_shape=[jax.ShapeDtypeStruct((t, xbc), BF16), jax.ShapeDtypeStruct((CONV_WIDTH, xbc), F32),
                   jax.ShapeDtypeStruct((1, xbc), F32)],
        compiler_params=_params(("parallel", "arbitrary")),
    )(du, cpre, proj, conv_w)


def _dt_prep(dtr, bias, name):
    def body(r_ref, b_ref, o_ref):
        v = r_ref[...] + b_ref[...]
        o_ref[...] = jnp.maximum(v, 0.0) + jnp.log1p(jnp.exp(-jnp.abs(v)))

    return _rowwise(body, [dtr], [bias], [(dtr.shape[1], F32)], tile=512, name=name)[0]


def _scan_prelude(d, dt_ref, dtt_ref, al_ref, alt_ref, hg):
    p = SSD_HEAD_DIM
    ch = SSD_CHUNK
    a_row = -jnp.exp(al_ref[...])
    a_col = -jnp.exp(alt_ref[...])
    dtc = dt_ref[...]
    dt_x = _head_expand(dtc, hg, p)
    dta_x = dt_x * _head_expand(a_row, hg, p)
    dta_t = dtt_ref[...] * a_col
    ri = lax.broadcasted_iota(jnp.int32, (ch, ch), 0)
    ci = lax.broadcasted_iota(jnp.int32, (ch, ch), 1)
    mask = (ci <= ri) if d == 0 else (ci >= ri)
    mask_t = (ci >= ri) if d == 0 else (ci <= ri)
    tri = mask.astype(F32)
    phi_x = jnp.dot(tri, dta_x, preferred_element_type=F32, precision=HIGHEST)
    phi_r = lax.dot_general(dta_t, tri, NT, preferred_element_type=F32, precision=HIGHEST)
    tot_x = jnp.sum(dta_x, axis=0, keepdims=True)
    return dtc, dt_x, phi_x, phi_r, tot_x, mask, mask_t


def _scan_specs(d, nc, hg, dm):
    p, n, ch = SSD_HEAD_DIM, SSD_STATE, SSD_CHUNK
    w = hg * p
    b0 = dm["HI"] // n
    c0 = (dm["HI"] + dm["GN"]) // n

    def row(b, c):
        return b * nc + c

    return [
        pl.BlockSpec((ch, w), lambda b, g, c: (row(b, c), g)),
        pl.BlockSpec((ch, n), lambda b, g, c: (row(b, c), b0 + g)),
        pl.BlockSpec((ch, n), lambda b, g, c: (row(b, c), c0 + g)),
        pl.BlockSpec((None, None, ch, hg), lambda b, g, c: (d, g, row(b, c), 0)),
        pl.BlockSpec((None, None, 8, ch), lambda b, g, c: (d, g, 0, row(b, c))),
        pl.BlockSpec((None, None, 1, hg), lambda b, g, c: (d, g, 0, 0)),
        pl.BlockSpec((None, None, 8, 1), lambda b, g, c: (d, g, 0, 0)),
    ]


def _remap(spec, f):
    return pl.BlockSpec(spec.block_shape, lambda b, g, c, im=spec.index_map: im(b, g, f(c)))


def _scan_fwd(u, dtg, dttg, alg, altg, d, dm, name, side=None):
    t = u.shape[0]
    p, n, ch, hg = SSD_HEAD_DIM, SSD_STATE, SSD_CHUNK, dm["HG"]
    w = hg * p
    nc = SEQ // ch
    order = (lambda c: c) if d == 0 else (lambda c: nc - 1 - c)

    def body(x_ref, b_ref, c_ref, dt_ref, dtt_ref, al_ref, alt_ref, y_ref, sin_ref, s_ref):
        @pl.when(pl.program_id(2) == 0)
        def _():
            s_ref[...] = jnp.zeros_like(s_ref)

        dtc, dt_x, phi_x, phi_r, tot_x, mask, _ = _scan_prelude(d, dt_ref, dtt_ref, al_ref, alt_ref, hg)
        lane_head = lax.broadcasted_iota(jnp.int32, (1, w), 1) >> int(math.log2(p))
        cm, bm = c_ref[...], b_ref[...]
        cb = lax.dot_general(cm, bm, NT, preferred_element_type=F32)
        xdt = x_ref[...].astype(F32) * dt_x
        xdt_b = xdt.astype(BF16)
        ydiag = jnp.zeros((ch, w), F32)
        for j in range(hg):
            seg = phi_x[:, j * p:j * p + 1] - phi_r[j:j + 1, :]
            mj = (cb * jnp.exp(jnp.where(mask, seg, NEG_BIG))).astype(BF16)
            ydiag = ydiag + jnp.dot(mj, jnp.where(lane_head == j, xdt_b, jnp.zeros_like(xdt_b)), preferred_element_type=F32)
        s = s_ref[...]
        y_ref[...] = ydiag + jnp.dot(cm, s.astype(BF16), preferred_element_type=F32) * jnp.exp(phi_x)
        sin_ref[...] = s
        wm = (xdt * jnp.exp(tot_x - phi_x)).astype(BF16)
        s_ref[...] = s * jnp.exp(tot_x) + lax.dot_general(bm, wm, TN, preferred_element_type=F32)

    specs = [_remap(s, order) for s in _scan_specs(d, nc, hg, dm)]
    own, brought = _carry_call(
        side, body, name=name, grid=(t // SEQ, SSD_GROUPS, nc), in_specs=specs,
        out_specs=[_remap(pl.BlockSpec((ch, w), lambda b, g, c: (b * nc + c, g)), order),
                   _remap(pl.BlockSpec((None, None, n, w), lambda b, g, c: (b * nc + c, g, 0, 0)), order)],
        out_shape=[jax.ShapeDtypeStruct((t, dm["HI"]), F32), jax.ShapeDtypeStruct((t // ch, SSD_GROUPS, n, w), F32)],
        scratch_shapes=[pltpu.VMEM((n, w), F32)], sem=("parallel", "parallel", "arbitrary"),
        args=(u, u, u, dtg, dttg, alg, altg))
    return own if side is None else (own, brought)


def _scan_bwd(u, dtg, dttg, alg, altg, dy, sin, adds, d, dm, name, side=None):
    t = u.shape[0]
    p, n, ch, hg = SSD_HEAD_DIM, SSD_STATE, SSD_CHUNK, dm["HG"]
    w = hg * p
    nc = SEQ // ch
    order = (lambda c: nc - 1 - c) if d == 0 else (lambda c: c)
    has_bc_add = adds[1] is not None

    def body(*refs):
        x_ref, b_ref, c_ref, dt_ref, dtt_ref, al_ref, alt_ref, dy_ref, sin_ref, ax_ref = refs[:10]
        pos = 10
        ab_ref = ac_ref = None
        if has_bc_add:
            ab_ref, ac_ref = refs[10], refs[11]
            pos = 12
        dxs_ref, db_ref, dc_ref, rq_ref, xdx_ref, ds_ref = refs[pos:]

        @pl.when(pl.program_id(2) == 0)
        def _():
            ds_ref[...] = jnp.zeros_like(ds_ref)

        dtc, dt_x, phi_x, phi_r, tot_x, mask, mask_t = _scan_prelude(d, dt_ref, dtt_ref, al_ref, alt_ref, hg)
        lane_head = lax.broadcasted_iota(jnp.int32, (1, w), 1) >> int(math.log2(p))
        cm, bm = c_ref[...], b_ref[...]
        cb = lax.dot_general(cm, bm, NT, preferred_element_type=F32)
        cb_t = lax.dot_general(bm, cm, NT, preferred_element_type=F32)
        xs = x_ref[...].astype(F32)
        xdt = xs * dt_x
        xdt_b = xdt.astype(BF16)
        dy = dy_ref[...]
        dy_b = dy.astype(BF16)
        zero_b = jnp.zeros_like(dy_b)
        col = lax.broadcasted_iota(jnp.int32, (1, hg), 1)
        dxp = jnp.zeros((ch, w), F32)
        a_ls = jnp.zeros((ch, ch), F32)
        a_sl = jnp.zeros((ch, ch), F32)
        dphi = jnp.zeros((ch, hg), F32)
        for j in range(hg):
            pc = phi_x[:, j * p:j * p + 1]
            pr = phi_r[j:j + 1, :]
            l_ls = jnp.exp(jnp.where(mask, pc - pr, NEG_BIG))
            l_sl = jnp.exp(jnp.where(mask_t, pr - pc, NEG_BIG))
            dy_j = jnp.where(lane_head == j, dy_b, zero_b)
            xdt_j = jnp.where(lane_head == j, xdt_b, zero_b)
            dxp = dxp + jnp.dot((cb_t * l_sl).astype(BF16), dy_j, preferred_element_type=F32)
            g_ls = l_ls * lax.dot_general(dy_j, xdt_b, NT, preferred_element_type=F32)
            g_sl = l_sl * lax.dot_general(xdt_j, dy_b, NT, preferred_element_type=F32)
            a_ls = a_ls + g_ls
            a_sl = a_sl + g_sl
            pair = jnp.sum(g_ls * cb, axis=1, keepdims=True) - jnp.sum(g_sl * cb_t, axis=1, keepdims=True)
            dphi = jnp.where(col == j, pair, dphi)
        ds = ds_ref[...]
        ds_b = ds.astype(BF16)
        sin = sin_ref[...]
        sin_b = sin.astype(BF16)
        e_tp = jnp.exp(tot_x - phi_x)
        e_p = jnp.exp(phi_x)
        dxp_off = e_tp * jnp.dot(bm, ds_b, preferred_element_type=F32)
        dxp = dxp + dxp_off
        dxs_ref[...] = ax_ref[...] + dxp * dt_x
        xdx_ref[...] = _head_sums(xs * dxp, hg, p)
        y_off = jnp.dot(cm, sin_b, preferred_element_type=F32) * e_p
        st_t = _head_sums(xdt * dxp_off, hg, p)
        dphi = dphi + _head_sums(dy * y_off, hg, p) - st_t
        dtot = _head_sums(jnp.sum(ds * sin, axis=0, keepdims=True) * jnp.exp(tot_x), hg, p) + jnp.sum(st_t, axis=0, keepdims=True)
        cum = jnp.dot(mask_t.astype(F32), _head_expand(dphi, hg, p), preferred_element_type=F32, precision=HIGHEST)
        ddta = jnp.zeros((ch, hg), F32)
        for j in range(hg):
            ddta = jnp.where(col == j, cum[:, j * p:j * p + 1], ddta)
        rq_ref[...] = ddta + dtot
        dye = (dy * e_p).astype(BF16)
        dcv = jnp.dot(a_ls.astype(BF16), bm, preferred_element_type=F32)
        dcv = dcv + lax.dot_general(dye, sin_b, NT, preferred_element_type=F32)
        dbv = jnp.dot(a_sl.astype(BF16), cm, preferred_element_type=F32)
        dbv = dbv + lax.dot_general((xdt * e_tp).astype(BF16), ds_b, NT, preferred_element_type=F32)
        if has_bc_add:
            dcv = dcv + ac_ref[...]
            dbv = dbv + ab_ref[...]
        dc_ref[...] = dcv
        db_ref[...] = dbv
        ds_ref[...] = ds * jnp.exp(tot_x) + lax.dot_general(cm, dye, TN, preferred_element_type=F32)

    def sp(spec):
        return _remap(spec, order)

    xw = pl.BlockSpec((ch, w), lambda b, g, c: (b * nc + c, g))
    gn_blk = pl.BlockSpec((ch, n), lambda b, g, c: (b * nc + c, g))
    small = pl.BlockSpec((None, ch, hg), lambda b, g, c: (g, b * nc + c, 0))
    in_specs = [sp(s) for s in _scan_specs(d, nc, hg, dm)]
    in_specs += [sp(xw), sp(pl.BlockSpec((None, None, n, w), lambda b, g, c: (b * nc + c, g, 0, 0))), sp(xw)]
    args = [u, u, u, dtg, dttg, alg, altg, dy, sin, adds[0]]
    if has_bc_add:
        in_specs += [sp(gn_blk), sp(gn_blk)]
        args += [adds[1], adds[2]]
    own, brought = _carry_call(
        side, body, name=name, grid=(t // SEQ, SSD_GROUPS, nc), in_specs=in_specs,
        out_specs=[sp(xw), sp(gn_blk), sp(gn_blk), sp(small), sp(small)],
        out_shape=[jax.ShapeDtypeStruct((t, dm["HI"]), F32), jax.ShapeDtypeStruct((t, dm["GN"]), F32),
                   jax.ShapeDtypeStruct((t, dm["GN"]), F32), jax.ShapeDtypeStruct((SSD_GROUPS, t, hg), F32),
                   jax.ShapeDtypeStruct((SSD_GROUPS, t, hg), F32)],
        scratch_shapes=[pltpu.VMEM((n, w), F32)], sem=("parallel", "parallel", "arbitrary"), args=args)
    return own if side is None else (own, brought)


def _ssd_param_bwd(rq_f, rq_r, xdx, dtp, dtr, bias, alog, name):
    def body(rf_ref, rr_ref, xdx_ref, dt_ref, dtr_ref, b_ref, al_ref, o_ref, db_ref, da_ref):
        a = -jnp.exp(al_ref[...])
        d_dta = rf_ref[...] + rr_ref[...]
        ddt = a * d_dta + xdx_ref[...]
        ddtr = ddt * _sigmoid(dtr_ref[...] + b_ref[...])
        o_ref[...] = ddtr
        db_ref[...] += jnp.sum(ddtr, axis=0, keepdims=True)
        da_ref[...] += a * jnp.sum(dt_ref[...] * d_dta, axis=0, keepdims=True)

    return _rowwise(body, [rq_f, rq_r, xdx, dtp, dtr], [bias, alog], [(LANES, F32)],
                    [((1, LANES), F32), ((1, LANES), F32)], tile=512, name=name)


def _ssd_out_fwd(y_f, y_b, u, proj, dcols, gn, hi, name):
    def body(yf_ref, yb_ref, x_ref, z_ref, d_ref, g_ref, o_ref):
        ytot = yf_ref[...] + yb_ref[...] + d_ref[...] * x_ref[...].astype(F32)
        zv = z_ref[...].astype(F32)
        yz = ytot * (zv * _sigmoid(zv))
        rstd = lax.rsqrt(jnp.mean(yz * yz, axis=-1, keepdims=True) + EPS)
        o_ref[...] = (yz * rstd * g_ref[...]).astype(BF16)

    return _rowwise(body, [y_f, y_b, (u, 0, hi), (proj, 0, hi)], [dcols, gn], [(hi, BF16)], tile=256, name=name)[0]


def _ssd_out_bwd(dya, y_f, y_b, u, proj, dcols, gn, hi, name):
    def body(dy_ref, yf_ref, yb_ref, x_ref, z_ref, d_ref, g_ref, dyt_ref, dxs_ref, dz_ref, dg_ref, dd_ref):
        xv = x_ref[...].astype(F32)
        ytot = yf_ref[...] + yb_ref[...] + d_ref[...] * xv
        zv = z_ref[...].astype(F32)
        sg = _sigmoid(zv)
        sz = zv * sg
        yz = ytot * sz
        rstd = lax.rsqrt(jnp.mean(yz * yz, axis=-1, keepdims=True) + EPS)
        yn = yz * rstd
        dv = dy_ref[...]
        dg_ref[...] += jnp.sum(dv * yn, axis=0, keepdims=True)
        dn = dv * g_ref[...]
        dyz = rstd * (dn - yn * jnp.mean(dn * yn, axis=-1, keepdims=True))
        dyt = dyz * sz
        dyt_ref[...] = dyt
        dxs_ref[...] = dyt * d_ref[...]
        dz_ref[...] = (dyz * ytot * (sg * (1.0 + zv * (1.0 - sg)))).astype(BF16)
        dd_ref[...] += jnp.sum(dyt * xv, axis=0, keepdims=True)

    return _rowwise(body, [dya, y_f, y_b, (u, 0, hi), (proj, 0, hi)], [dcols, gn],
                    [(hi, F32), (hi, F32), (hi, BF16)], [((1, hi), F32), ((1, hi), F32)], tile=128, name=name)


def _gate_fwd(pa, pb, pc, proj, off, d, name):
    def body(a_ref, b_ref, c_ref, g0_ref, g1_ref, g2_ref, o_ref):
        acc = _sigmoid(g0_ref[...].astype(F32)) * a_ref[...]
        acc = acc + _sigmoid(g1_ref[...].astype(F32)) * b_ref[...]
        acc = acc + _sigmoid(g2_ref[...].astype(F32)) * c_ref[...]
        o_ref[...] = acc.astype(BF16)

    rows = [pa, pb, pc] + [(proj, off + k * d, d) for k in range(3)]
    return _rowwise(body, rows, [], [(d, BF16)], tile=256, name=name)[0]


def _gate_bwd(dm_, pa, pb, pc, proj, off, d, name):
    def body(dm_ref, a_ref, b_ref, c_ref, g0_ref, g1_ref, g2_ref, da_ref, db_ref, dc_ref, dg0_ref, dg1_ref, dg2_ref):
        dmv = dm_ref[...]
        for p_ref, g_ref, dp_ref, dg_ref in ((a_ref, g0_ref, da_ref, dg0_ref), (b_ref, g1_ref, db_ref, dg1_ref),
                                             (c_ref, g2_ref, dc_ref, dg2_ref)):
            sg = _sigmoid(g_ref[...].astype(F32))
            dp_ref[...] = (dmv * sg).astype(BF16)
            dg_ref[...] = (dmv * p_ref[...] * sg * (1.0 - sg)).astype(BF16)

    rows = [dm_, pa, pb, pc] + [(proj, off + k * d, d) for k in range(3)]
    return _rowwise(body, rows, [], [(d, BF16)] * 6, tile=128, name=name)


def _adamw(w, g, m, v, name):
    nl, rows, cols = w.shape
    tile = _pick(rows, ROW_TILES)
    c1 = 1.0 / (1.0 - ADAM_B1 ** ADAM_STEP)
    c2 = 1.0 / (1.0 - ADAM_B2 ** ADAM_STEP)

    def body(w_ref, g_ref, m_ref, v_ref, d_ref, nm_ref, nv_ref):
        gv = g_ref[...]
        nm = ADAM_B1 * m_ref[...] + (1.0 - ADAM_B1) * gv
        nv = ADAM_B2 * v_ref[...] + (1.0 - ADAM_B2) * (gv * gv)
        nm_ref[...] = nm
        nv_ref[...] = nv
        d_ref[...] = -ADAM_LR * ((nm * c1) / (jnp.sqrt(nv * c2) + ADAM_EPS) + ADAM_WD * w_ref[...])

    blk = pl.BlockSpec((None, tile, cols), lambda l, i: (l, i, 0))
    return pl.pallas_call(
        body, name=name, grid=(nl, rows // tile), in_specs=[blk] * 4, out_specs=[blk] * 3,
        out_shape=[jax.ShapeDtypeStruct(w.shape, F32)] * 3, compiler_params=_params(("parallel", "parallel")),
    )(w, g, m, v)


ANY = pl.BlockSpec(memory_space=pl.ANY)


def _place():
    x, y, c = lax.axis_index("x"), lax.axis_index("y"), lax.axis_index("c")
    chips = [(1 - x, y), (x, 1 - y), (1 - x, 1 - y)]
    return x, y, c, chips


def _gather_copies(src, out, ssem, rsem, base):
    x, y, c, chips = _place()
    k = 2 * x + y

    def copy(j, kk, layer, to, own=False):
        return pltpu.make_async_remote_copy(
            src_ref=src.at[layer] if own else out.at[layer, kk], dst_ref=out.at[layer, kk],
            send_sem=ssem.at[base + j], recv_sem=rsem.at[base + j], device_id=to, device_id_type=MESH)

    first = [copy(j, k, c, (cx, cy, c), own=True) for j, (cx, cy) in enumerate(chips)]
    passed = [copy(3 + j, 2 * cx + cy, c, (x, y, 1 - c)) for j, (cx, cy) in enumerate(chips)]
    landed = [copy(j, 2 * cx + cy, c, (x, y, c)) for j, (cx, cy) in enumerate(chips)]
    handed = [copy(3 + j, 2 * cx + cy, 1 - c, (x, y, c)) for j, (cx, cy) in enumerate(chips)]
    return first, passed, landed, handed


def _gather_start(src, out, ssem, rsem, base):
    for cp in _gather_copies(src, out, ssem, rsem, base)[0]:
        cp.start()


def _gather_finish(src, out, ssem, rsem, base):
    first, passed, landed, handed = _gather_copies(src, out, ssem, rsem, base)
    for arrived, onward in zip(landed, passed):
        arrived.wait_recv()
        onward.start()
    for arrived in handed:
        arrived.wait_recv()
    for cp in first + passed:
        cp.wait_send()


def _gather_side(arrs):
    n = len(arrs)

    def start(srcs, outs, sems):
        for a in range(n):
            _gather_start(srcs[a], outs[a], sems[0], sems[1], 6 * a)

    def finish(srcs, outs, sems):
        for a in range(n):
            _gather_finish(srcs[a], outs[a], sems[0], sems[1], 6 * a)

    outs = [jax.ShapeDtypeStruct((a.shape[0], 4) + a.shape[1:], a.dtype) for a in arrs]
    return _Side(list(arrs), outs, [pltpu.SemaphoreType.DMA((6 * n,)), pltpu.SemaphoreType.DMA((6 * n,))], start, finish)


def _put_own(st, arr):
    kchip = 2 * lax.axis_index("x") + lax.axis_index("y")
    return lax.dynamic_update_slice(st, arr[:, None], (0, kchip) + (0,) * (arr.ndim - 1))


def _chip_copies(src, out, ssem, rsem, base=0):
    x, y, c, chips = _place()
    return [pltpu.make_async_remote_copy(src_ref=src.at[2 * cx + cy], dst_ref=out.at[j], send_sem=ssem.at[base + j],
                                         recv_sem=rsem.at[base + j], device_id=(cx, cy, c), device_id_type=MESH)
            for j, (cx, cy) in enumerate(chips)]


def _chips_side(parts):
    n = len(parts)

    def start(srcs, outs, sems):
        for a in range(n):
            for cp in _chip_copies(srcs[a], outs[a], sems[0], sems[1], 3 * a):
                cp.start()

    def finish(srcs, outs, sems):
        for a in range(n):
            for cp in _chip_copies(srcs[a], outs[a], sems[0], sems[1], 3 * a):
                cp.wait()

    return _Side(list(parts), [jax.ShapeDtypeStruct((3,) + p.shape[1:], p.dtype) for p in parts],
                 [pltpu.SemaphoreType.DMA((3 * n,)), pltpu.SemaphoreType.DMA((3 * n,))], start, finish)


def _gather_chips(arr, name):
    def body(src, out, ssem, rsem):
        _gather_start(src, out, ssem, rsem, 0)
        _gather_finish(src, out, ssem, rsem, 0)

    st = pl.pallas_call(
        body, name=name, in_specs=[ANY], out_specs=ANY,
        out_shape=jax.ShapeDtypeStruct((arr.shape[0], 4) + arr.shape[1:], arr.dtype),
        scratch_shapes=[pltpu.SemaphoreType.DMA((6,)), pltpu.SemaphoreType.DMA((6,))],
    )(arr)
    return _put_own(st, arr)


def _pair_swap(g, name):
    def body(src, out, ssem, rsem):
        x, y, c, _ = _place()
        cp = pltpu.make_async_remote_copy(src_ref=src.at[1 - c], dst_ref=out, send_sem=ssem, recv_sem=rsem,
                                          device_id=(x, y, 1 - c), device_id_type=MESH)
        cp.start()
        cp.wait()

    return pl.pallas_call(
        body, name=name, in_specs=[ANY], out_specs=ANY, out_shape=jax.ShapeDtypeStruct(g.shape[1:], g.dtype),
        scratch_shapes=[pltpu.SemaphoreType.DMA, pltpu.SemaphoreType.DMA],
    )(g)


def _chip_exchange(p, name):
    def body(src, out, ssem, rsem):
        cps = _chip_copies(src, out, ssem, rsem)
        for cp in cps:
            cp.start()
        for cp in cps:
            cp.wait()

    return pl.pallas_call(
        body, name=name, in_specs=[ANY], out_specs=ANY, out_shape=jax.ShapeDtypeStruct((3,) + p.shape[1:], p.dtype),
        scratch_shapes=[pltpu.SemaphoreType.DMA((3,)), pltpu.SemaphoreType.DMA((3,))],
    )(p)


def _pair_share(r, name):
    def body(src, out, ssem, rsem):
        x, y, c, _ = _place()
        cp = pltpu.make_async_remote_copy(src_ref=src, dst_ref=out, send_sem=ssem, recv_sem=rsem,
                                          device_id=(x, y, 1 - c), device_id_type=MESH)
        cp.start()
        cp.wait()

    theirs = pl.pallas_call(
        body, name=name, in_specs=[ANY], out_specs=ANY, out_shape=jax.ShapeDtypeStruct(r.shape, r.dtype),
        scratch_shapes=[pltpu.SemaphoreType.DMA, pltpu.SemaphoreType.DMA],
    )(r)
    first = lax.axis_index("c") == 0
    return jnp.stack([jnp.where(first, r, theirs), jnp.where(first, theirs, r)])


def _sum_pair(g, got, sel, name):
    two, four, rows, cols = g.shape
    flat = four * rows
    tile = _pick(flat, ROW_TILES)

    def body(sel_ref, a_ref, b_ref, o_ref):
        o_ref[...] = (a_ref[...].astype(F32) + b_ref[...].astype(F32)).astype(BF16)

    blk = pl.BlockSpec((tile, cols), lambda i, s: (i, 0))
    return pl.pallas_call(
        body, name=name, out_shape=jax.ShapeDtypeStruct((flat, cols), BF16),
        grid_spec=pltpu.PrefetchScalarGridSpec(
            num_scalar_prefetch=1, grid=(flat // tile,),
            in_specs=[pl.BlockSpec((None, tile, cols), lambda i, s: (s[0], i, 0)), blk], out_specs=blk),
        compiler_params=_params(("parallel",)),
    )(sel, g.reshape(two, flat, cols), got.reshape(flat, cols)).reshape(four, rows, cols)


def _sum4(a, b, sel, name):
    _, rows, cols = a.shape
    tile = _pick(rows, ROW_TILES)

    def body(sel_ref, a_ref, b0_ref, b1_ref, b2_ref, o_ref):
        acc = a_ref[...].astype(F32) + b0_ref[...].astype(F32)
        acc = acc + b1_ref[...].astype(F32)
        o_ref[...] = acc + b2_ref[...].astype(F32)

    bspec = [pl.BlockSpec((None, tile, cols), lambda i, s, j=j: (j, i, 0)) for j in range(3)]
    return pl.pallas_call(
        body, name=name, out_shape=jax.ShapeDtypeStruct((rows, cols), F32),
        grid_spec=pltpu.PrefetchScalarGridSpec(
            num_scalar_prefetch=1, grid=(rows // tile,),
            in_specs=[pl.BlockSpec((None, tile, cols), lambda i, s: (s[0], i, 0))] + bspec,
            out_specs=pl.BlockSpec((tile, cols), lambda i, s: (i, 0))),
        compiler_params=_params(("parallel",)),
    )(sel, a, b, b, b)


def _reduce_scatter(g, name):
    part = _reduce_pair(g, name)
    return _reduce_chips(part, _chip_exchange(part, name + "_chips"), name)


def _reduce_pair(g, name):
    c = lax.axis_index("c").astype(jnp.int32).reshape(1)
    got = _pair_swap(g, name + "_pair")
    return _sum_pair(g, got, c, name + "_add2")


def _reduce_chips(part, others, name):
    k = (2 * lax.axis_index("x") + lax.axis_index("y")).astype(jnp.int32).reshape(1)
    total = _sum4(part, others, k, name + "_add4")
    return _pair_share(total, name + "_share")


class _EarlyReduce:
    def __init__(self, names):
        self.names, self.grads, self.part, self.others = names, None, {}, {}

    def early(self, n, g0):
        if n in self.names:
            g = jnp.stack([g0, self.grads[1][n]])
            self.part[n] = _reduce_pair(g if n == "w_in" else _restack(n, g), "rs_" + n)

    def side(self, names):
        return _chips_side([self.part[n] for n in names])

    def took(self, names, brought):
        for n, o in zip(names, brought):
            self.others[n] = o

    def finish(self, n):
        return _reduce_chips(self.part[n], self.others[n], "rs_" + n)


def _all_reduce_small(buf, name):
    rows = buf.shape[0]

    def body(src, out, slots, ssem, rsem):
        x, y, c, _ = _place()
        me = 4 * x + 2 * y + c
        slots[me] = src[...]
        cps = []
        for j in range(1, 8):
            px, py, pc = x ^ (j >> 2), y ^ ((j >> 1) & 1), c ^ (j & 1)
            cps.append(pltpu.make_async_remote_copy(src_ref=src, dst_ref=slots.at[me], send_sem=ssem.at[j - 1],
                                                    recv_sem=rsem.at[j - 1], device_id=(px, py, pc), device_id_type=MESH))
        for cp in cps:
            cp.start()
        for j in range(1, 8):
            peer = me ^ j
            pltpu.make_async_remote_copy(src_ref=src, dst_ref=slots.at[peer], send_sem=ssem.at[j - 1], recv_sem=rsem.at[j - 1],
                                         device_id=(x, y, c), device_id_type=MESH).wait_recv()
        for cp in cps:
            cp.wait_send()
        acc = slots[0]
        for d in range(1, 8):
            acc = acc + slots[d]
        out[...] = acc

    vm = pl.BlockSpec(memory_space=pltpu.VMEM)
    return pl.pallas_call(
        body, name=name, in_specs=[vm], out_specs=vm, out_shape=jax.ShapeDtypeStruct((rows, LANES), F32),
        scratch_shapes=[pltpu.VMEM((8, rows, LANES), F32), pltpu.SemaphoreType.DMA((7,)), pltpu.SemaphoreType.DMA((7,))],
    )(buf)


def _pack(arrs):
    flat = jnp.concatenate([a.astype(F32).reshape(-1) for a in arrs])
    n = flat.shape[0]
    padded = -(-n // (8 * LANES)) * (8 * LANES)
    return jnp.pad(flat, (0, padded - n)).reshape(padded // LANES, LANES)


def _unpack(buf, like):
    flat = buf.reshape(-1)
    out, pos = [], 0
    for a in like:
        out.append(flat[pos:pos + a.size].reshape(a.shape))
        pos += a.size
    return out


def _stride(t2d, dil):
    t, w = t2d.shape
    b = t // SEQ
    return t2d.reshape(b, SEQ // dil, dil, w).transpose(0, 2, 1, 3).reshape(b * dil, SEQ // dil, w)


def _unstride(t3d, dil):
    bb, n, w = t3d.shape
    b = bb // dil
    return t3d.reshape(b, dil, n, w).transpose(0, 2, 1, 3).reshape(b * SEQ, w)


def _stat_cols(st, dil, heads):
    s3 = _stride(st, dil)
    return s3.transpose(0, 2, 1)[..., None]


def _stat_rows(col):
    bb, h, n, _ = col.shape
    return col.reshape(bb, h, 1, n)


def _scan_params(dtp, alog, dm):
    t = dtp.shape[0]
    g, hg = SSD_GROUPS, dm["HG"]
    dt4 = dtp[:, :dm["H2"]].reshape(t, 2, g, hg)
    dtg = dt4.transpose(1, 2, 0, 3)
    dttg = jnp.pad(dt4.transpose(1, 2, 3, 0), ((0, 0), (0, 0), (0, 8 - hg), (0, 0)))
    al = alog.reshape(2, g, 1, hg)
    alt = jnp.pad(alog.reshape(2, g, hg, 1), ((0, 0), (0, 0), (0, 8 - hg), (0, 0)))
    return dtg, dttg, al, alt


def _layer_fwd(x, wl, tabs, dm, li, late=None):
    d = dm["D"]
    nm = f"l{li}_"
    h = _rms_fwd(x, wl["g_mix"], nm + "rms1")
    late = late or {}
    if "proj" not in late:
        proj = _mm(h, wl["w_main"], tb=True, name=nm + "proj")
    else:
        proj, brought = _mm(h, wl["w_main"], tb=True, side=late["proj"][0], name=nm + "proj")
        wl.update(late["proj"][1](brought))
    dtr = _mm(h, wl["w_dt"], tb=True, out_dtype=F32, name=nm + "proj_dt")
    cpre, u = _conv_fwd(proj, dm["OFF_XBC"], wl["conv_w"], wl["conv_b"], dm["XBC"], nm + "conv")
    dtp = _dt_prep(dtr, wl["dt_bias"], nm + "dt")
    sp = _scan_params(dtp, wl["a_log"], dm)
    scans = []
    for dirn, key in ((0, "scan_f"), (1, "scan_b")):
        if key in late:
            own, brought = _scan_fwd(u, *sp, dirn, dm, nm + key, side=late[key][0])
            wl.update(late[key][1](brought))
        else:
            own = _scan_fwd(u, *sp, dirn, dm, nm + key)
        scans.append(own)
    (y_f, s_f), (y_b, s_b) = scans
    y_a = _ssd_out_fwd(y_f, y_b, u, proj, wl["d_cols"], wl["ssd_norm"], dm["HI"], nm + "ssd_out")
    qkv = _rope(proj, dm["OFF_QKV"], dm["QW"] // HEAD_DIM, tabs[0], dm, nm + "rope")
    ng, dw = dm["NG"], dm["DW"]
    outs, lses, xgs = [], [], []
    for gi, (window, dil) in enumerate(DIL_PATTERNS):
        cols = [qkv[:, s * ng * dw + gi * dw:s * ng * dw + (gi + 1) * dw] for s in range(3)]
        xg = _stride(jnp.concatenate(cols, axis=1), dil)
        o, lse = _attn_fwd(xg, xg, xg, 0, 1, 2, DIL_HEADS, 1, SEQ // dil, window // (2 * dil), None, F32, nm + f"dil{gi}")
        xgs.append(xg)
        outs.append(_unstride(o, dil))
        lses.append(_unstride(lse[..., 0].transpose(0, 2, 1), dil))
    y_bm, lse_tot = _dil_combine(outs, lses, nm + "dil_mix")
    bsz = x.shape[0] // SEQ
    xw = qkv[:, dm["QKVD"]:].reshape(bsz, SEQ, dm["WQ"] + 2 * dm["WK"])
    rep = WIN_Q_HEADS // WIN_KV_HEADS
    y_c3, lse_w = _attn_fwd(xw, xw, xw, 0, rep, rep + 1, WIN_Q_HEADS, rep, SEQ, WIN_HALF,
                            wl["sink"].reshape(WIN_Q_HEADS, 1, 1), BF16, nm + "win")
    y_c = y_c3.reshape(x.shape[0], dm["WQ"])
    pa = _mm(y_a, wl["w_a"], out_dtype=F32, name=nm + "pa")
    pb = _mm(y_bm, wl["w_b"], out_dtype=F32, name=nm + "pb")
    pc = _mm(y_c, wl["w_c"], out_dtype=F32, name=nm + "pc")
    merged = _gate_fwd(pa, pb, pc, proj, dm["OFF_GATE"], d, nm + "gate")
    x1 = _mm(merged, wl["w_out"], add=x, out_dtype=F32, name=nm + "out")
    hm = _rms_fwd(x1, wl["g_mlp"], nm + "rms2")
    up, act = _mm(hm, wl["w_up"], epi="relu2", name=nm + "up")
    x2 = _mm(act, wl["w_down"], add=x1, out_dtype=F32, name=nm + "down")
    saved = dict(x=x, h=h, proj=proj, dtr=dtr, cpre=cpre, u=u, dtp=dtp, y_f=y_f, y_b=y_b, s_f=s_f, s_b=s_b, y_a=y_a,
                 xw=xw, xgs=xgs, y_bm=y_bm, lse_tot=lse_tot, y_c=y_c, lse_w=lse_w, pa=pa, pb=pb, pc=pc,
                 merged=merged, x1=x1, hm=hm, up=up, act=act)
    return x2, saved


def _layer_bwd(dx2, wl, sv, tabs, dm, li, early=None):
    d = dm["D"]
    t = dx2.shape[0]
    bsz = t // SEQ
    nm = f"l{li}b_"
    gr = {}
    dup = _mm(dx2, wl["w_down"], tb=True, aux=sv["up"], epi="relu2_bwd", name=nm + "dup")
    gr["w_down"] = _mm(sv["act"], dx2, ta=True, name=nm + "gw_down")
    dhm = _mm(dup, wl["w_up"], tb=True, out_dtype=F32, name=nm + "dhm")
    gr["w_up"] = _mm(sv["hm"], dup, ta=True, name=nm + "gw_up")
    if early is not None:
        early.early("w_down", gr["w_down"])
        early.early("w_up", gr["w_up"])
    dx1, gmlp = _rms_bwd(sv["x1"], wl["g_mlp"], dhm, dx2, nm + "rms2")
    gr["g_mlp"] = gmlp[0]
    dmerged = _mm(dx1, wl["w_out"], tb=True, out_dtype=F32, name=nm + "dmerged")
    gr["w_out"] = _mm(sv["merged"], dx1, ta=True, name=nm + "gw_out")
    dpa, dpb, dpc, dg0, dg1, dg2 = _gate_bwd(dmerged, sv["pa"], sv["pb"], sv["pc"], sv["proj"], dm["OFF_GATE"], d, nm + "gate")
    dya = _mm(dpa, wl["w_a"], tb=True, out_dtype=F32, name=nm + "dya")
    gr["w_a"] = _mm(sv["y_a"], dpa, ta=True, name=nm + "gw_a")
    dyb = _mm(dpb, wl["w_b"], tb=True, out_dtype=F32, name=nm + "dyb")
    gr["w_b"] = _mm(sv["y_bm"], dpb, ta=True, name=nm + "gw_b")
    dyc = _mm(dpc, wl["w_c"], tb=True, out_dtype=F32, name=nm + "dyc")
    gr["w_c"] = _mm(sv["y_c"], dpc, ta=True, name=nm + "gw_c")
    if early is not None:
        for n in ("w_out", "w_a", "w_b", "w_c"):
            early.early(n, gr[n])
    ng, dw = dm["NG"], dm["DW"]
    xw = sv["xw"]
    rep = WIN_Q_HEADS // WIN_KV_HEADS
    delta_w = _head_dots(dyc, sv["y_c"], WIN_Q_HEADS, nm + "win_delta")
    dl_col = _stat_cols(delta_w, 1, WIN_Q_HEADS)
    lse_w = sv["lse_w"]
    dyc3 = dyc.reshape(bsz, SEQ, dm["WQ"])
    wargs = (0, rep, rep + 1, WIN_Q_HEADS, rep, SEQ, WIN_HALF)
    dq_w = _attn_dq(xw, xw, xw, dyc3, lse_w, dl_col, *wargs, nm + "win_dq")
    dk_w, dv_w = _attn_dkv(xw, xw, xw, dyc3, _stat_rows(lse_w), _stat_rows(dl_col), *wargs, nm + "win_dkv")
    lse_w2 = lse_w[..., 0].transpose(0, 2, 1).reshape(t, WIN_Q_HEADS)
    gr["sink"] = _sink_grad(lse_w2, delta_w, wl["sink"], nm + "sink")[0]
    delta_d = _head_dots(dyb, sv["y_bm"], DIL_HEADS, nm + "dil_delta")
    dqs, dks, dvs = [], [], []
    for gi, (window, dil) in enumerate(DIL_PATTERNS):
        xg = sv["xgs"][gi]
        n = SEQ // dil
        do_g = _stride(dyb, dil)
        lse_c = _stat_cols(sv["lse_tot"], dil, DIL_HEADS)
        dl_c = _stat_cols(delta_d, dil, DIL_HEADS)
        dargs = (0, 1, 2, DIL_HEADS, 1, n, window // (2 * dil))
        dq = _attn_dq(xg, xg, xg, do_g, lse_c, dl_c, *dargs, nm + f"dil{gi}_dq")
        dk, dv = _attn_dkv(xg, xg, xg, do_g, _stat_rows(lse_c), _stat_rows(dl_c), *dargs, nm + f"dil{gi}_dkv")
        dqs.append(_unstride(dq, dil))
        dks.append(_unstride(dk, dil))
        dvs.append(_unstride(dv, dil))
    dqkv_r = jnp.concatenate(dqs + dks + dvs + [dq_w.reshape(t, dm["WQ"]), dk_w.reshape(t, dm["WK"]), dv_w.reshape(t, dm["WK"])],
                             axis=1)
    dqkv = _rope(dqkv_r, 0, dm["QW"] // HEAD_DIM, tabs[1], dm, nm + "rope")
    hi, gn = dm["HI"], dm["GN"]
    dyt, dxs0, dz, gnorm, dd_cols = _ssd_out_bwd(dya, sv["y_f"], sv["y_b"], sv["u"], sv["proj"], wl["d_cols"],
                                                          wl["ssd_norm"], hi, nm + "ssd_out")
    gr["ssd_norm"] = gnorm[0]
    gr["d_skip"] = dd_cols.reshape(SSD_HEADS, SSD_HEAD_DIM).sum(axis=1)
    sp = _scan_params(sv["dtp"], wl["a_log"], dm)
    if early is None:
        dxs1, db1, dc1, rq_f, xdx_f = _scan_bwd(sv["u"], *sp, dyt, sv["s_f"], (dxs0, None, None), 0, dm, nm + "scan_f")
        dxs2, db2, dc2, rq_r, xdx_r = _scan_bwd(sv["u"], *sp, dyt, sv["s_b"], (dxs1, db1, dc1), 1, dm, nm + "scan_b")
    else:
        ride_f, ride_b = ("w_up", "w_out", "w_a"), ("w_down", "w_b", "w_c")
        (dxs1, db1, dc1, rq_f, xdx_f), got = _scan_bwd(sv["u"], *sp, dyt, sv["s_f"], (dxs0, None, None), 0, dm, nm + "scan_f",
                                                       side=early.side(ride_f))
        early.took(ride_f, got)
        (dxs2, db2, dc2, rq_r, xdx_r), got = _scan_bwd(sv["u"], *sp, dyt, sv["s_b"], (dxs1, db1, dc1), 1, dm, nm + "scan_b",
                                                       side=early.side(ride_b))
        early.took(ride_b, got)

    def heads(a):
        return a.transpose(1, 0, 2).reshape(t, SSD_HEADS)

    zpad = jnp.zeros((t, LANES - dm["H2"]), F32)
    zh = jnp.zeros((t, SSD_HEADS), F32)
    rqf_p = jnp.concatenate([heads(rq_f), zh, zpad], axis=1)
    rqr_p = jnp.concatenate([zh, heads(rq_r), zpad], axis=1)
    xdx_p = jnp.concatenate([heads(xdx_f), heads(xdx_r), zpad], axis=1)
    ddtr, dbias, dalog = _ssd_param_bwd(rqf_p, rqr_p, xdx_p, sv["dtp"], sv["dtr"], wl["dt_bias"], wl["a_log_p"], nm + "ssd_par")
    gr["dt_bias"] = dbias[0, :dm["H2"]].reshape(2, SSD_HEADS)
    gr["a_log"] = dalog[0, :dm["H2"]].reshape(2, SSD_HEADS)
    du = jnp.concatenate([dxs2, db2, dc2], axis=1)
    dxbc, gr["conv_w"], gcb = _conv_bwd(du, sv["cpre"], sv["proj"], dm["OFF_XBC"], wl["conv_w"], nm + "conv")
    gr["conv_b"] = gcb[0]
    dproj = jnp.concatenate([dz, dxbc, dqkv, dg0, dg1, dg2], axis=1)
    dh_dt = _mm(ddtr, wl["w_dt"], out_dtype=F32, name=nm + "dh_dt")
    gw_main = _mm(dproj, sv["h"], ta=True, name=nm + "gw_main")
    gw_dt = _mm(ddtr, sv["h"], ta=True, name=nm + "gw_dt")
    o1, h2 = dm["OFF_QKV"], dm["H2"]
    gw_in_t = jnp.concatenate([gw_main[:o1], gw_dt[:h2], gw_main[o1:]], axis=0)
    gr["w_in"] = gw_in_t.reshape(4, (dm["NM"] + h2) // 4, d)
    if early is None:
        dh = _mm(dproj, wl["w_main"], add=dh_dt, out_dtype=F32, name=nm + "dh")
    else:
        early.early("w_in", gr["w_in"])
        dh, got = _mm(dproj, wl["w_main"], add=dh_dt, out_dtype=F32, side=early.side(("w_in",)), name=nm + "dh")
        early.took(("w_in",), got)
    dx, gmix = _rms_bwd(sv["x"], wl["g_mix"], dh, dx1, nm + "rms1")
    gr["g_mix"] = gmix[0]
    return dx, gr


def _layer_weights(full, li, dm):
    st = full["w_in"]
    o1 = dm["OFF_QKV"]
    h2 = dm["H2"]
    d = dm["D"]
    w_in_t = st[li].reshape(4 * st.shape[2], d)
    wl = dict(
        w_main=jnp.concatenate([w_in_t[:o1], w_in_t[o1 + h2:]], axis=0),
        w_dt=jnp.pad(w_in_t[o1:o1 + h2], ((0, LANES - h2), (0, 0))),
        **{n: (full[n][li] if n in full else None) for n in ("w_a", "w_b", "w_c", "w_out", "w_up", "w_down")},
        conv_w=full["conv_w"][li], conv_b=full["conv_b"][li][None, :],
        g_mix=full["g_mix"][li][None, :], g_mlp=full["g_mlp"][li][None, :], ssd_norm=full["ssd_norm"][li][None, :],
        d_cols=jnp.repeat(full["d_skip"][li], SSD_HEAD_DIM)[None, :],
        sink=full["sink"][li][None, :],
        a_log=full["a_log"][li],
        a_log_p=jnp.pad(full["a_log"][li].reshape(1, h2), ((0, 0), (0, LANES - h2))),
        dt_bias=jnp.pad(full["dt_bias"][li].reshape(1, h2), ((0, 0), (0, LANES - h2))),
    )
    assert wl["w_main"].shape == (dm["NM"], d)
    return wl


def _local_step(x, target, full, depth, late=None, early=None):
    bsz, seq, d = x.shape
    assert seq == SEQ
    dm = _dims(d)
    assert dm["OFF_GATE"] % d == 0 and dm["HI"] % (dm["HG"] * SSD_HEAD_DIM) == 0 and dm["H2"] <= LANES
    tabs = (_rope_tables(1.0), _rope_tables(-1.0))
    xt = x.reshape(bsz * seq, d)
    wls, saves = [], []
    for li in range(depth):
        wl = _layer_weights(full, li, dm)
        hosts = {}
        if li == 0 and late is not None:
            for key, (side, arrived) in late.items():
                hosts[key] = (side, lambda brought, arrived=arrived: {n: full[n][0] for n in arrived(brought)})
        xt, sv = _layer_fwd(xt, wl, tabs, dm, li, late=hosts)
        wls.append(wl)
        saves.append(sv)
    dx, loss, g_final = _loss_head(xt, full["g_final"][None, :], target.reshape(bsz * seq, d), "loss_head")
    grads = [None] * depth
    if early is not None:
        early.grads = grads
    for li in reversed(range(depth)):
        dx, grads[li] = _layer_bwd(dx, wls[li], saves[li], tabs, dm, li, early=early if li == 0 else None)
    return loss, dx.reshape(bsz, seq, d), grads, g_final[0]


BIG = ("w_in", "w_a", "w_b", "w_c", "w_out", "w_up", "w_down")
COL_SHARDED = ("w_in", "w_b", "w_up")
SMALL = ("g_mix", "conv_w", "conv_b", "dt_bias", "a_log", "d_skip", "ssd_norm", "sink", "g_mlp", "g_final")
ORDER = ("g_mix", "w_in", "conv_w", "conv_b", "dt_bias", "a_log", "d_skip", "ssd_norm", "w_a", "w_b", "w_c", "sink",
         "w_out", "g_mlp", "w_up", "w_down", "g_final")


def _unstack(name, st):
    nl, _, r, c = st.shape
    if name in COL_SHARDED:
        return jnp.moveaxis(st, 1, 2).reshape(nl, r, 4 * c)
    return st.reshape(nl, 4 * r, c)


def _restack(name, gfull):
    nl, r, c = gfull.shape
    if name in COL_SHARDED:
        return jnp.moveaxis(gfull.reshape(nl, r, 4, c // 4), 2, 1)
    return gfull.reshape(nl, 4, r // 4, c)


def kernel(x, g_mix, w_in, conv_w, conv_b, dt_bias, a_log, d_skip, ssd_norm, w_a, w_b, w_c, sink, w_out, g_mlp, w_up, w_down, g_final, loss_target, m_g_mix, m_w_in, m_conv_w, m_conv_b, m_dt_bias, m_a_log, m_d_skip, m_ssd_norm, m_w_a, m_w_b, m_w_c, m_sink, m_w_out, m_g_mlp, m_w_up, m_w_down, m_g_final, v_g_mix, v_w_in, v_conv_w, v_conv_b, v_dt_bias, v_a_log, v_d_skip, v_ssd_norm, v_w_a, v_w_b, v_w_c, v_sink, v_w_out, v_g_mlp, v_w_up, v_w_down, v_g_final):
    w = dict(g_mix=g_mix, w_in=w_in, conv_w=conv_w, conv_b=conv_b, dt_bias=dt_bias, a_log=a_log, d_skip=d_skip,
             ssd_norm=ssd_norm, w_a=w_a, w_b=w_b, w_c=w_c, sink=sink, w_out=w_out, g_mlp=g_mlp, w_up=w_up, w_down=w_down,
             g_final=g_final)
    m = dict(g_mix=m_g_mix, w_in=m_w_in, conv_w=m_conv_w, conv_b=m_conv_b, dt_bias=m_dt_bias, a_log=m_a_log,
             d_skip=m_d_skip, ssd_norm=m_ssd_norm, w_a=m_w_a, w_b=m_w_b, w_c=m_w_c, sink=m_sink, w_out=m_w_out,
             g_mlp=m_g_mlp, w_up=m_w_up, w_down=m_w_down, g_final=m_g_final)
    v = dict(g_mix=v_g_mix, w_in=v_w_in, conv_w=v_conv_w, conv_b=v_conv_b, dt_bias=v_dt_bias, a_log=v_a_log,
             d_skip=v_d_skip, ssd_norm=v_ssd_norm, w_a=v_w_a, w_b=v_w_b, w_c=v_w_c, sink=v_sink, w_out=v_w_out,
             g_mlp=v_g_mlp, w_up=v_w_up, w_down=v_w_down, g_final=v_g_final)
    depth = w_in.shape[0]
    assert depth == 2
    kchip = 2 * lax.axis_index("x") + lax.axis_index("y")

    full = {n: w[n] for n in SMALL if n != "conv_w"}
    tr = lambda a: jnp.swapaxes(a, 1, 2)
    full["w_in"] = _gather_chips(tr(w_in).astype(BF16), "gather_w_in")
    cw = _gather_chips(conv_w, "gather_conv_w")
    full["conv_w"] = jnp.moveaxis(cw, 1, 2).reshape(depth, CONV_WIDTH, 4 * conv_w.shape[2])
    riders = {"proj": ("w_up",), "scan_f": ("w_down",), "scan_b": ("w_a", "w_b", "w_c", "w_out")}
    late = {}
    for key, names in riders.items():
        shards = [w[n].astype(BF16) for n in names]

        def arrived(brought, names=names, shards=shards):
            for n, shard, st in zip(names, shards, brought):
                full[n] = _unstack(n, _put_own(st, shard))
            return names

        late[key] = (_gather_side(shards), arrived)

    early = _EarlyReduce(BIG)
    loss_part, grad_x, grads, gg_final = _local_step(x, loss_target, full, depth, late=late, early=early)
    gsh = {n: early.finish(n) for n in BIG}
    small_names = [n for n in SMALL if n != "g_final"]
    small_g = [jnp.stack([grads[li][n] for li in range(depth)]) for n in small_names] + [gg_final, loss_part[0, :1]]
    red = _unpack(_all_reduce_small(_pack(small_g), "allreduce_small"), small_g)
    for n, a in zip(small_names + ["g_final"], red):
        gsh[n] = a
    loss = red[-1][0]
    cshard = conv_w.shape[2]
    gsh["conv_w"] = lax.dynamic_slice_in_dim(gsh["conv_w"], kchip * cshard, cshard, axis=2)

    delta, new_m, new_v = {}, {}, {}
    for n in BIG:
        if n == "w_in":
            outs_t = _adamw(tr(w[n]), gsh[n], tr(m[n]), tr(v[n]), "adamw_" + n)
            delta[n], new_m[n], new_v[n] = [tr(o) for o in outs_t]
            gsh[n] = tr(gsh[n])
        else:
            delta[n], new_m[n], new_v[n] = _adamw(w[n], gsh[n], m[n], v[n], "adamw_" + n)
    sm = list(SMALL)
    packed = [_pack([d_[n] for n in sm])[None] for d_ in (w, gsh, m, v)]
    outs = [o[0] for o in _adamw(*packed, "adamw_small")]
    for dst, buf in zip((delta, new_m, new_v), outs):
        for n, a in zip(sm, _unpack(buf, [w[n] for n in sm])):
            dst[n] = a
    return (loss, grad_x, *[gsh[n] for n in ORDER], *[delta[n] for n in ORDER], *[new_m[n] for n in ORDER],
            *[new_v[n] for n in ORDER])
```

```python
import functools
import math

import jax
import jax.numpy as jnp
from jax import lax
from jax.experimental import pallas as pl
from jax.experimental.pallas import tpu as pltpu

F32 = jnp.float32
BF16 = jnp.bfloat16

SEQ = 2048
SSD_HEADS = 32
SSD_HEAD_DIM = 64
SSD_GROUPS = 8
SSD_STATE = 128
SSD_CHUNK = 128
CONV_WIDTH = 5
HEAD_DIM = 128
ROPE_DIM = 32
ROPE_THETA = 500000.0
DIL_PATTERNS = ((128, 1), (512, 4), (2048, 16))
DIL_HEADS = 8
WIN_Q_HEADS = 16
WIN_KV_HEADS = 4
WIN_HALF = 128
EPS = 1e-6
NEG_BIG = -1e30
ADAM_LR = 0.001
ADAM_B1 = 0.9
ADAM_B2 = 0.999
ADAM_EPS = 1e-08
ADAM_WD = 0.01
ADAM_STEP = 10

LANES = 128
ATT_BLK = 128
ROW_TILES = (320, 256, 128, 80, 64, 32, 16, 8)
VMEM_LIMIT = 48 * 1024 * 1024
MESH = pl.DeviceIdType.MESH
HIGHEST = lax.Precision.HIGHEST
NT = (((1,), (1,)), ((), ()))
TN = (((0,), (0,)), ((), ()))
NN = (((1,), (0,)), ((), ()))


def _dims(d_model):
    hi = SSD_HEADS * SSD_HEAD_DIM
    gn = SSD_GROUPS * SSD_STATE
    ng = len(DIL_PATTERNS)
    dw = DIL_HEADS * HEAD_DIM
    wq = WIN_Q_HEADS * HEAD_DIM
    wk = WIN_KV_HEADS * HEAD_DIM
    d = dict(D=d_model, HI=hi, GN=gn, XBC=hi + 2 * gn, H2=2 * SSD_HEADS, NG=ng, DW=dw, WQ=wq, WK=wk,
             QKVD=3 * ng * dw, QW=3 * ng * dw + wq + 2 * wk, HG=SSD_HEADS // SSD_GROUPS)
    d["OFF_XBC"] = hi
    d["OFF_QKV"] = hi + d["XBC"]
    d["OFF_GATE"] = d["OFF_QKV"] + d["QW"]
    d["NM"] = d["OFF_GATE"] + 3 * d_model
    return d


def _pick(n, prefs):
    for p in prefs:
        if n % p == 0:
            return p
    return n


def _params(sem):
    return pltpu.CompilerParams(dimension_semantics=sem, vmem_limit_bytes=VMEM_LIMIT)


def _sigmoid(x):
    return 1.0 / (1.0 + jnp.exp(-x))


class _Side:
    def __init__(self, args, outs, sems, start, finish):
        self.args, self.outs, self.sems, self.start, self.finish = args, outs, sems, start, finish


def _carry(side, body, n_in, n_out, n_scratch, grid):
    if side is None:
        return body
    n_sin, n_sout = len(side.args), len(side.outs)

    def wrapped(*refs):
        o0 = n_in + n_sin
        s0 = o0 + n_out + n_sout
        s_in, s_out, s_sem = refs[n_in:o0], refs[o0 + n_out:s0], refs[s0 + n_scratch:]
        ids = [pl.program_id(ax) for ax in range(len(grid))]
        first = functools.reduce(jnp.logical_and, [i == 0 for i in ids])
        last = functools.reduce(jnp.logical_and, [i == g - 1 for i, g in zip(ids, grid)])

        @pl.when(first)
        def _():
            side.start(s_in, s_out, s_sem)

        body(*refs[:n_in], *refs[o0:o0 + n_out], *refs[s0:s0 + n_scratch])

        @pl.when(last)
        def _():
            side.finish(s_in, s_out, s_sem)

    return wrapped


def _carry_call(side, body, *, name, grid, in_specs, out_specs, out_shape, scratch_shapes, sem, args):
    n_in, n_out = len(in_specs), len(out_specs)
    in_specs, out_specs, out_shape, scratch_shapes, args = (list(v) for v in (in_specs, out_specs, out_shape,
                                                                              scratch_shapes, args))
    body = _carry(side, body, n_in, n_out, len(scratch_shapes), grid)
    if side is not None:
        any_spec = pl.BlockSpec(memory_space=pl.ANY)
        in_specs += [any_spec] * len(side.args)
        args += list(side.args)
        out_specs += [any_spec] * len(side.outs)
        out_shape += list(side.outs)
        scratch_shapes += list(side.sems)
        sem = ("arbitrary",) * len(grid)
    res = pl.pallas_call(body, name=name, grid=grid, in_specs=in_specs, out_specs=out_specs, out_shape=out_shape,
                         scratch_shapes=scratch_shapes, compiler_params=_params(sem))(*args)
    return list(res[:n_out]), list(res[n_out:])


def _mm(a, b, *, ta=False, tb=False, add=None, aux=None, epi=None, out_dtype=BF16, side=None, name):
    if ta:
        kdim, m = a.shape
    else:
        m, kdim = a.shape
    if tb:
        n, k2 = b.shape
    else:
        k2, n = b.shape
    assert kdim == k2, (a.shape, b.shape, ta, tb)
    tm = _pick(m, (1024, 512, 256, 128, 64, 32, 16, 8))
    tn = _pick(n, (1024, 512, 256, 128))
    tk = _pick(kdim, (2048, 1024, 512, 256, 128))
    nk = kdim // tk
    dims = (((0 if ta else 1,), (1 if tb else 0,)), ((), ()))
    n_in = 2 + (add is not None) + (aux is not None)
    n_out = 2 if epi == "relu2" else 1
    n_sin = len(side.args) if side else 0
    n_sout = len(side.outs) if side else 0
    grid = (m // tm, n // tn, nk)

    def body(*refs):
        a_ref, b_ref = refs[0], refs[1]
        pos = 2
        add_ref = aux_ref = None
        if add is not None:
            add_ref = refs[pos]
            pos += 1
        if aux is not None:
            aux_ref = refs[pos]
            pos += 1
        out_refs = refs[n_in + n_sin:n_in + n_sin + n_out]
        acc_ref = refs[n_in + n_sin + n_out + n_sout]
        k = pl.program_id(2)
        if side is not None:
            s_in = refs[n_in:n_in + n_sin]
            s_out = refs[n_in + n_sin + n_out:n_in + n_sin + n_out + n_sout]
            s_sem = refs[n_in + n_sin + n_out + n_sout + 1:]
            i, j = pl.program_id(0), pl.program_id(1)

            @pl.when((i == 0) & (j == 0) & (k == 0))
            def _():
                side.start(s_in, s_out, s_sem)

        @pl.when(k == 0)
        def _():
            acc_ref[...] = jnp.zeros_like(acc_ref)

        acc_ref[...] += lax.dot_general(a_ref[...].astype(BF16), b_ref[...].astype(BF16), dims,
                                        preferred_element_type=F32)

        @pl.when(k == nk - 1)
        def _():
            r = acc_ref[...]
            if add_ref is not None:
                r = r + add_ref[...].astype(F32)
            if epi == "relu2":
                out_refs[0][...] = r.astype(out_refs[0].dtype)
                out_refs[1][...] = jnp.square(jnp.maximum(r, 0.0)).astype(out_refs[1].dtype)
            elif epi == "relu2_bwd":
                out_refs[0][...] = (r * 2.0 * jnp.maximum(aux_ref[...].astype(F32), 0.0)).astype(out_refs[0].dtype)
            else:
                out_refs[0][...] = r.astype(out_refs[0].dtype)

        if side is not None:
            @pl.when((i == grid[0] - 1) & (j == grid[1] - 1) & (k == nk - 1))
            def _():
                side.finish(s_in, s_out, s_sem)

    a_spec = pl.BlockSpec((tk, tm), lambda i, j, k: (k, i)) if ta else pl.BlockSpec((tm, tk), lambda i, j, k: (i, k))
    b_spec = pl.BlockSpec((tn, tk), lambda i, j, k: (j, k)) if tb else pl.BlockSpec((tk, tn), lambda i, j, k: (k, j))
    o_spec = pl.BlockSpec((tm, tn), lambda i, j, k: (i, j))
    in_specs = [a_spec, b_spec]
    args = [a, b]
    if add is not None:
        in_specs.append(o_spec)
        args.append(add)
    if aux is not None:
        in_specs.append(o_spec)
        args.append(aux)
    out_shape = [jax.ShapeDtypeStruct((m, n), out_dtype)] * n_out
    out_specs = [o_spec] * n_out
    scratch = [pltpu.VMEM((tm, tn), F32)]
    sem = ("parallel", "parallel", "arbitrary")
    if side is not None:
        any_spec = pl.BlockSpec(memory_space=pl.ANY)
        in_specs += [any_spec] * n_sin
        args += list(side.args)
        out_shape += list(side.outs)
        out_specs += [any_spec] * n_sout
        scratch += list(side.sems)
        sem = ("arbitrary", "arbitrary", "arbitrary")
    res = pl.pallas_call(
        body, name=name, grid=grid, in_specs=in_specs, out_specs=out_specs, out_shape=out_shape, scratch_shapes=scratch,
        compiler_params=_params(sem),
    )(*args)
    if side is not None:
        return (res[0] if n_out == 1 else tuple(res[:n_out])), list(res[n_out:])
    return res if n_out == 2 else res[0]


def _rowwise(body, rows, fulls, outs, accs=(), *, tile, name):
    rows = [r if isinstance(r, tuple) else (r, 0, r.shape[1]) for r in rows]
    nrows = rows[0][0].shape[0]
    assert nrows % tile == 0, (nrows, tile)
    in_specs, args = [], []
    for arr, off, width in rows:
        assert arr.shape[0] == nrows and off % width == 0, (arr.shape, off, width)
        in_specs.append(pl.BlockSpec((tile, width), lambda i, o=off // width: (i, o)))
        args.append(arr)
    for arr in fulls:
        in_specs.append(pl.BlockSpec(arr.shape, lambda i, nd=arr.ndim: (0,) * nd))
        args.append(arr)
    out_specs, out_shape = [], []
    for cols, dt in outs:
        out_specs.append(pl.BlockSpec((tile, cols), lambda i: (i, 0)))
        out_shape.append(jax.ShapeDtypeStruct((nrows, cols), dt))
    for shp, dt in accs:
        out_specs.append(pl.BlockSpec(shp, lambda i, nd=len(shp): (0,) * nd))
        out_shape.append(jax.ShapeDtypeStruct(shp, dt))
    n_in, n_out = len(args), len(outs)

    def wrapped(*refs):
        acc_refs = refs[n_in + n_out:]
        if acc_refs:
            @pl.when(pl.program_id(0) == 0)
            def _():
                for r in acc_refs:
                    r[...] = jnp.zeros_like(r)
        body(*refs)

    return pl.pallas_call(
        wrapped, name=name, grid=(nrows // tile,), in_specs=in_specs, out_specs=out_specs, out_shape=out_shape,
        compiler_params=_params(("arbitrary",)),
    )(*args)


def _rms_fwd(x, g, name):
    def body(x_ref, g_ref, h_ref):
        xv = x_ref[...]
        rstd = lax.rsqrt(jnp.mean(xv * xv, axis=-1, keepdims=True) + EPS)
        h_ref[...] = (xv * rstd * g_ref[...]).astype(BF16)

    return _rowwise(body, [x], [g], [(x.shape[1], BF16)], tile=256, name=name)[0]


def _rms_bwd(x, g, dh, dres, name):
    def body(x_ref, dh_ref, dres_ref, g_ref, dx_ref, dg_ref):
        xv = x_ref[...]
        dv = dh_ref[...]
        rstd = lax.rsqrt(jnp.mean(xv * xv, axis=-1, keepdims=True) + EPS)
        xn = xv * rstd
        dg_ref[...] += jnp.sum(dv * xn, axis=0, keepdims=True)
        dn = dv * g_ref[...]
        dx_ref[...] = dres_ref[...] + rstd * (dn - xn * jnp.mean(dn * xn, axis=-1, keepdims=True))

    d = x.shape[1]
    return _rowwise(body, [x, dh, dres], [g], [(d, F32)], [((1, d), F32)], tile=256, name=name)


def _loss_head(x, g, target, name):
    d = x.shape[1]

    def body(x_ref, t_ref, g_ref, dx_ref, loss_ref, dg_ref):
        xv = x_ref[...]
        rstd = lax.rsqrt(jnp.mean(xv * xv, axis=-1, keepdims=True) + EPS)
        xn = xv * rstd
        err = xn * g_ref[...] - t_ref[...]
        loss_ref[...] += jnp.full((1, LANES), 0.5 / d, F32) * jnp.sum(err * err)
        dy = err * (1.0 / d)
        dg_ref[...] += jnp.sum(dy * xn, axis=0, keepdims=True)
        dn = dy * g_ref[...]
        dx_ref[...] = rstd * (dn - xn * jnp.mean(dn * xn, axis=-1, keepdims=True))

    return _rowwise(body, [x, target], [g], [(d, F32)], [((1, LANES), F32), ((1, d), F32)], tile=256, name=name)


def _rope_tables(sign):
    half = ROPE_DIM // 2
    inv = ROPE_THETA ** (-jnp.arange(0, ROPE_DIM, 2, dtype=F32) / ROPE_DIM)
    ang = jnp.arange(SEQ, dtype=F32)[:, None] * inv[None, :]
    cos, sin = jnp.cos(ang), jnp.sin(ang) * sign
    zeros = jnp.zeros((SEQ, HEAD_DIM - ROPE_DIM), F32)
    zh = jnp.zeros((SEQ, half), F32)
    c = jnp.concatenate([cos, cos, zeros + 1.0], axis=1)
    s_up = jnp.concatenate([-sin, zh, zeros], axis=1)
    s_dn = jnp.concatenate([zh, sin, zeros], axis=1)
    return c, s_up, s_dn


def _rope(src, off, nblk, tabs, dm, name):
    t = src.shape[0]
    tq = 256
    half = ROPE_DIM // 2
    win0 = 3 * dm["NG"] * DIL_HEADS
    win1 = win0 + WIN_Q_HEADS + WIN_KV_HEADS
    qw = nblk * HEAD_DIM
    assert nblk == dm["QW"] // HEAD_DIM
    wb = next(c for c in (1024, 768, 512, 384, 256, 128) if off % c == 0 and qw % c == 0)
    reps = wb // HEAD_DIM
    sb = SEQ // tq
    flag = (jnp.arange(qw, dtype=jnp.int32) // HEAD_DIM < win1).astype(F32)[None, :]

    def body(x_ref, c_ref, up_ref, dn_ref, f_ref, o_ref):
        xv = x_ref[...].astype(F32)

        def wide(r):
            v = r[...]
            return v if reps == 1 else jnp.concatenate([v] * reps, axis=1)

        rot = xv * wide(c_ref) + pltpu.roll(xv, wb - half, 1) * wide(up_ref) + pltpu.roll(xv, half, 1) * wide(dn_ref)
        o_ref[...] = jnp.where(f_ref[...] > 0.5, rot, xv).astype(BF16)

    tab_spec = pl.BlockSpec((tq, HEAD_DIM), lambda i, j: (i % sb, 0))
    return pl.pallas_call(
        body, name=name, grid=(t // tq, qw // wb),
        in_specs=[pl.BlockSpec((tq, wb), lambda i, j, o=off // wb: (i, o + j)), tab_spec, tab_spec, tab_spec,
                  pl.BlockSpec((1, wb), lambda i, j: (0, j))],
        out_specs=pl.BlockSpec((tq, wb), lambda i, j: (i, j)),
        out_shape=jax.ShapeDtypeStruct((t, qw), BF16),
        compiler_params=_params(("parallel", "parallel")),
    )(src, *tabs, flag)


def _band_mask(rows_start, cols_start, nrows, ncols, w, n, rows_are_q):
    r = rows_start + lax.broadcasted_iota(jnp.int32, (nrows, ncols), 0)
    c = cols_start + lax.broadcasted_iota(jnp.int32, (nrows, ncols), 1)
    del rows_are_q
    return (jnp.abs(r - c) <= w) & (c >= 0) & (c < n)


def _nbr_specs(make, nb):
    if nb == 1:
        return [make(lambda i: i)]
    return [make(lambda i: jnp.maximum(i - 1, 0)), make(lambda i: i), make(lambda i: jnp.minimum(i + 1, nb - 1))]


def _cat(refs, axis):
    vals = [r[...] for r in refs]
    return vals[0] if len(vals) == 1 else jnp.concatenate(vals, axis=axis)


def _head(ref, h):
    return ref[:, h * HEAD_DIM:(h + 1) * HEAD_DIM]


def _head_cat(refs, h, axis=0):
    vals = [_head(r, h) for r in refs]
    return vals[0] if len(vals) == 1 else jnp.concatenate(vals, axis=axis)


def _attn_fwd(qa, ka, va, qb, kb, vb, hq, rep, n, w, sink, out_dtype, name):
    bb = qa.shape[0]
    blk = ATT_BLK
    nb = n // blk
    nk = 1 if nb == 1 else 3
    hkv = hq // rep
    scale = HEAD_DIM ** -0.5
    has_sink = sink is not None

    def body(*refs):
        q_ref = refs[0]
        k_refs = refs[1:1 + nk]
        v_refs = refs[1 + nk:1 + 2 * nk]
        pos = 1 + 2 * nk
        sink_ref = refs[pos] if has_sink else None
        o_ref, lse_ref = refs[pos + has_sink], refs[pos + has_sink + 1]
        i = pl.program_id(1)
        k0 = (i - 1) * blk if nk == 3 else i * blk
        valid = _band_mask(i * blk, k0, blk, nk * blk, w, n, True)
        for g in range(hkv):
            kcat = _head_cat(k_refs, g)
            vcat = _head_cat(v_refs, g)
            for r in range(rep):
                h = g * rep + r
                s = lax.dot_general(_head(q_ref, h), kcat, NT, preferred_element_type=F32) * scale
                s = jnp.where(valid, s, NEG_BIG)
                m = jnp.max(s, axis=1, keepdims=True)
                if has_sink:
                    m = jnp.maximum(m, sink_ref[h])
                p = jnp.exp(s - m)
                l = jnp.sum(p, axis=1, keepdims=True)
                if has_sink:
                    l = l + jnp.exp(sink_ref[h] - m)
                o = lax.dot_general(p.astype(BF16), vcat, NN, preferred_element_type=F32) / l
                o_ref[:, h * HEAD_DIM:(h + 1) * HEAD_DIM] = o.astype(o_ref.dtype)
                lse_ref[h] = m + jnp.log(l)

    def mk(col, width):
        return lambda f: pl.BlockSpec((None, blk, width), lambda b, i, f=f: (b, f(i), col))

    qw, kw = hq * HEAD_DIM, hkv * HEAD_DIM
    in_specs = [pl.BlockSpec((None, blk, qw), lambda b, i: (b, i, qb))]
    in_specs += _nbr_specs(mk(kb, kw), nb) + _nbr_specs(mk(vb, kw), nb)
    args = [qa] + [ka] * nk + [va] * nk
    if has_sink:
        in_specs.append(pl.BlockSpec((hq, 1, 1), lambda b, i: (0, 0, 0)))
        args.append(sink)
    return pl.pallas_call(
        body, name=name, grid=(bb, nb), in_specs=in_specs,
        out_specs=[pl.BlockSpec((None, blk, qw), lambda b, i: (b, i, 0)),
                   pl.BlockSpec((None, hq, blk, 1), lambda b, i: (b, 0, i, 0))],
        out_shape=[jax.ShapeDtypeStruct((bb, n, qw), out_dtype), jax.ShapeDtypeStruct((bb, hq, n, 1), F32)],
        compiler_params=_params(("parallel", "parallel")),
    )(*args)


def _attn_dq(qa, ka, va, do, lse, delta, qb, kb, vb, hq, rep, n, w, name):
    bb = qa.shape[0]
    blk = ATT_BLK
    nb = n // blk
    nk = 1 if nb == 1 else 3
    hkv = hq // rep
    scale = HEAD_DIM ** -0.5

    def body(*refs):
        q_ref = refs[0]
        k_refs = refs[1:1 + nk]
        v_refs = refs[1 + nk:1 + 2 * nk]
        do_ref, lse_ref, dl_ref, dq_ref = refs[1 + 2 * nk:]
        i = pl.program_id(1)
        k0 = (i - 1) * blk if nk == 3 else i * blk
        valid = _band_mask(i * blk, k0, blk, nk * blk, w, n, True)
        for g in range(hkv):
            kcat = _head_cat(k_refs, g)
            vcat = _head_cat(v_refs, g)
            for r in range(rep):
                h = g * rep + r
                s = lax.dot_general(_head(q_ref, h), kcat, NT, preferred_element_type=F32) * scale
                p = jnp.exp(jnp.where(valid, s, NEG_BIG) - lse_ref[h])
                dp = lax.dot_general(_head(do_ref, h).astype(BF16), vcat, NT, preferred_element_type=F32)
                ds = p * (dp - dl_ref[h])
                dq = lax.dot_general(ds.astype(BF16), kcat, NN, preferred_element_type=F32) * scale
                dq_ref[:, h * HEAD_DIM:(h + 1) * HEAD_DIM] = dq.astype(BF16)

    def mk(col, width):
        return lambda f: pl.BlockSpec((None, blk, width), lambda b, i, f=f: (b, f(i), col))

    qw, kw = hq * HEAD_DIM, hkv * HEAD_DIM
    col_spec = pl.BlockSpec((None, hq, blk, 1), lambda b, i: (b, 0, i, 0))
    in_specs = [pl.BlockSpec((None, blk, qw), lambda b, i: (b, i, qb))]
    in_specs += _nbr_specs(mk(kb, kw), nb) + _nbr_specs(mk(vb, kw), nb)
    in_specs += [pl.BlockSpec((None, blk, qw), lambda b, i: (b, i, 0)), col_spec, col_spec]
    return pl.pallas_call(
        body, name=name, grid=(bb, nb), in_specs=in_specs,
        out_specs=pl.BlockSpec((None, blk, qw), lambda b, i: (b, i, 0)),
        out_shape=jax.ShapeDtypeStruct((bb, n, qw), BF16),
        compiler_params=_params(("parallel", "parallel")),
    )(qa, *([ka] * nk), *([va] * nk), do, lse, delta)


def _attn_dkv(qa, ka, va, do, lse_row, delta_row, qb, kb, vb, hq, rep, n, w, name):
    bb = qa.shape[0]
    blk = ATT_BLK
    nb = n // blk
    nq = 1 if nb == 1 else 3
    hkv = hq // rep
    scale = HEAD_DIM ** -0.5

    def body(*refs):
        k_ref, v_ref = refs[0], refs[1]
        q_refs = refs[2:2 + nq]
        do_refs = refs[2 + nq:2 + 2 * nq]
        lse_refs = refs[2 + 2 * nq:2 + 3 * nq]
        dl_refs = refs[2 + 3 * nq:2 + 4 * nq]
        dk_ref, dv_ref = refs[2 + 4 * nq:]
        j = pl.program_id(1)
        q0 = (j - 1) * blk if nq == 3 else j * blk
        valid = _band_mask(j * blk, q0, blk, nq * blk, w, n, False)
        for g in range(hkv):
            kg, vg = _head(k_ref, g), _head(v_ref, g)
            dk = jnp.zeros((blk, HEAD_DIM), F32)
            dv = jnp.zeros((blk, HEAD_DIM), F32)
            for r in range(rep):
                h = g * rep + r
                qcat = _head_cat(q_refs, h)
                docat = _head_cat(do_refs, h).astype(BF16)
                lse = lse_refs[0][h] if nq == 1 else jnp.concatenate([lr[h] for lr in lse_refs], axis=1)
                dl = dl_refs[0][h] if nq == 1 else jnp.concatenate([dr[h] for dr in dl_refs], axis=1)
                st = lax.dot_general(kg, qcat, NT, preferred_element_type=F32) * scale
                pt = jnp.exp(jnp.where(valid, st, NEG_BIG) - lse)
                dv = dv + lax.dot_general(pt.astype(BF16), docat, NN, preferred_element_type=F32)
                dpt = lax.dot_general(vg, docat, NT, preferred_element_type=F32)
                dst = pt * (dpt - dl)
                dk = dk + lax.dot_general(dst.astype(BF16), qcat, NN, preferred_element_type=F32) * scale
            dk_ref[:, g * HEAD_DIM:(g + 1) * HEAD_DIM] = dk.astype(BF16)
            dv_ref[:, g * HEAD_DIM:(g + 1) * HEAD_DIM] = dv.astype(BF16)

    qw, kw = hq * HEAD_DIM, hkv * HEAD_DIM

    def mkq(col):
        return lambda f: pl.BlockSpec((None, blk, qw), lambda b, j, f=f: (b, f(j), col))

    def mkrow(f):
        return pl.BlockSpec((None, hq, 1, blk), lambda b, j, f=f: (b, 0, 0, f(j)))

    in_specs = [pl.BlockSpec((None, blk, kw), lambda b, j: (b, j, kb)), pl.BlockSpec((None, blk, kw), lambda b, j: (b, j, vb))]
    in_specs += _nbr_specs(mkq(qb), nb) + _nbr_specs(mkq(0), nb) + _nbr_specs(mkrow, nb) + _nbr_specs(mkrow, nb)
    o_spec = pl.BlockSpec((None, blk, kw), lambda b, j: (b, j, 0))
    return pl.pallas_call(
        body, name=name, grid=(bb, nb), in_specs=in_specs, out_specs=[o_spec, o_spec],
        out_shape=[jax.ShapeDtypeStruct((bb, n, kw), BF16)] * 2,
        compiler_params=_params(("parallel", "parallel")),
    )(ka, va, *([qa] * nq), *([do] * nq), *([lse_row] * nq), *([delta_row] * nq))


def _head_expand(v, nh, width):
    lane_head = lax.broadcasted_iota(jnp.int32, (1, nh * width), 1) >> int(math.log2(width))
    out = jnp.zeros((v.shape[0], nh * width), F32)
    for j in range(nh):
        out = jnp.where(lane_head == j, v[:, j:j + 1], out)
    return out


def _head_sums(m, nh, width):
    lane_head = lax.broadcasted_iota(jnp.int32, (1, nh * width), 1) >> int(math.log2(width))
    col = lax.broadcasted_iota(jnp.int32, (1, nh), 1)
    out = jnp.zeros((m.shape[0], nh), F32)
    for j in range(nh):
        sj = jnp.sum(jnp.where(lane_head == j, m, 0.0), axis=1, keepdims=True)
        out = jnp.where(col == j, sj, out)
    return out


def _head_dots(a, b, nh, name):
    def body(a_ref, b_ref, o_ref):
        o_ref[...] = _head_sums(a_ref[...].astype(F32) * b_ref[...].astype(F32), nh, HEAD_DIM)

    return _rowwise(body, [a, b], [], [(nh, F32)], tile=256, name=name)[0]


def _dil_combine(outs, lses, name):
    ng = len(outs)

    def body(*refs):
        o_refs, l_refs = refs[:ng], refs[ng:2 * ng]
        y_ref, lt_ref = refs[2 * ng], refs[2 * ng + 1]
        ls = [r[...] for r in l_refs]
        m = functools.reduce(jnp.maximum, ls)
        es = [jnp.exp(v - m) for v in ls]
        tot = functools.reduce(jnp.add, es)
        acc = jnp.zeros(o_refs[0].shape, F32)
        for o_ref, e in zip(o_refs, es):
            acc = acc + _head_expand(e / tot, DIL_HEADS, HEAD_DIM) * o_ref[...]
        y_ref[...] = acc.astype(BF16)
        lt_ref[...] = m + jnp.log(tot)

    dw = outs[0].shape[1]
    return _rowwise(body, list(outs) + list(lses), [], [(dw, BF16), (DIL_HEADS, F32)], tile=256, name=name)


def _sink_grad(lse, delta, sink, name):
    def body(l_ref, d_ref, s_ref, o_ref):
        o_ref[...] -= jnp.sum(jnp.exp(s_ref[...] - l_ref[...]) * d_ref[...], axis=0, keepdims=True)

    return _rowwise(body, [lse, delta], [sink], [], [((1, lse.shape[1]), F32)], tile=512, name=name)[0]


def _shift_rows(x, d, nrows):
    if d == 0:
        return x
    rolled = pltpu.roll(x, (-d) % nrows, 0)
    row = lax.broadcasted_iota(jnp.int32, x.shape, 0)
    ok = (row + d >= 0) & (row + d < nrows)
    return jnp.where(ok, rolled, 0.0)


def _conv_fwd(proj, off, conv_w, conv_b, xbc, name):
    t = proj.shape[0]
    tc = _pick(xbc, (256, 128))
    assert off % tc == 0
    pad = (CONV_WIDTH - 1) // 2

    def body(x_ref, w_ref, b_ref, c_ref, u_ref):
        xv = x_ref[...].astype(F32)
        acc = jnp.zeros_like(xv) + b_ref[...]
        for k in range(CONV_WIDTH):
            acc = acc + w_ref[k:k + 1, :] * _shift_rows(xv, k - pad, SEQ)
        c_ref[...] = acc.astype(BF16)
        u_ref[...] = (acc * _sigmoid(acc)).astype(BF16)

    o_spec = pl.BlockSpec((SEQ, tc), lambda b, j: (b, j))
    return pl.pallas_call(
        body, name=name, grid=(t // SEQ, xbc // tc),
        in_specs=[pl.BlockSpec((SEQ, tc), lambda b, j, o=off // tc: (b, o + j)),
                  pl.BlockSpec((CONV_WIDTH, tc), lambda b, j: (0, j)), pl.BlockSpec((1, tc), lambda b, j: (0, j))],
        out_specs=[o_spec, o_spec], out_shape=[jax.ShapeDtypeStruct((t, xbc), BF16)] * 2,
        compiler_params=_params(("parallel", "parallel")),
    )(proj, conv_w, conv_b)


def _conv_bwd(du, cpre, proj, off, conv_w, name):
    t, xbc = du.shape
    tc = _pick(xbc, (256, 128))
    pad = (CONV_WIDTH - 1) // 2

    def body(du_ref, c_ref, x_ref, w_ref, dx_ref, dw_ref, db_ref):
        @pl.when(pl.program_id(1) == 0)
        def _():
            dw_ref[...] = jnp.zeros_like(dw_ref)
            db_ref[...] = jnp.zeros_like(db_ref)

        cv = c_ref[...].astype(F32)
        sg = _sigmoid(cv)
        dc = du_ref[...] * (sg * (1.0 + cv * (1.0 - sg)))
        xv = x_ref[...].astype(F32)
        dx = jnp.zeros_like(dc)
        for k in range(CONV_WIDTH):
            dx = dx + w_ref[k:k + 1, :] * _shift_rows(dc, pad - k, SEQ)
            dw_ref[k:k + 1, :] += jnp.sum(dc * _shift_rows(xv, k - pad, SEQ), axis=0, keepdims=True)
        db_ref[...] += jnp.sum(dc, axis=0, keepdims=True)
        dx_ref[...] = dx.astype(BF16)

    blk = pl.BlockSpec((SEQ, tc), lambda j, b: (b, j))
    return pl.pallas_call(
        body, name=name, grid=(xbc // tc, t // SEQ),
        in_specs=[blk, blk, pl.BlockSpec((SEQ, tc), lambda j, b, o=off // tc: (b, o + j)),
                  pl.BlockSpec((CONV_WIDTH, tc), lambda j, b: (0, j))],
        out_specs=[blk, pl.BlockSpec((CONV_WIDTH, tc), lambda j, b: (0, j)), pl.BlockSpec((1, tc), lambda j, b: (0, j))],
        out_shape=[jax.ShapeDtypeStruct((t, xbc), BF16), jax.ShapeDtypeStruct((CONV_WIDTH, xbc), F32),
                   jax.ShapeDtypeStruct((1, xbc), F32)],
        compiler_params=_params(("parallel", "arbitrary")),
    )(du, cpre, proj, conv_w)


def _dt_prep(dtr, bias, name):
    def body(r_ref, b_ref, o_ref):
        v = r_ref[...] + b_ref[...]
        o_ref[...] = jnp.maximum(v, 0.0) + jnp.log1p(jnp.exp(-jnp.abs(v)))

    return _rowwise(body, [dtr], [bias], [(dtr.shape[1], F32)], tile=512, name=name)[0]


def _scan_prelude(d, dt_ref, dtt_ref, al_ref, alt_ref, hg):
    p = SSD_HEAD_DIM
    ch = SSD_CHUNK
    a_row = -jnp.exp(al_ref[...])
    a_col = -jnp.exp(alt_ref[...])
    dtc = dt_ref[...]
    dt_x = _head_expand(dtc, hg, p)
    dta_x = dt_x * _head_expand(a_row, hg, p)
    dta_t = dtt_ref[...] * a_col
    ri = lax.broadcasted_iota(jnp.int32, (ch, ch), 0)
    ci = lax.broadcasted_iota(jnp.int32, (ch, ch), 1)
    mask = (ci <= ri) if d == 0 else (ci >= ri)
    mask_t = (ci >= ri) if d == 0 else (ci <= ri)
    tri = mask.astype(F32)
    phi_x = jnp.dot(tri, dta_x, preferred_element_type=F32, precision=HIGHEST)
    phi_r = lax.dot_general(dta_t, tri, NT, preferred_element_type=F32, precision=HIGHEST)
    tot_x = jnp.sum(dta_x, axis=0, keepdims=True)
    return dtc, dt_x, phi_x, phi_r, tot_x, mask, mask_t


def _scan_specs(d, nc, hg, dm):
    p, n, ch = SSD_HEAD_DIM, SSD_STATE, SSD_CHUNK
    w = hg * p
    b0 = dm["HI"] // n
    c0 = (dm["HI"] + dm["GN"]) // n

    def row(b, c):
        return b * nc + c

    return [
        pl.BlockSpec((ch, w), lambda b, g, c: (row(b, c), g)),
        pl.BlockSpec((ch, n), lambda b, g, c: (row(b, c), b0 + g)),
        pl.BlockSpec((ch, n), lambda b, g, c: (row(b, c), c0 + g)),
        pl.BlockSpec((None, None, ch, hg), lambda b, g, c: (d, g, row(b, c), 0)),
        pl.BlockSpec((None, None, 8, ch), lambda b, g, c: (d, g, 0, row(b, c))),
        pl.BlockSpec((None, None, 1, hg), lambda b, g, c: (d, g, 0, 0)),
        pl.BlockSpec((None, None, 8, 1), lambda b, g, c: (d, g, 0, 0)),
    ]


def _remap(spec, f):
    return pl.BlockSpec(spec.block_shape, lambda b, g, c, im=spec.index_map: im(b, g, f(c)))


def _scan_fwd(u, dtg, dttg, alg, altg, d, dm, name, side=None):
    t = u.shape[0]
    p, n, ch, hg = SSD_HEAD_DIM, SSD_STATE, SSD_CHUNK, dm["HG"]
    w = hg * p
    nc = SEQ // ch
    order = (lambda c: c) if d == 0 else (lambda c: nc - 1 - c)

    def body(x_ref, b_ref, c_ref, dt_ref, dtt_ref, al_ref, alt_ref, y_ref, sin_ref, s_ref):
        @pl.when(pl.program_id(2) == 0)
        def _():
            s_ref[...] = jnp.zeros_like(s_ref)

        dtc, dt_x, phi_x, phi_r, tot_x, mask, _ = _scan_prelude(d, dt_ref, dtt_ref, al_ref, alt_ref, hg)
        lane_head = lax.broadcasted_iota(jnp.int32, (1, w), 1) >> int(math.log2(p))
        cm, bm = c_ref[...], b_ref[...]
        cb = lax.dot_general(cm, bm, NT, preferred_element_type=F32)
        xdt = x_ref[...].astype(F32) * dt_x
        xdt_b = xdt.astype(BF16)
        ydiag = jnp.zeros((ch, w), F32)
        for j in range(hg):
            seg = phi_x[:, j * p:j * p + 1] - phi_r[j:j + 1, :]
            mj = (cb * jnp.exp(jnp.where(mask, seg, NEG_BIG))).astype(BF16)
            ydiag = ydiag + jnp.dot(mj, jnp.where(lane_head == j, xdt_b, jnp.zeros_like(xdt_b)), preferred_element_type=F32)
        s = s_ref[...]
        y_ref[...] = ydiag + jnp.dot(cm, s.astype(BF16), preferred_element_type=F32) * jnp.exp(phi_x)
        sin_ref[...] = s
        wm = (xdt * jnp.exp(tot_x - phi_x)).astype(BF16)
        s_ref[...] = s * jnp.exp(tot_x) + lax.dot_general(bm, wm, TN, preferred_element_type=F32)

    specs = [_remap(s, order) for s in _scan_specs(d, nc, hg, dm)]
    own, brought = _carry_call(
        side, body, name=name, grid=(t // SEQ, SSD_GROUPS, nc), in_specs=specs,
        out_specs=[_remap(pl.BlockSpec((ch, w), lambda b, g, c: (b * nc + c, g)), order),
                   _remap(pl.BlockSpec((None, None, n, w), lambda b, g, c: (b * nc + c, g, 0, 0)), order)],
        out_shape=[jax.ShapeDtypeStruct((t, dm["HI"]), F32), jax.ShapeDtypeStruct((t // ch, SSD_GROUPS, n, w), F32)],
        scratch_shapes=[pltpu.VMEM((n, w), F32)], sem=("parallel", "parallel", "arbitrary"),
        args=(u, u, u, dtg, dttg, alg, altg))
    return own if side is None else (own, brought)


def _scan_bwd(u, dtg, dttg, alg, altg, dy, sin, adds, d, dm, name, side=None):
    t = u.shape[0]
    p, n, ch, hg = SSD_HEAD_DIM, SSD_STATE, SSD_CHUNK, dm["HG"]
    w = hg * p
    nc = SEQ // ch
    order = (lambda c: nc - 1 - c) if d == 0 else (lambda c: c)
    has_bc_add = adds[1] is not None

    def body(*refs):
        x_ref, b_ref, c_ref, dt_ref, dtt_ref, al_ref, alt_ref, dy_ref, sin_ref, ax_ref = refs[:10]
        pos = 10
        ab_ref = ac_ref = None
        if has_bc_add:
            ab_ref, ac_ref = refs[10], refs[11]
            pos = 12
        dxs_ref, db_ref, dc_ref, rq_ref, xdx_ref, ds_ref = refs[pos:]

        @pl.when(pl.program_id(2) == 0)
        def _():
            ds_ref[...] = jnp.zeros_like(ds_ref)

        dtc, dt_x, phi_x, phi_r, tot_x, mask, mask_t = _scan_prelude(d, dt_ref, dtt_ref, al_ref, alt_ref, hg)
        lane_head = lax.broadcasted_iota(jnp.int32, (1, w), 1) >> int(math.log2(p))
        cm, bm = c_ref[...], b_ref[...]
        cb = lax.dot_general(cm, bm, NT, preferred_element_type=F32)
        cb_t = lax.dot_general(bm, cm, NT, preferred_element_type=F32)
        xs = x_ref[...].astype(F32)
        xdt = xs * dt_x
        xdt_b = xdt.astype(BF16)
        dy = dy_ref[...]
        dy_b = dy.astype(BF16)
        zero_b = jnp.zeros_like(dy_b)
        col = lax.broadcasted_iota(jnp.int32, (1, hg), 1)
        dxp = jnp.zeros((ch, w), F32)
        a_ls = jnp.zeros((ch, ch), F32)
        a_sl = jnp.zeros((ch, ch), F32)
        dphi = jnp.zeros((ch, hg), F32)
        for j in range(hg):
            pc = phi_x[:, j * p:j * p + 1]
            pr = phi_r[j:j + 1, :]
            l_ls = jnp.exp(jnp.where(mask, pc - pr, NEG_BIG))
            l_sl = jnp.exp(jnp.where(mask_t, pr - pc, NEG_BIG))
            dy_j = jnp.where(lane_head == j, dy_b, zero_b)
            xdt_j = jnp.where(lane_head == j, xdt_b, zero_b)
            dxp = dxp + jnp.dot((cb_t * l_sl).astype(BF16), dy_j, preferred_element_type=F32)
            g_ls = l_ls * lax.dot_general(dy_j, xdt_b, NT, preferred_element_type=F32)
            g_sl = l_sl * lax.dot_general(xdt_j, dy_b, NT, preferred_element_type=F32)
            a_ls = a_ls + g_ls
            a_sl = a_sl + g_sl
            pair = jnp.sum(g_ls * cb, axis=1, keepdims=True) - jnp.sum(g_sl * cb_t, axis=1, keepdims=True)
            dphi = jnp.where(col == j, pair, dphi)
        ds = ds_ref[...]
        ds_b = ds.astype(BF16)
        sin = sin_ref[...]
        sin_b = sin.astype(BF16)
        e_tp = jnp.exp(tot_x - phi_x)
        e_p = jnp.exp(phi_x)
        dxp_off = e_tp * jnp.dot(bm, ds_b, preferred_element_type=F32)
        dxp = dxp + dxp_off
        dxs_ref[...] = ax_ref[...] + dxp * dt_x
        xdx_ref[...] = _head_sums(xs * dxp, hg, p)
        y_off = jnp.dot(cm, sin_b, preferred_element_type=F32) * e_p
        st_t = _head_sums(xdt * dxp_off, hg, p)
        dphi = dphi + _head_sums(dy * y_off, hg, p) - st_t
        dtot = _head_sums(jnp.sum(ds * sin, axis=0, keepdims=True) * jnp.exp(tot_x), hg, p) + jnp.sum(st_t, axis=0, keepdims=True)
        cum = jnp.dot(mask_t.astype(F32), _head_expand(dphi, hg, p), preferred_element_type=F32, precision=HIGHEST)
        ddta = jnp.zeros((ch, hg), F32)
        for j in range(hg):
            ddta = jnp.where(col == j, cum[:, j * p:j * p + 1], ddta)
        rq_ref[...] = ddta + dtot
        dye = (dy * e_p).astype(BF16)
        dcv = jnp.dot(a_ls.astype(BF16), bm, preferred_element_type=F32)
        dcv = dcv + lax.dot_general(dye, sin_b, NT, preferred_element_type=F32)
        dbv = jnp.dot(a_sl.astype(BF16), cm, preferred_element_type=F32)
        dbv = dbv + lax.dot_general((xdt * e_tp).astype(BF16), ds_b, NT, preferred_element_type=F32)
        if has_bc_add:
            dcv = dcv + ac_ref[...]
            dbv = dbv + ab_ref[...]
        dc_ref[...] = dcv
        db_ref[...] = dbv
        ds_ref[...] = ds * jnp.exp(tot_x) + lax.dot_general(cm, dye, TN, preferred_element_type=F32)

    def sp(spec):
        return _remap(spec, order)

    xw = pl.BlockSpec((ch, w), lambda b, g, c: (b * nc + c, g))
    gn_blk = pl.BlockSpec((ch, n), lambda b, g, c: (b * nc + c, g))
    small = pl.BlockSpec((None, ch, hg), lambda b, g, c: (g, b * nc + c, 0))
    in_specs = [sp(s) for s in _scan_specs(d, nc, hg, dm)]
    in_specs += [sp(xw), sp(pl.BlockSpec((None, None, n, w), lambda b, g, c: (b * nc + c, g, 0, 0))), sp(xw)]
    args = [u, u, u, dtg, dttg, alg, altg, dy, sin, adds[0]]
    if has_bc_add:
        in_specs += [sp(gn_blk), sp(gn_blk)]
        args += [adds[1], adds[2]]
    own, brought = _carry_call(
        side, body, name=name, grid=(t // SEQ, SSD_GROUPS, nc), in_specs=in_specs,
        out_specs=[sp(xw), sp(gn_blk), sp(gn_blk), sp(small), sp(small)],
        out_shape=[jax.ShapeDtypeStruct((t, dm["HI"]), F32), jax.ShapeDtypeStruct((t, dm["GN"]), F32),
                   jax.ShapeDtypeStruct((t, dm["GN"]), F32), jax.ShapeDtypeStruct((SSD_GROUPS, t, hg), F32),
                   jax.ShapeDtypeStruct((SSD_GROUPS, t, hg), F32)],
        scratch_shapes=[pltpu.VMEM((n, w), F32)], sem=("parallel", "parallel", "arbitrary"), args=args)
    return own if side is None else (own, brought)


def _ssd_param_bwd(rq_f, rq_r, xdx, dtp, dtr, bias, alog, name):
    def body(rf_ref, rr_ref, xdx_ref, dt_ref, dtr_ref, b_ref, al_ref, o_ref, db_ref, da_ref):
        a = -jnp.exp(al_ref[...])
        d_dta = rf_ref[...] + rr_ref[...]
        ddt = a * d_dta + xdx_ref[...]
        ddtr = ddt * _sigmoid(dtr_ref[...] + b_ref[...])
        o_ref[...] = ddtr
        db_ref[...] += jnp.sum(ddtr, axis=0, keepdims=True)
        da_ref[...] += a * jnp.sum(dt_ref[...] * d_dta, axis=0, keepdims=True)

    return _rowwise(body, [rq_f, rq_r, xdx, dtp, dtr], [bias, alog], [(LANES, F32)],
                    [((1, LANES), F32), ((1, LANES), F32)], tile=512, name=name)


def _ssd_out_fwd(y_f, y_b, u, proj, dcols, gn, hi, name):
    def body(yf_ref, yb_ref, x_ref, z_ref, d_ref, g_ref, o_ref):
        ytot = yf_ref[...] + yb_ref[...] + d_ref[...] * x_ref[...].astype(F32)
        zv = z_ref[...].astype(F32)
        yz = ytot * (zv * _sigmoid(zv))
        rstd = lax.rsqrt(jnp.mean(yz * yz, axis=-1, keepdims=True) + EPS)
        o_ref[...] = (yz * rstd * g_ref[...]).astype(BF16)

    return _rowwise(body, [y_f, y_b, (u, 0, hi), (proj, 0, hi)], [dcols, gn], [(hi, BF16)], tile=256, name=name)[0]


def _ssd_out_bwd(dya, y_f, y_b, u, proj, dcols, gn, hi, name):
    def body(dy_ref, yf_ref, yb_ref, x_ref, z_ref, d_ref, g_ref, dyt_ref, dxs_ref, dz_ref, dg_ref, dd_ref):
        xv = x_ref[...].astype(F32)
        ytot = yf_ref[...] + yb_ref[...] + d_ref[...] * xv
        zv = z_ref[...].astype(F32)
        sg = _sigmoid(zv)
        sz = zv * sg
        yz = ytot * sz
        rstd = lax.rsqrt(jnp.mean(yz * yz, axis=-1, keepdims=True) + EPS)
        yn = yz * rstd
        dv = dy_ref[...]
        dg_ref[...] += jnp.sum(dv * yn, axis=0, keepdims=True)
        dn = dv * g_ref[...]
        dyz = rstd * (dn - yn * jnp.mean(dn * yn, axis=-1, keepdims=True))
        dyt = dyz * sz
        dyt_ref[...] = dyt
        dxs_ref[...] = dyt * d_ref[...]
        dz_ref[...] = (dyz * ytot * (sg * (1.0 + zv * (1.0 - sg)))).astype(BF16)
        dd_ref[...] += jnp.sum(dyt * xv, axis=0, keepdims=True)

    return _rowwise(body, [dya, y_f, y_b, (u, 0, hi), (proj, 0, hi)], [dcols, gn],
                    [(hi, F32), (hi, F32), (hi, BF16)], [((1, hi), F32), ((1, hi), F32)], tile=128, name=name)


def _gate_fwd(pa, pb, pc, proj, off, d, name):
    def body(a_ref, b_ref, c_ref, g0_ref, g1_ref, g2_ref, o_ref):
        acc = _sigmoid(g0_ref[...].astype(F32)) * a_ref[...]
        acc = acc + _sigmoid(g1_ref[...].astype(F32)) * b_ref[...]
        acc = acc + _sigmoid(g2_ref[...].astype(F32)) * c_ref[...]
        o_ref[...] = acc.astype(BF16)

    rows = [pa, pb, pc] + [(proj, off + k * d, d) for k in range(3)]
    return _rowwise(body, rows, [], [(d, BF16)], tile=256, name=name)[0]


def _gate_bwd(dm_, pa, pb, pc, proj, off, d, name):
    def body(dm_ref, a_ref, b_ref, c_ref, g0_ref, g1_ref, g2_ref, da_ref, db_ref, dc_ref, dg0_ref, dg1_ref, dg2_ref):
        dmv = dm_ref[...]
        for p_ref, g_ref, dp_ref, dg_ref in ((a_ref, g0_ref, da_ref, dg0_ref), (b_ref, g1_ref, db_ref, dg1_ref),
                                             (c_ref, g2_ref, dc_ref, dg2_ref)):
            sg = _sigmoid(g_ref[...].astype(F32))
            dp_ref[...] = (dmv * sg).astype(BF16)
            dg_ref[...] = (dmv * p_ref[...] * sg * (1.0 - sg)).astype(BF16)

    rows = [dm_, pa, pb, pc] + [(proj, off + k * d, d) for k in range(3)]
    return _rowwise(body, rows, [], [(d, BF16)] * 6, tile=128, name=name)


def _adamw(w, g, m, v, name):
    nl, rows, cols = w.shape
    tile = _pick(rows, ROW_TILES)
    c1 = 1.0 / (1.0 - ADAM_B1 ** ADAM_STEP)
    c2 = 1.0 / (1.0 - ADAM_B2 ** ADAM_STEP)

    def body(w_ref, g_ref, m_ref, v_ref, d_ref, nm_ref, nv_ref):
        gv = g_ref[...]
        nm = ADAM_B1 * m_ref[...] + (1.0 - ADAM_B1) * gv
        nv = ADAM_B2 * v_ref[...] + (1.0 - ADAM_B2) * (gv * gv)
        nm_ref[...] = nm
        nv_ref[...] = nv
        d_ref[...] = -ADAM_LR * ((nm * c1) / (jnp.sqrt(nv * c2) + ADAM_EPS) + ADAM_WD * w_ref[...])

    blk = pl.BlockSpec((None, tile, cols), lambda l, i: (l, i, 0))
    return pl.pallas_call(
        body, name=name, grid=(nl, rows // tile), in_specs=[blk] * 4, out_specs=[blk] * 3,
        out_shape=[jax.ShapeDtypeStruct(w.shape, F32)] * 3, compiler_params=_params(("parallel", "parallel")),
    )(w, g, m, v)


ANY = pl.BlockSpec(memory_space=pl.ANY)


def _place():
    x, y, c = lax.axis_index("x"), lax.axis_index("y"), lax.axis_index("c")
    chips = [(1 - x, y), (x, 1 - y), (1 - x, 1 - y)]
    return x, y, c, chips


def _gather_copies(src, out, ssem, rsem, base):
    x, y, c, chips = _place()
    k = 2 * x + y

    def copy(j, kk, layer, to, own=False):
        return pltpu.make_async_remote_copy(
            src_ref=src.at[layer] if own else out.at[layer, kk], dst_ref=out.at[layer, kk],
            send_sem=ssem.at[base + j], recv_sem=rsem.at[base + j], device_id=to, device_id_type=MESH)

    first = [copy(j, k, c, (cx, cy, c), own=True) for j, (cx, cy) in enumerate(chips)]
    passed = [copy(3 + j, 2 * cx + cy, c, (x, y, 1 - c)) for j, (cx, cy) in enumerate(chips)]
    landed = [copy(j, 2 * cx + cy, c, (x, y, c)) for j, (cx, cy) in enumerate(chips)]
    handed = [copy(3 + j, 2 * cx + cy, 1 - c, (x, y, c)) for j, (cx, cy) in enumerate(chips)]
    return first, passed, landed, handed


def _gather_start(src, out, ssem, rsem, base):
    for cp in _gather_copies(src, out, ssem, rsem, base)[0]:
        cp.start()


def _gather_finish(src, out, ssem, rsem, base):
    first, passed, landed, handed = _gather_copies(src, out, ssem, rsem, base)
    for arrived, onward in zip(landed, passed):
        arrived.wait_recv()
        onward.start()
    for arrived in handed:
        arrived.wait_recv()
    for cp in first + passed:
        cp.wait_send()


def _gather_side(arrs):
    n = len(arrs)

    def start(srcs, outs, sems):
        for a in range(n):
            _gather_start(srcs[a], outs[a], sems[0], sems[1], 6 * a)

    def finish(srcs, outs, sems):
        for a in range(n):
            _gather_finish(srcs[a], outs[a], sems[0], sems[1], 6 * a)

    outs = [jax.ShapeDtypeStruct((a.shape[0], 4) + a.shape[1:], a.dtype) for a in arrs]
    return _Side(list(arrs), outs, [pltpu.SemaphoreType.DMA((6 * n,)), pltpu.SemaphoreType.DMA((6 * n,))], start, finish)


def _put_own(st, arr):
    kchip = 2 * lax.axis_index("x") + lax.axis_index("y")
    return lax.dynamic_update_slice(st, arr[:, None], (0, kchip) + (0,) * (arr.ndim - 1))


def _chip_copies(src, out, ssem, rsem, base=0):
    x, y, c, chips = _place()
    return [pltpu.make_async_remote_copy(src_ref=src.at[2 * cx + cy], dst_ref=out.at[j], send_sem=ssem.at[base + j],
                                         recv_sem=rsem.at[base + j], device_id=(cx, cy, c), device_id_type=MESH)
            for j, (cx, cy) in enumerate(chips)]


def _chips_side(parts):
    n = len(parts)

    def start(srcs, outs, sems):
        for a in range(n):
            for cp in _chip_copies(srcs[a], outs[a], sems[0], sems[1], 3 * a):
                cp.start()

    def finish(srcs, outs, sems):
        for a in range(n):
            for cp in _chip_copies(srcs[a], outs[a], sems[0], sems[1], 3 * a):
                cp.wait()

    return _Side(list(parts), [jax.ShapeDtypeStruct((3,) + p.shape[1:], p.dtype) for p in parts],
                 [pltpu.SemaphoreType.DMA((3 * n,)), pltpu.SemaphoreType.DMA((3 * n,))], start, finish)


def _gather_chips(arr, name):
    def body(src, out, ssem, rsem):
        _gather_start(src, out, ssem, rsem, 0)
        _gather_finish(src, out, ssem, rsem, 0)

    st = pl.pallas_call(
        body, name=name, in_specs=[ANY], out_specs=ANY,
        out_shape=jax.ShapeDtypeStruct((arr.shape[0], 4) + arr.shape[1:], arr.dtype),
        scratch_shapes=[pltpu.SemaphoreType.DMA((6,)), pltpu.SemaphoreType.DMA((6,))],
    )(arr)
    return _put_own(st, arr)


def _pair_swap(g0, g1, name):
    def body(src0, src1, out, ssem, rsem):
        x, y, c, _ = _place()

        def swap(src):
            cp = pltpu.make_async_remote_copy(src_ref=src, dst_ref=out, send_sem=ssem, recv_sem=rsem,
                                              device_id=(x, y, 1 - c), device_id_type=MESH)
            cp.start()
            cp.wait()

        @pl.when(c == 0)
        def _():
            swap(src1)

        @pl.when(c == 1)
        def _():
            swap(src0)

    return pl.pallas_call(
        body, name=name, in_specs=[ANY, ANY], out_specs=ANY, out_shape=jax.ShapeDtypeStruct(g0.shape, g0.dtype),
        scratch_shapes=[pltpu.SemaphoreType.DMA, pltpu.SemaphoreType.DMA],
    )(g0, g1)


def _chip_exchange(p, name):
    def body(src, out, ssem, rsem):
        cps = _chip_copies(src, out, ssem, rsem)
        for cp in cps:
            cp.start()
        for cp in cps:
            cp.wait()

    return pl.pallas_call(
        body, name=name, in_specs=[ANY], out_specs=ANY, out_shape=jax.ShapeDtypeStruct((3,) + p.shape[1:], p.dtype),
        scratch_shapes=[pltpu.SemaphoreType.DMA((3,)), pltpu.SemaphoreType.DMA((3,))],
    )(p)


def _pair_share(r, name):
    def body(src, out, ssem, rsem):
        x, y, c, _ = _place()
        cp = pltpu.make_async_remote_copy(src_ref=src, dst_ref=out, send_sem=ssem, recv_sem=rsem,
                                          device_id=(x, y, 1 - c), device_id_type=MESH)
        cp.start()
        cp.wait()

    theirs = pl.pallas_call(
        body, name=name, in_specs=[ANY], out_specs=ANY, out_shape=jax.ShapeDtypeStruct(r.shape, r.dtype),
        scratch_shapes=[pltpu.SemaphoreType.DMA, pltpu.SemaphoreType.DMA],
    )(r)
    first = lax.axis_index("c") == 0
    return jnp.stack([jnp.where(first, r, theirs), jnp.where(first, theirs, r)])


def _sum_pair(g0, g1, got, sel, name):
    four, rows, cols = g0.shape
    flat = four * rows
    tile = _pick(flat, ROW_TILES)

    def body(sel_ref, a0_ref, a1_ref, b_ref, o_ref):
        mine = jnp.where(sel_ref[0] == 0, a0_ref[...], a1_ref[...])
        o_ref[...] = (mine.astype(F32) + b_ref[...].astype(F32)).astype(BF16)

    def layer(l):
        return pl.BlockSpec((tile, cols), lambda i, s: (jnp.where(s[0] == l, i, 0), 0))

    blk = pl.BlockSpec((tile, cols), lambda i, s: (i, 0))
    return pl.pallas_call(
        body, name=name, out_shape=jax.ShapeDtypeStruct((flat, cols), BF16),
        grid_spec=pltpu.PrefetchScalarGridSpec(
            num_scalar_prefetch=1, grid=(flat // tile,), in_specs=[layer(0), layer(1), blk], out_specs=blk),
        compiler_params=_params(("arbitrary",)),
    )(sel, g0.reshape(flat, cols), g1.reshape(flat, cols), got.reshape(flat, cols)).reshape(four, rows, cols)


def _sum4(a, b, sel, name):
    _, rows, cols = a.shape
    tile = _pick(rows, ROW_TILES)

    def body(sel_ref, a_ref, b0_ref, b1_ref, b2_ref, o_ref):
        acc = a_ref[...].astype(F32) + b0_ref[...].astype(F32)
        acc = acc + b1_ref[...].astype(F32)
        o_ref[...] = acc + b2_ref[...].astype(F32)

    bspec = [pl.BlockSpec((None, tile, cols), lambda i, s, j=j: (j, i, 0)) for j in range(3)]
    return pl.pallas_call(
        body, name=name, out_shape=jax.ShapeDtypeStruct((rows, cols), F32),
        grid_spec=pltpu.PrefetchScalarGridSpec(
            num_scalar_prefetch=1, grid=(rows // tile,),
            in_specs=[pl.BlockSpec((None, tile, cols), lambda i, s: (s[0], i, 0))] + bspec,
            out_specs=pl.BlockSpec((tile, cols), lambda i, s: (i, 0))),
        compiler_params=_params(("parallel",)),
    )(sel, a, b, b, b)


def _reduce_scatter(g0, g1, name):
    part = _reduce_pair(g0, g1, name)
    return _reduce_chips(part, _chip_exchange(part, name + "_chips"), name)


def _reduce_pair(g0, g1, name):
    c = lax.axis_index("c").astype(jnp.int32).reshape(1)
    got = _pair_swap(g0, g1, name + "_pair")
    return _sum_pair(g0, g1, got, c, name + "_add2")


def _reduce_chips(part, others, name):
    k = (2 * lax.axis_index("x") + lax.axis_index("y")).astype(jnp.int32).reshape(1)
    total = _sum4(part, others, k, name + "_add4")
    return _pair_share(total, name + "_share")


class _EarlyReduce:
    def __init__(self, names):
        self.names, self.grads, self.part, self.others = names, None, {}, {}

    def early(self, n, g0):
        if n in self.names:
            shards = [g if n == "w_in" else _restack(n, g) for g in (g0, self.grads[1][n])]
            self.part[n] = _reduce_pair(*shards, "rs_" + n)

    def side(self, names):
        return _chips_side([self.part[n] for n in names])

    def took(self, names, brought):
        for n, o in zip(names, brought):
            self.others[n] = o

    def finish(self, n):
        return _reduce_chips(self.part[n], self.others[n], "rs_" + n)


def _all_reduce_small(buf, name):
    rows = buf.shape[0]

    def body(src, out, slots, ssem, rsem):
        x, y, c, _ = _place()
        me = 4 * x + 2 * y + c
        slots[me] = src[...]
        cps = []
        for j in range(1, 8):
            px, py, pc = x ^ (j >> 2), y ^ ((j >> 1) & 1), c ^ (j & 1)
            cps.append(pltpu.make_async_remote_copy(src_ref=src, dst_ref=slots.at[me], send_sem=ssem.at[j - 1],
                                                    recv_sem=rsem.at[j - 1], device_id=(px, py, pc), device_id_type=MESH))
        for cp in cps:
            cp.start()
        for j in range(1, 8):
            peer = me ^ j
            pltpu.make_async_remote_copy(src_ref=src, dst_ref=slots.at[peer], send_sem=ssem.at[j - 1], recv_sem=rsem.at[j - 1],
                                         device_id=(x, y, c), device_id_type=MESH).wait_recv()
        for cp in cps:
            cp.wait_send()
        acc = slots[0]
        for d in range(1, 8):
            acc = acc + slots[d]
        out[...] = acc

    vm = pl.BlockSpec(memory_space=pltpu.VMEM)
    return pl.pallas_call(
        body, name=name, in_specs=[vm], out_specs=vm, out_shape=jax.ShapeDtypeStruct((rows, LANES), F32),
        scratch_shapes=[pltpu.VMEM((8, rows, LANES), F32), pltpu.SemaphoreType.DMA((7,)), pltpu.SemaphoreType.DMA((7,))],
    )(buf)


def _pack(arrs):
    flat = jnp.concatenate([a.astype(F32).reshape(-1) for a in arrs])
    n = flat.shape[0]
    padded = -(-n // (8 * LANES)) * (8 * LANES)
    return jnp.pad(flat, (0, padded - n)).reshape(padded // LANES, LANES)


def _unpack(buf, like):
    flat = buf.reshape(-1)
    out, pos = [], 0
    for a in like:
        out.append(flat[pos:pos + a.size].reshape(a.shape))
        pos += a.size
    return out


def _stride(t2d, dil):
    t, w = t2d.shape
    b = t // SEQ
    return t2d.reshape(b, SEQ // dil, dil, w).transpose(0, 2, 1, 3).reshape(b * dil, SEQ // dil, w)


def _unstride(t3d, dil):
    bb, n, w = t3d.shape
    b = bb // dil
    return t3d.reshape(b, dil, n, w).transpose(0, 2, 1, 3).reshape(b * SEQ, w)


def _stat_cols(st, dil, heads):
    s3 = _stride(st, dil)
    return s3.transpose(0, 2, 1)[..., None]


def _stat_rows(col):
    bb, h, n, _ = col.shape
    return col.reshape(bb, h, 1, n)


def _scan_params(dtp, alog, dm):
    t = dtp.shape[0]
    g, hg = SSD_GROUPS, dm["HG"]
    dt4 = dtp[:, :dm["H2"]].reshape(t, 2, g, hg)
    dtg = dt4.transpose(1, 2, 0, 3)
    dttg = jnp.pad(dt4.transpose(1, 2, 3, 0), ((0, 0), (0, 0), (0, 8 - hg), (0, 0)))
    al = alog.reshape(2, g, 1, hg)
    alt = jnp.pad(alog.reshape(2, g, hg, 1), ((0, 0), (0, 0), (0, 8 - hg), (0, 0)))
    return dtg, dttg, al, alt


def _layer_fwd(x, wl, tabs, dm, li, late=None):
    d = dm["D"]
    nm = f"l{li}_"
    h = _rms_fwd(x, wl["g_mix"], nm + "rms1")
    late = late or {}
    if "proj" not in late:
        proj = _mm(h, wl["w_main"], tb=True, name=nm + "proj")
    else:
        proj, brought = _mm(h, wl["w_main"], tb=True, side=late["proj"][0], name=nm + "proj")
        wl.update(late["proj"][1](brought))
    dtr = _mm(h, wl["w_dt"], tb=True, out_dtype=F32, name=nm + "proj_dt")
    cpre, u = _conv_fwd(proj, dm["OFF_XBC"], wl["conv_w"], wl["conv_b"], dm["XBC"], nm + "conv")
    dtp = _dt_prep(dtr, wl["dt_bias"], nm + "dt")
    sp = _scan_params(dtp, wl["a_log"], dm)
    scans = []
    for dirn, key in ((0, "scan_f"), (1, "scan_b")):
        if key in late:
            own, brought = _scan_fwd(u, *sp, dirn, dm, nm + key, side=late[key][0])
            wl.update(late[key][1](brought))
        else:
            own = _scan_fwd(u, *sp, dirn, dm, nm + key)
        scans.append(own)
    (y_f, s_f), (y_b, s_b) = scans
    y_a = _ssd_out_fwd(y_f, y_b, u, proj, wl["d_cols"], wl["ssd_norm"], dm["HI"], nm + "ssd_out")
    qkv = _rope(proj, dm["OFF_QKV"], dm["QW"] // HEAD_DIM, tabs[0], dm, nm + "rope")
    ng, dw = dm["NG"], dm["DW"]
    outs, lses, xgs = [], [], []
    for gi, (window, dil) in enumerate(DIL_PATTERNS):
        cols = [qkv[:, s * ng * dw + gi * dw:s * ng * dw + (gi + 1) * dw] for s in range(3)]
        xg = _stride(jnp.concatenate(cols, axis=1), dil)
        o, lse = _attn_fwd(xg, xg, xg, 0, 1, 2, DIL_HEADS, 1, SEQ // dil, window // (2 * dil), None, F32, nm + f"dil{gi}")
        xgs.append(xg)
        outs.append(_unstride(o, dil))
        lses.append(_unstride(lse[..., 0].transpose(0, 2, 1), dil))
    y_bm, lse_tot = _dil_combine(outs, lses, nm + "dil_mix")
    bsz = x.shape[0] // SEQ
    xw = qkv[:, dm["QKVD"]:].reshape(bsz, SEQ, dm["WQ"] + 2 * dm["WK"])
    rep = WIN_Q_HEADS // WIN_KV_HEADS
    y_c3, lse_w = _attn_fwd(xw, xw, xw, 0, rep, rep + 1, WIN_Q_HEADS, rep, SEQ, WIN_HALF,
                            wl["sink"].reshape(WIN_Q_HEADS, 1, 1), BF16, nm + "win")
    y_c = y_c3.reshape(x.shape[0], dm["WQ"])
    pa = _mm(y_a, wl["w_a"], out_dtype=F32, name=nm + "pa")
    pb = _mm(y_bm, wl["w_b"], out_dtype=F32, name=nm + "pb")
    pc = _mm(y_c, wl["w_c"], out_dtype=F32, name=nm + "pc")
    merged = _gate_fwd(pa, pb, pc, proj, dm["OFF_GATE"], d, nm + "gate")
    x1 = _mm(merged, wl["w_out"], add=x, out_dtype=F32, name=nm + "out")
    hm = _rms_fwd(x1, wl["g_mlp"], nm + "rms2")
    up, act = _mm(hm, wl["w_up"], epi="relu2", name=nm + "up")
    x2 = _mm(act, wl["w_down"], add=x1, out_dtype=F32, name=nm + "down")
    saved = dict(x=x, h=h, proj=proj, dtr=dtr, cpre=cpre, u=u, dtp=dtp, y_f=y_f, y_b=y_b, s_f=s_f, s_b=s_b, y_a=y_a,
                 xw=xw, xgs=xgs, y_bm=y_bm, lse_tot=lse_tot, y_c=y_c, lse_w=lse_w, pa=pa, pb=pb, pc=pc,
                 merged=merged, x1=x1, hm=hm, up=up, act=act)
    return x2, saved


def _layer_bwd(dx2, wl, sv, tabs, dm, li, early=None):
    d = dm["D"]
    t = dx2.shape[0]
    bsz = t // SEQ
    nm = f"l{li}b_"
    gr = {}
    dup = _mm(dx2, wl["w_down"], tb=True, aux=sv["up"], epi="relu2_bwd", name=nm + "dup")
    gr["w_down"] = _mm(sv["act"], dx2, ta=True, name=nm + "gw_down")
    dhm = _mm(dup, wl["w_up"], tb=True, out_dtype=F32, name=nm + "dhm")
    gr["w_up"] = _mm(sv["hm"], dup, ta=True, name=nm + "gw_up")
    if early is not None:
        early.early("w_down", gr["w_down"])
        early.early("w_up", gr["w_up"])
    dx1, gmlp = _rms_bwd(sv["x1"], wl["g_mlp"], dhm, dx2, nm + "rms2")
    gr["g_mlp"] = gmlp[0]
    dmerged = _mm(dx1, wl["w_out"], tb=True, out_dtype=F32, name=nm + "dmerged")
    gr["w_out"] = _mm(sv["merged"], dx1, ta=True, name=nm + "gw_out")
    dpa, dpb, dpc, dg0, dg1, dg2 = _gate_bwd(dmerged, sv["pa"], sv["pb"], sv["pc"], sv["proj"], dm["OFF_GATE"], d, nm + "gate")
    dya = _mm(dpa, wl["w_a"], tb=True, out_dtype=F32, name=nm + "dya")
    gr["w_a"] = _mm(sv["y_a"], dpa, ta=True, name=nm + "gw_a")
    dyb = _mm(dpb, wl["w_b"], tb=True, out_dtype=F32, name=nm + "dyb")
    gr["w_b"] = _mm(sv["y_bm"], dpb, ta=True, name=nm + "gw_b")
    dyc = _mm(dpc, wl["w_c"], tb=True, out_dtype=F32, name=nm + "dyc")
    gr["w_c"] = _mm(sv["y_c"], dpc, ta=True, name=nm + "gw_c")
    if early is not None:
        for n in ("w_out", "w_a", "w_b", "w_c"):
            early.early(n, gr[n])
    ng, dw = dm["NG"], dm["DW"]
    xw = sv["xw"]
    rep = WIN_Q_HEADS // WIN_KV_HEADS
    delta_w = _head_dots(dyc, sv["y_c"], WIN_Q_HEADS, nm + "win_delta")
    dl_col = _stat_cols(delta_w, 1, WIN_Q_HEADS)
    lse_w = sv["lse_w"]
    dyc3 = dyc.reshape(bsz, SEQ, dm["WQ"])
    wargs = (0, rep, rep + 1, WIN_Q_HEADS, rep, SEQ, WIN_HALF)
    dq_w = _attn_dq(xw, xw, xw, dyc3, lse_w, dl_col, *wargs, nm + "win_dq")
    dk_w, dv_w = _attn_dkv(xw, xw, xw, dyc3, _stat_rows(lse_w), _stat_rows(dl_col), *wargs, nm + "win_dkv")
    lse_w2 = lse_w[..., 0].transpose(0, 2, 1).reshape(t, WIN_Q_HEADS)
    gr["sink"] = _sink_grad(lse_w2, delta_w, wl["sink"], nm + "sink")[0]
    delta_d = _head_dots(dyb, sv["y_bm"], DIL_HEADS, nm + "dil_delta")
    dqs, dks, dvs = [], [], []
    for gi, (window, dil) in enumerate(DIL_PATTERNS):
        xg = sv["xgs"][gi]
        n = SEQ // dil
        do_g = _stride(dyb, dil)
        lse_c = _stat_cols(sv["lse_tot"], dil, DIL_HEADS)
        dl_c = _stat_cols(delta_d, dil, DIL_HEADS)
        dargs = (0, 1, 2, DIL_HEADS, 1, n, window // (2 * dil))
        dq = _attn_dq(xg, xg, xg, do_g, lse_c, dl_c, *dargs, nm + f"dil{gi}_dq")
        dk, dv = _attn_dkv(xg, xg, xg, do_g, _stat_rows(lse_c), _stat_rows(dl_c), *dargs, nm + f"dil{gi}_dkv")
        dqs.append(_unstride(dq, dil))
        dks.append(_unstride(dk, dil))
        dvs.append(_unstride(dv, dil))
    dqkv_r = jnp.concatenate(dqs + dks + dvs + [dq_w.reshape(t, dm["WQ"]), dk_w.reshape(t, dm["WK"]), dv_w.reshape(t, dm["WK"])],
                             axis=1)
    dqkv = _rope(dqkv_r, 0, dm["QW"] // HEAD_DIM, tabs[1], dm, nm + "rope")
    hi, gn = dm["HI"], dm["GN"]
    dyt, dxs0, dz, gnorm, dd_cols = _ssd_out_bwd(dya, sv["y_f"], sv["y_b"], sv["u"], sv["proj"], wl["d_cols"],
                                                          wl["ssd_norm"], hi, nm + "ssd_out")
    gr["ssd_norm"] = gnorm[0]
    gr["d_skip"] = dd_cols.reshape(SSD_HEADS, SSD_HEAD_DIM).sum(axis=1)
    sp = _scan_params(sv["dtp"], wl["a_log"], dm)
    if early is None:
        dxs1, db1, dc1, rq_f, xdx_f = _scan_bwd(sv["u"], *sp, dyt, sv["s_f"], (dxs0, None, None), 0, dm, nm + "scan_f")
        dxs2, db2, dc2, rq_r, xdx_r = _scan_bwd(sv["u"], *sp, dyt, sv["s_b"], (dxs1, db1, dc1), 1, dm, nm + "scan_b")
    else:
        ride_f, ride_b = ("w_up", "w_out", "w_a"), ("w_down", "w_b", "w_c")
        (dxs1, db1, dc1, rq_f, xdx_f), got = _scan_bwd(sv["u"], *sp, dyt, sv["s_f"], (dxs0, None, None), 0, dm, nm + "scan_f",
                                                       side=early.side(ride_f))
        early.took(ride_f, got)
        (dxs2, db2, dc2, rq_r, xdx_r), got = _scan_bwd(sv["u"], *sp, dyt, sv["s_b"], (dxs1, db1, dc1), 1, dm, nm + "scan_b",
                                                       side=early.side(ride_b))
        early.took(ride_b, got)

    def heads(a):
        return a.transpose(1, 0, 2).reshape(t, SSD_HEADS)

    zpad = jnp.zeros((t, LANES - dm["H2"]), F32)
    zh = jnp.zeros((t, SSD_HEADS), F32)
    rqf_p = jnp.concatenate([heads(rq_f), zh, zpad], axis=1)
    rqr_p = jnp.concatenate([zh, heads(rq_r), zpad], axis=1)
    xdx_p = jnp.concatenate([heads(xdx_f), heads(xdx_r), zpad], axis=1)
    ddtr, dbias, dalog = _ssd_param_bwd(rqf_p, rqr_p, xdx_p, sv["dtp"], sv["dtr"], wl["dt_bias"], wl["a_log_p"], nm + "ssd_par")
    gr["dt_bias"] = dbias[0, :dm["H2"]].reshape(2, SSD_HEADS)
    gr["a_log"] = dalog[0, :dm["H2"]].reshape(2, SSD_HEADS)
    du = jnp.concatenate([dxs2, db2, dc2], axis=1)
    dxbc, gr["conv_w"], gcb = _conv_bwd(du, sv["cpre"], sv["proj"], dm["OFF_XBC"], wl["conv_w"], nm + "conv")
    gr["conv_b"] = gcb[0]
    dproj = jnp.concatenate([dz, dxbc, dqkv, dg0, dg1, dg2], axis=1)
    dh_dt = _mm(ddtr, wl["w_dt"], out_dtype=F32, name=nm + "dh_dt")
    gw_main = _mm(dproj, sv["h"], ta=True, name=nm + "gw_main")
    gw_dt = _mm(ddtr, sv["h"], ta=True, name=nm + "gw_dt")
    o1, h2 = dm["OFF_QKV"], dm["H2"]
    gw_in_t = jnp.concatenate([gw_main[:o1], gw_dt[:h2], gw_main[o1:]], axis=0)
    gr["w_in"] = gw_in_t.reshape(4, (dm["NM"] + h2) // 4, d)
    if early is None:
        dh = _mm(dproj, wl["w_main"], add=dh_dt, out_dtype=F32, name=nm + "dh")
    else:
        early.early("w_in", gr["w_in"])
        dh, got = _mm(dproj, wl["w_main"], add=dh_dt, out_dtype=F32, side=early.side(("w_in",)), name=nm + "dh")
        early.took(("w_in",), got)
    dx, gmix = _rms_bwd(sv["x"], wl["g_mix"], dh, dx1, nm + "rms1")
    gr["g_mix"] = gmix[0]
    return dx, gr


def _layer_weights(full, li, dm):
    st = full["w_in"]
    o1 = dm["OFF_QKV"]
    h2 = dm["H2"]
    d = dm["D"]
    w_in_t = st[li].reshape(4 * st.shape[2], d)
    wl = dict(
        w_main=jnp.concatenate([w_in_t[:o1], w_in_t[o1 + h2:]], axis=0),
        w_dt=jnp.pad(w_in_t[o1:o1 + h2], ((0, LANES - h2), (0, 0))),
        **{n: (full[n][li] if n in full else None) for n in ("w_a", "w_b", "w_c", "w_out", "w_up", "w_down")},
        conv_w=full["conv_w"][li], conv_b=full["conv_b"][li][None, :],
        g_mix=full["g_mix"][li][None, :], g_mlp=full["g_mlp"][li][None, :], ssd_norm=full["ssd_norm"][li][None, :],
        d_cols=jnp.repeat(full["d_skip"][li], SSD_HEAD_DIM)[None, :],
        sink=full["sink"][li][None, :],
        a_log=full["a_log"][li],
        a_log_p=jnp.pad(full["a_log"][li].reshape(1, h2), ((0, 0), (0, LANES - h2))),
        dt_bias=jnp.pad(full["dt_bias"][li].reshape(1, h2), ((0, 0), (0, LANES - h2))),
    )
    assert wl["w_main"].shape == (dm["NM"], d)
    return wl


def _local_step(x, target, full, depth, late=None, early=None):
    bsz, seq, d = x.shape
    assert seq == SEQ
    dm = _dims(d)
    assert dm["OFF_GATE"] % d == 0 and dm["HI"] % (dm["HG"] * SSD_HEAD_DIM) == 0 and dm["H2"] <= LANES
    tabs = (_rope_tables(1.0), _rope_tables(-1.0))
    xt = x.reshape(bsz * seq, d)
    wls, saves = [], []
    for li in range(depth):
        wl = _layer_weights(full, li, dm)
        hosts = {}
        if li == 0 and late is not None:
            for key, (side, arrived) in late.items():
                hosts[key] = (side, lambda brought, arrived=arrived: {n: full[n][0] for n in arrived(brought)})
        xt, sv = _layer_fwd(xt, wl, tabs, dm, li, late=hosts)
        wls.append(wl)
        saves.append(sv)
    dx, loss, g_final = _loss_head(xt, full["g_final"][None, :], target.reshape(bsz * seq, d), "loss_head")
    grads = [None] * depth
    if early is not None:
        early.grads = grads
    for li in reversed(range(depth)):
        dx, grads[li] = _layer_bwd(dx, wls[li], saves[li], tabs, dm, li, early=early if li == 0 else None)
    return loss, dx.reshape(bsz, seq, d), grads, g_final[0]


BIG = ("w_in", "w_a", "w_b", "w_c", "w_out", "w_up", "w_down")
COL_SHARDED = ("w_in", "w_b", "w_up")
SMALL = ("g_mix", "conv_w", "conv_b", "dt_bias", "a_log", "d_skip", "ssd_norm", "sink", "g_mlp", "g_final")
ORDER = ("g_mix", "w_in", "conv_w", "conv_b", "dt_bias", "a_log", "d_skip", "ssd_norm", "w_a", "w_b", "w_c", "sink",
         "w_out", "g_mlp", "w_up", "w_down", "g_final")


def _unstack(name, st):
    nl, _, r, c = st.shape
    if name in COL_SHARDED:
        return jnp.moveaxis(st, 1, 2).reshape(nl, r, 4 * c)
    return st.reshape(nl, 4 * r, c)


def _restack(name, gfull):
    r, c = gfull.shape
    if name in COL_SHARDED:
        return jnp.moveaxis(gfull.reshape(r, 4, c // 4), 1, 0)
    return gfull.reshape(4, r // 4, c)


def kernel(x, g_mix, w_in, conv_w, conv_b, dt_bias, a_log, d_skip, ssd_norm, w_a, w_b, w_c, sink, w_out, g_mlp, w_up, w_down, g_final, loss_target, m_g_mix, m_w_in, m_conv_w, m_conv_b, m_dt_bias, m_a_log, m_d_skip, m_ssd_norm, m_w_a, m_w_b, m_w_c, m_sink, m_w_out, m_g_mlp, m_w_up, m_w_down, m_g_final, v_g_mix, v_w_in, v_conv_w, v_conv_b, v_dt_bias, v_a_log, v_d_skip, v_ssd_norm, v_w_a, v_w_b, v_w_c, v_sink, v_w_out, v_g_mlp, v_w_up, v_w_down, v_g_final):
    w = dict(g_mix=g_mix, w_in=w_in, conv_w=conv_w, conv_b=conv_b, dt_bias=dt_bias, a_log=a_log, d_skip=d_skip,
             ssd_norm=ssd_norm, w_a=w_a, w_b=w_b, w_c=w_c, sink=sink, w_out=w_out, g_mlp=g_mlp, w_up=w_up, w_down=w_down,
             g_final=g_final)
    m = dict(g_mix=m_g_mix, w_in=m_w_in, conv_w=m_conv_w, conv_b=m_conv_b, dt_bias=m_dt_bias, a_log=m_a_log,
             d_skip=m_d_skip, ssd_norm=m_ssd_norm, w_a=m_w_a, w_b=m_w_b, w_c=m_w_c, sink=m_sink, w_out=m_w_out,
             g_mlp=m_g_mlp, w_up=m_w_up, w_down=m_w_down, g_final=m_g_final)
    v = dict(g_mix=v_g_mix, w_in=v_w_in, conv_w=v_conv_w, conv_b=v_conv_b, dt_bias=v_dt_bias, a_log=v_a_log,
             d_skip=v_d_skip, ssd_norm=v_ssd_norm, w_a=v_w_a, w_b=v_w_b, w_c=v_w_c, sink=v_sink, w_out=v_w_out,
             g_mlp=v_g_mlp, w_up=v_w_up, w_down=v_w_down, g_final=v_g_final)
    depth = w_in.shape[0]
    assert depth == 2
    kchip = 2 * lax.axis_index("x") + lax.axis_index("y")

    full = {n: w[n] for n in SMALL if n != "conv_w"}
    tr = lambda a: jnp.swapaxes(a, 1, 2)
    full["w_in"] = _gather_chips(tr(w_in).astype(BF16), "gather_w_in")
    cw = _gather_chips(conv_w, "gather_conv_w")
    full["conv_w"] = jnp.moveaxis(cw, 1, 2).reshape(depth, CONV_WIDTH, 4 * conv_w.shape[2])
    riders = {"proj": ("w_up",), "scan_f": ("w_down",), "scan_b": ("w_a", "w_b", "w_c", "w_out")}
    late = {}
    for key, names in riders.items():
        shards = [w[n].astype(BF16) for n in names]

        def arrived(brought, names=names, shards=shards):
            for n, shard, st in zip(names, shards, brought):
                full[n] = _unstack(n, _put_own(st, shard))
            return names

        late[key] = (_gather_side(shards), arrived)

    early = _EarlyReduce(BIG)
    loss_part, grad_x, grads, gg_final = _local_step(x, loss_target, full, depth, late=late, early=early)
    gsh = {n: early.finish(n) for n in BIG}
    small_names = [n for n in SMALL if n != "g_final"]
    small_g = [jnp.stack([grads[li][n] for li in range(depth)]) for n in small_names] + [gg_final, loss_part[0, :1]]
    red = _unpack(_all_reduce_small(_pack(small_g), "allreduce_small"), small_g)
    for n, a in zip(small_names + ["g_final"], red):
        gsh[n] = a
    loss = red[-1][0]
    cshard = conv_w.shape[2]
    gsh["conv_w"] = lax.dynamic_slice_in_dim(gsh["conv_w"], kchip * cshard, cshard, axis=2)

    delta, new_m, new_v = {}, {}, {}
    for n in BIG:
        if n == "w_in":
            outs_t = _adamw(tr(w[n]), gsh[n], tr(m[n]), tr(v[n]), "adamw_" + n)
            delta[n], new_m[n], new_v[n] = [tr(o) for o in outs_t]
            gsh[n] = tr(gsh[n])
        else:
            delta[n], new_m[n], new_v[n] = _adamw(w[n], gsh[n], m[n], v[n], "adamw_" + n)
    sm = list(SMALL)
    packed = [_pack([d_[n] for n in sm])[None] for d_ in (w, gsh, m, v)]
    outs = [o[0] for o in _adamw(*packed, "adamw_small")]
    for dst, buf in zip((delta, new_m, new_v), outs):
        for n, a in zip(sm, _unpack(buf, [w[n] for n in sm])):
            dst[n] = a
    return (loss, grad_x, *[gsh[n] for n in ORDER], *[delta[n] for n in ORDER], *[new_m[n] for n in ORDER],
            *[new_v[n] for n in ORDER])
```

```python
import functools
import math

import jax
import jax.numpy as jnp
from jax import lax
from jax.experimental import pallas as pl
from jax.experimental.pallas import tpu as pltpu

F32 = jnp.float32
BF16 = jnp.bfloat16

SEQ = 2048
SSD_HEADS = 32
SSD_HEAD_DIM = 64
SSD_GROUPS = 8
SSD_STATE = 128
SSD_CHUNK = 128
CONV_WIDTH = 5
HEAD_DIM = 128
ROPE_DIM = 32
ROPE_THETA = 500000.0
DIL_PATTERNS = ((128, 1), (512, 4), (2048, 16))
DIL_HEADS = 8
WIN_Q_HEADS = 16
WIN_KV_HEADS = 4
WIN_HALF = 128
EPS = 1e-6
NEG_BIG = -1e30
ADAM_LR = 0.001
ADAM_B1 = 0.9
ADAM_B2 = 0.999
ADAM_EPS = 1e-08
ADAM_WD = 0.01
ADAM_STEP = 10

LANES = 128
ATT_BLK = 128
ROW_TILES = (320, 256, 128, 80, 64, 32, 16, 8)
VMEM_LIMIT = 48 * 1024 * 1024
MESH = pl.DeviceIdType.MESH
HIGHEST = lax.Precision.HIGHEST
NT = (((1,), (1,)), ((), ()))
TN = (((0,), (0,)), ((), ()))
NN = (((1,), (0,)), ((), ()))


def _dims(d_model):
    hi = SSD_HEADS * SSD_HEAD_DIM
    gn = SSD_GROUPS * SSD_STATE
    ng = len(DIL_PATTERNS)
    dw = DIL_HEADS * HEAD_DIM
    wq = WIN_Q_HEADS * HEAD_DIM
    wk = WIN_KV_HEADS * HEAD_DIM
    d = dict(D=d_model, HI=hi, GN=gn, XBC=hi + 2 * gn, H2=2 * SSD_HEADS, NG=ng, DW=dw, WQ=wq, WK=wk,
             QKVD=3 * ng * dw, QW=3 * ng * dw + wq + 2 * wk, HG=SSD_HEADS // SSD_GROUPS)
    d["OFF_XBC"] = hi
    d["OFF_QKV"] = hi + d["XBC"]
    d["OFF_GATE"] = d["OFF_QKV"] + d["QW"]
    d["NM"] = d["OFF_GATE"] + 3 * d_model
    return d


def _pick(n, prefs):
    for p in prefs:
        if n % p == 0:
            return p
    return n


def _params(sem):
    return pltpu.CompilerParams(dimension_semantics=sem, vmem_limit_bytes=VMEM_LIMIT)


def _sigmoid(x):
    return 1.0 / (1.0 + jnp.exp(-x))


class _Side:
    def __init__(self, args, outs, sems, start, finish):
        self.args, self.outs, self.sems, self.start, self.finish = args, outs, sems, start, finish


def _carry(side, body, n_in, n_out, n_scratch, grid):
    if side is None:
        return body
    n_sin, n_sout = len(side.args), len(side.outs)

    def wrapped(*refs):
        o0 = n_in + n_sin
        s0 = o0 + n_out + n_sout
        s_in, s_out, s_sem = refs[n_in:o0], refs[o0 + n_out:s0], refs[s0 + n_scratch:]
        ids = [pl.program_id(ax) for ax in range(len(grid))]
        first = functools.reduce(jnp.logical_and, [i == 0 for i in ids])
        last = functools.reduce(jnp.logical_and, [i == g - 1 for i, g in zip(ids, grid)])

        @pl.when(first)
        def _():
            side.start(s_in, s_out, s_sem)

        body(*refs[:n_in], *refs[o0:o0 + n_out], *refs[s0:s0 + n_scratch])

        @pl.when(last)
        def _():
            side.finish(s_in, s_out, s_sem)

    return wrapped


def _carry_call(side, body, *, name, grid, in_specs, out_specs, out_shape, scratch_shapes, sem, args):
    n_in, n_out = len(in_specs), len(out_specs)
    in_specs, out_specs, out_shape, scratch_shapes, args = (list(v) for v in (in_specs, out_specs, out_shape,
                                                                              scratch_shapes, args))
    body = _carry(side, body, n_in, n_out, len(scratch_shapes), grid)
    if side is not None:
        any_spec = pl.BlockSpec(memory_space=pl.ANY)
        in_specs += [any_spec] * len(side.args)
        args += list(side.args)
        out_specs += [any_spec] * len(side.outs)
        out_shape += list(side.outs)
        scratch_shapes += list(side.sems)
        sem = ("arbitrary",) * len(grid)
    res = pl.pallas_call(body, name=name, grid=grid, in_specs=in_specs, out_specs=out_specs, out_shape=out_shape,
                         scratch_shapes=scratch_shapes, compiler_params=_params(sem))(*args)
    return list(res[:n_out]), list(res[n_out:])


def _mm(a, b, *, ta=False, tb=False, add=None, aux=None, epi=None, rope=None, out_dtype=BF16, side=None, name):
    if ta:
        kdim, m = a.shape
    else:
        m, kdim = a.shape
    if tb:
        n, k2 = b.shape
    else:
        k2, n = b.shape
    assert kdim == k2, (a.shape, b.shape, ta, tb)
    tm = _pick(m, (1024, 512, 256, 128, 64, 32, 16, 8))
    tn = _pick(n, (1024, 512, 256, 128))
    tk = _pick(kdim, (2048, 1024, 512, 256, 128))
    nk = kdim // tk
    dims = (((0 if ta else 1,), (1 if tb else 0,)), ((), ()))
    n_in = 2 + (add is not None) + (aux is not None) + 3 * (rope is not None)
    n_out = 2 if epi == "relu2" else 1
    n_sin = len(side.args) if side else 0
    n_sout = len(side.outs) if side else 0
    grid = (m // tm, n // tn, nk)
    if rope is not None:
        assert epi is None and add is None and SEQ % tm == 0 and tn % HEAD_DIM == 0

    def body(*refs):
        a_ref, b_ref = refs[0], refs[1]
        pos = 2
        add_ref = aux_ref = None
        if add is not None:
            add_ref = refs[pos]
            pos += 1
        if aux is not None:
            aux_ref = refs[pos]
            pos += 1
        if rope is not None:
            tab_refs = refs[pos:pos + 3]
        out_refs = refs[n_in + n_sin:n_in + n_sin + n_out]
        acc_ref = refs[n_in + n_sin + n_out + n_sout]
        k = pl.program_id(2)
        if side is not None:
            s_in = refs[n_in:n_in + n_sin]
            s_out = refs[n_in + n_sin + n_out:n_in + n_sin + n_out + n_sout]
            s_sem = refs[n_in + n_sin + n_out + n_sout + 1:]
            i, j = pl.program_id(0), pl.program_id(1)

            @pl.when((i == 0) & (j == 0) & (k == 0))
            def _():
                side.start(s_in, s_out, s_sem)

        @pl.when(k == 0)
        def _():
            acc_ref[...] = jnp.zeros_like(acc_ref)

        acc_ref[...] += lax.dot_general(a_ref[...].astype(BF16), b_ref[...].astype(BF16), dims,
                                        preferred_element_type=F32)

        jj = pl.program_id(1)

        def rotate():
            half = ROPE_DIM // 2
            nhb = tn // HEAD_DIM
            cv, uv, dv = (r[...] for r in tab_refs)
            for hb in range(nhb):
                blk = slice(hb * HEAD_DIM, (hb + 1) * HEAD_DIM)
                rh = acc_ref[:, blk]
                rot = rh * cv + pltpu.roll(rh, HEAD_DIM - half, 1) * uv + pltpu.roll(rh, half, 1) * dv
                g = jj * nhb + hb
                out_refs[0][:, blk] = jnp.where((g >= rope[1]) & (g < rope[2]), rot, rh).astype(out_refs[0].dtype)

        @pl.when(k == nk - 1)
        def _():
            r = acc_ref[...]
            if add_ref is not None:
                r = r + add_ref[...].astype(F32)
            if rope is not None:
                nhb = tn // HEAD_DIM
                touched = (jj * nhb < rope[2]) & ((jj + 1) * nhb > rope[1])
                pl.when(touched)(rotate)

                @pl.when(jnp.logical_not(touched))
                def _():
                    out_refs[0][...] = r.astype(out_refs[0].dtype)
            elif epi == "relu2":
                out_refs[0][...] = r.astype(out_refs[0].dtype)
                out_refs[1][...] = jnp.square(jnp.maximum(r, 0.0)).astype(out_refs[1].dtype)
            elif epi == "relu2_bwd":
                out_refs[0][...] = (r * 2.0 * jnp.maximum(aux_ref[...].astype(F32), 0.0)).astype(out_refs[0].dtype)
            else:
                out_refs[0][...] = r.astype(out_refs[0].dtype)

        if side is not None:
            @pl.when((i == grid[0] - 1) & (j == grid[1] - 1) & (k == nk - 1))
            def _():
                side.finish(s_in, s_out, s_sem)

    a_spec = pl.BlockSpec((tk, tm), lambda i, j, k: (k, i)) if ta else pl.BlockSpec((tm, tk), lambda i, j, k: (i, k))
    b_spec = pl.BlockSpec((tn, tk), lambda i, j, k: (j, k)) if tb else pl.BlockSpec((tk, tn), lambda i, j, k: (k, j))
    o_spec = pl.BlockSpec((tm, tn), lambda i, j, k: (i, j))
    in_specs = [a_spec, b_spec]
    args = [a, b]
    if add is not None:
        in_specs.append(o_spec)
        args.append(add)
    if aux is not None:
        in_specs.append(o_spec)
        args.append(aux)
    if rope is not None:
        in_specs += [pl.BlockSpec((tm, HEAD_DIM), lambda i, j, k: (i % (SEQ // tm), 0))] * 3
        args += list(rope[0])
    out_shape = [jax.ShapeDtypeStruct((m, n), out_dtype)] * n_out
    out_specs = [o_spec] * n_out
    scratch = [pltpu.VMEM((tm, tn), F32)]
    sem = ("parallel", "parallel", "arbitrary")
    if side is not None:
        any_spec = pl.BlockSpec(memory_space=pl.ANY)
        in_specs += [any_spec] * n_sin
        args += list(side.args)
        out_shape += list(side.outs)
        out_specs += [any_spec] * n_sout
        scratch += list(side.sems)
        sem = ("arbitrary", "arbitrary", "arbitrary")
    res = pl.pallas_call(
        body, name=name, grid=grid, in_specs=in_specs, out_specs=out_specs, out_shape=out_shape, scratch_shapes=scratch,
        compiler_params=_params(sem),
    )(*args)
    if side is not None:
        return (res[0] if n_out == 1 else tuple(res[:n_out])), list(res[n_out:])
    return res if n_out == 2 else res[0]


def _rowwise(body, rows, fulls, outs, accs=(), *, tile, name):
    rows = [r if isinstance(r, tuple) else (r, 0, r.shape[1]) for r in rows]
    nrows = rows[0][0].shape[0]
    assert nrows % tile == 0, (nrows, tile)
    in_specs, args = [], []
    for arr, off, width in rows:
        assert arr.shape[0] == nrows and off % width == 0, (arr.shape, off, width)
        in_specs.append(pl.BlockSpec((tile, width), lambda i, o=off // width: (i, o)))
        args.append(arr)
    for arr in fulls:
        in_specs.append(pl.BlockSpec(arr.shape, lambda i, nd=arr.ndim: (0,) * nd))
        args.append(arr)
    out_specs, out_shape = [], []
    for cols, dt in outs:
        out_specs.append(pl.BlockSpec((tile, cols), lambda i: (i, 0)))
        out_shape.append(jax.ShapeDtypeStruct((nrows, cols), dt))
    for shp, dt in accs:
        out_specs.append(pl.BlockSpec(shp, lambda i, nd=len(shp): (0,) * nd))
        out_shape.append(jax.ShapeDtypeStruct(shp, dt))
    n_in, n_out = len(args), len(outs)

    def wrapped(*refs):
        acc_refs = refs[n_in + n_out:]
        if acc_refs:
            @pl.when(pl.program_id(0) == 0)
            def _():
                for r in acc_refs:
                    r[...] = jnp.zeros_like(r)
        body(*refs)

    return pl.pallas_call(
        wrapped, name=name, grid=(nrows // tile,), in_specs=in_specs, out_specs=out_specs, out_shape=out_shape,
        compiler_params=_params(("arbitrary",)),
    )(*args)


def _rms_fwd(x, g, name):
    def body(x_ref, g_ref, h_ref):
        xv = x_ref[...]
        rstd = lax.rsqrt(jnp.mean(xv * xv, axis=-1, keepdims=True) + EPS)
        h_ref[...] = (xv * rstd * g_ref[...]).astype(BF16)

    return _rowwise(body, [x], [g], [(x.shape[1], BF16)], tile=256, name=name)[0]


def _rms_bwd(x, g, dh, dres, name):
    def body(x_ref, dh_ref, dres_ref, g_ref, dx_ref, dg_ref):
        xv = x_ref[...]
        dv = dh_ref[...]
        rstd = lax.rsqrt(jnp.mean(xv * xv, axis=-1, keepdims=True) + EPS)
        xn = xv * rstd
        dg_ref[...] += jnp.sum(dv * xn, axis=0, keepdims=True)
        dn = dv * g_ref[...]
        dx_ref[...] = dres_ref[...] + rstd * (dn - xn * jnp.mean(dn * xn, axis=-1, keepdims=True))

    d = x.shape[1]
    return _rowwise(body, [x, dh, dres], [g], [(d, F32)], [((1, d), F32)], tile=256, name=name)


def _loss_head(x, g, target, name):
    d = x.shape[1]

    def body(x_ref, t_ref, g_ref, dx_ref, loss_ref, dg_ref):
        xv = x_ref[...]
        rstd = lax.rsqrt(jnp.mean(xv * xv, axis=-1, keepdims=True) + EPS)
        xn = xv * rstd
        err = xn * g_ref[...] - t_ref[...]
        loss_ref[...] += jnp.full((1, LANES), 0.5 / d, F32) * jnp.sum(err * err)
        dy = err * (1.0 / d)
        dg_ref[...] += jnp.sum(dy * xn, axis=0, keepdims=True)
        dn = dy * g_ref[...]
        dx_ref[...] = rstd * (dn - xn * jnp.mean(dn * xn, axis=-1, keepdims=True))

    return _rowwise(body, [x, target], [g], [(d, F32)], [((1, LANES), F32), ((1, d), F32)], tile=256, name=name)


def _rope_tables(sign):
    half = ROPE_DIM // 2
    inv = ROPE_THETA ** (-jnp.arange(0, ROPE_DIM, 2, dtype=F32) / ROPE_DIM)
    ang = jnp.arange(SEQ, dtype=F32)[:, None] * inv[None, :]
    cos, sin = jnp.cos(ang), jnp.sin(ang) * sign
    zeros = jnp.zeros((SEQ, HEAD_DIM - ROPE_DIM), F32)
    zh = jnp.zeros((SEQ, half), F32)
    c = jnp.concatenate([cos, cos, zeros + 1.0], axis=1)
    s_up = jnp.concatenate([-sin, zh, zeros], axis=1)
    s_dn = jnp.concatenate([zh, sin, zeros], axis=1)
    return c, s_up, s_dn


def _rope(src, off, nblk, tabs, dm, name):
    t = src.shape[0]
    tq = 256
    half = ROPE_DIM // 2
    win0 = 3 * dm["NG"] * DIL_HEADS
    win1 = win0 + WIN_Q_HEADS + WIN_KV_HEADS
    qw = nblk * HEAD_DIM
    assert nblk == dm["QW"] // HEAD_DIM
    wb = next(c for c in (1024, 768, 512, 384, 256, 128) if off % c == 0 and qw % c == 0)
    reps = wb // HEAD_DIM
    sb = SEQ // tq
    flag = (jnp.arange(qw, dtype=jnp.int32) // HEAD_DIM < win1).astype(F32)[None, :]

    def body(x_ref, c_ref, up_ref, dn_ref, f_ref, o_ref):
        xv = x_ref[...].astype(F32)

        def wide(r):
            v = r[...]
            return v if reps == 1 else jnp.concatenate([v] * reps, axis=1)

        rot = xv * wide(c_ref) + pltpu.roll(xv, wb - half, 1) * wide(up_ref) + pltpu.roll(xv, half, 1) * wide(dn_ref)
        o_ref[...] = jnp.where(f_ref[...] > 0.5, rot, xv).astype(BF16)

    tab_spec = pl.BlockSpec((tq, HEAD_DIM), lambda i, j: (i % sb, 0))
    return pl.pallas_call(
        body, name=name, grid=(t // tq, qw // wb),
        in_specs=[pl.BlockSpec((tq, wb), lambda i, j, o=off // wb: (i, o + j)), tab_spec, tab_spec, tab_spec,
                  pl.BlockSpec((1, wb), lambda i, j: (0, j))],
        out_specs=pl.BlockSpec((tq, wb), lambda i, j: (i, j)),
        out_shape=jax.ShapeDtypeStruct((t, qw), BF16),
        compiler_params=_params(("parallel", "parallel")),
    )(src, *tabs, flag)


def _band_mask(rows_start, cols_start, nrows, ncols, w, n, rows_are_q):
    r = rows_start + lax.broadcasted_iota(jnp.int32, (nrows, ncols), 0)
    c = cols_start + lax.broadcasted_iota(jnp.int32, (nrows, ncols), 1)
    del rows_are_q
    return (jnp.abs(r - c) <= w) & (c >= 0) & (c < n)


def _nbr_specs(make, nb):
    if nb == 1:
        return [make(lambda i: i)]
    return [make(lambda i: jnp.maximum(i - 1, 0)), make(lambda i: i), make(lambda i: jnp.minimum(i + 1, nb - 1))]


def _cat(refs, axis):
    vals = [r[...] for r in refs]
    return vals[0] if len(vals) == 1 else jnp.concatenate(vals, axis=axis)


def _head(ref, h):
    return ref[:, h * HEAD_DIM:(h + 1) * HEAD_DIM]


def _head_cat(refs, h, axis=0):
    vals = [_head(r, h) for r in refs]
    return vals[0] if len(vals) == 1 else jnp.concatenate(vals, axis=axis)


def _attn_fwd(qa, ka, va, qb, kb, vb, hq, rep, n, w, sink, out_dtype, name):
    bb = qa.shape[0]
    blk = ATT_BLK
    nb = n // blk
    nk = 1 if nb == 1 else 3
    hkv = hq // rep
    scale = HEAD_DIM ** -0.5
    has_sink = sink is not None

    def body(*refs):
        q_ref = refs[0]
        k_refs = refs[1:1 + nk]
        v_refs = refs[1 + nk:1 + 2 * nk]
        pos = 1 + 2 * nk
        sink_ref = refs[pos] if has_sink else None
        o_ref, lse_ref = refs[pos + has_sink], refs[pos + has_sink + 1]
        i = pl.program_id(1)
        k0 = (i - 1) * blk if nk == 3 else i * blk
        valid = _band_mask(i * blk, k0, blk, nk * blk, w, n, True)
        for g in range(hkv):
            kcat = _head_cat(k_refs, g)
            vcat = _head_cat(v_refs, g)
            for r in range(rep):
                h = g * rep + r
                s = lax.dot_general(_head(q_ref, h), kcat, NT, preferred_element_type=F32) * scale
                s = jnp.where(valid, s, NEG_BIG)
                m = jnp.max(s, axis=1, keepdims=True)
                if has_sink:
                    m = jnp.maximum(m, sink_ref[h])
                p = jnp.exp(s - m)
                l = jnp.sum(p, axis=1, keepdims=True)
                if has_sink:
                    l = l + jnp.exp(sink_ref[h] - m)
                o = lax.dot_general(p.astype(BF16), vcat, NN, preferred_element_type=F32) / l
                o_ref[:, h * HEAD_DIM:(h + 1) * HEAD_DIM] = o.astype(o_ref.dtype)
                lse_ref[h] = m + jnp.log(l)

    def mk(col, width):
        return lambda f: pl.BlockSpec((None, blk, width), lambda b, i, f=f: (b, f(i), col))

    qw, kw = hq * HEAD_DIM, hkv * HEAD_DIM
    in_specs = [pl.BlockSpec((None, blk, qw), lambda b, i: (b, i, qb))]
    in_specs += _nbr_specs(mk(kb, kw), nb) + _nbr_specs(mk(vb, kw), nb)
    args = [qa] + [ka] * nk + [va] * nk
    if has_sink:
        in_specs.append(pl.BlockSpec((hq, 1, 1), lambda b, i: (0, 0, 0)))
        args.append(sink)
    return pl.pallas_call(
        body, name=name, grid=(bb, nb), in_specs=in_specs,
        out_specs=[pl.BlockSpec((None, blk, qw), lambda b, i: (b, i, 0)),
                   pl.BlockSpec((None, hq, blk, 1), lambda b, i: (b, 0, i, 0))],
        out_shape=[jax.ShapeDtypeStruct((bb, n, qw), out_dtype), jax.ShapeDtypeStruct((bb, hq, n, 1), F32)],
        compiler_params=_params(("parallel", "parallel")),
    )(*args)


def _attn_dq(qa, ka, va, do, lse, delta, qb, kb, vb, hq, rep, n, w, name):
    bb = qa.shape[0]
    blk = ATT_BLK
    nb = n // blk
    nk = 1 if nb == 1 else 3
    hkv = hq // rep
    scale = HEAD_DIM ** -0.5

    def body(*refs):
        q_ref = refs[0]
        k_refs = refs[1:1 + nk]
        v_refs = refs[1 + nk:1 + 2 * nk]
        do_ref, lse_ref, dl_ref, dq_ref = refs[1 + 2 * nk:]
        i = pl.program_id(1)
        k0 = (i - 1) * blk if nk == 3 else i * blk
        valid = _band_mask(i * blk, k0, blk, nk * blk, w, n, True)
        for g in range(hkv):
            kcat = _head_cat(k_refs, g)
            vcat = _head_cat(v_refs, g)
            for r in range(rep):
                h = g * rep + r
                s = lax.dot_general(_head(q_ref, h), kcat, NT, preferred_element_type=F32) * scale
                p = jnp.exp(jnp.where(valid, s, NEG_BIG) - lse_ref[h])
                dp = lax.dot_general(_head(do_ref, h).astype(BF16), vcat, NT, preferred_element_type=F32)
                ds = p * (dp - dl_ref[h])
                dq = lax.dot_general(ds.astype(BF16), kcat, NN, preferred_element_type=F32) * scale
                dq_ref[:, h * HEAD_DIM:(h + 1) * HEAD_DIM] = dq.astype(BF16)

    def mk(col, width):
        return lambda f: pl.BlockSpec((None, blk, width), lambda b, i, f=f: (b, f(i), col))

    qw, kw = hq * HEAD_DIM, hkv * HEAD_DIM
    col_spec = pl.BlockSpec((None, hq, blk, 1), lambda b, i: (b, 0, i, 0))
    in_specs = [pl.BlockSpec((None, blk, qw), lambda b, i: (b, i, qb))]
    in_specs += _nbr_specs(mk(kb, kw), nb) + _nbr_specs(mk(vb, kw), nb)
    in_specs += [pl.BlockSpec((None, blk, qw), lambda b, i: (b, i, 0)), col_spec, col_spec]
    return pl.pallas_call(
        body, name=name, grid=(bb, nb), in_specs=in_specs,
        out_specs=pl.BlockSpec((None, blk, qw), lambda b, i: (b, i, 0)),
        out_shape=jax.ShapeDtypeStruct((bb, n, qw), BF16),
        compiler_params=_params(("parallel", "parallel")),
    )(qa, *([ka] * nk), *([va] * nk), do, lse, delta)


def _attn_dkv(qa, ka, va, do, lse_row, delta_row, qb, kb, vb, hq, rep, n, w, name):
    bb = qa.shape[0]
    blk = ATT_BLK
    nb = n // blk
    nq = 1 if nb == 1 else 3
    hkv = hq // rep
    scale = HEAD_DIM ** -0.5

    def body(*refs):
        k_ref, v_ref = refs[0], refs[1]
        q_refs = refs[2:2 + nq]
        do_refs = refs[2 + nq:2 + 2 * nq]
        lse_refs = refs[2 + 2 * nq:2 + 3 * nq]
        dl_refs = refs[2 + 3 * nq:2 + 4 * nq]
        dk_ref, dv_ref = refs[2 + 4 * nq:]
        j = pl.program_id(1)
        q0 = (j - 1) * blk if nq == 3 else j * blk
        valid = _band_mask(j * blk, q0, blk, nq * blk, w, n, False)
        for g in range(hkv):
            kg, vg = _head(k_ref, g), _head(v_ref, g)
            dk = jnp.zeros((blk, HEAD_DIM), F32)
            dv = jnp.zeros((blk, HEAD_DIM), F32)
            for r in range(rep):
                h = g * rep + r
                qcat = _head_cat(q_refs, h)
                docat = _head_cat(do_refs, h).astype(BF16)
                lse = lse_refs[0][h] if nq == 1 else jnp.concatenate([lr[h] for lr in lse_refs], axis=1)
                dl = dl_refs[0][h] if nq == 1 else jnp.concatenate([dr[h] for dr in dl_refs], axis=1)
                st = lax.dot_general(kg, qcat, NT, preferred_element_type=F32) * scale
                pt = jnp.exp(jnp.where(valid, st, NEG_BIG) - lse)
                dv = dv + lax.dot_general(pt.astype(BF16), docat, NN, preferred_element_type=F32)
                dpt = lax.dot_general(vg, docat, NT, preferred_element_type=F32)
                dst = pt * (dpt - dl)
                dk = dk + lax.dot_general(dst.astype(BF16), qcat, NN, preferred_element_type=F32) * scale
            dk_ref[:, g * HEAD_DIM:(g + 1) * HEAD_DIM] = dk.astype(BF16)
            dv_ref[:, g * HEAD_DIM:(g + 1) * HEAD_DIM] = dv.astype(BF16)

    qw, kw = hq * HEAD_DIM, hkv * HEAD_DIM

    def mkq(col):
        return lambda f: pl.BlockSpec((None, blk, qw), lambda b, j, f=f: (b, f(j), col))

    def mkrow(f):
        return pl.BlockSpec((None, hq, 1, blk), lambda b, j, f=f: (b, 0, 0, f(j)))

    in_specs = [pl.BlockSpec((None, blk, kw), lambda b, j: (b, j, kb)), pl.BlockSpec((None, blk, kw), lambda b, j: (b, j, vb))]
    in_specs += _nbr_specs(mkq(qb), nb) + _nbr_specs(mkq(0), nb) + _nbr_specs(mkrow, nb) + _nbr_specs(mkrow, nb)
    o_spec = pl.BlockSpec((None, blk, kw), lambda b, j: (b, j, 0))
    return pl.pallas_call(
        body, name=name, grid=(bb, nb), in_specs=in_specs, out_specs=[o_spec, o_spec],
        out_shape=[jax.ShapeDtypeStruct((bb, n, kw), BF16)] * 2,
        compiler_params=_params(("parallel", "parallel")),
    )(ka, va, *([qa] * nq), *([do] * nq), *([lse_row] * nq), *([delta_row] * nq))


def _head_expand(v, nh, width):
    lane_head = lax.broadcasted_iota(jnp.int32, (1, nh * width), 1) >> int(math.log2(width))
    out = jnp.zeros((v.shape[0], nh * width), F32)
    for j in range(nh):
        out = jnp.where(lane_head == j, v[:, j:j + 1], out)
    return out


def _head_sums(m, nh, width):
    lane_head = lax.broadcasted_iota(jnp.int32, (1, nh * width), 1) >> int(math.log2(width))
    col = lax.broadcasted_iota(jnp.int32, (1, nh), 1)
    out = jnp.zeros((m.shape[0], nh), F32)
    for j in range(nh):
        sj = jnp.sum(jnp.where(lane_head == j, m, 0.0), axis=1, keepdims=True)
        out = jnp.where(col == j, sj, out)
    return out


def _head_dots(a, b, nh, name):
    def body(a_ref, b_ref, o_ref):
        o_ref[...] = _head_sums(a_ref[...].astype(F32) * b_ref[...].astype(F32), nh, HEAD_DIM)

    return _rowwise(body, [a, b], [], [(nh, F32)], tile=256, name=name)[0]


def _dil_combine(outs, lses, name):
    ng = len(outs)

    def body(*refs):
        o_refs, l_refs = refs[:ng], refs[ng:2 * ng]
        y_ref, lt_ref = refs[2 * ng], refs[2 * ng + 1]
        ls = [r[...] for r in l_refs]
        m = functools.reduce(jnp.maximum, ls)
        es = [jnp.exp(v - m) for v in ls]
        tot = functools.reduce(jnp.add, es)
        acc = jnp.zeros(o_refs[0].shape, F32)
        for o_ref, e in zip(o_refs, es):
            acc = acc + _head_expand(e / tot, DIL_HEADS, HEAD_DIM) * o_ref[...]
        y_ref[...] = acc.astype(BF16)
        lt_ref[...] = m + jnp.log(tot)

    dw = outs[0].shape[1]
    return _rowwise(body, list(outs) + list(lses), [], [(dw, BF16), (DIL_HEADS, F32)], tile=256, name=name)


def _sink_grad(lse, delta, sink, name):
    def body(l_ref, d_ref, s_ref, o_ref):
        o_ref[...] -= jnp.sum(jnp.exp(s_ref[...] - l_ref[...]) * d_ref[...], axis=0, keepdims=True)

    return _rowwise(body, [lse, delta], [sink], [], [((1, lse.shape[1]), F32)], tile=512, name=name)[0]


def _shift_rows(x, d, nrows):
    if d == 0:
        return x
    rolled = pltpu.roll(x, (-d) % nrows, 0)
    row = lax.broadcasted_iota(jnp.int32, x.shape, 0)
    ok = (row + d >= 0) & (row + d < nrows)
    return jnp.where(ok, rolled, 0.0)


def _conv_fwd(proj, off, conv_w, conv_b, xbc, name):
    t = proj.shape[0]
    tc = _pick(xbc, (256, 128))
    assert off % tc == 0
    pad = (CONV_WIDTH - 1) // 2

    def body(x_ref, w_ref, b_ref, c_ref, u_ref):
        xv = x_ref[...].astype(F32)
        acc = jnp.zeros_like(xv) + b_ref[...]
        for k in range(CONV_WIDTH):
            acc = acc + w_ref[k:k + 1, :] * _shift_rows(xv, k - pad, SEQ)
        c_ref[...] = acc.astype(BF16)
        u_ref[...] = (acc * _sigmoid(acc)).astype(BF16)

    o_spec = pl.BlockSpec((SEQ, tc), lambda b, j: (b, j))
    return pl.pallas_call(
        body, name=name, grid=(t // SEQ, xbc // tc),
        in_specs=[pl.BlockSpec((SEQ, tc), lambda b, j, o=off // tc: (b, o + j)),
                  pl.BlockSpec((CONV_WIDTH, tc), lambda b, j: (0, j)), pl.BlockSpec((1, tc), lambda b, j: (0, j))],
        out_specs=[o_spec, o_spec], out_shape=[jax.ShapeDtypeStruct((t, xbc), BF16)] * 2,
        compiler_params=_params(("parallel", "parallel")),
    )(proj, conv_w, conv_b)


def _conv_bwd(du, cpre, proj, off, conv_w, name):
    t, xbc = du.shape
    tc = _pick(xbc, (256, 128))
    pad = (CONV_WIDTH - 1) // 2

    def body(du_ref, c_ref, x_ref, w_ref, dx_ref, dw_ref, db_ref):
        @pl.when(pl.program_id(1) == 0)
        def _():
            dw_ref[...] = jnp.zeros_like(dw_ref)
            db_ref[...] = jnp.zeros_like(db_ref)

        cv = c_ref[...].astype(F32)
        sg = _sigmoid(cv)
        dc = du_ref[...] * (sg * (1.0 + cv * (1.0 - sg)))
        xv = x_ref[...].astype(F32)
        dx = jnp.zeros_like(dc)
        for k in range(CONV_WIDTH):
            dx = dx + w_ref[k:k + 1, :] * _shift_rows(dc, pad - k, SEQ)
            dw_ref[k:k + 1, :] += jnp.sum(dc * _shift_rows(xv, k - pad, SEQ), axis=0, keepdims=True)
        db_ref[...] += jnp.sum(dc, axis=0, keepdims=True)
        dx_ref[...] = dx.astype(BF16)

    blk = pl.BlockSpec((SEQ, tc), lambda j, b: (b, j))
    return pl.pallas_call(
        body, name=name, grid=(xbc // tc, t // SEQ),
        in_specs=[blk, blk, pl.BlockSpec((SEQ, tc), lambda j, b, o=off // tc: (b, o + j)),
                  pl.BlockSpec((CONV_WIDTH, tc), lambda j, b: (0, j))],
        out_specs=[blk, pl.BlockSpec((CONV_WIDTH, tc), lambda j, b: (0, j)), pl.BlockSpec((1, tc), lambda j, b: (0, j))],
        out_shape=[jax.ShapeDtypeStruct((t, xbc), BF16), jax.ShapeDtypeStruct((CONV_WIDTH, xbc), F32),
                   jax.ShapeDtypeStruct((1, xbc), F32)],
        compiler_params=_params(("parallel", "arbitrary")),
    )(du, cpre, proj, conv_w)


def _dt_prep(dtr, bias, name):
    def body(r_ref, b_ref, o_ref):
        v = r_ref[...] + b_ref[...]
        o_ref[...] = jnp.maximum(v, 0.0) + jnp.log1p(jnp.exp(-jnp.abs(v)))

    return _rowwise(body, [dtr], [bias], [(dtr.shape[1], F32)], tile=512, name=name)[0]


def _scan_prelude(d, dt_ref, dtt_ref, al_ref, alt_ref, hg):
    p = SSD_HEAD_DIM
    ch = SSD_CHUNK
    a_row = -jnp.exp(al_ref[...])
    a_col = -jnp.exp(alt_ref[...])
    dtc = dt_ref[...]
    dt_x = _head_expand(dtc, hg, p)
    dta_x = dt_x * _head_expand(a_row, hg, p)
    dta_t = dtt_ref[...] * a_col
    ri = lax.broadcasted_iota(jnp.int32, (ch, ch), 0)
    ci = lax.broadcasted_iota(jnp.int32, (ch, ch), 1)
    mask = (ci <= ri) if d == 0 else (ci >= ri)
    mask_t = (ci >= ri) if d == 0 else (ci <= ri)
    tri = mask.astype(F32)
    phi_x = jnp.dot(tri, dta_x, preferred_element_type=F32, precision=HIGHEST)
    phi_r = lax.dot_general(dta_t, tri, NT, preferred_element_type=F32, precision=HIGHEST)
    tot_x = jnp.sum(dta_x, axis=0, keepdims=True)
    return dtc, dt_x, phi_x, phi_r, tot_x, mask, mask_t


def _scan_specs(d, nc, hg, dm):
    p, n, ch = SSD_HEAD_DIM, SSD_STATE, SSD_CHUNK
    w = hg * p
    b0 = dm["HI"] // n
    c0 = (dm["HI"] + dm["GN"]) // n

    def row(b, c):
        return b * nc + c

    return [
        pl.BlockSpec((ch, w), lambda b, g, c: (row(b, c), g)),
        pl.BlockSpec((ch, n), lambda b, g, c: (row(b, c), b0 + g)),
        pl.BlockSpec((ch, n), lambda b, g, c: (row(b, c), c0 + g)),
        pl.BlockSpec((None, None, ch, hg), lambda b, g, c: (d, g, row(b, c), 0)),
        pl.BlockSpec((None, None, 8, ch), lambda b, g, c: (d, g, 0, row(b, c))),
        pl.BlockSpec((None, None, 1, hg), lambda b, g, c: (d, g, 0, 0)),
        pl.BlockSpec((None, None, 8, 1), lambda b, g, c: (d, g, 0, 0)),
    ]


def _remap(spec, f):
    return pl.BlockSpec(spec.block_shape, lambda b, g, c, im=spec.index_map: im(b, g, f(c)))


def _scan_fwd(u, dtg, dttg, alg, altg, d, dm, name, side=None):
    t = u.shape[0]
    p, n, ch, hg = SSD_HEAD_DIM, SSD_STATE, SSD_CHUNK, dm["HG"]
    w = hg * p
    nc = SEQ // ch
    order = (lambda c: c) if d == 0 else (lambda c: nc - 1 - c)

    def body(x_ref, b_ref, c_ref, dt_ref, dtt_ref, al_ref, alt_ref, y_ref, sin_ref, s_ref):
        @pl.when(pl.program_id(2) == 0)
        def _():
            s_ref[...] = jnp.zeros_like(s_ref)

        dtc, dt_x, phi_x, phi_r, tot_x, mask, _ = _scan_prelude(d, dt_ref, dtt_ref, al_ref, alt_ref, hg)
        lane_head = lax.broadcasted_iota(jnp.int32, (1, w), 1) >> int(math.log2(p))
        cm, bm = c_ref[...], b_ref[...]
        cb = lax.dot_general(cm, bm, NT, preferred_element_type=F32)
        xdt = x_ref[...].astype(F32) * dt_x
        xdt_b = xdt.astype(BF16)
        ydiag = jnp.zeros((ch, w), F32)
        for j in range(hg):
            seg = phi_x[:, j * p:j * p + 1] - phi_r[j:j + 1, :]
            mj = (cb * jnp.exp(jnp.where(mask, seg, NEG_BIG))).astype(BF16)
            ydiag = ydiag + jnp.dot(mj, jnp.where(lane_head == j, xdt_b, jnp.zeros_like(xdt_b)), preferred_element_type=F32)
        s = s_ref[...]
        y_ref[...] = ydiag + jnp.dot(cm, s.astype(BF16), preferred_element_type=F32) * jnp.exp(phi_x)
        sin_ref[...] = s
        wm = (xdt * jnp.exp(tot_x - phi_x)).astype(BF16)
        s_ref[...] = s * jnp.exp(tot_x) + lax.dot_general(bm, wm, TN, preferred_element_type=F32)

    specs = [_remap(s, order) for s in _scan_specs(d, nc, hg, dm)]
    own, brought = _carry_call(
        side, body, name=name, grid=(t // SEQ, SSD_GROUPS, nc), in_specs=specs,
        out_specs=[_remap(pl.BlockSpec((ch, w), lambda b, g, c: (b * nc + c, g)), order),
                   _remap(pl.BlockSpec((None, None, n, w), lambda b, g, c: (b * nc + c, g, 0, 0)), order)],
        out_shape=[jax.ShapeDtypeStruct((t, dm["HI"]), F32), jax.ShapeDtypeStruct((t // ch, SSD_GROUPS, n, w), F32)],
        scratch_shapes=[pltpu.VMEM((n, w), F32)], sem=("parallel", "parallel", "arbitrary"),
        args=(u, u, u, dtg, dttg, alg, altg))
    return own if side is None else (own, brought)


def _scan_bwd(u, dtg, dttg, alg, altg, dy, sin, adds, d, dm, name, side=None):
    t = u.shape[0]
    p, n, ch, hg = SSD_HEAD_DIM, SSD_STATE, SSD_CHUNK, dm["HG"]
    w = hg * p
    nc = SEQ // ch
    order = (lambda c: nc - 1 - c) if d == 0 else (lambda c: c)
    has_bc_add = adds[1] is not None

    def body(*refs):
        x_ref, b_ref, c_ref, dt_ref, dtt_ref, al_ref, alt_ref, dy_ref, sin_ref, ax_ref = refs[:10]
        pos = 10
        ab_ref = ac_ref = None
        if has_bc_add:
            ab_ref, ac_ref = refs[10], refs[11]
            pos = 12
        dxs_ref, db_ref, dc_ref, rq_ref, xdx_ref, ds_ref = refs[pos:]

        @pl.when(pl.program_id(2) == 0)
        def _():
            ds_ref[...] = jnp.zeros_like(ds_ref)

        dtc, dt_x, phi_x, phi_r, tot_x, mask, mask_t = _scan_prelude(d, dt_ref, dtt_ref, al_ref, alt_ref, hg)
        lane_head = lax.broadcasted_iota(jnp.int32, (1, w), 1) >> int(math.log2(p))
        cm, bm = c_ref[...], b_ref[...]
        cb = lax.dot_general(cm, bm, NT, preferred_element_type=F32)
        cb_t = lax.dot_general(bm, cm, NT, preferred_element_type=F32)
        xs = x_ref[...].astype(F32)
        xdt = xs * dt_x
        xdt_b = xdt.astype(BF16)
        dy = dy_ref[...]
        dy_b = dy.astype(BF16)
        zero_b = jnp.zeros_like(dy_b)
        col = lax.broadcasted_iota(jnp.int32, (1, hg), 1)
        dxp = jnp.zeros((ch, w), F32)
        a_ls = jnp.zeros((ch, ch), F32)
        a_sl = jnp.zeros((ch, ch), F32)
        dphi = jnp.zeros((ch, hg), F32)
        for j in range(hg):
            pc = phi_x[:, j * p:j * p + 1]
            pr = phi_r[j:j + 1, :]
            l_ls = jnp.exp(jnp.where(mask, pc - pr, NEG_BIG))
            l_sl = jnp.exp(jnp.where(mask_t, pr - pc, NEG_BIG))
            dy_j = jnp.where(lane_head == j, dy_b, zero_b)
            xdt_j = jnp.where(lane_head == j, xdt_b, zero_b)
            dxp = dxp + jnp.dot((cb_t * l_sl).astype(BF16), dy_j, preferred_element_type=F32)
            g_ls = l_ls * lax.dot_general(dy_j, xdt_b, NT, preferred_element_type=F32)
            g_sl = l_sl * lax.dot_general(xdt_j, dy_b, NT, preferred_element_type=F32)
            a_ls = a_ls + g_ls
            a_sl = a_sl + g_sl
            pair = jnp.sum(g_ls * cb, axis=1, keepdims=True) - jnp.sum(g_sl * cb_t, axis=1, keepdims=True)
            dphi = jnp.where(col == j, pair, dphi)
        ds = ds_ref[...]
        ds_b = ds.astype(BF16)
        sin = sin_ref[...]
        sin_b = sin.astype(BF16)
        e_tp = jnp.exp(tot_x - phi_x)
        e_p = jnp.exp(phi_x)
        dxp_off = e_tp * jnp.dot(bm, ds_b, preferred_element_type=F32)
        dxp = dxp + dxp_off
        dxs_ref[...] = ax_ref[...] + dxp * dt_x
        xdx_ref[...] = _head_sums(xs * dxp, hg, p)
        y_off = jnp.dot(cm, sin_b, preferred_element_type=F32) * e_p
        st_t = _head_sums(xdt * dxp_off, hg, p)
        dphi = dphi + _head_sums(dy * y_off, hg, p) - st_t
        dtot = _head_sums(jnp.sum(ds * sin, axis=0, keepdims=True) * jnp.exp(tot_x), hg, p) + jnp.sum(st_t, axis=0, keepdims=True)
        cum = jnp.dot(mask_t.astype(F32), _head_expand(dphi, hg, p), preferred_element_type=F32, precision=HIGHEST)
        ddta = jnp.zeros((ch, hg), F32)
        for j in range(hg):
            ddta = jnp.where(col == j, cum[:, j * p:j * p + 1], ddta)
        rq_ref[...] = ddta + dtot
        dye = (dy * e_p).astype(BF16)
        dcv = jnp.dot(a_ls.astype(BF16), bm, preferred_element_type=F32)
        dcv = dcv + lax.dot_general(dye, sin_b, NT, preferred_element_type=F32)
        dbv = jnp.dot(a_sl.astype(BF16), cm, preferred_element_type=F32)
        dbv = dbv + lax.dot_general((xdt * e_tp).astype(BF16), ds_b, NT, preferred_element_type=F32)
        if has_bc_add:
            dcv = dcv + ac_ref[...]
            dbv = dbv + ab_ref[...]
        dc_ref[...] = dcv
        db_ref[...] = dbv
        ds_ref[...] = ds * jnp.exp(tot_x) + lax.dot_general(cm, dye, TN, preferred_element_type=F32)

    def sp(spec):
        return _remap(spec, order)

    xw = pl.BlockSpec((ch, w), lambda b, g, c: (b * nc + c, g))
    gn_blk = pl.BlockSpec((ch, n), lambda b, g, c: (b * nc + c, g))
    small = pl.BlockSpec((None, ch, hg), lambda b, g, c: (g, b * nc + c, 0))
    in_specs = [sp(s) for s in _scan_specs(d, nc, hg, dm)]
    in_specs += [sp(xw), sp(pl.BlockSpec((None, None, n, w), lambda b, g, c: (b * nc + c, g, 0, 0))), sp(xw)]
    args = [u, u, u, dtg, dttg, alg, altg, dy, sin, adds[0]]
    if has_bc_add:
        in_specs += [sp(gn_blk), sp(gn_blk)]
        args += [adds[1], adds[2]]
    own, brought = _carry_call(
        side, body, name=name, grid=(t // SEQ, SSD_GROUPS, nc), in_specs=in_specs,
        out_specs=[sp(xw), sp(gn_blk), sp(gn_blk), sp(small), sp(small)],
        out_shape=[jax.ShapeDtypeStruct((t, dm["HI"]), F32), jax.ShapeDtypeStruct((t, dm["GN"]), F32),
                   jax.ShapeDtypeStruct((t, dm["GN"]), F32), jax.ShapeDtypeStruct((SSD_GROUPS, t, hg), F32),
                   jax.ShapeDtypeStruct((SSD_GROUPS, t, hg), F32)],
        scratch_shapes=[pltpu.VMEM((n, w), F32)], sem=("parallel", "parallel", "arbitrary"), args=args)
    return own if side is None else (own, brought)


def _ssd_param_bwd(rq_f, rq_r, xdx, dtp, dtr, bias, alog, name):
    def body(rf_ref, rr_ref, xdx_ref, dt_ref, dtr_ref, b_ref, al_ref, o_ref, db_ref, da_ref):
        a = -jnp.exp(al_ref[...])
        d_dta = rf_ref[...] + rr_ref[...]
        ddt = a * d_dta + xdx_ref[...]
        ddtr = ddt * _sigmoid(dtr_ref[...] + b_ref[...])
        o_ref[...] = ddtr
        db_ref[...] += jnp.sum(ddtr, axis=0, keepdims=True)
        da_ref[...] += a * jnp.sum(dt_ref[...] * d_dta, axis=0, keepdims=True)

    return _rowwise(body, [rq_f, rq_r, xdx, dtp, dtr], [bias, alog], [(LANES, F32)],
                    [((1, LANES), F32), ((1, LANES), F32)], tile=512, name=name)


def _ssd_out_fwd(y_f, y_b, u, proj, dcols, gn, hi, name):
    def body(yf_ref, yb_ref, x_ref, z_ref, d_ref, g_ref, o_ref):
        ytot = yf_ref[...] + yb_ref[...] + d_ref[...] * x_ref[...].astype(F32)
        zv = z_ref[...].astype(F32)
        yz = ytot * (zv * _sigmoid(zv))
        rstd = lax.rsqrt(jnp.mean(yz * yz, axis=-1, keepdims=True) + EPS)
        o_ref[...] = (yz * rstd * g_ref[...]).astype(BF16)

    return _rowwise(body, [y_f, y_b, (u, 0, hi), (proj, 0, hi)], [dcols, gn], [(hi, BF16)], tile=256, name=name)[0]


def _ssd_out_bwd(dya, y_f, y_b, u, proj, dcols, gn, hi, name):
    def body(dy_ref, yf_ref, yb_ref, x_ref, z_ref, d_ref, g_ref, dyt_ref, dxs_ref, dz_ref, dg_ref, dd_ref):
        xv = x_ref[...].astype(F32)
        ytot = yf_ref[...] + yb_ref[...] + d_ref[...] * xv
        zv = z_ref[...].astype(F32)
        sg = _sigmoid(zv)
        sz = zv * sg
        yz = ytot * sz
        rstd = lax.rsqrt(jnp.mean(yz * yz, axis=-1, keepdims=True) + EPS)
        yn = yz * rstd
        dv = dy_ref[...]
        dg_ref[...] += jnp.sum(dv * yn, axis=0, keepdims=True)
        dn = dv * g_ref[...]
        dyz = rstd * (dn - yn * jnp.mean(dn * yn, axis=-1, keepdims=True))
        dyt = dyz * sz
        dyt_ref[...] = dyt
        dxs_ref[...] = dyt * d_ref[...]
        dz_ref[...] = (dyz * ytot * (sg * (1.0 + zv * (1.0 - sg)))).astype(BF16)
        dd_ref[...] += jnp.sum(dyt * xv, axis=0, keepdims=True)

    return _rowwise(body, [dya, y_f, y_b, (u, 0, hi), (proj, 0, hi)], [dcols, gn],
                    [(hi, F32), (hi, F32), (hi, BF16)], [((1, hi), F32), ((1, hi), F32)], tile=128, name=name)


def _gate_fwd(pa, pb, pc, proj, off, d, name):
    def body(a_ref, b_ref, c_ref, g0_ref, g1_ref, g2_ref, o_ref):
        acc = _sigmoid(g0_ref[...].astype(F32)) * a_ref[...]
        acc = acc + _sigmoid(g1_ref[...].astype(F32)) * b_ref[...]
        acc = acc + _sigmoid(g2_ref[...].astype(F32)) * c_ref[...]
        o_ref[...] = acc.astype(BF16)

    rows = [pa, pb, pc] + [(proj, off + k * d, d) for k in range(3)]
    return _rowwise(body, rows, [], [(d, BF16)], tile=256, name=name)[0]


def _gate_bwd(dm_, pa, pb, pc, proj, off, d, name):
    def body(dm_ref, a_ref, b_ref, c_ref, g0_ref, g1_ref, g2_ref, da_ref, db_ref, dc_ref, dg0_ref, dg1_ref, dg2_ref):
        dmv = dm_ref[...]
        for p_ref, g_ref, dp_ref, dg_ref in ((a_ref, g0_ref, da_ref, dg0_ref), (b_ref, g1_ref, db_ref, dg1_ref),
                                             (c_ref, g2_ref, dc_ref, dg2_ref)):
            sg = _sigmoid(g_ref[...].astype(F32))
            dp_ref[...] = (dmv * sg).astype(BF16)
            dg_ref[...] = (dmv * p_ref[...] * sg * (1.0 - sg)).astype(BF16)

    rows = [dm_, pa, pb, pc] + [(proj, off + k * d, d) for k in range(3)]
    return _rowwise(body, rows, [], [(d, BF16)] * 6, tile=128, name=name)


def _adamw(w, g, m, v, name):
    nl, rows, cols = w.shape
    tile = _pick(rows, ROW_TILES)
    c1 = 1.0 / (1.0 - ADAM_B1 ** ADAM_STEP)
    c2 = 1.0 / (1.0 - ADAM_B2 ** ADAM_STEP)

    def body(w_ref, g_ref, m_ref, v_ref, d_ref, nm_ref, nv_ref):
        gv = g_ref[...]
        nm = ADAM_B1 * m_ref[...] + (1.0 - ADAM_B1) * gv
        nv = ADAM_B2 * v_ref[...] + (1.0 - ADAM_B2) * (gv * gv)
        nm_ref[...] = nm
        nv_ref[...] = nv
        d_ref[...] = -ADAM_LR * ((nm * c1) / (jnp.sqrt(nv * c2) + ADAM_EPS) + ADAM_WD * w_ref[...])

    blk = pl.BlockSpec((None, tile, cols), lambda l, i: (l, i, 0))
    return pl.pallas_call(
        body, name=name, grid=(nl, rows // tile), in_specs=[blk] * 4, out_specs=[blk] * 3,
        out_shape=[jax.ShapeDtypeStruct(w.shape, F32)] * 3, compiler_params=_params(("parallel", "parallel")),
    )(w, g, m, v)


ANY = pl.BlockSpec(memory_space=pl.ANY)


def _place():
    x, y, c = lax.axis_index("x"), lax.axis_index("y"), lax.axis_index("c")
    chips = [(1 - x, y), (x, 1 - y), (1 - x, 1 - y)]
    return x, y, c, chips


def _gather_copies(src, out, ssem, rsem, base):
    x, y, c, chips = _place()
    k = 2 * x + y

    def copy(j, kk, layer, to, own=False):
        return pltpu.make_async_remote_copy(
            src_ref=src.at[layer] if own else out.at[layer, kk], dst_ref=out.at[layer, kk],
            send_sem=ssem.at[base + j], recv_sem=rsem.at[base + j], device_id=to, device_id_type=MESH)

    first = [copy(j, k, c, (cx, cy, c), own=True) for j, (cx, cy) in enumerate(chips)]
    passed = [copy(3 + j, 2 * cx + cy, c, (x, y, 1 - c)) for j, (cx, cy) in enumerate(chips)]
    landed = [copy(j, 2 * cx + cy, c, (x, y, c)) for j, (cx, cy) in enumerate(chips)]
    handed = [copy(3 + j, 2 * cx + cy, 1 - c, (x, y, c)) for j, (cx, cy) in enumerate(chips)]
    return first, passed, landed, handed


def _gather_start(src, out, ssem, rsem, base):
    for cp in _gather_copies(src, out, ssem, rsem, base)[0]:
        cp.start()


def _gather_finish(src, out, ssem, rsem, base):
    first, passed, landed, handed = _gather_copies(src, out, ssem, rsem, base)
    for arrived, onward in zip(landed, passed):
        arrived.wait_recv()
        onward.start()
    for arrived in handed:
        arrived.wait_recv()
    for cp in first + passed:
        cp.wait_send()


def _gather_side(arrs):
    n = len(arrs)

    def start(srcs, outs, sems):
        for a in range(n):
            _gather_start(srcs[a], outs[a], sems[0], sems[1], 6 * a)

    def finish(srcs, outs, sems):
        for a in range(n):
            _gather_finish(srcs[a], outs[a], sems[0], sems[1], 6 * a)

    outs = [jax.ShapeDtypeStruct((a.shape[0], 4) + a.shape[1:], a.dtype) for a in arrs]
    return _Side(list(arrs), outs, [pltpu.SemaphoreType.DMA((6 * n,)), pltpu.SemaphoreType.DMA((6 * n,))], start, finish)


def _put_own(st, arr):
    kchip = 2 * lax.axis_index("x") + lax.axis_index("y")
    return lax.dynamic_update_slice(st, arr[:, None], (0, kchip) + (0,) * (arr.ndim - 1))


def _chip_copies(src, out, ssem, rsem, base=0):
    x, y, c, chips = _place()
    return [pltpu.make_async_remote_copy(src_ref=src.at[2 * cx + cy], dst_ref=out.at[j], send_sem=ssem.at[base + j],
                                         recv_sem=rsem.at[base + j], device_id=(cx, cy, c), device_id_type=MESH)
            for j, (cx, cy) in enumerate(chips)]


def _chips_side(parts):
    n = len(parts)

    def start(srcs, outs, sems):
        for a in range(n):
            for cp in _chip_copies(srcs[a], outs[a], sems[0], sems[1], 3 * a):
                cp.start()

    def finish(srcs, outs, sems):
        for a in range(n):
            for cp in _chip_copies(srcs[a], outs[a], sems[0], sems[1], 3 * a):
                cp.wait()

    return _Side(list(parts), [jax.ShapeDtypeStruct((3,) + p.shape[1:], p.dtype) for p in parts],
                 [pltpu.SemaphoreType.DMA((3 * n,)), pltpu.SemaphoreType.DMA((3 * n,))], start, finish)


def _gather_chips(arr, name):
    def body(src, out, ssem, rsem):
        _gather_start(src, out, ssem, rsem, 0)
        _gather_finish(src, out, ssem, rsem, 0)

    st = pl.pallas_call(
        body, name=name, in_specs=[ANY], out_specs=ANY,
        out_shape=jax.ShapeDtypeStruct((arr.shape[0], 4) + arr.shape[1:], arr.dtype),
        scratch_shapes=[pltpu.SemaphoreType.DMA((6,)), pltpu.SemaphoreType.DMA((6,))],
    )(arr)
    return _put_own(st, arr)


def _pair_swap(g0, g1, name):
    def body(src0, src1, out, ssem, rsem):
        x, y, c, _ = _place()

        def swap(src):
            cp = pltpu.make_async_remote_copy(src_ref=src, dst_ref=out, send_sem=ssem, recv_sem=rsem,
                                              device_id=(x, y, 1 - c), device_id_type=MESH)
            cp.start()
            cp.wait()

        @pl.when(c == 0)
        def _():
            swap(src1)

        @pl.when(c == 1)
        def _():
            swap(src0)

    return pl.pallas_call(
        body, name=name, in_specs=[ANY, ANY], out_specs=ANY, out_shape=jax.ShapeDtypeStruct(g0.shape, g0.dtype),
        scratch_shapes=[pltpu.SemaphoreType.DMA, pltpu.SemaphoreType.DMA],
    )(g0, g1)


def _chip_exchange(p, name):
    def body(src, out, ssem, rsem):
        cps = _chip_copies(src, out, ssem, rsem)
        for cp in cps:
            cp.start()
        for cp in cps:
            cp.wait()

    return pl.pallas_call(
        body, name=name, in_specs=[ANY], out_specs=ANY, out_shape=jax.ShapeDtypeStruct((3,) + p.shape[1:], p.dtype),
        scratch_shapes=[pltpu.SemaphoreType.DMA((3,)), pltpu.SemaphoreType.DMA((3,))],
    )(p)


def _pair_share(r, name):
    def body(src, out, ssem, rsem):
        x, y, c, _ = _place()
        cp = pltpu.make_async_remote_copy(src_ref=src, dst_ref=out, send_sem=ssem, recv_sem=rsem,
                                          device_id=(x, y, 1 - c), device_id_type=MESH)
        cp.start()
        cp.wait()

    theirs = pl.pallas_call(
        body, name=name, in_specs=[ANY], out_specs=ANY, out_shape=jax.ShapeDtypeStruct(r.shape, r.dtype),
        scratch_shapes=[pltpu.SemaphoreType.DMA, pltpu.SemaphoreType.DMA],
    )(r)
    first = lax.axis_index("c") == 0
    return jnp.stack([jnp.where(first, r, theirs), jnp.where(first, theirs, r)])


def _sum_pair(g0, g1, got, sel, name):
    four, rows, cols = g0.shape
    flat = four * rows
    tile = _pick(flat, ROW_TILES)

    def body(sel_ref, a0_ref, a1_ref, b_ref, o_ref):
        mine = jnp.where(sel_ref[0] == 0, a0_ref[...], a1_ref[...])
        o_ref[...] = (mine.astype(F32) + b_ref[...].astype(F32)).astype(BF16)

    def layer(l):
        return pl.BlockSpec((tile, cols), lambda i, s: (jnp.where(s[0] == l, i, 0), 0))

    blk = pl.BlockSpec((tile, cols), lambda i, s: (i, 0))
    return pl.pallas_call(
        body, name=name, out_shape=jax.ShapeDtypeStruct((flat, cols), BF16),
        grid_spec=pltpu.PrefetchScalarGridSpec(
            num_scalar_prefetch=1, grid=(flat // tile,), in_specs=[layer(0), layer(1), blk], out_specs=blk),
        compiler_params=_params(("arbitrary",)),
    )(sel, g0.reshape(flat, cols), g1.reshape(flat, cols), got.reshape(flat, cols)).reshape(four, rows, cols)


def _sum4(a, b, sel, name):
    _, rows, cols = a.shape
    tile = _pick(rows, ROW_TILES)

    def body(sel_ref, a_ref, b0_ref, b1_ref, b2_ref, o_ref):
        acc = a_ref[...].astype(F32) + b0_ref[...].astype(F32)
        acc = acc + b1_ref[...].astype(F32)
        o_ref[...] = acc + b2_ref[...].astype(F32)

    bspec = [pl.BlockSpec((None, tile, cols), lambda i, s, j=j: (j, i, 0)) for j in range(3)]
    return pl.pallas_call(
        body, name=name, out_shape=jax.ShapeDtypeStruct((rows, cols), F32),
        grid_spec=pltpu.PrefetchScalarGridSpec(
            num_scalar_prefetch=1, grid=(rows // tile,),
            in_specs=[pl.BlockSpec((None, tile, cols), lambda i, s: (s[0], i, 0))] + bspec,
            out_specs=pl.BlockSpec((tile, cols), lambda i, s: (i, 0))),
        compiler_params=_params(("parallel",)),
    )(sel, a, b, b, b)


def _reduce_scatter(g0, g1, name):
    part = _reduce_pair(g0, g1, name)
    return _reduce_chips(part, _chip_exchange(part, name + "_chips"), name)


def _reduce_pair(g0, g1, name):
    c = lax.axis_index("c").astype(jnp.int32).reshape(1)
    got = _pair_swap(g0, g1, name + "_pair")
    return _sum_pair(g0, g1, got, c, name + "_add2")


def _reduce_chips(part, others, name):
    k = (2 * lax.axis_index("x") + lax.axis_index("y")).astype(jnp.int32).reshape(1)
    total = _sum4(part, others, k, name + "_add4")
    return _pair_share(total, name + "_share")


class _EarlyReduce:
    def __init__(self, names):
        self.names, self.grads, self.part, self.others = names, None, {}, {}

    def early(self, n, g0):
        if n in self.names:
            shards = [g if n == "w_in" else _restack(n, g) for g in (g0, self.grads[1][n])]
            self.part[n] = _reduce_pair(*shards, "rs_" + n)

    def side(self, names):
        return _chips_side([self.part[n] for n in names])

    def took(self, names, brought):
        for n, o in zip(names, brought):
            self.others[n] = o

    def finish(self, n):
        return _reduce_chips(self.part[n], self.others[n], "rs_" + n)


def _all_reduce_small(buf, name):
    rows = buf.shape[0]

    def body(src, out, slots, ssem, rsem):
        x, y, c, _ = _place()
        me = 4 * x + 2 * y + c
        slots[me] = src[...]
        cps = []
        for j in range(1, 8):
            px, py, pc = x ^ (j >> 2), y ^ ((j >> 1) & 1), c ^ (j & 1)
            cps.append(pltpu.make_async_remote_copy(src_ref=src, dst_ref=slots.at[me], send_sem=ssem.at[j - 1],
                                                    recv_sem=rsem.at[j - 1], device_id=(px, py, pc), device_id_type=MESH))
        for cp in cps:
            cp.start()
        for j in range(1, 8):
            peer = me ^ j
            pltpu.make_async_remote_copy(src_ref=src, dst_ref=slots.at[peer], send_sem=ssem.at[j - 1], recv_sem=rsem.at[j - 1],
                                         device_id=(x, y, c), device_id_type=MESH).wait_recv()
        for cp in cps:
            cp.wait_send()
        acc = slots[0]
        for d in range(1, 8):
            acc = acc + slots[d]
        out[...] = acc

    vm = pl.BlockSpec(memory_space=pltpu.VMEM)
    return pl.pallas_call(
        body, name=name, in_specs=[vm], out_specs=vm, out_shape=jax.ShapeDtypeStruct((rows, LANES), F32),
        scratch_shapes=[pltpu.VMEM((8, rows, LANES), F32), pltpu.SemaphoreType.DMA((7,)), pltpu.SemaphoreType.DMA((7,))],
    )(buf)


def _pack(arrs):
    flat = jnp.concatenate([a.astype(F32).reshape(-1) for a in arrs])
    n = flat.shape[0]
    padded = -(-n // (8 * LANES)) * (8 * LANES)
    return jnp.pad(flat, (0, padded - n)).reshape(padded // LANES, LANES)


def _unpack(buf, like):
    flat = buf.reshape(-1)
    out, pos = [], 0
    for a in like:
        out.append(flat[pos:pos + a.size].reshape(a.shape))
        pos += a.size
    return out


def _stride(t2d, dil):
    t, w = t2d.shape
    b = t // SEQ
    return t2d.reshape(b, SEQ // dil, dil, w).transpose(0, 2, 1, 3).reshape(b * dil, SEQ // dil, w)


def _unstride(t3d, dil):
    bb, n, w = t3d.shape
    b = bb // dil
    return t3d.reshape(b, dil, n, w).transpose(0, 2, 1, 3).reshape(b * SEQ, w)


def _stat_cols(st, dil, heads):
    s3 = _stride(st, dil)
    return s3.transpose(0, 2, 1)[..., None]


def _stat_rows(col):
    bb, h, n, _ = col.shape
    return col.reshape(bb, h, 1, n)


def _scan_params(dtp, alog, dm):
    t = dtp.shape[0]
    g, hg = SSD_GROUPS, dm["HG"]
    dt4 = dtp[:, :dm["H2"]].reshape(t, 2, g, hg)
    dtg = dt4.transpose(1, 2, 0, 3)
    dttg = jnp.pad(dt4.transpose(1, 2, 3, 0), ((0, 0), (0, 0), (0, 8 - hg), (0, 0)))
    al = alog.reshape(2, g, 1, hg)
    alt = jnp.pad(alog.reshape(2, g, hg, 1), ((0, 0), (0, 0), (0, 8 - hg), (0, 0)))
    return dtg, dttg, al, alt


def _layer_fwd(x, wl, tabs, dm, li, late=None):
    d = dm["D"]
    nm = f"l{li}_"
    h = _rms_fwd(x, wl["g_mix"], nm + "rms1")
    late = late or {}
    qkv0 = dm["OFF_QKV"] // HEAD_DIM
    rope = (tabs[0], qkv0, qkv0 + 3 * dm["NG"] * DIL_HEADS + WIN_Q_HEADS + WIN_KV_HEADS)
    if "proj" not in late:
        proj = _mm(h, wl["w_main"], tb=True, rope=rope, name=nm + "proj")
    else:
        proj, brought = _mm(h, wl["w_main"], tb=True, rope=rope, side=late["proj"][0], name=nm + "proj")
        wl.update(late["proj"][1](brought))
    dtr = _mm(h, wl["w_dt"], tb=True, out_dtype=F32, name=nm + "proj_dt")
    cpre, u = _conv_fwd(proj, dm["OFF_XBC"], wl["conv_w"], wl["conv_b"], dm["XBC"], nm + "conv")
    dtp = _dt_prep(dtr, wl["dt_bias"], nm + "dt")
    sp = _scan_params(dtp, wl["a_log"], dm)
    scans = []
    for dirn, key in ((0, "scan_f"), (1, "scan_b")):
        if key in late:
            own, brought = _scan_fwd(u, *sp, dirn, dm, nm + key, side=late[key][0])
            wl.update(late[key][1](brought))
        else:
            own = _scan_fwd(u, *sp, dirn, dm, nm + key)
        scans.append(own)
    (y_f, s_f), (y_b, s_b) = scans
    y_a = _ssd_out_fwd(y_f, y_b, u, proj, wl["d_cols"], wl["ssd_norm"], dm["HI"], nm + "ssd_out")
    qkv = proj[:, dm["OFF_QKV"]:dm["OFF_QKV"] + dm["QW"]]
    ng, dw = dm["NG"], dm["DW"]
    outs, lses, xgs = [], [], []
    for gi, (window, dil) in enumerate(DIL_PATTERNS):
        cols = [qkv[:, s * ng * dw + gi * dw:s * ng * dw + (gi + 1) * dw] for s in range(3)]
        xg = _stride(jnp.concatenate(cols, axis=1), dil)
        o, lse = _attn_fwd(xg, xg, xg, 0, 1, 2, DIL_HEADS, 1, SEQ // dil, window // (2 * dil), None, F32, nm + f"dil{gi}")
        xgs.append(xg)
        outs.append(_unstride(o, dil))
        lses.append(_unstride(lse[..., 0].transpose(0, 2, 1), dil))
    y_bm, lse_tot = _dil_combine(outs, lses, nm + "dil_mix")
    bsz = x.shape[0] // SEQ
    xw = qkv[:, dm["QKVD"]:].reshape(bsz, SEQ, dm["WQ"] + 2 * dm["WK"])
    rep = WIN_Q_HEADS // WIN_KV_HEADS
    y_c3, lse_w = _attn_fwd(xw, xw, xw, 0, rep, rep + 1, WIN_Q_HEADS, rep, SEQ, WIN_HALF,
                            wl["sink"].reshape(WIN_Q_HEADS, 1, 1), BF16, nm + "win")
    y_c = y_c3.reshape(x.shape[0], dm["WQ"])
    pa = _mm(y_a, wl["w_a"], out_dtype=F32, name=nm + "pa")
    pb = _mm(y_bm, wl["w_b"], out_dtype=F32, name=nm + "pb")
    pc = _mm(y_c, wl["w_c"], out_dtype=F32, name=nm + "pc")
    merged = _gate_fwd(pa, pb, pc, proj, dm["OFF_GATE"], d, nm + "gate")
    x1 = _mm(merged, wl["w_out"], add=x, out_dtype=F32, name=nm + "out")
    hm = _rms_fwd(x1, wl["g_mlp"], nm + "rms2")
    up, act = _mm(hm, wl["w_up"], epi="relu2", name=nm + "up")
    x2 = _mm(act, wl["w_down"], add=x1, out_dtype=F32, name=nm + "down")
    saved = dict(x=x, h=h, proj=proj, dtr=dtr, cpre=cpre, u=u, dtp=dtp, y_f=y_f, y_b=y_b, s_f=s_f, s_b=s_b, y_a=y_a,
                 xw=xw, xgs=xgs, y_bm=y_bm, lse_tot=lse_tot, y_c=y_c, lse_w=lse_w, pa=pa, pb=pb, pc=pc,
                 merged=merged, x1=x1, hm=hm, up=up, act=act)
    return x2, saved


def _layer_bwd(dx2, wl, sv, tabs, dm, li, early=None):
    d = dm["D"]
    t = dx2.shape[0]
    bsz = t // SEQ
    nm = f"l{li}b_"
    gr = {}
    dup = _mm(dx2, wl["w_down"], tb=True, aux=sv["up"], epi="relu2_bwd", name=nm + "dup")
    gr["w_down"] = _mm(sv["act"], dx2, ta=True, name=nm + "gw_down")
    dhm = _mm(dup, wl["w_up"], tb=True, out_dtype=F32, name=nm + "dhm")
    gr["w_up"] = _mm(sv["hm"], dup, ta=True, name=nm + "gw_up")
    if early is not None:
        early.early("w_down", gr["w_down"])
        early.early("w_up", gr["w_up"])
    dx1, gmlp = _rms_bwd(sv["x1"], wl["g_mlp"], dhm, dx2, nm + "rms2")
    gr["g_mlp"] = gmlp[0]
    dmerged = _mm(dx1, wl["w_out"], tb=True, out_dtype=F32, name=nm + "dmerged")
    gr["w_out"] = _mm(sv["merged"], dx1, ta=True, name=nm + "gw_out")
    dpa, dpb, dpc, dg0, dg1, dg2 = _gate_bwd(dmerged, sv["pa"], sv["pb"], sv["pc"], sv["proj"], dm["OFF_GATE"], d, nm + "gate")
    dya = _mm(dpa, wl["w_a"], tb=True, out_dtype=F32, name=nm + "dya")
    gr["w_a"] = _mm(sv["y_a"], dpa, ta=True, name=nm + "gw_a")
    dyb = _mm(dpb, wl["w_b"], tb=True, out_dtype=F32, name=nm + "dyb")
    gr["w_b"] = _mm(sv["y_bm"], dpb, ta=True, name=nm + "gw_b")
    dyc = _mm(dpc, wl["w_c"], tb=True, out_dtype=F32, name=nm + "dyc")
    gr["w_c"] = _mm(sv["y_c"], dpc, ta=True, name=nm + "gw_c")
    if early is not None:
        for n in ("w_out", "w_a", "w_b", "w_c"):
            early.early(n, gr[n])
    ng, dw = dm["NG"], dm["DW"]
    xw = sv["xw"]
    rep = WIN_Q_HEADS // WIN_KV_HEADS
    delta_w = _head_dots(dyc, sv["y_c"], WIN_Q_HEADS, nm + "win_delta")
    dl_col = _stat_cols(delta_w, 1, WIN_Q_HEADS)
    lse_w = sv["lse_w"]
    dyc3 = dyc.reshape(bsz, SEQ, dm["WQ"])
    wargs = (0, rep, rep + 1, WIN_Q_HEADS, rep, SEQ, WIN_HALF)
    dq_w = _attn_dq(xw, xw, xw, dyc3, lse_w, dl_col, *wargs, nm + "win_dq")
    dk_w, dv_w = _attn_dkv(xw, xw, xw, dyc3, _stat_rows(lse_w), _stat_rows(dl_col), *wargs, nm + "win_dkv")
    lse_w2 = lse_w[..., 0].transpose(0, 2, 1).reshape(t, WIN_Q_HEADS)
    gr["sink"] = _sink_grad(lse_w2, delta_w, wl["sink"], nm + "sink")[0]
    delta_d = _head_dots(dyb, sv["y_bm"], DIL_HEADS, nm + "dil_delta")
    dqs, dks, dvs = [], [], []
    for gi, (window, dil) in enumerate(DIL_PATTERNS):
        xg = sv["xgs"][gi]
        n = SEQ // dil
        do_g = _stride(dyb, dil)
        lse_c = _stat_cols(sv["lse_tot"], dil, DIL_HEADS)
        dl_c = _stat_cols(delta_d, dil, DIL_HEADS)
        dargs = (0, 1, 2, DIL_HEADS, 1, n, window // (2 * dil))
        dq = _attn_dq(xg, xg, xg, do_g, lse_c, dl_c, *dargs, nm + f"dil{gi}_dq")
        dk, dv = _attn_dkv(xg, xg, xg, do_g, _stat_rows(lse_c), _stat_rows(dl_c), *dargs, nm + f"dil{gi}_dkv")
        dqs.append(_unstride(dq, dil))
        dks.append(_unstride(dk, dil))
        dvs.append(_unstride(dv, dil))
    dqkv_r = jnp.concatenate(dqs + dks + dvs + [dq_w.reshape(t, dm["WQ"]), dk_w.reshape(t, dm["WK"]), dv_w.reshape(t, dm["WK"])],
                             axis=1)
    dqkv = _rope(dqkv_r, 0, dm["QW"] // HEAD_DIM, tabs[1], dm, nm + "rope")
    hi, gn = dm["HI"], dm["GN"]
    dyt, dxs0, dz, gnorm, dd_cols = _ssd_out_bwd(dya, sv["y_f"], sv["y_b"], sv["u"], sv["proj"], wl["d_cols"],
                                                          wl["ssd_norm"], hi, nm + "ssd_out")
    gr["ssd_norm"] = gnorm[0]
    gr["d_skip"] = dd_cols.reshape(SSD_HEADS, SSD_HEAD_DIM).sum(axis=1)
    sp = _scan_params(sv["dtp"], wl["a_log"], dm)
    if early is None:
        dxs1, db1, dc1, rq_f, xdx_f = _scan_bwd(sv["u"], *sp, dyt, sv["s_f"], (dxs0, None, None), 0, dm, nm + "scan_f")
        dxs2, db2, dc2, rq_r, xdx_r = _scan_bwd(sv["u"], *sp, dyt, sv["s_b"], (dxs1, db1, dc1), 1, dm, nm + "scan_b")
    else:
        ride_f, ride_b = ("w_up", "w_out", "w_a"), ("w_down", "w_b", "w_c")
        (dxs1, db1, dc1, rq_f, xdx_f), got = _scan_bwd(sv["u"], *sp, dyt, sv["s_f"], (dxs0, None, None), 0, dm, nm + "scan_f",
                                                       side=early.side(ride_f))
        early.took(ride_f, got)
        (dxs2, db2, dc2, rq_r, xdx_r), got = _scan_bwd(sv["u"], *sp, dyt, sv["s_b"], (dxs1, db1, dc1), 1, dm, nm + "scan_b",
                                                       side=early.side(ride_b))
        early.took(ride_b, got)

    def heads(a):
        return a.transpose(1, 0, 2).reshape(t, SSD_HEADS)

    zpad = jnp.zeros((t, LANES - dm["H2"]), F32)
    zh = jnp.zeros((t, SSD_HEADS), F32)
    rqf_p = jnp.concatenate([heads(rq_f), zh, zpad], axis=1)
    rqr_p = jnp.concatenate([zh, heads(rq_r), zpad], axis=1)
    xdx_p = jnp.concatenate([heads(xdx_f), heads(xdx_r), zpad], axis=1)
    ddtr, dbias, dalog = _ssd_param_bwd(rqf_p, rqr_p, xdx_p, sv["dtp"], sv["dtr"], wl["dt_bias"], wl["a_log_p"], nm + "ssd_par")
    gr["dt_bias"] = dbias[0, :dm["H2"]].reshape(2, SSD_HEADS)
    gr["a_log"] = dalog[0, :dm["H2"]].reshape(2, SSD_HEADS)
    du = jnp.concatenate([dxs2, db2, dc2], axis=1)
    dxbc, gr["conv_w"], gcb = _conv_bwd(du, sv["cpre"], sv["proj"], dm["OFF_XBC"], wl["conv_w"], nm + "conv")
    gr["conv_b"] = gcb[0]
    dproj = jnp.concatenate([dz, dxbc, dqkv, dg0, dg1, dg2], axis=1)
    dh_dt = _mm(ddtr, wl["w_dt"], out_dtype=F32, name=nm + "dh_dt")
    gw_main = _mm(dproj, sv["h"], ta=True, name=nm + "gw_main")
    gw_dt = _mm(ddtr, sv["h"], ta=True, name=nm + "gw_dt")
    o1, h2 = dm["OFF_QKV"], dm["H2"]
    gw_in_t = jnp.concatenate([gw_main[:o1], gw_dt[:h2], gw_main[o1:]], axis=0)
    gr["w_in"] = gw_in_t.reshape(4, (dm["NM"] + h2) // 4, d)
    if early is None:
        dh = _mm(dproj, wl["w_main"], add=dh_dt, out_dtype=F32, name=nm + "dh")
    else:
        early.early("w_in", gr["w_in"])
        dh, got = _mm(dproj, wl["w_main"], add=dh_dt, out_dtype=F32, side=early.side(("w_in",)), name=nm + "dh")
        early.took(("w_in",), got)
    dx, gmix = _rms_bwd(sv["x"], wl["g_mix"], dh, dx1, nm + "rms1")
    gr["g_mix"] = gmix[0]
    return dx, gr


def _layer_weights(full, li, dm):
    st = full["w_in"]
    o1 = dm["OFF_QKV"]
    h2 = dm["H2"]
    d = dm["D"]
    w_in_t = st[li].reshape(4 * st.shape[2], d)
    wl = dict(
        w_main=jnp.concatenate([w_in_t[:o1], w_in_t[o1 + h2:]], axis=0),
        w_dt=jnp.pad(w_in_t[o1:o1 + h2], ((0, LANES - h2), (0, 0))),
        **{n: (full[n][li] if n in full else None) for n in ("w_a", "w_b", "w_c", "w_out", "w_up", "w_down")},
        conv_w=full["conv_w"][li], conv_b=full["conv_b"][li][None, :],
        g_mix=full["g_mix"][li][None, :], g_mlp=full["g_mlp"][li][None, :], ssd_norm=full["ssd_norm"][li][None, :],
        d_cols=jnp.repeat(full["d_skip"][li], SSD_HEAD_DIM)[None, :],
        sink=full["sink"][li][None, :],
        a_log=full["a_log"][li],
        a_log_p=jnp.pad(full["a_log"][li].reshape(1, h2), ((0, 0), (0, LANES - h2))),
        dt_bias=jnp.pad(full["dt_bias"][li].reshape(1, h2), ((0, 0), (0, LANES - h2))),
    )
    assert wl["w_main"].shape == (dm["NM"], d)
    return wl


def _local_step(x, target, full, depth, late=None, early=None):
    bsz, seq, d = x.shape
    assert seq == SEQ
    dm = _dims(d)
    assert dm["OFF_GATE"] % d == 0 and dm["HI"] % (dm["HG"] * SSD_HEAD_DIM) == 0 and dm["H2"] <= LANES
    tabs = (_rope_tables(1.0), _rope_tables(-1.0))
    xt = x.reshape(bsz * seq, d)
    wls, saves = [], []
    for li in range(depth):
        wl = _layer_weights(full, li, dm)
        hosts = {}
        if li == 0 and late is not None:
            for key, (side, arrived) in late.items():
                hosts[key] = (side, lambda brought, arrived=arrived: {n: full[n][0] for n in arrived(brought)})
        xt, sv = _layer_fwd(xt, wl, tabs, dm, li, late=hosts)
        wls.append(wl)
        saves.append(sv)
    dx, loss, g_final = _loss_head(xt, full["g_final"][None, :], target.reshape(bsz * seq, d), "loss_head")
    grads = [None] * depth
    if early is not None:
        early.grads = grads
    for li in reversed(range(depth)):
        dx, grads[li] = _layer_bwd(dx, wls[li], saves[li], tabs, dm, li, early=early if li == 0 else None)
    return loss, dx.reshape(bsz, seq, d), grads, g_final[0]


BIG = ("w_in", "w_a", "w_b", "w_c", "w_out", "w_up", "w_down")
COL_SHARDED = ("w_in", "w_b", "w_up")
SMALL = ("g_mix", "conv_w", "conv_b", "dt_bias", "a_log", "d_skip", "ssd_norm", "sink", "g_mlp", "g_final")
ORDER = ("g_mix", "w_in", "conv_w", "conv_b", "dt_bias", "a_log", "d_skip", "ssd_norm", "w_a", "w_b", "w_c", "sink",
         "w_out", "g_mlp", "w_up", "w_down", "g_final")


def _unstack(name, st):
    nl, _, r, c = st.shape
    if name in COL_SHARDED:
        return jnp.moveaxis(st, 1, 2).reshape(nl, r, 4 * c)
    return st.reshape(nl, 4 * r, c)


def _restack(name, gfull):
    r, c = gfull.shape
    if name in COL_SHARDED:
        return jnp.moveaxis(gfull.reshape(r, 4, c // 4), 1, 0)
    return gfull.reshape(4, r // 4, c)


def kernel(x, g_mix, w_in, conv_w, conv_b, dt_bias, a_log, d_skip, ssd_norm, w_a, w_b, w_c, sink, w_out, g_mlp, w_up, w_down, g_final, loss_target, m_g_mix, m_w_in, m_conv_w, m_conv_b, m_dt_bias, m_a_log, m_d_skip, m_ssd_norm, m_w_a, m_w_b, m_w_c, m_sink, m_w_out, m_g_mlp, m_w_up, m_w_down, m_g_final, v_g_mix, v_w_in, v_conv_w, v_conv_b, v_dt_bias, v_a_log, v_d_skip, v_ssd_norm, v_w_a, v_w_b, v_w_c, v_sink, v_w_out, v_g_mlp, v_w_up, v_w_down, v_g_final):
    w = dict(g_mix=g_mix, w_in=w_in, conv_w=conv_w, conv_b=conv_b, dt_bias=dt_bias, a_log=a_log, d_skip=d_skip,
             ssd_norm=ssd_norm, w_a=w_a, w_b=w_b, w_c=w_c, sink=sink, w_out=w_out, g_mlp=g_mlp, w_up=w_up, w_down=w_down,
             g_final=g_final)
    m = dict(g_mix=m_g_mix, w_in=m_w_in, conv_w=m_conv_w, conv_b=m_conv_b, dt_bias=m_dt_bias, a_log=m_a_log,
             d_skip=m_d_skip, ssd_norm=m_ssd_norm, w_a=m_w_a, w_b=m_w_b, w_c=m_w_c, sink=m_sink, w_out=m_w_out,
             g_mlp=m_g_mlp, w_up=m_w_up, w_down=m_w_down, g_final=m_g_final)
    v = dict(g_mix=v_g_mix, w_in=v_w_in, conv_w=v_conv_w, conv_b=v_conv_b, dt_bias=v_dt_bias, a_log=v_a_log,
             d_skip=v_d_skip, ssd_norm=v_ssd_norm, w_a=v_w_a, w_b=v_w_b, w_c=v_w_c, sink=v_sink, w_out=v_w_out,
             g_mlp=v_g_mlp, w_up=v_w_up, w_down=v_w_down, g_final=v_g_final)
    depth = w_in.shape[0]
    assert depth == 2
    kchip = 2 * lax.axis_index("x") + lax.axis_index("y")

    full = {n: w[n] for n in SMALL if n != "conv_w"}
    tr = lambda a: jnp.swapaxes(a, 1, 2)
    full["w_in"] = _gather_chips(tr(w_in).astype(BF16), "gather_w_in")
    cw = _gather_chips(conv_w, "gather_conv_w")
    full["conv_w"] = jnp.moveaxis(cw, 1, 2).reshape(depth, CONV_WIDTH, 4 * conv_w.shape[2])
    riders = {"proj": ("w_up",), "scan_f": ("w_down",), "scan_b": ("w_a", "w_b", "w_c", "w_out")}
    late = {}
    for key, names in riders.items():
        shards = [w[n].astype(BF16) for n in names]

        def arrived(brought, names=names, shards=shards):
            for n, shard, st in zip(names, shards, brought):
                full[n] = _unstack(n, _put_own(st, shard))
            return names

        late[key] = (_gather_side(shards), arrived)

    early = _EarlyReduce(BIG)
    loss_part, grad_x, grads, gg_final = _local_step(x, loss_target, full, depth, late=late, early=early)
    gsh = {n: early.finish(n) for n in BIG}
    small_names = [n for n in SMALL if n != "g_final"]
    small_g = [jnp.stack([grads[li][n] for li in range(depth)]) for n in small_names] + [gg_final, loss_part[0, :1]]
    red = _unpack(_all_reduce_small(_pack(small_g), "allreduce_small"), small_g)
    for n, a in zip(small_names + ["g_final"], red):
        gsh[n] = a
    loss = red[-1][0]
    cshard = conv_w.shape[2]
    gsh["conv_w"] = lax.dynamic_slice_in_dim(gsh["conv_w"], kchip * cshard, cshard, axis=2)

    delta, new_m, new_v = {}, {}, {}
    for n in BIG:
        if n == "w_in":
            outs_t = _adamw(tr(w[n]), gsh[n], tr(m[n]), tr(v[n]), "adamw_" + n)
            delta[n], new_m[n], new_v[n] = [tr(o) for o in outs_t]
            gsh[n] = tr(gsh[n])
        else:
            delta[n], new_m[n], new_v[n] = _adamw(w[n], gsh[n], m[n], v[n], "adamw_" + n)
    sm = list(SMALL)
    packed = [_pack([d_[n] for n in sm])[None] for d_ in (w, gsh, m, v)]
    outs = [o[0] for o in _adamw(*packed, "adamw_small")]
    for dst, buf in zip((delta, new_m, new_v), outs):
        for n, a in zip(sm, _unpack(buf, [w[n] for n in sm])):
            dst[n] = a
    return (loss, grad_x, *[gsh[n] for n in ORDER], *[delta[n] for n in ORDER], *[new_m[n] for n in ORDER],
            *[new_v[n] for n in ORDER])
```

```python
import functools
import math

import jax
import jax.numpy as jnp
from jax import lax
from jax.experimental import pallas as pl
from jax.experimental.pallas import tpu as pltpu

F32 = jnp.float32
BF16 = jnp.bfloat16

SEQ = 2048
SSD_HEADS = 32
SSD_HEAD_DIM = 64
SSD_GROUPS = 8
SSD_STATE = 128
SSD_CHUNK = 128
CONV_WIDTH = 5
HEAD_DIM = 128
ROPE_DIM = 32
ROPE_THETA = 500000.0
DIL_PATTERNS = ((128, 1), (512, 4), (2048, 16))
DIL_HEADS = 8
WIN_Q_HEADS = 16
WIN_KV_HEADS = 4
WIN_HALF = 128
EPS = 1e-6
NEG_BIG = -1e30
ADAM_LR = 0.001
ADAM_B1 = 0.9
ADAM_B2 = 0.999
ADAM_EPS = 1e-08
ADAM_WD = 0.01
ADAM_STEP = 10

LANES = 128
ATT_BLK = 128
ROW_TILES = (320, 256, 128, 80, 64, 32, 16, 8)
VMEM_LIMIT = 48 * 1024 * 1024
MESH = pl.DeviceIdType.MESH
HIGHEST = lax.Precision.HIGHEST
NT = (((1,), (1,)), ((), ()))
TN = (((0,), (0,)), ((), ()))
NN = (((1,), (0,)), ((), ()))


def _dims(d_model):
    hi = SSD_HEADS * SSD_HEAD_DIM
    gn = SSD_GROUPS * SSD_STATE
    ng = len(DIL_PATTERNS)
    dw = DIL_HEADS * HEAD_DIM
    wq = WIN_Q_HEADS * HEAD_DIM
    wk = WIN_KV_HEADS * HEAD_DIM
    d = dict(D=d_model, HI=hi, GN=gn, XBC=hi + 2 * gn, H2=2 * SSD_HEADS, NG=ng, DW=dw, WQ=wq, WK=wk,
             QKVD=3 * ng * dw, QW=3 * ng * dw + wq + 2 * wk, HG=SSD_HEADS // SSD_GROUPS)
    d["OFF_XBC"] = hi
    d["OFF_QKV"] = hi + d["XBC"]
    d["OFF_GATE"] = d["OFF_QKV"] + d["QW"]
    d["NM"] = d["OFF_GATE"] + 3 * d_model
    return d


def _pick(n, prefs):
    for p in prefs:
        if n % p == 0:
            return p
    return n


def _params(sem):
    return pltpu.CompilerParams(dimension_semantics=sem, vmem_limit_bytes=VMEM_LIMIT)


def _sigmoid(x):
    return 1.0 / (1.0 + jnp.exp(-x))


class _Side:
    def __init__(self, args, outs, sems, start, finish):
        self.args, self.outs, self.sems, self.start, self.finish = args, outs, sems, start, finish


def _carry(side, body, n_in, n_out, n_scratch, grid):
    if side is None:
        return body
    n_sin, n_sout = len(side.args), len(side.outs)

    def wrapped(*refs):
        o0 = n_in + n_sin
        s0 = o0 + n_out + n_sout
        s_in, s_out, s_sem = refs[n_in:o0], refs[o0 + n_out:s0], refs[s0 + n_scratch:]
        ids = [pl.program_id(ax) for ax in range(len(grid))]
        first = functools.reduce(jnp.logical_and, [i == 0 for i in ids])
        last = functools.reduce(jnp.logical_and, [i == g - 1 for i, g in zip(ids, grid)])

        @pl.when(first)
        def _():
            side.start(s_in, s_out, s_sem)

        body(*refs[:n_in], *refs[o0:o0 + n_out], *refs[s0:s0 + n_scratch])

        @pl.when(last)
        def _():
            side.finish(s_in, s_out, s_sem)

    return wrapped


def _carry_call(side, body, *, name, grid, in_specs, out_specs, out_shape, scratch_shapes, sem, args):
    n_in, n_out = len(in_specs), len(out_specs)
    in_specs, out_specs, out_shape, scratch_shapes, args = (list(v) for v in (in_specs, out_specs, out_shape,
                                                                              scratch_shapes, args))
    body = _carry(side, body, n_in, n_out, len(scratch_shapes), grid)
    if side is not None:
        any_spec = pl.BlockSpec(memory_space=pl.ANY)
        in_specs += [any_spec] * len(side.args)
        args += list(side.args)
        out_specs += [any_spec] * len(side.outs)
        out_shape += list(side.outs)
        scratch_shapes += list(side.sems)
        sem = ("arbitrary",) * len(grid)
    res = pl.pallas_call(body, name=name, grid=grid, in_specs=in_specs, out_specs=out_specs, out_shape=out_shape,
                         scratch_shapes=scratch_shapes, compiler_params=_params(sem))(*args)
    return list(res[:n_out]), list(res[n_out:])


def _mm(a, b, *, ta=False, tb=False, add=None, aux=None, epi=None, rope=None, out_dtype=BF16, side=None, name):
    if ta:
        kdim, m = a.shape
    else:
        m, kdim = a.shape
    if tb:
        n, k2 = b.shape
    else:
        k2, n = b.shape
    assert kdim == k2, (a.shape, b.shape, ta, tb)
    tm = _pick(m, (1024, 512, 256, 128, 64, 32, 16, 8))
    tn = _pick(n, (1024, 512, 256, 128))
    tk = _pick(kdim, (2048, 1024, 512, 256, 128))
    nk = kdim // tk
    dims = (((0 if ta else 1,), (1 if tb else 0,)), ((), ()))
    n_in = 2 + (add is not None) + (aux is not None) + 3 * (rope is not None)
    n_out = 2 if epi == "relu2" else 1
    n_sin = len(side.args) if side else 0
    n_sout = len(side.outs) if side else 0
    grid = (m // tm, n // tn, nk)
    if rope is not None:
        assert epi is None and add is None and SEQ % tm == 0 and tn % HEAD_DIM == 0

    def body(*refs):
        a_ref, b_ref = refs[0], refs[1]
        pos = 2
        add_ref = aux_ref = None
        if add is not None:
            add_ref = refs[pos]
            pos += 1
        if aux is not None:
            aux_ref = refs[pos]
            pos += 1
        if rope is not None:
            tab_refs = refs[pos:pos + 3]
        out_refs = refs[n_in + n_sin:n_in + n_sin + n_out]
        acc_ref = refs[n_in + n_sin + n_out + n_sout]
        k = pl.program_id(2)
        if side is not None:
            s_in = refs[n_in:n_in + n_sin]
            s_out = refs[n_in + n_sin + n_out:n_in + n_sin + n_out + n_sout]
            s_sem = refs[n_in + n_sin + n_out + n_sout + 1:]
            i, j = pl.program_id(0), pl.program_id(1)

            @pl.when((i == 0) & (j == 0) & (k == 0))
            def _():
                side.start(s_in, s_out, s_sem)

        @pl.when(k == 0)
        def _():
            acc_ref[...] = jnp.zeros_like(acc_ref)

        acc_ref[...] += lax.dot_general(a_ref[...].astype(BF16), b_ref[...].astype(BF16), dims,
                                        preferred_element_type=F32)

        jj = pl.program_id(1)

        def rotate():
            half = ROPE_DIM // 2
            nhb = tn // HEAD_DIM
            cv, uv, dv = (r[...] for r in tab_refs)
            for hb in range(nhb):
                blk = slice(hb * HEAD_DIM, (hb + 1) * HEAD_DIM)
                rh = acc_ref[:, blk]
                rot = rh * cv + pltpu.roll(rh, HEAD_DIM - half, 1) * uv + pltpu.roll(rh, half, 1) * dv
                g = jj * nhb + hb
                out_refs[0][:, blk] = jnp.where((g >= rope[1]) & (g < rope[2]), rot, rh).astype(out_refs[0].dtype)

        @pl.when(k == nk - 1)
        def _():
            r = acc_ref[...]
            if add_ref is not None:
                r = r + add_ref[...].astype(F32)
            if rope is not None:
                nhb = tn // HEAD_DIM
                touched = (jj * nhb < rope[2]) & ((jj + 1) * nhb > rope[1])
                pl.when(touched)(rotate)

                @pl.when(jnp.logical_not(touched))
                def _():
                    out_refs[0][...] = r.astype(out_refs[0].dtype)
            elif epi == "relu2":
                out_refs[0][...] = r.astype(out_refs[0].dtype)
                out_refs[1][...] = jnp.square(jnp.maximum(r, 0.0)).astype(out_refs[1].dtype)
            elif epi == "relu2_bwd":
                out_refs[0][...] = (r * 2.0 * jnp.maximum(aux_ref[...].astype(F32), 0.0)).astype(out_refs[0].dtype)
            else:
                out_refs[0][...] = r.astype(out_refs[0].dtype)

        if side is not None:
            @pl.when((i == grid[0] - 1) & (j == grid[1] - 1) & (k == nk - 1))
            def _():
                side.finish(s_in, s_out, s_sem)

    a_spec = pl.BlockSpec((tk, tm), lambda i, j, k: (k, i)) if ta else pl.BlockSpec((tm, tk), lambda i, j, k: (i, k))
    b_spec = pl.BlockSpec((tn, tk), lambda i, j, k: (j, k)) if tb else pl.BlockSpec((tk, tn), lambda i, j, k: (k, j))
    o_spec = pl.BlockSpec((tm, tn), lambda i, j, k: (i, j))
    in_specs = [a_spec, b_spec]
    args = [a, b]
    if add is not None:
        in_specs.append(o_spec)
        args.append(add)
    if aux is not None:
        in_specs.append(o_spec)
        args.append(aux)
    if rope is not None:
        in_specs += [pl.BlockSpec((tm, HEAD_DIM), lambda i, j, k: (i % (SEQ // tm), 0))] * 3
        args += list(rope[0])
    out_shape = [jax.ShapeDtypeStruct((m, n), out_dtype)] * n_out
    out_specs = [o_spec] * n_out
    scratch = [pltpu.VMEM((tm, tn), F32)]
    sem = ("parallel", "parallel", "arbitrary")
    if side is not None:
        any_spec = pl.BlockSpec(memory_space=pl.ANY)
        in_specs += [any_spec] * n_sin
        args += list(side.args)
        out_shape += list(side.outs)
        out_specs += [any_spec] * n_sout
        scratch += list(side.sems)
        sem = ("arbitrary", "arbitrary", "arbitrary")
    res = pl.pallas_call(
        body, name=name, grid=grid, in_specs=in_specs, out_specs=out_specs, out_shape=out_shape, scratch_shapes=scratch,
        compiler_params=_params(sem),
    )(*args)
    if side is not None:
        return (res[0] if n_out == 1 else tuple(res[:n_out])), list(res[n_out:])
    return res if n_out == 2 else res[0]


def _rowwise(body, rows, fulls, outs, accs=(), *, tile, name):
    rows = [r if isinstance(r, tuple) else (r, 0, r.shape[1]) for r in rows]
    nrows = rows[0][0].shape[0]
    assert nrows % tile == 0, (nrows, tile)
    in_specs, args = [], []
    for arr, off, width in rows:
        assert arr.shape[0] == nrows and off % width == 0, (arr.shape, off, width)
        in_specs.append(pl.BlockSpec((tile, width), lambda i, o=off // width: (i, o)))
        args.append(arr)
    for arr in fulls:
        in_specs.append(pl.BlockSpec(arr.shape, lambda i, nd=arr.ndim: (0,) * nd))
        args.append(arr)
    out_specs, out_shape = [], []
    for cols, dt in outs:
        out_specs.append(pl.BlockSpec((tile, cols), lambda i: (i, 0)))
        out_shape.append(jax.ShapeDtypeStruct((nrows, cols), dt))
    for shp, dt in accs:
        out_specs.append(pl.BlockSpec(shp, lambda i, nd=len(shp): (0,) * nd))
        out_shape.append(jax.ShapeDtypeStruct(shp, dt))
    n_in, n_out = len(args), len(outs)

    def wrapped(*refs):
        acc_refs = refs[n_in + n_out:]
        if acc_refs:
            @pl.when(pl.program_id(0) == 0)
            def _():
                for r in acc_refs:
                    r[...] = jnp.zeros_like(r)
        body(*refs)

    return pl.pallas_call(
        wrapped, name=name, grid=(nrows // tile,), in_specs=in_specs, out_specs=out_specs, out_shape=out_shape,
        compiler_params=_params(("arbitrary",)),
    )(*args)


def _rms_fwd(x, g, name):
    def body(x_ref, g_ref, h_ref):
        xv = x_ref[...]
        rstd = lax.rsqrt(jnp.mean(xv * xv, axis=-1, keepdims=True) + EPS)
        h_ref[...] = (xv * rstd * g_ref[...]).astype(BF16)

    return _rowwise(body, [x], [g], [(x.shape[1], BF16)], tile=256, name=name)[0]


def _rms_bwd(x, g, dh, dres, name):
    def body(x_ref, dh_ref, dres_ref, g_ref, dx_ref, dg_ref):
        xv = x_ref[...]
        dv = dh_ref[...]
        rstd = lax.rsqrt(jnp.mean(xv * xv, axis=-1, keepdims=True) + EPS)
        xn = xv * rstd
        dg_ref[...] += jnp.sum(dv * xn, axis=0, keepdims=True)
        dn = dv * g_ref[...]
        dx_ref[...] = dres_ref[...] + rstd * (dn - xn * jnp.mean(dn * xn, axis=-1, keepdims=True))

    d = x.shape[1]
    return _rowwise(body, [x, dh, dres], [g], [(d, F32)], [((1, d), F32)], tile=256, name=name)


def _loss_head(x, g, target, name):
    d = x.shape[1]

    def body(x_ref, t_ref, g_ref, dx_ref, loss_ref, dg_ref):
        xv = x_ref[...]
        rstd = lax.rsqrt(jnp.mean(xv * xv, axis=-1, keepdims=True) + EPS)
        xn = xv * rstd
        err = xn * g_ref[...] - t_ref[...]
        loss_ref[...] += jnp.full((1, LANES), 0.5 / d, F32) * jnp.sum(err * err)
        dy = err * (1.0 / d)
        dg_ref[...] += jnp.sum(dy * xn, axis=0, keepdims=True)
        dn = dy * g_ref[...]
        dx_ref[...] = rstd * (dn - xn * jnp.mean(dn * xn, axis=-1, keepdims=True))

    return _rowwise(body, [x, target], [g], [(d, F32)], [((1, LANES), F32), ((1, d), F32)], tile=256, name=name)


def _rope_tables(sign):
    half = ROPE_DIM // 2
    inv = ROPE_THETA ** (-jnp.arange(0, ROPE_DIM, 2, dtype=F32) / ROPE_DIM)
    ang = jnp.arange(SEQ, dtype=F32)[:, None] * inv[None, :]
    cos, sin = jnp.cos(ang), jnp.sin(ang) * sign
    zeros = jnp.zeros((SEQ, HEAD_DIM - ROPE_DIM), F32)
    zh = jnp.zeros((SEQ, half), F32)
    c = jnp.concatenate([cos, cos, zeros + 1.0], axis=1)
    s_up = jnp.concatenate([-sin, zh, zeros], axis=1)
    s_dn = jnp.concatenate([zh, sin, zeros], axis=1)
    return c, s_up, s_dn


def _rope(src, off, nblk, tabs, dm, name):
    t = src.shape[0]
    tq = 256
    half = ROPE_DIM // 2
    win0 = 3 * dm["NG"] * DIL_HEADS
    win1 = win0 + WIN_Q_HEADS + WIN_KV_HEADS
    qw = nblk * HEAD_DIM
    assert nblk == dm["QW"] // HEAD_DIM
    wb = next(c for c in (1024, 768, 512, 384, 256, 128) if off % c == 0 and qw % c == 0)
    reps = wb // HEAD_DIM
    sb = SEQ // tq
    flag = (jnp.arange(qw, dtype=jnp.int32) // HEAD_DIM < win1).astype(F32)[None, :]

    def body(x_ref, c_ref, up_ref, dn_ref, f_ref, o_ref):
        xv = x_ref[...].astype(F32)

        def wide(r):
            v = r[...]
            return v if reps == 1 else jnp.concatenate([v] * reps, axis=1)

        rot = xv * wide(c_ref) + pltpu.roll(xv, wb - half, 1) * wide(up_ref) + pltpu.roll(xv, half, 1) * wide(dn_ref)
        o_ref[...] = jnp.where(f_ref[...] > 0.5, rot, xv).astype(BF16)

    tab_spec = pl.BlockSpec((tq, HEAD_DIM), lambda i, j: (i % sb, 0))
    return pl.pallas_call(
        body, name=name, grid=(t // tq, qw // wb),
        in_specs=[pl.BlockSpec((tq, wb), lambda i, j, o=off // wb: (i, o + j)), tab_spec, tab_spec, tab_spec,
                  pl.BlockSpec((1, wb), lambda i, j: (0, j))],
        out_specs=pl.BlockSpec((tq, wb), lambda i, j: (i, j)),
        out_shape=jax.ShapeDtypeStruct((t, qw), BF16),
        compiler_params=_params(("parallel", "parallel")),
    )(src, *tabs, flag)


def _band_mask(rows_start, cols_start, nrows, ncols, w, n, rows_are_q):
    r = rows_start + lax.broadcasted_iota(jnp.int32, (nrows, ncols), 0)
    c = cols_start + lax.broadcasted_iota(jnp.int32, (nrows, ncols), 1)
    del rows_are_q
    return (jnp.abs(r - c) <= w) & (c >= 0) & (c < n)


def _nbr_specs(make, nb):
    if nb == 1:
        return [make(lambda i: i)]
    return [make(lambda i: jnp.maximum(i - 1, 0)), make(lambda i: i), make(lambda i: jnp.minimum(i + 1, nb - 1))]


def _cat(refs, axis):
    vals = [r[...] for r in refs]
    return vals[0] if len(vals) == 1 else jnp.concatenate(vals, axis=axis)


def _head(ref, h):
    return ref[:, h * HEAD_DIM:(h + 1) * HEAD_DIM]


def _head_cat(refs, h, axis=0):
    vals = [_head(r, h) for r in refs]
    return vals[0] if len(vals) == 1 else jnp.concatenate(vals, axis=axis)


def _attn_fwd(qa, ka, va, qb, kb, vb, hq, rep, n, w, sink, out_dtype, name):
    bb = qa.shape[0]
    blk = ATT_BLK
    nb = n // blk
    nk = 1 if nb == 1 else 3
    hkv = hq // rep
    scale = HEAD_DIM ** -0.5
    has_sink = sink is not None

    def body(*refs):
        q_ref = refs[0]
        k_refs = refs[1:1 + nk]
        v_refs = refs[1 + nk:1 + 2 * nk]
        pos = 1 + 2 * nk
        sink_ref = refs[pos] if has_sink else None
        o_ref, lse_ref = refs[pos + has_sink], refs[pos + has_sink + 1]
        i = pl.program_id(1)
        k0 = (i - 1) * blk if nk == 3 else i * blk
        valid = _band_mask(i * blk, k0, blk, nk * blk, w, n, True)
        for g in range(hkv):
            kcat = _head_cat(k_refs, g)
            vcat = _head_cat(v_refs, g)
            for r in range(rep):
                h = g * rep + r
                s = lax.dot_general(_head(q_ref, h), kcat, NT, preferred_element_type=F32) * scale
                s = jnp.where(valid, s, NEG_BIG)
                m = jnp.max(s, axis=1, keepdims=True)
                if has_sink:
                    m = jnp.maximum(m, sink_ref[h])
                p = jnp.exp(s - m)
                l = jnp.sum(p, axis=1, keepdims=True)
                if has_sink:
                    l = l + jnp.exp(sink_ref[h] - m)
                o = lax.dot_general(p.astype(BF16), vcat, NN, preferred_element_type=F32) / l
                o_ref[:, h * HEAD_DIM:(h + 1) * HEAD_DIM] = o.astype(o_ref.dtype)
                lse_ref[h] = m + jnp.log(l)

    def mk(col, width):
        return lambda f: pl.BlockSpec((None, blk, width), lambda b, i, f=f: (b, f(i), col))

    qw, kw = hq * HEAD_DIM, hkv * HEAD_DIM
    in_specs = [pl.BlockSpec((None, blk, qw), lambda b, i: (b, i, qb))]
    in_specs += _nbr_specs(mk(kb, kw), nb) + _nbr_specs(mk(vb, kw), nb)
    args = [qa] + [ka] * nk + [va] * nk
    if has_sink:
        in_specs.append(pl.BlockSpec((hq, 1, 1), lambda b, i: (0, 0, 0)))
        args.append(sink)
    return pl.pallas_call(
        body, name=name, grid=(bb, nb), in_specs=in_specs,
        out_specs=[pl.BlockSpec((None, blk, qw), lambda b, i: (b, i, 0)),
                   pl.BlockSpec((None, hq, blk, 1), lambda b, i: (b, 0, i, 0))],
        out_shape=[jax.ShapeDtypeStruct((bb, n, qw), out_dtype), jax.ShapeDtypeStruct((bb, hq, n, 1), F32)],
        compiler_params=_params(("parallel", "parallel")),
    )(*args)


def _attn_dq(qa, ka, va, do, lse, delta, qb, kb, vb, hq, rep, n, w, name):
    bb = qa.shape[0]
    blk = ATT_BLK
    nb = n // blk
    nk = 1 if nb == 1 else 3
    hkv = hq // rep
    scale = HEAD_DIM ** -0.5

    def body(*refs):
        q_ref = refs[0]
        k_refs = refs[1:1 + nk]
        v_refs = refs[1 + nk:1 + 2 * nk]
        do_ref, lse_ref, dl_ref, dq_ref = refs[1 + 2 * nk:]
        i = pl.program_id(1)
        k0 = (i - 1) * blk if nk == 3 else i * blk
        valid = _band_mask(i * blk, k0, blk, nk * blk, w, n, True)
        for g in range(hkv):
            kcat = _head_cat(k_refs, g)
            vcat = _head_cat(v_refs, g)
            for r in range(rep):
                h = g * rep + r
                s = lax.dot_general(_head(q_ref, h), kcat, NT, preferred_element_type=F32) * scale
                p = jnp.exp(jnp.where(valid, s, NEG_BIG) - lse_ref[h])
                dp = lax.dot_general(_head(do_ref, h).astype(BF16), vcat, NT, preferred_element_type=F32)
                ds = p * (dp - dl_ref[h])
                dq = lax.dot_general(ds.astype(BF16), kcat, NN, preferred_element_type=F32) * scale
                dq_ref[:, h * HEAD_DIM:(h + 1) * HEAD_DIM] = dq.astype(BF16)

    def mk(col, width):
        return lambda f: pl.BlockSpec((None, blk, width), lambda b, i, f=f: (b, f(i), col))

    qw, kw = hq * HEAD_DIM, hkv * HEAD_DIM
    col_spec = pl.BlockSpec((None, hq, blk, 1), lambda b, i: (b, 0, i, 0))
    in_specs = [pl.BlockSpec((None, blk, qw), lambda b, i: (b, i, qb))]
    in_specs += _nbr_specs(mk(kb, kw), nb) + _nbr_specs(mk(vb, kw), nb)
    in_specs += [pl.BlockSpec((None, blk, qw), lambda b, i: (b, i, 0)), col_spec, col_spec]
    return pl.pallas_call(
        body, name=name, grid=(bb, nb), in_specs=in_specs,
        out_specs=pl.BlockSpec((None, blk, qw), lambda b, i: (b, i, 0)),
        out_shape=jax.ShapeDtypeStruct((bb, n, qw), BF16),
        compiler_params=_params(("parallel", "parallel")),
    )(qa, *([ka] * nk), *([va] * nk), do, lse, delta)


def _attn_dkv(qa, ka, va, do, lse_row, delta_row, qb, kb, vb, hq, rep, n, w, name):
    bb = qa.shape[0]
    blk = ATT_BLK
    nb = n // blk
    nq = 1 if nb == 1 else 3
    hkv = hq // rep
    scale = HEAD_DIM ** -0.5

    def body(*refs):
        k_ref, v_ref = refs[0], refs[1]
        q_refs = refs[2:2 + nq]
        do_refs = refs[2 + nq:2 + 2 * nq]
        lse_refs = refs[2 + 2 * nq:2 + 3 * nq]
        dl_refs = refs[2 + 3 * nq:2 + 4 * nq]
        dk_ref, dv_ref = refs[2 + 4 * nq:]
        j = pl.program_id(1)
        q0 = (j - 1) * blk if nq == 3 else j * blk
        valid = _band_mask(j * blk, q0, blk, nq * blk, w, n, False)
        for g in range(hkv):
            kg, vg = _head(k_ref, g), _head(v_ref, g)
            dk = jnp.zeros((blk, HEAD_DIM), F32)
            dv = jnp.zeros((blk, HEAD_DIM), F32)
            for r in range(rep):
                h = g * rep + r
                qcat = _head_cat(q_refs, h)
                docat = _head_cat(do_refs, h).astype(BF16)
                lse = lse_refs[0][h] if nq == 1 else jnp.concatenate([lr[h] for lr in lse_refs], axis=1)
                dl = dl_refs[0][h] if nq == 1 else jnp.concatenate([dr[h] for dr in dl_refs], axis=1)
                st = lax.dot_general(kg, qcat, NT, preferred_element_type=F32) * scale
                pt = jnp.exp(jnp.where(valid, st, NEG_BIG) - lse)
                dv = dv + lax.dot_general(pt.astype(BF16), docat, NN, preferred_element_type=F32)
                dpt = lax.dot_general(vg, docat, NT, preferred_element_type=F32)
                dst = pt * (dpt - dl)
                dk = dk + lax.dot_general(dst.astype(BF16), qcat, NN, preferred_element_type=F32) * scale
            dk_ref[:, g * HEAD_DIM:(g + 1) * HEAD_DIM] = dk.astype(BF16)
            dv_ref[:, g * HEAD_DIM:(g + 1) * HEAD_DIM] = dv.astype(BF16)

    qw, kw = hq * HEAD_DIM, hkv * HEAD_DIM

    def mkq(col):
        return lambda f: pl.BlockSpec((None, blk, qw), lambda b, j, f=f: (b, f(j), col))

    def mkrow(f):
        return pl.BlockSpec((None, hq, 1, blk), lambda b, j, f=f: (b, 0, 0, f(j)))

    in_specs = [pl.BlockSpec((None, blk, kw), lambda b, j: (b, j, kb)), pl.BlockSpec((None, blk, kw), lambda b, j: (b, j, vb))]
    in_specs += _nbr_specs(mkq(qb), nb) + _nbr_specs(mkq(0), nb) + _nbr_specs(mkrow, nb) + _nbr_specs(mkrow, nb)
    o_spec = pl.BlockSpec((None, blk, kw), lambda b, j: (b, j, 0))
    return pl.pallas_call(
        body, name=name, grid=(bb, nb), in_specs=in_specs, out_specs=[o_spec, o_spec],
        out_shape=[jax.ShapeDtypeStruct((bb, n, kw), BF16)] * 2,
        compiler_params=_params(("parallel", "parallel")),
    )(ka, va, *([qa] * nq), *([do] * nq), *([lse_row] * nq), *([delta_row] * nq))


def _head_expand(v, nh, width):
    lane_head = lax.broadcasted_iota(jnp.int32, (1, nh * width), 1) >> int(math.log2(width))
    out = jnp.zeros((v.shape[0], nh * width), F32)
    for j in range(nh):
        out = jnp.where(lane_head == j, v[:, j:j + 1], out)
    return out


def _head_sums(m, nh, width):
    lane_head = lax.broadcasted_iota(jnp.int32, (1, nh * width), 1) >> int(math.log2(width))
    col = lax.broadcasted_iota(jnp.int32, (1, nh), 1)
    out = jnp.zeros((m.shape[0], nh), F32)
    for j in range(nh):
        sj = jnp.sum(jnp.where(lane_head == j, m, 0.0), axis=1, keepdims=True)
        out = jnp.where(col == j, sj, out)
    return out


def _head_dots(a, b, nh, name):
    def body(a_ref, b_ref, o_ref):
        o_ref[...] = _head_sums(a_ref[...].astype(F32) * b_ref[...].astype(F32), nh, HEAD_DIM)

    return _rowwise(body, [a, b], [], [(nh, F32)], tile=256, name=name)[0]


def _dil_combine(outs, lses, name):
    ng = len(outs)

    def body(*refs):
        o_refs, l_refs = refs[:ng], refs[ng:2 * ng]
        y_ref, lt_ref = refs[2 * ng], refs[2 * ng + 1]
        ls = [r[...] for r in l_refs]
        m = functools.reduce(jnp.maximum, ls)
        es = [jnp.exp(v - m) for v in ls]
        tot = functools.reduce(jnp.add, es)
        acc = jnp.zeros(o_refs[0].shape, F32)
        for o_ref, e in zip(o_refs, es):
            acc = acc + _head_expand(e / tot, DIL_HEADS, HEAD_DIM) * o_ref[...]
        y_ref[...] = acc.astype(BF16)
        lt_ref[...] = m + jnp.log(tot)

    dw = outs[0].shape[1]
    return _rowwise(body, list(outs) + list(lses), [], [(dw, BF16), (DIL_HEADS, F32)], tile=256, name=name)


def _sink_grad(lse, delta, sink, name):
    def body(l_ref, d_ref, s_ref, o_ref):
        o_ref[...] -= jnp.sum(jnp.exp(s_ref[...] - l_ref[...]) * d_ref[...], axis=0, keepdims=True)

    return _rowwise(body, [lse, delta], [sink], [], [((1, lse.shape[1]), F32)], tile=512, name=name)[0]


def _shift_rows(x, d, nrows):
    if d == 0:
        return x
    rolled = pltpu.roll(x, (-d) % nrows, 0)
    row = lax.broadcasted_iota(jnp.int32, x.shape, 0)
    ok = (row + d >= 0) & (row + d < nrows)
    return jnp.where(ok, rolled, 0.0)


def _conv_fwd(proj, off, conv_w, conv_b, xbc, name):
    t = proj.shape[0]
    tc = _pick(xbc, (256, 128))
    assert off % tc == 0
    pad = (CONV_WIDTH - 1) // 2

    def body(x_ref, w_ref, b_ref, c_ref, u_ref):
        xv = x_ref[...].astype(F32)
        acc = jnp.zeros_like(xv) + b_ref[...]
        for k in range(CONV_WIDTH):
            acc = acc + w_ref[k:k + 1, :] * _shift_rows(xv, k - pad, SEQ)
        c_ref[...] = acc.astype(BF16)
        u_ref[...] = (acc * _sigmoid(acc)).astype(BF16)

    o_spec = pl.BlockSpec((SEQ, tc), lambda b, j: (b, j))
    return pl.pallas_call(
        body, name=name, grid=(t // SEQ, xbc // tc),
        in_specs=[pl.BlockSpec((SEQ, tc), lambda b, j, o=off // tc: (b, o + j)),
                  pl.BlockSpec((CONV_WIDTH, tc), lambda b, j: (0, j)), pl.BlockSpec((1, tc), lambda b, j: (0, j))],
        out_specs=[o_spec, o_spec], out_shape=[jax.ShapeDtypeStruct((t, xbc), BF16)] * 2,
        compiler_params=_params(("parallel", "parallel")),
    )(proj, conv_w, conv_b)


def _conv_bwd(du, cpre, proj, off, conv_w, name):
    t, xbc = du.shape
    tc = _pick(xbc, (256, 128))
    pad = (CONV_WIDTH - 1) // 2

    def body(du_ref, c_ref, x_ref, w_ref, dx_ref, dw_ref, db_ref):
        @pl.when(pl.program_id(1) == 0)
        def _():
            dw_ref[...] = jnp.zeros_like(dw_ref)
            db_ref[...] = jnp.zeros_like(db_ref)

        cv = c_ref[...].astype(F32)
        sg = _sigmoid(cv)
        dc = du_ref[...] * (sg * (1.0 + cv * (1.0 - sg)))
        xv = x_ref[...].astype(F32)
        dx = jnp.zeros_like(dc)
        for k in range(CONV_WIDTH):
            dx = dx + w_ref[k:k + 1, :] * _shift_rows(dc, pad - k, SEQ)
            dw_ref[k:k + 1, :] += jnp.sum(dc * _shift_rows(xv, k - pad, SEQ), axis=0, keepdims=True)
        db_ref[...] += jnp.sum(dc, axis=0, keepdims=True)
        dx_ref[...] = dx.astype(BF16)

    blk = pl.BlockSpec((SEQ, tc), lambda j, b: (b, j))
    return pl.pallas_call(
        body, name=name, grid=(xbc // tc, t // SEQ),
        in_specs=[blk, blk, pl.BlockSpec((SEQ, tc), lambda j, b, o=off // tc: (b, o + j)),
                  pl.BlockSpec((CONV_WIDTH, tc), lambda j, b: (0, j))],
        out_specs=[blk, pl.BlockSpec((CONV_WIDTH, tc), lambda j, b: (0, j)), pl.BlockSpec((1, tc), lambda j, b: (0, j))],
        out_shape=[jax.ShapeDtypeStruct((t, xbc), BF16), jax.ShapeDtypeStruct((CONV_WIDTH, xbc), F32),
                   jax.ShapeDtypeStruct((1, xbc), F32)],
        compiler_params=_params(("parallel", "arbitrary")),
    )(du, cpre, proj, conv_w)


def _dt_prep(dtr, bias, name):
    def body(r_ref, b_ref, o_ref):
        v = r_ref[...] + b_ref[...]
        o_ref[...] = jnp.maximum(v, 0.0) + jnp.log1p(jnp.exp(-jnp.abs(v)))

    return _rowwise(body, [dtr], [bias], [(dtr.shape[1], F32)], tile=512, name=name)[0]


def _scan_prelude(d, dt_ref, dtt_ref, al_ref, alt_ref, hg):
    p = SSD_HEAD_DIM
    ch = SSD_CHUNK
    a_row = -jnp.exp(al_ref[...])
    a_col = -jnp.exp(alt_ref[...])
    dtc = dt_ref[...]
    dt_x = _head_expand(dtc, hg, p)
    dta_x = dt_x * _head_expand(a_row, hg, p)
    dta_t = dtt_ref[...] * a_col
    ri = lax.broadcasted_iota(jnp.int32, (ch, ch), 0)
    ci = lax.broadcasted_iota(jnp.int32, (ch, ch), 1)
    mask = (ci <= ri) if d == 0 else (ci >= ri)
    mask_t = (ci >= ri) if d == 0 else (ci <= ri)
    tri = mask.astype(F32)
    phi_x = jnp.dot(tri, dta_x, preferred_element_type=F32, precision=HIGHEST)
    phi_r = lax.dot_general(dta_t, tri, NT, preferred_element_type=F32, precision=HIGHEST)
    tot_x = jnp.sum(dta_x, axis=0, keepdims=True)
    return dtc, dt_x, phi_x, phi_r, tot_x, mask, mask_t


def _scan_specs(d, nc, hg, dm):
    p, n, ch = SSD_HEAD_DIM, SSD_STATE, SSD_CHUNK
    w = hg * p
    b0 = dm["HI"] // n
    c0 = (dm["HI"] + dm["GN"]) // n

    def row(b, c):
        return b * nc + c

    return [
        pl.BlockSpec((ch, w), lambda b, g, c: (row(b, c), g)),
        pl.BlockSpec((ch, n), lambda b, g, c: (row(b, c), b0 + g)),
        pl.BlockSpec((ch, n), lambda b, g, c: (row(b, c), c0 + g)),
        pl.BlockSpec((None, None, ch, hg), lambda b, g, c: (d, g, row(b, c), 0)),
        pl.BlockSpec((None, None, 8, ch), lambda b, g, c: (d, g, 0, row(b, c))),
        pl.BlockSpec((None, None, 1, hg), lambda b, g, c: (d, g, 0, 0)),
        pl.BlockSpec((None, None, 8, 1), lambda b, g, c: (d, g, 0, 0)),
    ]


def _remap(spec, f):
    return pl.BlockSpec(spec.block_shape, lambda b, g, c, im=spec.index_map: im(b, g, f(c)))


def _scan_fwd(u, dtg, dttg, alg, altg, d, dm, name, side=None):
    t = u.shape[0]
    p, n, ch, hg = SSD_HEAD_DIM, SSD_STATE, SSD_CHUNK, dm["HG"]
    w = hg * p
    nc = SEQ // ch
    order = (lambda c: c) if d == 0 else (lambda c: nc - 1 - c)

    def body(x_ref, b_ref, c_ref, dt_ref, dtt_ref, al_ref, alt_ref, y_ref, sin_ref, s_ref):
        @pl.when(pl.program_id(2) == 0)
        def _():
            s_ref[...] = jnp.zeros_like(s_ref)

        dtc, dt_x, phi_x, phi_r, tot_x, mask, _ = _scan_prelude(d, dt_ref, dtt_ref, al_ref, alt_ref, hg)
        lane_head = lax.broadcasted_iota(jnp.int32, (1, w), 1) >> int(math.log2(p))
        cm, bm = c_ref[...], b_ref[...]
        cb = lax.dot_general(cm, bm, NT, preferred_element_type=F32)
        xdt = x_ref[...].astype(F32) * dt_x
        xdt_b = xdt.astype(BF16)
        ydiag = jnp.zeros((ch, w), F32)
        for j in range(hg):
            seg = phi_x[:, j * p:j * p + 1] - phi_r[j:j + 1, :]
            mj = (cb * jnp.exp(jnp.where(mask, seg, NEG_BIG))).astype(BF16)
            ydiag = ydiag + jnp.dot(mj, jnp.where(lane_head == j, xdt_b, jnp.zeros_like(xdt_b)), preferred_element_type=F32)
        s = s_ref[...]
        y_ref[...] = ydiag + jnp.dot(cm, s.astype(BF16), preferred_element_type=F32) * jnp.exp(phi_x)
        sin_ref[...] = s
        wm = (xdt * jnp.exp(tot_x - phi_x)).astype(BF16)
        s_ref[...] = s * jnp.exp(tot_x) + lax.dot_general(bm, wm, TN, preferred_element_type=F32)

    specs = [_remap(s, order) for s in _scan_specs(d, nc, hg, dm)]
    own, brought = _carry_call(
        side, body, name=name, grid=(t // SEQ, SSD_GROUPS, nc), in_specs=specs,
        out_specs=[_remap(pl.BlockSpec((ch, w), lambda b, g, c: (b * nc + c, g)), order),
                   _remap(pl.BlockSpec((None, None, n, w), lambda b, g, c: (b * nc + c, g, 0, 0)), order)],
        out_shape=[jax.ShapeDtypeStruct((t, dm["HI"]), F32), jax.ShapeDtypeStruct((t // ch, SSD_GROUPS, n, w), F32)],
        scratch_shapes=[pltpu.VMEM((n, w), F32)], sem=("parallel", "parallel", "arbitrary"),
        args=(u, u, u, dtg, dttg, alg, altg))
    return own if side is None else (own, brought)


def _scan_bwd(u, dtg, dttg, alg, altg, dy, sin, adds, d, dm, name, side=None):
    t = u.shape[0]
    p, n, ch, hg = SSD_HEAD_DIM, SSD_STATE, SSD_CHUNK, dm["HG"]
    w = hg * p
    nc = SEQ // ch
    order = (lambda c: nc - 1 - c) if d == 0 else (lambda c: c)
    has_bc_add = adds[1] is not None

    def body(*refs):
        x_ref, b_ref, c_ref, dt_ref, dtt_ref, al_ref, alt_ref, dy_ref, sin_ref, ax_ref = refs[:10]
        pos = 10
        ab_ref = ac_ref = None
        if has_bc_add:
            ab_ref, ac_ref = refs[10], refs[11]
            pos = 12
        dxs_ref, db_ref, dc_ref, rq_ref, xdx_ref, ds_ref = refs[pos:]

        @pl.when(pl.program_id(2) == 0)
        def _():
            ds_ref[...] = jnp.zeros_like(ds_ref)

        dtc, dt_x, phi_x, phi_r, tot_x, mask, mask_t = _scan_prelude(d, dt_ref, dtt_ref, al_ref, alt_ref, hg)
        lane_head = lax.broadcasted_iota(jnp.int32, (1, w), 1) >> int(math.log2(p))
        cm, bm = c_ref[...], b_ref[...]
        cb = lax.dot_general(cm, bm, NT, preferred_element_type=F32)
        cb_t = lax.dot_general(bm, cm, NT, preferred_element_type=F32)
        xs = x_ref[...].astype(F32)
        xdt = xs * dt_x
        xdt_b = xdt.astype(BF16)
        dy = dy_ref[...]
        dy_b = dy.astype(BF16)
        zero_b = jnp.zeros_like(dy_b)
        col = lax.broadcasted_iota(jnp.int32, (1, hg), 1)
        dxp = jnp.zeros((ch, w), F32)
        a_ls = jnp.zeros((ch, ch), F32)
        a_sl = jnp.zeros((ch, ch), F32)
        dphi = jnp.zeros((ch, hg), F32)
        for j in range(hg):
            pc = phi_x[:, j * p:j * p + 1]
            pr = phi_r[j:j + 1, :]
            l_ls = jnp.exp(jnp.where(mask, pc - pr, NEG_BIG))
            l_sl = jnp.exp(jnp.where(mask_t, pr - pc, NEG_BIG))
            dy_j = jnp.where(lane_head == j, dy_b, zero_b)
            xdt_j = jnp.where(lane_head == j, xdt_b, zero_b)
            dxp = dxp + jnp.dot((cb_t * l_sl).astype(BF16), dy_j, preferred_element_type=F32)
            g_ls = l_ls * lax.dot_general(dy_j, xdt_b, NT, preferred_element_type=F32)
            g_sl = l_sl * lax.dot_general(xdt_j, dy_b, NT, preferred_element_type=F32)
            a_ls = a_ls + g_ls
            a_sl = a_sl + g_sl
            pair = jnp.sum(g_ls * cb, axis=1, keepdims=True) - jnp.sum(g_sl * cb_t, axis=1, keepdims=True)
            dphi = jnp.where(col == j, pair, dphi)
        ds = ds_ref[...]
        ds_b = ds.astype(BF16)
        sin = sin_ref[...]
        sin_b = sin.astype(BF16)
        e_tp = jnp.exp(tot_x - phi_x)
        e_p = jnp.exp(phi_x)
        dxp_off = e_tp * jnp.dot(bm, ds_b, preferred_element_type=F32)
        dxp = dxp + dxp_off
        dxs_ref[...] = ax_ref[...] + dxp * dt_x
        xdx_ref[...] = _head_sums(xs * dxp, hg, p)
        y_off = jnp.dot(cm, sin_b, preferred_element_type=F32) * e_p
        st_t = _head_sums(xdt * dxp_off, hg, p)
        dphi = dphi + _head_sums(dy * y_off, hg, p) - st_t
        dtot = _head_sums(jnp.sum(ds * sin, axis=0, keepdims=True) * jnp.exp(tot_x), hg, p) + jnp.sum(st_t, axis=0, keepdims=True)
        cum = jnp.dot(mask_t.astype(F32), _head_expand(dphi, hg, p), preferred_element_type=F32, precision=HIGHEST)
        ddta = jnp.zeros((ch, hg), F32)
        for j in range(hg):
            ddta = jnp.where(col == j, cum[:, j * p:j * p + 1], ddta)
        rq_ref[...] = ddta + dtot
        dye = (dy * e_p).astype(BF16)
        dcv = jnp.dot(a_ls.astype(BF16), bm, preferred_element_type=F32)
        dcv = dcv + lax.dot_general(dye, sin_b, NT, preferred_element_type=F32)
        dbv = jnp.dot(a_sl.astype(BF16), cm, preferred_element_type=F32)
        dbv = dbv + lax.dot_general((xdt * e_tp).astype(BF16), ds_b, NT, preferred_element_type=F32)
        if has_bc_add:
            dcv = dcv + ac_ref[...]
            dbv = dbv + ab_ref[...]
        dc_ref[...] = dcv
        db_ref[...] = dbv
        ds_ref[...] = ds * jnp.exp(tot_x) + lax.dot_general(cm, dye, TN, preferred_element_type=F32)

    def sp(spec):
        return _remap(spec, order)

    xw = pl.BlockSpec((ch, w), lambda b, g, c: (b * nc + c, g))
    gn_blk = pl.BlockSpec((ch, n), lambda b, g, c: (b * nc + c, g))
    small = pl.BlockSpec((None, ch, hg), lambda b, g, c: (g, b * nc + c, 0))
    in_specs = [sp(s) for s in _scan_specs(d, nc, hg, dm)]
    in_specs += [sp(xw), sp(pl.BlockSpec((None, None, n, w), lambda b, g, c: (b * nc + c, g, 0, 0))), sp(xw)]
    args = [u, u, u, dtg, dttg, alg, altg, dy, sin, adds[0]]
    if has_bc_add:
        in_specs += [sp(gn_blk), sp(gn_blk)]
        args += [adds[1], adds[2]]
    own, brought = _carry_call(
        side, body, name=name, grid=(t // SEQ, SSD_GROUPS, nc), in_specs=in_specs,
        out_specs=[sp(xw), sp(gn_blk), sp(gn_blk), sp(small), sp(small)],
        out_shape=[jax.ShapeDtypeStruct((t, dm["HI"]), F32), jax.ShapeDtypeStruct((t, dm["GN"]), F32),
                   jax.ShapeDtypeStruct((t, dm["GN"]), F32), jax.ShapeDtypeStruct((SSD_GROUPS, t, hg), F32),
                   jax.ShapeDtypeStruct((SSD_GROUPS, t, hg), F32)],
        scratch_shapes=[pltpu.VMEM((n, w), F32)], sem=("parallel", "parallel", "arbitrary"), args=args)
    return own if side is None else (own, brought)


def _ssd_param_bwd(rq_f, rq_r, xdx, dtp, dtr, bias, alog, name):
    def body(rf_ref, rr_ref, xdx_ref, dt_ref, dtr_ref, b_ref, al_ref, o_ref, db_ref, da_ref):
        a = -jnp.exp(al_ref[...])
        d_dta = rf_ref[...] + rr_ref[...]
        ddt = a * d_dta + xdx_ref[...]
        ddtr = ddt * _sigmoid(dtr_ref[...] + b_ref[...])
        o_ref[...] = ddtr
        db_ref[...] += jnp.sum(ddtr, axis=0, keepdims=True)
        da_ref[...] += a * jnp.sum(dt_ref[...] * d_dta, axis=0, keepdims=True)

    return _rowwise(body, [rq_f, rq_r, xdx, dtp, dtr], [bias, alog], [(LANES, F32)],
                    [((1, LANES), F32), ((1, LANES), F32)], tile=512, name=name)


def _ssd_out_fwd(y_f, y_b, u, proj, dcols, gn, hi, name):
    def body(yf_ref, yb_ref, x_ref, z_ref, d_ref, g_ref, o_ref):
        ytot = yf_ref[...] + yb_ref[...] + d_ref[...] * x_ref[...].astype(F32)
        zv = z_ref[...].astype(F32)
        yz = ytot * (zv * _sigmoid(zv))
        rstd = lax.rsqrt(jnp.mean(yz * yz, axis=-1, keepdims=True) + EPS)
        o_ref[...] = (yz * rstd * g_ref[...]).astype(BF16)

    return _rowwise(body, [y_f, y_b, (u, 0, hi), (proj, 0, hi)], [dcols, gn], [(hi, BF16)], tile=256, name=name)[0]


def _ssd_out_bwd(dya, y_f, y_b, u, proj, dcols, gn, hi, name):
    def body(dy_ref, yf_ref, yb_ref, x_ref, z_ref, d_ref, g_ref, dyt_ref, dxs_ref, dz_ref, dg_ref, dd_ref):
        xv = x_ref[...].astype(F32)
        ytot = yf_ref[...] + yb_ref[...] + d_ref[...] * xv
        zv = z_ref[...].astype(F32)
        sg = _sigmoid(zv)
        sz = zv * sg
        yz = ytot * sz
        rstd = lax.rsqrt(jnp.mean(yz * yz, axis=-1, keepdims=True) + EPS)
        yn = yz * rstd
        dv = dy_ref[...]
        dg_ref[...] += jnp.sum(dv * yn, axis=0, keepdims=True)
        dn = dv * g_ref[...]
        dyz = rstd * (dn - yn * jnp.mean(dn * yn, axis=-1, keepdims=True))
        dyt = dyz * sz
        dyt_ref[...] = dyt
        dxs_ref[...] = dyt * d_ref[...]
        dz_ref[...] = (dyz * ytot * (sg * (1.0 + zv * (1.0 - sg)))).astype(BF16)
        dd_ref[...] += jnp.sum(dyt * xv, axis=0, keepdims=True)

    return _rowwise(body, [dya, y_f, y_b, (u, 0, hi), (proj, 0, hi)], [dcols, gn],
                    [(hi, F32), (hi, F32), (hi, BF16)], [((1, hi), F32), ((1, hi), F32)], tile=128, name=name)


def _gate_fwd(pa, pb, pc, proj, off, d, name):
    def body(a_ref, b_ref, c_ref, g0_ref, g1_ref, g2_ref, o_ref):
        acc = _sigmoid(g0_ref[...].astype(F32)) * a_ref[...]
        acc = acc + _sigmoid(g1_ref[...].astype(F32)) * b_ref[...]
        acc = acc + _sigmoid(g2_ref[...].astype(F32)) * c_ref[...]
        o_ref[...] = acc.astype(BF16)

    rows = [pa, pb, pc] + [(proj, off + k * d, d) for k in range(3)]
    return _rowwise(body, rows, [], [(d, BF16)], tile=256, name=name)[0]


def _gate_bwd(dm_, pa, pb, pc, proj, off, d, name):
    def body(dm_ref, a_ref, b_ref, c_ref, g0_ref, g1_ref, g2_ref, da_ref, db_ref, dc_ref, dg0_ref, dg1_ref, dg2_ref):
        dmv = dm_ref[...]
        for p_ref, g_ref, dp_ref, dg_ref in ((a_ref, g0_ref, da_ref, dg0_ref), (b_ref, g1_ref, db_ref, dg1_ref),
                                             (c_ref, g2_ref, dc_ref, dg2_ref)):
            sg = _sigmoid(g_ref[...].astype(F32))
            dp_ref[...] = (dmv * sg).astype(BF16)
            dg_ref[...] = (dmv * p_ref[...] * sg * (1.0 - sg)).astype(BF16)

    rows = [dm_, pa, pb, pc] + [(proj, off + k * d, d) for k in range(3)]
    return _rowwise(body, rows, [], [(d, BF16)] * 6, tile=128, name=name)


def _adamw(w, g, m, v, name):
    nl, rows, cols = w.shape
    tile = _pick(rows, ROW_TILES)
    c1 = 1.0 / (1.0 - ADAM_B1 ** ADAM_STEP)
    c2 = 1.0 / (1.0 - ADAM_B2 ** ADAM_STEP)

    def body(w_ref, g_ref, m_ref, v_ref, d_ref, nm_ref, nv_ref):
        gv = g_ref[...]
        nm = ADAM_B1 * m_ref[...] + (1.0 - ADAM_B1) * gv
        nv = ADAM_B2 * v_ref[...] + (1.0 - ADAM_B2) * (gv * gv)
        nm_ref[...] = nm
        nv_ref[...] = nv
        d_ref[...] = -ADAM_LR * ((nm * c1) / (jnp.sqrt(nv * c2) + ADAM_EPS) + ADAM_WD * w_ref[...])

    blk = pl.BlockSpec((None, tile, cols), lambda l, i: (l, i, 0))
    return pl.pallas_call(
        body, name=name, grid=(nl, rows // tile), in_specs=[blk] * 4, out_specs=[blk] * 3,
        out_shape=[jax.ShapeDtypeStruct(w.shape, F32)] * 3, compiler_params=_params(("parallel", "parallel")),
    )(w, g, m, v)


ANY = pl.BlockSpec(memory_space=pl.ANY)


def _place():
    x, y, c = lax.axis_index("x"), lax.axis_index("y"), lax.axis_index("c")
    chips = [(1 - x, y), (x, 1 - y), (1 - x, 1 - y)]
    return x, y, c, chips


def _gather_copies(src, out, ssem, rsem, base):
    x, y, c, chips = _place()
    k = 2 * x + y

    def copy(j, kk, layer, to, own=False):
        return pltpu.make_async_remote_copy(
            src_ref=src.at[layer] if own else out.at[layer, kk], dst_ref=out.at[layer, kk],
            send_sem=ssem.at[base + j], recv_sem=rsem.at[base + j], device_id=to, device_id_type=MESH)

    first = [copy(j, k, c, (cx, cy, c), own=True) for j, (cx, cy) in enumerate(chips)]
    passed = [copy(3 + j, 2 * cx + cy, c, (x, y, 1 - c)) for j, (cx, cy) in enumerate(chips)]
    landed = [copy(j, 2 * cx + cy, c, (x, y, c)) for j, (cx, cy) in enumerate(chips)]
    handed = [copy(3 + j, 2 * cx + cy, 1 - c, (x, y, c)) for j, (cx, cy) in enumerate(chips)]
    return first, passed, landed, handed


def _gather_start(src, out, ssem, rsem, base):
    for cp in _gather_copies(src, out, ssem, rsem, base)[0]:
        cp.start()


def _gather_finish(src, out, ssem, rsem, base):
    first, passed, landed, handed = _gather_copies(src, out, ssem, rsem, base)
    for arrived, onward in zip(landed, passed):
        arrived.wait_recv()
        onward.start()
    for arrived in handed:
        arrived.wait_recv()
    for cp in first + passed:
        cp.wait_send()


def _gather_side(arrs):
    n = len(arrs)

    def start(srcs, outs, sems):
        for a in range(n):
            _gather_start(srcs[a], outs[a], sems[0], sems[1], 6 * a)

    def finish(srcs, outs, sems):
        for a in range(n):
            _gather_finish(srcs[a], outs[a], sems[0], sems[1], 6 * a)

    outs = [jax.ShapeDtypeStruct((a.shape[0], 4) + a.shape[1:], a.dtype) for a in arrs]
    return _Side(list(arrs), outs, [pltpu.SemaphoreType.DMA((6 * n,)), pltpu.SemaphoreType.DMA((6 * n,))], start, finish)


def _put_own(st, arr):
    kchip = 2 * lax.axis_index("x") + lax.axis_index("y")
    return lax.dynamic_update_slice(st, arr[:, None], (0, kchip) + (0,) * (arr.ndim - 1))


def _chip_copies(src, out, ssem, rsem, base=0):
    x, y, c, chips = _place()
    return [pltpu.make_async_remote_copy(src_ref=src.at[2 * cx + cy], dst_ref=out.at[j], send_sem=ssem.at[base + j],
                                         recv_sem=rsem.at[base + j], device_id=(cx, cy, c), device_id_type=MESH)
            for j, (cx, cy) in enumerate(chips)]


def _chips_side(parts):
    n = len(parts)

    def start(srcs, outs, sems):
        for a in range(n):
            for cp in _chip_copies(srcs[a], outs[a], sems[0], sems[1], 3 * a):
                cp.start()

    def finish(srcs, outs, sems):
        for a in range(n):
            for cp in _chip_copies(srcs[a], outs[a], sems[0], sems[1], 3 * a):
                cp.wait()

    return _Side(list(parts), [jax.ShapeDtypeStruct((3,) + p.shape[1:], p.dtype) for p in parts],
                 [pltpu.SemaphoreType.DMA((3 * n,)), pltpu.SemaphoreType.DMA((3 * n,))], start, finish)


def _gather_chips(arr, name):
    def body(src, out, ssem, rsem):
        _gather_start(src, out, ssem, rsem, 0)
        _gather_finish(src, out, ssem, rsem, 0)

    st = pl.pallas_call(
        body, name=name, in_specs=[ANY], out_specs=ANY,
        out_shape=jax.ShapeDtypeStruct((arr.shape[0], 4) + arr.shape[1:], arr.dtype),
        scratch_shapes=[pltpu.SemaphoreType.DMA((6,)), pltpu.SemaphoreType.DMA((6,))],
    )(arr)
    return _put_own(st, arr)


def _pair_swap(g0, g1, name):
    def body(src0, src1, out, ssem, rsem):
        x, y, c, _ = _place()

        def swap(src):
            cp = pltpu.make_async_remote_copy(src_ref=src, dst_ref=out, send_sem=ssem, recv_sem=rsem,
                                              device_id=(x, y, 1 - c), device_id_type=MESH)
            cp.start()
            cp.wait()

        @pl.when(c == 0)
        def _():
            swap(src1)

        @pl.when(c == 1)
        def _():
            swap(src0)

    return pl.pallas_call(
        body, name=name, in_specs=[ANY, ANY], out_specs=ANY, out_shape=jax.ShapeDtypeStruct(g0.shape, g0.dtype),
        scratch_shapes=[pltpu.SemaphoreType.DMA, pltpu.SemaphoreType.DMA],
    )(g0, g1)


def _chip_exchange(p, name):
    def body(src, out, ssem, rsem):
        cps = _chip_copies(src, out, ssem, rsem)
        for cp in cps:
            cp.start()
        for cp in cps:
            cp.wait()

    return pl.pallas_call(
        body, name=name, in_specs=[ANY], out_specs=ANY, out_shape=jax.ShapeDtypeStruct((3,) + p.shape[1:], p.dtype),
        scratch_shapes=[pltpu.SemaphoreType.DMA((3,)), pltpu.SemaphoreType.DMA((3,))],
    )(p)


def _pair_share(r, name):
    def body(src, out, ssem, rsem):
        x, y, c, _ = _place()
        cp = pltpu.make_async_remote_copy(src_ref=src, dst_ref=out, send_sem=ssem, recv_sem=rsem,
                                          device_id=(x, y, 1 - c), device_id_type=MESH)
        cp.start()
        cp.wait()

    theirs = pl.pallas_call(
        body, name=name, in_specs=[ANY], out_specs=ANY, out_shape=jax.ShapeDtypeStruct(r.shape, r.dtype),
        scratch_shapes=[pltpu.SemaphoreType.DMA, pltpu.SemaphoreType.DMA],
    )(r)
    first = lax.axis_index("c") == 0
    return jnp.stack([jnp.where(first, r, theirs), jnp.where(first, theirs, r)])


def _sum_pair(g0, g1, got, sel, name):
    four, rows, cols = g0.shape
    flat = four * rows
    tile = _pick(flat, ROW_TILES)

    def body(sel_ref, a0_ref, a1_ref, b_ref, o_ref):
        mine = jnp.where(sel_ref[0] == 0, a0_ref[...], a1_ref[...])
        o_ref[...] = (mine.astype(F32) + b_ref[...].astype(F32)).astype(BF16)

    def layer(l):
        return pl.BlockSpec((tile, cols), lambda i, s: (jnp.where(s[0] == l, i, 0), 0))

    blk = pl.BlockSpec((tile, cols), lambda i, s: (i, 0))
    return pl.pallas_call(
        body, name=name, out_shape=jax.ShapeDtypeStruct((flat, cols), BF16),
        grid_spec=pltpu.PrefetchScalarGridSpec(
            num_scalar_prefetch=1, grid=(flat // tile,), in_specs=[layer(0), layer(1), blk], out_specs=blk),
        compiler_params=_params(("arbitrary",)),
    )(sel, g0.reshape(flat, cols), g1.reshape(flat, cols), got.reshape(flat, cols)).reshape(four, rows, cols)


def _sum4(a, b, sel, name):
    _, rows, cols = a.shape
    tile = _pick(rows, ROW_TILES)

    def body(sel_ref, a_ref, b0_ref, b1_ref, b2_ref, o_ref):
        acc = a_ref[...].astype(F32) + b0_ref[...].astype(F32)
        acc = acc + b1_ref[...].astype(F32)
        o_ref[...] = acc + b2_ref[...].astype(F32)

    bspec = [pl.BlockSpec((None, tile, cols), lambda i, s, j=j: (j, i, 0)) for j in range(3)]
    return pl.pallas_call(
        body, name=name, out_shape=jax.ShapeDtypeStruct((rows, cols), F32),
        grid_spec=pltpu.PrefetchScalarGridSpec(
            num_scalar_prefetch=1, grid=(rows // tile,),
            in_specs=[pl.BlockSpec((None, tile, cols), lambda i, s: (s[0], i, 0))] + bspec,
            out_specs=pl.BlockSpec((tile, cols), lambda i, s: (i, 0))),
        compiler_params=_params(("parallel",)),
    )(sel, a, b, b, b)


def _reduce_scatter(g0, g1, name):
    part = _reduce_pair(g0, g1, name)
    return _reduce_chips(part, _chip_exchange(part, name + "_chips"), name)


def _reduce_pair(g0, g1, name):
    c = lax.axis_index("c").astype(jnp.int32).reshape(1)
    got = _pair_swap(g0, g1, name + "_pair")
    return _sum_pair(g0, g1, got, c, name + "_add2")


def _reduce_chips(part, others, name):
    k = (2 * lax.axis_index("x") + lax.axis_index("y")).astype(jnp.int32).reshape(1)
    total = _sum4(part, others, k, name + "_add4")
    return _pair_share(total, name + "_share")


class _EarlyReduce:
    def __init__(self, names):
        self.names, self.grads, self.part, self.others = names, None, {}, {}

    def early(self, n, g0):
        if n in self.names:
            shards = [g if n == "w_in" else _restack(n, g) for g in (g0, self.grads[1][n])]
            self.part[n] = _reduce_pair(*shards, "rs_" + n)

    def side(self, names):
        return _chips_side([self.part[n] for n in names])

    def took(self, names, brought):
        for n, o in zip(names, brought):
            self.others[n] = o

    def finish(self, n):
        return _reduce_chips(self.part[n], self.others[n], "rs_" + n)


def _all_reduce_small(buf, name):
    rows = buf.shape[0]

    def body(src, out, slots, ssem, rsem):
        x, y, c, _ = _place()
        me = 4 * x + 2 * y + c
        slots[me] = src[...]
        cps = []
        for j in range(1, 8):
            px, py, pc = x ^ (j >> 2), y ^ ((j >> 1) & 1), c ^ (j & 1)
            cps.append(pltpu.make_async_remote_copy(src_ref=src, dst_ref=slots.at[me], send_sem=ssem.at[j - 1],
                                                    recv_sem=rsem.at[j - 1], device_id=(px, py, pc), device_id_type=MESH))
        for cp in cps:
            cp.start()
        for j in range(1, 8):
            peer = me ^ j
            pltpu.make_async_remote_copy(src_ref=src, dst_ref=slots.at[peer], send_sem=ssem.at[j - 1], recv_sem=rsem.at[j - 1],
                                         device_id=(x, y, c), device_id_type=MESH).wait_recv()
        for cp in cps:
            cp.wait_send()
        acc = slots[0]
        for d in range(1, 8):
            acc = acc + slots[d]
        out[...] = acc

    vm = pl.BlockSpec(memory_space=pltpu.VMEM)
    return pl.pallas_call(
        body, name=name, in_specs=[vm], out_specs=vm, out_shape=jax.ShapeDtypeStruct((rows, LANES), F32),
        scratch_shapes=[pltpu.VMEM((8, rows, LANES), F32), pltpu.SemaphoreType.DMA((7,)), pltpu.SemaphoreType.DMA((7,))],
    )(buf)


def _pack(arrs):
    flat = jnp.concatenate([a.astype(F32).reshape(-1) for a in arrs])
    n = flat.shape[0]
    padded = -(-n // (8 * LANES)) * (8 * LANES)
    return jnp.pad(flat, (0, padded - n)).reshape(padded // LANES, LANES)


def _unpack(buf, like):
    flat = buf.reshape(-1)
    out, pos = [], 0
    for a in like:
        out.append(flat[pos:pos + a.size].reshape(a.shape))
        pos += a.size
    return out


def _stride(t2d, dil):
    t, w = t2d.shape
    b = t // SEQ
    return t2d.reshape(b, SEQ // dil, dil, w).transpose(0, 2, 1, 3).reshape(b * dil, SEQ // dil, w)


def _unstride(t3d, dil):
    bb, n, w = t3d.shape
    b = bb // dil
    return t3d.reshape(b, dil, n, w).transpose(0, 2, 1, 3).reshape(b * SEQ, w)


def _stat_cols(st, dil, heads):
    s3 = _stride(st, dil)
    return s3.transpose(0, 2, 1)[..., None]


def _stat_rows(col):
    bb, h, n, _ = col.shape
    return col.reshape(bb, h, 1, n)


def _scan_params(dtp, alog, dm):
    t = dtp.shape[0]
    g, hg = SSD_GROUPS, dm["HG"]
    dt4 = dtp[:, :dm["H2"]].reshape(t, 2, g, hg)
    dtg = dt4.transpose(1, 2, 0, 3)
    dttg = jnp.pad(dt4.transpose(1, 2, 3, 0), ((0, 0), (0, 0), (0, 8 - hg), (0, 0)))
    al = alog.reshape(2, g, 1, hg)
    alt = jnp.pad(alog.reshape(2, g, hg, 1), ((0, 0), (0, 0), (0, 8 - hg), (0, 0)))
    return dtg, dttg, al, alt


def _layer_fwd(x, wl, tabs, dm, li, late=None):
    d = dm["D"]
    nm = f"l{li}_"
    h = _rms_fwd(x, wl["g_mix"], nm + "rms1")
    late = late or {}
    qkv0 = dm["OFF_QKV"] // HEAD_DIM
    rope = (tabs[0], qkv0, qkv0 + 3 * dm["NG"] * DIL_HEADS + WIN_Q_HEADS + WIN_KV_HEADS)
    if "proj" not in late:
        proj = _mm(h, wl["w_main"], tb=True, rope=rope, name=nm + "proj")
    else:
        proj, brought = _mm(h, wl["w_main"], tb=True, rope=rope, side=late["proj"][0], name=nm + "proj")
        wl.update(late["proj"][1](brought))
    dtr = _mm(h, wl["w_dt"], tb=True, out_dtype=F32, name=nm + "proj_dt")
    cpre, u = _conv_fwd(proj, dm["OFF_XBC"], wl["conv_w"], wl["conv_b"], dm["XBC"], nm + "conv")
    dtp = _dt_prep(dtr, wl["dt_bias"], nm + "dt")
    sp = _scan_params(dtp, wl["a_log"], dm)
    scans = []
    for dirn, key in ((0, "scan_f"), (1, "scan_b")):
        if key in late:
            own, brought = _scan_fwd(u, *sp, dirn, dm, nm + key, side=late[key][0])
            wl.update(late[key][1](brought))
        else:
            own = _scan_fwd(u, *sp, dirn, dm, nm + key)
        scans.append(own)
    (y_f, s_f), (y_b, s_b) = scans
    y_a = _ssd_out_fwd(y_f, y_b, u, proj, wl["d_cols"], wl["ssd_norm"], dm["HI"], nm + "ssd_out")
    qkv = proj[:, dm["OFF_QKV"]:dm["OFF_QKV"] + dm["QW"]]
    ng, dw = dm["NG"], dm["DW"]
    outs, lses, xgs = [], [], []
    for gi, (window, dil) in enumerate(DIL_PATTERNS):
        cols = [qkv[:, s * ng * dw + gi * dw:s * ng * dw + (gi + 1) * dw] for s in range(3)]
        xg = _stride(jnp.concatenate(cols, axis=1), dil)
        o, lse = _attn_fwd(xg, xg, xg, 0, 1, 2, DIL_HEADS, 1, SEQ // dil, window // (2 * dil), None, F32, nm + f"dil{gi}")
        xgs.append(xg)
        outs.append(_unstride(o, dil))
        lses.append(_unstride(lse[..., 0].transpose(0, 2, 1), dil))
    y_bm, lse_tot = _dil_combine(outs, lses, nm + "dil_mix")
    bsz = x.shape[0] // SEQ
    xw = qkv[:, dm["QKVD"]:].reshape(bsz, SEQ, dm["WQ"] + 2 * dm["WK"])
    rep = WIN_Q_HEADS // WIN_KV_HEADS
    y_c3, lse_w = _attn_fwd(xw, xw, xw, 0, rep, rep + 1, WIN_Q_HEADS, rep, SEQ, WIN_HALF,
                            wl["sink"].reshape(WIN_Q_HEADS, 1, 1), BF16, nm + "win")
    y_c = y_c3.reshape(x.shape[0], dm["WQ"])
    pa = _mm(y_a, wl["w_a"], out_dtype=F32, name=nm + "pa")
    pb = _mm(y_bm, wl["w_b"], out_dtype=F32, name=nm + "pb")
    pc = _mm(y_c, wl["w_c"], out_dtype=F32, name=nm + "pc")
    merged = _gate_fwd(pa, pb, pc, proj, dm["OFF_GATE"], d, nm + "gate")
    x1 = _mm(merged, wl["w_out"], add=x, out_dtype=F32, name=nm + "out")
    hm = _rms_fwd(x1, wl["g_mlp"], nm + "rms2")
    up, act = _mm(hm, wl["w_up"], epi="relu2", name=nm + "up")
    x2 = _mm(act, wl["w_down"], add=x1, out_dtype=F32, name=nm + "down")
    saved = dict(x=x, h=h, proj=proj, dtr=dtr, cpre=cpre, u=u, dtp=dtp, y_f=y_f, y_b=y_b, s_f=s_f, s_b=s_b, y_a=y_a,
                 xw=xw, xgs=xgs, y_bm=y_bm, lse_tot=lse_tot, y_c=y_c, lse_w=lse_w, pa=pa, pb=pb, pc=pc,
                 merged=merged, x1=x1, hm=hm, up=up, act=act)
    return x2, saved


def _layer_bwd(dx2, wl, sv, tabs, dm, li, early=None):
    d = dm["D"]
    t = dx2.shape[0]
    bsz = t // SEQ
    nm = f"l{li}b_"
    gr = {}
    dup = _mm(dx2, wl["w_down"], tb=True, aux=sv["up"], epi="relu2_bwd", name=nm + "dup")
    gr["w_down"] = _mm(sv["act"], dx2, ta=True, name=nm + "gw_down")
    dhm = _mm(dup, wl["w_up"], tb=True, out_dtype=F32, name=nm + "dhm")
    gr["w_up"] = _mm(sv["hm"], dup, ta=True, name=nm + "gw_up")
    if early is not None:
        early.early("w_down", gr["w_down"])
        early.early("w_up", gr["w_up"])
    dx1, gmlp = _rms_bwd(sv["x1"], wl["g_mlp"], dhm, dx2, nm + "rms2")
    gr["g_mlp"] = gmlp[0]
    dmerged = _mm(dx1, wl["w_out"], tb=True, out_dtype=F32, name=nm + "dmerged")
    gr["w_out"] = _mm(sv["merged"], dx1, ta=True, name=nm + "gw_out")
    dpa, dpb, dpc, dg0, dg1, dg2 = _gate_bwd(dmerged, sv["pa"], sv["pb"], sv["pc"], sv["proj"], dm["OFF_GATE"], d, nm + "gate")
    dya = _mm(dpa, wl["w_a"], tb=True, out_dtype=F32, name=nm + "dya")
    gr["w_a"] = _mm(sv["y_a"], dpa, ta=True, name=nm + "gw_a")
    dyb = _mm(dpb, wl["w_b"], tb=True, out_dtype=F32, name=nm + "dyb")
    gr["w_b"] = _mm(sv["y_bm"], dpb, ta=True, name=nm + "gw_b")
    dyc = _mm(dpc, wl["w_c"], tb=True, out_dtype=F32, name=nm + "dyc")
    gr["w_c"] = _mm(sv["y_c"], dpc, ta=True, name=nm + "gw_c")
    if early is not None:
        for n in ("w_out", "w_a", "w_b", "w_c"):
            early.early(n, gr[n])
    ng, dw = dm["NG"], dm["DW"]
    xw = sv["xw"]
    rep = WIN_Q_HEADS // WIN_KV_HEADS
    delta_w = _head_dots(dyc, sv["y_c"], WIN_Q_HEADS, nm + "win_delta")
    dl_col = _stat_cols(delta_w, 1, WIN_Q_HEADS)
    lse_w = sv["lse_w"]
    dyc3 = dyc.reshape(bsz, SEQ, dm["WQ"])
    wargs = (0, rep, rep + 1, WIN_Q_HEADS, rep, SEQ, WIN_HALF)
    dq_w = _attn_dq(xw, xw, xw, dyc3, lse_w, dl_col, *wargs, nm + "win_dq")
    dk_w, dv_w = _attn_dkv(xw, xw, xw, dyc3, _stat_rows(lse_w), _stat_rows(dl_col), *wargs, nm + "win_dkv")
    lse_w2 = lse_w[..., 0].transpose(0, 2, 1).reshape(t, WIN_Q_HEADS)
    gr["sink"] = _sink_grad(lse_w2, delta_w, wl["sink"], nm + "sink")[0]
    delta_d = _head_dots(dyb, sv["y_bm"], DIL_HEADS, nm + "dil_delta")
    dqs, dks, dvs = [], [], []
    for gi, (window, dil) in enumerate(DIL_PATTERNS):
        xg = sv["xgs"][gi]
        n = SEQ // dil
        do_g = _stride(dyb, dil)
        lse_c = _stat_cols(sv["lse_tot"], dil, DIL_HEADS)
        dl_c = _stat_cols(delta_d, dil, DIL_HEADS)
        dargs = (0, 1, 2, DIL_HEADS, 1, n, window // (2 * dil))
        dq = _attn_dq(xg, xg, xg, do_g, lse_c, dl_c, *dargs, nm + f"dil{gi}_dq")
        dk, dv = _attn_dkv(xg, xg, xg, do_g, _stat_rows(lse_c), _stat_rows(dl_c), *dargs, nm + f"dil{gi}_dkv")
        dqs.append(_unstride(dq, dil))
        dks.append(_unstride(dk, dil))
        dvs.append(_unstride(dv, dil))
    dqkv_r = jnp.concatenate(dqs + dks + dvs + [dq_w.reshape(t, dm["WQ"]), dk_w.reshape(t, dm["WK"]), dv_w.reshape(t, dm["WK"])],
                             axis=1)
    dqkv = _rope(dqkv_r, 0, dm["QW"] // HEAD_DIM, tabs[1], dm, nm + "rope")
    hi, gn = dm["HI"], dm["GN"]
    dyt, dxs0, dz, gnorm, dd_cols = _ssd_out_bwd(dya, sv["y_f"], sv["y_b"], sv["u"], sv["proj"], wl["d_cols"],
                                                          wl["ssd_norm"], hi, nm + "ssd_out")
    gr["ssd_norm"] = gnorm[0]
    gr["d_skip"] = dd_cols.reshape(SSD_HEADS, SSD_HEAD_DIM).sum(axis=1)
    sp = _scan_params(sv["dtp"], wl["a_log"], dm)
    if early is None:
        dxs1, db1, dc1, rq_f, xdx_f = _scan_bwd(sv["u"], *sp, dyt, sv["s_f"], (dxs0, None, None), 0, dm, nm + "scan_f")
        dxs2, db2, dc2, rq_r, xdx_r = _scan_bwd(sv["u"], *sp, dyt, sv["s_b"], (dxs1, db1, dc1), 1, dm, nm + "scan_b")
    else:
        ride_f, ride_b = ("w_up", "w_out"), ("w_down", "w_b")
        (dxs1, db1, dc1, rq_f, xdx_f), got = _scan_bwd(sv["u"], *sp, dyt, sv["s_f"], (dxs0, None, None), 0, dm, nm + "scan_f",
                                                       side=early.side(ride_f))
        early.took(ride_f, got)
        (dxs2, db2, dc2, rq_r, xdx_r), got = _scan_bwd(sv["u"], *sp, dyt, sv["s_b"], (dxs1, db1, dc1), 1, dm, nm + "scan_b",
                                                       side=early.side(ride_b))
        early.took(ride_b, got)

    def heads(a):
        return a.transpose(1, 0, 2).reshape(t, SSD_HEADS)

    zpad = jnp.zeros((t, LANES - dm["H2"]), F32)
    zh = jnp.zeros((t, SSD_HEADS), F32)
    rqf_p = jnp.concatenate([heads(rq_f), zh, zpad], axis=1)
    rqr_p = jnp.concatenate([zh, heads(rq_r), zpad], axis=1)
    xdx_p = jnp.concatenate([heads(xdx_f), heads(xdx_r), zpad], axis=1)
    ddtr, dbias, dalog = _ssd_param_bwd(rqf_p, rqr_p, xdx_p, sv["dtp"], sv["dtr"], wl["dt_bias"], wl["a_log_p"], nm + "ssd_par")
    gr["dt_bias"] = dbias[0, :dm["H2"]].reshape(2, SSD_HEADS)
    gr["a_log"] = dalog[0, :dm["H2"]].reshape(2, SSD_HEADS)
    du = jnp.concatenate([dxs2, db2, dc2], axis=1)
    dxbc, gr["conv_w"], gcb = _conv_bwd(du, sv["cpre"], sv["proj"], dm["OFF_XBC"], wl["conv_w"], nm + "conv")
    gr["conv_b"] = gcb[0]
    dproj = jnp.concatenate([dz, dxbc, dqkv, dg0, dg1, dg2], axis=1)
    dh_dt = _mm(ddtr, wl["w_dt"], out_dtype=F32, name=nm + "dh_dt")
    if early is None:
        gw_main = _mm(dproj, sv["h"], ta=True, name=nm + "gw_main")
    else:
        ride_g = ("w_a", "w_c")
        gw_main, got = _mm(dproj, sv["h"], ta=True, side=early.side(ride_g), name=nm + "gw_main")
        early.took(ride_g, got)
    gw_dt = _mm(ddtr, sv["h"], ta=True, name=nm + "gw_dt")
    o1, h2 = dm["OFF_QKV"], dm["H2"]
    gw_in_t = jnp.concatenate([gw_main[:o1], gw_dt[:h2], gw_main[o1:]], axis=0)
    gr["w_in"] = gw_in_t.reshape(4, (dm["NM"] + h2) // 4, d)
    if early is None:
        dh = _mm(dproj, wl["w_main"], add=dh_dt, out_dtype=F32, name=nm + "dh")
    else:
        early.early("w_in", gr["w_in"])
        dh, got = _mm(dproj, wl["w_main"], add=dh_dt, out_dtype=F32, side=early.side(("w_in",)), name=nm + "dh")
        early.took(("w_in",), got)
    dx, gmix = _rms_bwd(sv["x"], wl["g_mix"], dh, dx1, nm + "rms1")
    gr["g_mix"] = gmix[0]
    return dx, gr


def _layer_weights(full, li, dm):
    st = full["w_in"]
    o1 = dm["OFF_QKV"]
    h2 = dm["H2"]
    d = dm["D"]
    w_in_t = st[li].reshape(4 * st.shape[2], d)
    wl = dict(
        w_main=jnp.concatenate([w_in_t[:o1], w_in_t[o1 + h2:]], axis=0),
        w_dt=jnp.pad(w_in_t[o1:o1 + h2], ((0, LANES - h2), (0, 0))),
        **{n: (full[n][li] if n in full else None) for n in ("w_a", "w_b", "w_c", "w_out", "w_up", "w_down")},
        conv_w=full["conv_w"][li], conv_b=full["conv_b"][li][None, :],
        g_mix=full["g_mix"][li][None, :], g_mlp=full["g_mlp"][li][None, :], ssd_norm=full["ssd_norm"][li][None, :],
        d_cols=jnp.repeat(full["d_skip"][li], SSD_HEAD_DIM)[None, :],
        sink=full["sink"][li][None, :],
        a_log=full["a_log"][li],
        a_log_p=jnp.pad(full["a_log"][li].reshape(1, h2), ((0, 0), (0, LANES - h2))),
        dt_bias=jnp.pad(full["dt_bias"][li].reshape(1, h2), ((0, 0), (0, LANES - h2))),
    )
    assert wl["w_main"].shape == (dm["NM"], d)
    return wl


def _local_step(x, target, full, depth, late=None, early=None):
    bsz, seq, d = x.shape
    assert seq == SEQ
    dm = _dims(d)
    assert dm["OFF_GATE"] % d == 0 and dm["HI"] % (dm["HG"] * SSD_HEAD_DIM) == 0 and dm["H2"] <= LANES
    tabs = (_rope_tables(1.0), _rope_tables(-1.0))
    xt = x.reshape(bsz * seq, d)
    wls, saves = [], []
    for li in range(depth):
        wl = _layer_weights(full, li, dm)
        hosts = {}
        if li == 0 and late is not None:
            for key, (side, arrived) in late.items():
                hosts[key] = (side, lambda brought, arrived=arrived: {n: full[n][0] for n in arrived(brought)})
        xt, sv = _layer_fwd(xt, wl, tabs, dm, li, late=hosts)
        wls.append(wl)
        saves.append(sv)
    dx, loss, g_final = _loss_head(xt, full["g_final"][None, :], target.reshape(bsz * seq, d), "loss_head")
    grads = [None] * depth
    if early is not None:
        early.grads = grads
    for li in reversed(range(depth)):
        dx, grads[li] = _layer_bwd(dx, wls[li], saves[li], tabs, dm, li, early=early if li == 0 else None)
    return loss, dx.reshape(bsz, seq, d), grads, g_final[0]


BIG = ("w_in", "w_a", "w_b", "w_c", "w_out", "w_up", "w_down")
COL_SHARDED = ("w_in", "w_b", "w_up")
SMALL = ("g_mix", "conv_w", "conv_b", "dt_bias", "a_log", "d_skip", "ssd_norm", "sink", "g_mlp", "g_final")
ORDER = ("g_mix", "w_in", "conv_w", "conv_b", "dt_bias", "a_log", "d_skip", "ssd_norm", "w_a", "w_b", "w_c", "sink",
         "w_out", "g_mlp", "w_up", "w_down", "g_final")


def _unstack(name, st):
    nl, _, r, c = st.shape
    if name in COL_SHARDED:
        return jnp.moveaxis(st, 1, 2).reshape(nl, r, 4 * c)
    return st.reshape(nl, 4 * r, c)


def _restack(name, gfull):
    r, c = gfull.shape
    if name in COL_SHARDED:
        return jnp.moveaxis(gfull.reshape(r, 4, c // 4), 1, 0)
    return gfull.reshape(4, r // 4, c)


def kernel(x, g_mix, w_in, conv_w, conv_b, dt_bias, a_log, d_skip, ssd_norm, w_a, w_b, w_c, sink, w_out, g_mlp, w_up, w_down, g_final, loss_target, m_g_mix, m_w_in, m_conv_w, m_conv_b, m_dt_bias, m_a_log, m_d_skip, m_ssd_norm, m_w_a, m_w_b, m_w_c, m_sink, m_w_out, m_g_mlp, m_w_up, m_w_down, m_g_final, v_g_mix, v_w_in, v_conv_w, v_conv_b, v_dt_bias, v_a_log, v_d_skip, v_ssd_norm, v_w_a, v_w_b, v_w_c, v_sink, v_w_out, v_g_mlp, v_w_up, v_w_down, v_g_final):
    w = dict(g_mix=g_mix, w_in=w_in, conv_w=conv_w, conv_b=conv_b, dt_bias=dt_bias, a_log=a_log, d_skip=d_skip,
             ssd_norm=ssd_norm, w_a=w_a, w_b=w_b, w_c=w_c, sink=sink, w_out=w_out, g_mlp=g_mlp, w_up=w_up, w_down=w_down,
             g_final=g_final)
    m = dict(g_mix=m_g_mix, w_in=m_w_in, conv_w=m_conv_w, conv_b=m_conv_b, dt_bias=m_dt_bias, a_log=m_a_log,
             d_skip=m_d_skip, ssd_norm=m_ssd_norm, w_a=m_w_a, w_b=m_w_b, w_c=m_w_c, sink=m_sink, w_out=m_w_out,
             g_mlp=m_g_mlp, w_up=m_w_up, w_down=m_w_down, g_final=m_g_final)
    v = dict(g_mix=v_g_mix, w_in=v_w_in, conv_w=v_conv_w, conv_b=v_conv_b, dt_bias=v_dt_bias, a_log=v_a_log,
             d_skip=v_d_skip, ssd_norm=v_ssd_norm, w_a=v_w_a, w_b=v_w_b, w_c=v_w_c, sink=v_sink, w_out=v_w_out,
             g_mlp=v_g_mlp, w_up=v_w_up, w_down=v_w_down, g_final=v_g_final)
    depth = w_in.shape[0]
    assert depth == 2
    kchip = 2 * lax.axis_index("x") + lax.axis_index("y")

    full = {n: w[n] for n in SMALL if n != "conv_w"}
    tr = lambda a: jnp.swapaxes(a, 1, 2)
    full["w_in"] = _gather_chips(tr(w_in).astype(BF16), "gather_w_in")
    cw = _gather_chips(conv_w, "gather_conv_w")
    full["conv_w"] = jnp.moveaxis(cw, 1, 2).reshape(depth, CONV_WIDTH, 4 * conv_w.shape[2])
    riders = {"proj": ("w_up", "w_a", "w_b"), "scan_f": ("w_down",), "scan_b": ("w_c", "w_out")}
    late = {}
    for key, names in riders.items():
        shards = [w[n].astype(BF16) for n in names]

        def arrived(brought, names=names, shards=shards):
            for n, shard, st in zip(names, shards, brought):
                full[n] = _unstack(n, _put_own(st, shard))
            return names

        late[key] = (_gather_side(shards), arrived)

    early = _EarlyReduce(BIG)
    loss_part, grad_x, grads, gg_final = _local_step(x, loss_target, full, depth, late=late, early=early)
    gsh = {n: early.finish(n) for n in BIG}
    small_names = [n for n in SMALL if n != "g_final"]
    small_g = [jnp.stack([grads[li][n] for li in range(depth)]) for n in small_names] + [gg_final, loss_part[0, :1]]
    red = _unpack(_all_reduce_small(_pack(small_g), "allreduce_small"), small_g)
    for n, a in zip(small_names + ["g_final"], red):
        gsh[n] = a
    loss = red[-1][0]
    cshard = conv_w.shape[2]
    gsh["conv_w"] = lax.dynamic_slice_in_dim(gsh["conv_w"], kchip * cshard, cshard, axis=2)

    delta, new_m, new_v = {}, {}, {}
    for n in BIG:
        if n == "w_in":
            outs_t = _adamw(tr(w[n]), gsh[n], tr(m[n]), tr(v[n]), "adamw_" + n)
            delta[n], new_m[n], new_v[n] = [tr(o) for o in outs_t]
            gsh[n] = tr(gsh[n])
        else:
            delta[n], new_m[n], new_v[n] = _adamw(w[n], gsh[n], m[n], v[n], "adamw_" + n)
    sm = list(SMALL)
    packed = [_pack([d_[n] for n in sm])[None] for d_ in (w, gsh, m, v)]
    outs = [o[0] for o in _adamw(*packed, "adamw_small")]
    for dst, buf in zip((delta, new_m, new_v), outs):
        for n, a in zip(sm, _unpack(buf, [w[n] for n in sm])):
            dst[n] = a
    return (loss, grad_x, *[gsh[n] for n in ORDER], *[delta[n] for n in ORDER], *[new_m[n] for n in ORDER],
            *[new_v[n] for n in ORDER])
```

```python
import functools
import math

import jax
import jax.numpy as jnp
from jax import lax
from jax.experimental import pallas as pl
from jax.experimental.pallas import tpu as pltpu

F32 = jnp.float32
BF16 = jnp.bfloat16

SEQ = 2048
SSD_HEADS = 32
SSD_HEAD_DIM = 64
SSD_GROUPS = 8
SSD_STATE = 128
SSD_CHUNK = 128
CONV_WIDTH = 5
HEAD_DIM = 128
ROPE_DIM = 32
ROPE_THETA = 500000.0
DIL_PATTERNS = ((128, 1), (512, 4), (2048, 16))
DIL_HEADS = 8
WIN_Q_HEADS = 16
WIN_KV_HEADS = 4
WIN_HALF = 128
EPS = 1e-6
NEG_BIG = -1e30
ADAM_LR = 0.001
ADAM_B1 = 0.9
ADAM_B2 = 0.999
ADAM_EPS = 1e-08
ADAM_WD = 0.01
ADAM_STEP = 10

LANES = 128
ATT_BLK = 128
ROW_TILES = (320, 256, 128, 80, 64, 32, 16, 8)
VMEM_LIMIT = 48 * 1024 * 1024
MESH = pl.DeviceIdType.MESH
HIGHEST = lax.Precision.HIGHEST
NT = (((1,), (1,)), ((), ()))
TN = (((0,), (0,)), ((), ()))
NN = (((1,), (0,)), ((), ()))


def _dims(d_model):
    hi = SSD_HEADS * SSD_HEAD_DIM
    gn = SSD_GROUPS * SSD_STATE
    ng = len(DIL_PATTERNS)
    dw = DIL_HEADS * HEAD_DIM
    wq = WIN_Q_HEADS * HEAD_DIM
    wk = WIN_KV_HEADS * HEAD_DIM
    d = dict(D=d_model, HI=hi, GN=gn, XBC=hi + 2 * gn, H2=2 * SSD_HEADS, NG=ng, DW=dw, WQ=wq, WK=wk,
             QKVD=3 * ng * dw, QW=3 * ng * dw + wq + 2 * wk, HG=SSD_HEADS // SSD_GROUPS)
    d["OFF_XBC"] = hi
    d["OFF_QKV"] = hi + d["XBC"]
    d["OFF_GATE"] = d["OFF_QKV"] + d["QW"]
    d["NM"] = d["OFF_GATE"] + 3 * d_model
    return d


def _pick(n, prefs):
    for p in prefs:
        if n % p == 0:
            return p
    return n


def _params(sem):
    return pltpu.CompilerParams(dimension_semantics=sem, vmem_limit_bytes=VMEM_LIMIT)


def _sigmoid(x):
    return 1.0 / (1.0 + jnp.exp(-x))


class _Side:
    def __init__(self, args, outs, sems, start, finish):
        self.args, self.outs, self.sems, self.start, self.finish = args, outs, sems, start, finish


def _carry(side, body, n_in, n_out, n_scratch, grid):
    if side is None:
        return body
    n_sin, n_sout = len(side.args), len(side.outs)

    def wrapped(*refs):
        o0 = n_in + n_sin
        s0 = o0 + n_out + n_sout
        s_in, s_out, s_sem = refs[n_in:o0], refs[o0 + n_out:s0], refs[s0 + n_scratch:]
        ids = [pl.program_id(ax) for ax in range(len(grid))]
        first = functools.reduce(jnp.logical_and, [i == 0 for i in ids])
        last = functools.reduce(jnp.logical_and, [i == g - 1 for i, g in zip(ids, grid)])

        @pl.when(first)
        def _():
            side.start(s_in, s_out, s_sem)

        body(*refs[:n_in], *refs[o0:o0 + n_out], *refs[s0:s0 + n_scratch])

        @pl.when(last)
        def _():
            side.finish(s_in, s_out, s_sem)

    return wrapped


def _carry_call(side, body, *, name, grid, in_specs, out_specs, out_shape, scratch_shapes, sem, args):
    n_in, n_out = len(in_specs), len(out_specs)
    in_specs, out_specs, out_shape, scratch_shapes, args = (list(v) for v in (in_specs, out_specs, out_shape,
                                                                              scratch_shapes, args))
    body = _carry(side, body, n_in, n_out, len(scratch_shapes), grid)
    if side is not None:
        any_spec = pl.BlockSpec(memory_space=pl.ANY)
        in_specs += [any_spec] * len(side.args)
        args += list(side.args)
        out_specs += [any_spec] * len(side.outs)
        out_shape += list(side.outs)
        scratch_shapes += list(side.sems)
        sem = ("arbitrary",) * len(grid)
    res = pl.pallas_call(body, name=name, grid=grid, in_specs=in_specs, out_specs=out_specs, out_shape=out_shape,
                         scratch_shapes=scratch_shapes, compiler_params=_params(sem))(*args)
    return list(res[:n_out]), list(res[n_out:])


def _mm(a, b, *, ta=False, tb=False, add=None, aux=None, epi=None, rope=None, out_dtype=BF16, side=None, name):
    if ta:
        kdim, m = a.shape
    else:
        m, kdim = a.shape
    if tb:
        n, k2 = b.shape
    else:
        k2, n = b.shape
    assert kdim == k2, (a.shape, b.shape, ta, tb)
    tm = _pick(m, (1024, 512, 256, 128, 64, 32, 16, 8))
    tn = _pick(n, (1024, 512, 256, 128))
    tk = _pick(kdim, (2048, 1024, 512, 256, 128))
    nk = kdim // tk
    dims = (((0 if ta else 1,), (1 if tb else 0,)), ((), ()))
    n_in = 2 + (add is not None) + (aux is not None) + 3 * (rope is not None)
    n_out = 2 if epi == "relu2" else 1
    n_sin = len(side.args) if side else 0
    n_sout = len(side.outs) if side else 0
    grid = (m // tm, n // tn, nk)
    if rope is not None:
        assert epi is None and add is None and SEQ % tm == 0 and tn % HEAD_DIM == 0

    def body(*refs):
        a_ref, b_ref = refs[0], refs[1]
        pos = 2
        add_ref = aux_ref = None
        if add is not None:
            add_ref = refs[pos]
            pos += 1
        if aux is not None:
            aux_ref = refs[pos]
            pos += 1
        if rope is not None:
            tab_refs = refs[pos:pos + 3]
        out_refs = refs[n_in + n_sin:n_in + n_sin + n_out]
        acc_ref = refs[n_in + n_sin + n_out + n_sout]
        k = pl.program_id(2)
        if side is not None:
            s_in = refs[n_in:n_in + n_sin]
            s_out = refs[n_in + n_sin + n_out:n_in + n_sin + n_out + n_sout]
            s_sem = refs[n_in + n_sin + n_out + n_sout + 1:]
            i, j = pl.program_id(0), pl.program_id(1)

            @pl.when((i == 0) & (j == 0) & (k == 0))
            def _():
                side.start(s_in, s_out, s_sem)

        @pl.when(k == 0)
        def _():
            acc_ref[...] = jnp.zeros_like(acc_ref)

        acc_ref[...] += lax.dot_general(a_ref[...].astype(BF16), b_ref[...].astype(BF16), dims,
                                        preferred_element_type=F32)

        jj = pl.program_id(1)

        def rotate():
            half = ROPE_DIM // 2
            nhb = tn // HEAD_DIM
            cv, uv, dv = (r[...] for r in tab_refs)
            for hb in range(nhb):
                blk = slice(hb * HEAD_DIM, (hb + 1) * HEAD_DIM)
                rh = acc_ref[:, blk]
                rot = rh * cv + pltpu.roll(rh, HEAD_DIM - half, 1) * uv + pltpu.roll(rh, half, 1) * dv
                g = jj * nhb + hb
                out_refs[0][:, blk] = jnp.where((g >= rope[1]) & (g < rope[2]), rot, rh).astype(out_refs[0].dtype)

        @pl.when(k == nk - 1)
        def _():
            r = acc_ref[...]
            if add_ref is not None:
                r = r + add_ref[...].astype(F32)
            if rope is not None:
                nhb = tn // HEAD_DIM
                touched = (jj * nhb < rope[2]) & ((jj + 1) * nhb > rope[1])
                pl.when(touched)(rotate)

                @pl.when(jnp.logical_not(touched))
                def _():
                    out_refs[0][...] = r.astype(out_refs[0].dtype)
            elif epi == "relu2":
                out_refs[0][...] = r.astype(out_refs[0].dtype)
                out_refs[1][...] = jnp.square(jnp.maximum(r, 0.0)).astype(out_refs[1].dtype)
            elif epi == "relu2_bwd":
                out_refs[0][...] = (r * 2.0 * jnp.maximum(aux_ref[...].astype(F32), 0.0)).astype(out_refs[0].dtype)
            else:
                out_refs[0][...] = r.astype(out_refs[0].dtype)

        if side is not None:
            @pl.when((i == grid[0] - 1) & (j == grid[1] - 1) & (k == nk - 1))
            def _():
                side.finish(s_in, s_out, s_sem)

    a_spec = pl.BlockSpec((tk, tm), lambda i, j, k: (k, i)) if ta else pl.BlockSpec((tm, tk), lambda i, j, k: (i, k))
    b_spec = pl.BlockSpec((tn, tk), lambda i, j, k: (j, k)) if tb else pl.BlockSpec((tk, tn), lambda i, j, k: (k, j))
    o_spec = pl.BlockSpec((tm, tn), lambda i, j, k: (i, j))
    in_specs = [a_spec, b_spec]
    args = [a, b]
    if add is not None:
        in_specs.append(o_spec)
        args.append(add)
    if aux is not None:
        in_specs.append(o_spec)
        args.append(aux)
    if rope is not None:
        in_specs += [pl.BlockSpec((tm, HEAD_DIM), lambda i, j, k: (i % (SEQ // tm), 0))] * 3
        args += list(rope[0])
    out_shape = [jax.ShapeDtypeStruct((m, n), out_dtype)] * n_out
    out_specs = [o_spec] * n_out
    scratch = [pltpu.VMEM((tm, tn), F32)]
    sem = ("parallel", "parallel", "arbitrary")
    if side is not None:
        any_spec = pl.BlockSpec(memory_space=pl.ANY)
        in_specs += [any_spec] * n_sin
        args += list(side.args)
        out_shape += list(side.outs)
        out_specs += [any_spec] * n_sout
        scratch += list(side.sems)
        sem = ("arbitrary", "arbitrary", "arbitrary")
    res = pl.pallas_call(
        body, name=name, grid=grid, in_specs=in_specs, out_specs=out_specs, out_shape=out_shape, scratch_shapes=scratch,
        compiler_params=_params(sem),
    )(*args)
    if side is not None:
        return (res[0] if n_out == 1 else tuple(res[:n_out])), list(res[n_out:])
    return res if n_out == 2 else res[0]


def _rowwise(body, rows, fulls, outs, accs=(), *, tile, name):
    rows = [r if isinstance(r, tuple) else (r, 0, r.shape[1]) for r in rows]
    nrows = rows[0][0].shape[0]
    assert nrows % tile == 0, (nrows, tile)
    in_specs, args = [], []
    for arr, off, width in rows:
        assert arr.shape[0] == nrows and off % width == 0, (arr.shape, off, width)
        in_specs.append(pl.BlockSpec((tile, width), lambda i, o=off // width: (i, o)))
        args.append(arr)
    for arr in fulls:
        in_specs.append(pl.BlockSpec(arr.shape, lambda i, nd=arr.ndim: (0,) * nd))
        args.append(arr)
    out_specs, out_shape = [], []
    for cols, dt in outs:
        out_specs.append(pl.BlockSpec((tile, cols), lambda i: (i, 0)))
        out_shape.append(jax.ShapeDtypeStruct((nrows, cols), dt))
    for shp, dt in accs:
        out_specs.append(pl.BlockSpec(shp, lambda i, nd=len(shp): (0,) * nd))
        out_shape.append(jax.ShapeDtypeStruct(shp, dt))
    n_in, n_out = len(args), len(outs)

    def wrapped(*refs):
        acc_refs = refs[n_in + n_out:]
        if acc_refs:
            @pl.when(pl.program_id(0) == 0)
            def _():
                for r in acc_refs:
                    r[...] = jnp.zeros_like(r)
        body(*refs)

    return pl.pallas_call(
        wrapped, name=name, grid=(nrows // tile,), in_specs=in_specs, out_specs=out_specs, out_shape=out_shape,
        compiler_params=_params(("arbitrary",)),
    )(*args)


def _rms_fwd(x, g, name):
    def body(x_ref, g_ref, h_ref):
        xv = x_ref[...]
        rstd = lax.rsqrt(jnp.mean(xv * xv, axis=-1, keepdims=True) + EPS)
        h_ref[...] = (xv * rstd * g_ref[...]).astype(BF16)

    return _rowwise(body, [x], [g], [(x.shape[1], BF16)], tile=256, name=name)[0]


def _rms_bwd(x, g, dh, dres, name):
    def body(x_ref, dh_ref, dres_ref, g_ref, dx_ref, dg_ref):
        xv = x_ref[...]
        dv = dh_ref[...]
        rstd = lax.rsqrt(jnp.mean(xv * xv, axis=-1, keepdims=True) + EPS)
        xn = xv * rstd
        dg_ref[...] += jnp.sum(dv * xn, axis=0, keepdims=True)
        dn = dv * g_ref[...]
        dx_ref[...] = dres_ref[...] + rstd * (dn - xn * jnp.mean(dn * xn, axis=-1, keepdims=True))

    d = x.shape[1]
    return _rowwise(body, [x, dh, dres], [g], [(d, F32)], [((1, d), F32)], tile=256, name=name)


def _loss_head(x, g, target, name):
    d = x.shape[1]

    def body(x_ref, t_ref, g_ref, dx_ref, loss_ref, dg_ref):
        xv = x_ref[...]
        rstd = lax.rsqrt(jnp.mean(xv * xv, axis=-1, keepdims=True) + EPS)
        xn = xv * rstd
        err = xn * g_ref[...] - t_ref[...]
        loss_ref[...] += jnp.full((1, LANES), 0.5 / d, F32) * jnp.sum(err * err)
        dy = err * (1.0 / d)
        dg_ref[...] += jnp.sum(dy * xn, axis=0, keepdims=True)
        dn = dy * g_ref[...]
        dx_ref[...] = rstd * (dn - xn * jnp.mean(dn * xn, axis=-1, keepdims=True))

    return _rowwise(body, [x, target], [g], [(d, F32)], [((1, LANES), F32), ((1, d), F32)], tile=256, name=name)


def _rope_tables(sign):
    half = ROPE_DIM // 2
    inv = ROPE_THETA ** (-jnp.arange(0, ROPE_DIM, 2, dtype=F32) / ROPE_DIM)
    ang = jnp.arange(SEQ, dtype=F32)[:, None] * inv[None, :]
    cos, sin = jnp.cos(ang), jnp.sin(ang) * sign
    zeros = jnp.zeros((SEQ, HEAD_DIM - ROPE_DIM), F32)
    zh = jnp.zeros((SEQ, half), F32)
    c = jnp.concatenate([cos, cos, zeros + 1.0], axis=1)
    s_up = jnp.concatenate([-sin, zh, zeros], axis=1)
    s_dn = jnp.concatenate([zh, sin, zeros], axis=1)
    return c, s_up, s_dn


def _rope(src, off, nblk, tabs, dm, name):
    t = src.shape[0]
    tq = 256
    half = ROPE_DIM // 2
    win0 = 3 * dm["NG"] * DIL_HEADS
    win1 = win0 + WIN_Q_HEADS + WIN_KV_HEADS
    qw = nblk * HEAD_DIM
    assert nblk == dm["QW"] // HEAD_DIM
    wb = next(c for c in (1024, 768, 512, 384, 256, 128) if off % c == 0 and qw % c == 0)
    reps = wb // HEAD_DIM
    sb = SEQ // tq
    flag = (jnp.arange(qw, dtype=jnp.int32) // HEAD_DIM < win1).astype(F32)[None, :]

    def body(x_ref, c_ref, up_ref, dn_ref, f_ref, o_ref):
        xv = x_ref[...].astype(F32)

        def wide(r):
            v = r[...]
            return v if reps == 1 else jnp.concatenate([v] * reps, axis=1)

        rot = xv * wide(c_ref) + pltpu.roll(xv, wb - half, 1) * wide(up_ref) + pltpu.roll(xv, half, 1) * wide(dn_ref)
        o_ref[...] = jnp.where(f_ref[...] > 0.5, rot, xv).astype(BF16)

    tab_spec = pl.BlockSpec((tq, HEAD_DIM), lambda i, j: (i % sb, 0))
    return pl.pallas_call(
        body, name=name, grid=(t // tq, qw // wb),
        in_specs=[pl.BlockSpec((tq, wb), lambda i, j, o=off // wb: (i, o + j)), tab_spec, tab_spec, tab_spec,
                  pl.BlockSpec((1, wb), lambda i, j: (0, j))],
        out_specs=pl.BlockSpec((tq, wb), lambda i, j: (i, j)),
        out_shape=jax.ShapeDtypeStruct((t, qw), BF16),
        compiler_params=_params(("parallel", "parallel")),
    )(src, *tabs, flag)


def _band_mask(rows_start, cols_start, nrows, ncols, w, n, rows_are_q):
    r = rows_start + lax.broadcasted_iota(jnp.int32, (nrows, ncols), 0)
    c = cols_start + lax.broadcasted_iota(jnp.int32, (nrows, ncols), 1)
    del rows_are_q
    return (jnp.abs(r - c) <= w) & (c >= 0) & (c < n)


def _nbr_specs(make, nb):
    if nb == 1:
        return [make(lambda i: i)]
    return [make(lambda i: jnp.maximum(i - 1, 0)), make(lambda i: i), make(lambda i: jnp.minimum(i + 1, nb - 1))]


def _cat(refs, axis):
    vals = [r[...] for r in refs]
    return vals[0] if len(vals) == 1 else jnp.concatenate(vals, axis=axis)


def _head(ref, h):
    return ref[:, h * HEAD_DIM:(h + 1) * HEAD_DIM]


def _head_cat(refs, h, axis=0):
    vals = [_head(r, h) for r in refs]
    return vals[0] if len(vals) == 1 else jnp.concatenate(vals, axis=axis)


def _attn_fwd(qa, ka, va, qb, kb, vb, hq, rep, n, w, sink, out_dtype, name):
    bb = qa.shape[0]
    blk = ATT_BLK
    nb = n // blk
    nk = 1 if nb == 1 else 3
    hkv = hq // rep
    scale = HEAD_DIM ** -0.5
    has_sink = sink is not None

    def body(*refs):
        q_ref = refs[0]
        k_refs = refs[1:1 + nk]
        v_refs = refs[1 + nk:1 + 2 * nk]
        pos = 1 + 2 * nk
        sink_ref = refs[pos] if has_sink else None
        o_ref, lse_ref = refs[pos + has_sink], refs[pos + has_sink + 1]
        i = pl.program_id(1)
        k0 = (i - 1) * blk if nk == 3 else i * blk
        valid = _band_mask(i * blk, k0, blk, nk * blk, w, n, True)
        for g in range(hkv):
            kcat = _head_cat(k_refs, g)
            vcat = _head_cat(v_refs, g)
            for r in range(rep):
                h = g * rep + r
                s = lax.dot_general(_head(q_ref, h), kcat, NT, preferred_element_type=F32) * scale
                s = jnp.where(valid, s, NEG_BIG)
                m = jnp.max(s, axis=1, keepdims=True)
                if has_sink:
                    m = jnp.maximum(m, sink_ref[h])
                p = jnp.exp(s - m)
                l = jnp.sum(p, axis=1, keepdims=True)
                if has_sink:
                    l = l + jnp.exp(sink_ref[h] - m)
                o = lax.dot_general(p.astype(BF16), vcat, NN, preferred_element_type=F32) / l
                o_ref[:, h * HEAD_DIM:(h + 1) * HEAD_DIM] = o.astype(o_ref.dtype)
                lse_ref[h] = m + jnp.log(l)

    def mk(col, width):
        return lambda f: pl.BlockSpec((None, blk, width), lambda b, i, f=f: (b, f(i), col))

    qw, kw = hq * HEAD_DIM, hkv * HEAD_DIM
    in_specs = [pl.BlockSpec((None, blk, qw), lambda b, i: (b, i, qb))]
    in_specs += _nbr_specs(mk(kb, kw), nb) + _nbr_specs(mk(vb, kw), nb)
    args = [qa] + [ka] * nk + [va] * nk
    if has_sink:
        in_specs.append(pl.BlockSpec((hq, 1, 1), lambda b, i: (0, 0, 0)))
        args.append(sink)
    return pl.pallas_call(
        body, name=name, grid=(bb, nb), in_specs=in_specs,
        out_specs=[pl.BlockSpec((None, blk, qw), lambda b, i: (b, i, 0)),
                   pl.BlockSpec((None, hq, blk, 1), lambda b, i: (b, 0, i, 0))],
        out_shape=[jax.ShapeDtypeStruct((bb, n, qw), out_dtype), jax.ShapeDtypeStruct((bb, hq, n, 1), F32)],
        compiler_params=_params(("parallel", "parallel")),
    )(*args)


def _attn_dq(qa, ka, va, do, lse, delta, qb, kb, vb, hq, rep, n, w, name):
    bb = qa.shape[0]
    blk = ATT_BLK
    nb = n // blk
    nk = 1 if nb == 1 else 3
    hkv = hq // rep
    scale = HEAD_DIM ** -0.5

    def body(*refs):
        q_ref = refs[0]
        k_refs = refs[1:1 + nk]
        v_refs = refs[1 + nk:1 + 2 * nk]
        do_ref, lse_ref, dl_ref, dq_ref = refs[1 + 2 * nk:]
        i = pl.program_id(1)
        k0 = (i - 1) * blk if nk == 3 else i * blk
        valid = _band_mask(i * blk, k0, blk, nk * blk, w, n, True)
        for g in range(hkv):
            kcat = _head_cat(k_refs, g)
            vcat = _head_cat(v_refs, g)
            for r in range(rep):
                h = g * rep + r
                s = lax.dot_general(_head(q_ref, h), kcat, NT, preferred_element_type=F32) * scale
                p = jnp.exp(jnp.where(valid, s, NEG_BIG) - lse_ref[h])
                dp = lax.dot_general(_head(do_ref, h).astype(BF16), vcat, NT, preferred_element_type=F32)
                ds = p * (dp - dl_ref[h])
                dq = lax.dot_general(ds.astype(BF16), kcat, NN, preferred_element_type=F32) * scale
                dq_ref[:, h * HEAD_DIM:(h + 1) * HEAD_DIM] = dq.astype(BF16)

    def mk(col, width):
        return lambda f: pl.BlockSpec((None, blk, width), lambda b, i, f=f: (b, f(i), col))

    qw, kw = hq * HEAD_DIM, hkv * HEAD_DIM
    col_spec = pl.BlockSpec((None, hq, blk, 1), lambda b, i: (b, 0, i, 0))
    in_specs = [pl.BlockSpec((None, blk, qw), lambda b, i: (b, i, qb))]
    in_specs += _nbr_specs(mk(kb, kw), nb) + _nbr_specs(mk(vb, kw), nb)
    in_specs += [pl.BlockSpec((None, blk, qw), lambda b, i: (b, i, 0)), col_spec, col_spec]
    return pl.pallas_call(
        body, name=name, grid=(bb, nb), in_specs=in_specs,
        out_specs=pl.BlockSpec((None, blk, qw), lambda b, i: (b, i, 0)),
        out_shape=jax.ShapeDtypeStruct((bb, n, qw), BF16),
        compiler_params=_params(("parallel", "parallel")),
    )(qa, *([ka] * nk), *([va] * nk), do, lse, delta)


def _attn_dkv(qa, ka, va, do, lse_row, delta_row, qb, kb, vb, hq, rep, n, w, name):
    bb = qa.shape[0]
    blk = ATT_BLK
    nb = n // blk
    nq = 1 if nb == 1 else 3
    hkv = hq // rep
    scale = HEAD_DIM ** -0.5

    def body(*refs):
        k_ref, v_ref = refs[0], refs[1]
        q_refs = refs[2:2 + nq]
        do_refs = refs[2 + nq:2 + 2 * nq]
        lse_refs = refs[2 + 2 * nq:2 + 3 * nq]
        dl_refs = refs[2 + 3 * nq:2 + 4 * nq]
        dk_ref, dv_ref = refs[2 + 4 * nq:]
        j = pl.program_id(1)
        q0 = (j - 1) * blk if nq == 3 else j * blk
        valid = _band_mask(j * blk, q0, blk, nq * blk, w, n, False)
        for g in range(hkv):
            kg, vg = _head(k_ref, g), _head(v_ref, g)
            dk = jnp.zeros((blk, HEAD_DIM), F32)
            dv = jnp.zeros((blk, HEAD_DIM), F32)
            for r in range(rep):
                h = g * rep + r
                qcat = _head_cat(q_refs, h)
                docat = _head_cat(do_refs, h).astype(BF16)
                lse = lse_refs[0][h] if nq == 1 else jnp.concatenate([lr[h] for lr in lse_refs], axis=1)
                dl = dl_refs[0][h] if nq == 1 else jnp.concatenate([dr[h] for dr in dl_refs], axis=1)
                st = lax.dot_general(kg, qcat, NT, preferred_element_type=F32) * scale
                pt = jnp.exp(jnp.where(valid, st, NEG_BIG) - lse)
                dv = dv + lax.dot_general(pt.astype(BF16), docat, NN, preferred_element_type=F32)
                dpt = lax.dot_general(vg, docat, NT, preferred_element_type=F32)
                dst = pt * (dpt - dl)
                dk = dk + lax.dot_general(dst.astype(BF16), qcat, NN, preferred_element_type=F32) * scale
            dk_ref[:, g * HEAD_DIM:(g + 1) * HEAD_DIM] = dk.astype(BF16)
            dv_ref[:, g * HEAD_DIM:(g + 1) * HEAD_DIM] = dv.astype(BF16)

    qw, kw = hq * HEAD_DIM, hkv * HEAD_DIM

    def mkq(col):
        return lambda f: pl.BlockSpec((None, blk, qw), lambda b, j, f=f: (b, f(j), col))

    def mkrow(f):
        return pl.BlockSpec((None, hq, 1, blk), lambda b, j, f=f: (b, 0, 0, f(j)))

    in_specs = [pl.BlockSpec((None, blk, kw), lambda b, j: (b, j, kb)), pl.BlockSpec((None, blk, kw), lambda b, j: (b, j, vb))]
    in_specs += _nbr_specs(mkq(qb), nb) + _nbr_specs(mkq(0), nb) + _nbr_specs(mkrow, nb) + _nbr_specs(mkrow, nb)
    o_spec = pl.BlockSpec((None, blk, kw), lambda b, j: (b, j, 0))
    return pl.pallas_call(
        body, name=name, grid=(bb, nb), in_specs=in_specs, out_specs=[o_spec, o_spec],
        out_shape=[jax.ShapeDtypeStruct((bb, n, kw), BF16)] * 2,
        compiler_params=_params(("parallel", "parallel")),
    )(ka, va, *([qa] * nq), *([do] * nq), *([lse_row] * nq), *([delta_row] * nq))


def _head_expand(v, nh, width):
    lane_head = lax.broadcasted_iota(jnp.int32, (1, nh * width), 1) >> int(math.log2(width))
    out = jnp.zeros((v.shape[0], nh * width), F32)
    for j in range(nh):
        out = jnp.where(lane_head == j, v[:, j:j + 1], out)
    return out


def _head_sums(m, nh, width):
    lane_head = lax.broadcasted_iota(jnp.int32, (1, nh * width), 1) >> int(math.log2(width))
    col = lax.broadcasted_iota(jnp.int32, (1, nh), 1)
    out = jnp.zeros((m.shape[0], nh), F32)
    for j in range(nh):
        sj = jnp.sum(jnp.where(lane_head == j, m, 0.0), axis=1, keepdims=True)
        out = jnp.where(col == j, sj, out)
    return out


def _head_dots(a, b, nh, name):
    def body(a_ref, b_ref, o_ref):
        o_ref[...] = _head_sums(a_ref[...].astype(F32) * b_ref[...].astype(F32), nh, HEAD_DIM)

    return _rowwise(body, [a, b], [], [(nh, F32)], tile=256, name=name)[0]


def _dil_combine(outs, lses, name):
    ng = len(outs)

    def body(*refs):
        o_refs, l_refs = refs[:ng], refs[ng:2 * ng]
        y_ref, lt_ref = refs[2 * ng], refs[2 * ng + 1]
        ls = [r[...] for r in l_refs]
        m = functools.reduce(jnp.maximum, ls)
        es = [jnp.exp(v - m) for v in ls]
        tot = functools.reduce(jnp.add, es)
        acc = jnp.zeros(o_refs[0].shape, F32)
        for o_ref, e in zip(o_refs, es):
            acc = acc + _head_expand(e / tot, DIL_HEADS, HEAD_DIM) * o_ref[...]
        y_ref[...] = acc.astype(BF16)
        lt_ref[...] = m + jnp.log(tot)

    dw = outs[0].shape[1]
    return _rowwise(body, list(outs) + list(lses), [], [(dw, BF16), (DIL_HEADS, F32)], tile=256, name=name)


def _sink_grad(lse, delta, sink, name):
    def body(l_ref, d_ref, s_ref, o_ref):
        o_ref[...] -= jnp.sum(jnp.exp(s_ref[...] - l_ref[...]) * d_ref[...], axis=0, keepdims=True)

    return _rowwise(body, [lse, delta], [sink], [], [((1, lse.shape[1]), F32)], tile=512, name=name)[0]


def _shift_rows(x, d, nrows):
    if d == 0:
        return x
    rolled = pltpu.roll(x, (-d) % nrows, 0)
    row = lax.broadcasted_iota(jnp.int32, x.shape, 0)
    ok = (row + d >= 0) & (row + d < nrows)
    return jnp.where(ok, rolled, 0.0)


def _conv_fwd(proj, off, conv_w, conv_b, xbc, name):
    t = proj.shape[0]
    tc = _pick(xbc, (256, 128))
    assert off % tc == 0
    pad = (CONV_WIDTH - 1) // 2

    def body(x_ref, w_ref, b_ref, c_ref, u_ref):
        xv = x_ref[...].astype(F32)
        acc = jnp.zeros_like(xv) + b_ref[...]
        for k in range(CONV_WIDTH):
            acc = acc + w_ref[k:k + 1, :] * _shift_rows(xv, k - pad, SEQ)
        c_ref[...] = acc.astype(BF16)
        u_ref[...] = (acc * _sigmoid(acc)).astype(BF16)

    o_spec = pl.BlockSpec((SEQ, tc), lambda b, j: (b, j))
    return pl.pallas_call(
        body, name=name, grid=(t // SEQ, xbc // tc),
        in_specs=[pl.BlockSpec((SEQ, tc), lambda b, j, o=off // tc: (b, o + j)),
                  pl.BlockSpec((CONV_WIDTH, tc), lambda b, j: (0, j)), pl.BlockSpec((1, tc), lambda b, j: (0, j))],
        out_specs=[o_spec, o_spec], out_shape=[jax.ShapeDtypeStruct((t, xbc), BF16)] * 2,
        compiler_params=_params(("parallel", "parallel")),
    )(proj, conv_w, conv_b)


def _conv_bwd(du, cpre, proj, off, conv_w, name):
    t, xbc = du.shape
    tc = _pick(xbc, (256, 128))
    pad = (CONV_WIDTH - 1) // 2

    def body(du_ref, c_ref, x_ref, w_ref, dx_ref, dw_ref, db_ref):
        @pl.when(pl.program_id(1) == 0)
        def _():
            dw_ref[...] = jnp.zeros_like(dw_ref)
            db_ref[...] = jnp.zeros_like(db_ref)

        cv = c_ref[...].astype(F32)
        sg = _sigmoid(cv)
        dc = du_ref[...] * (sg * (1.0 + cv * (1.0 - sg)))
        xv = x_ref[...].astype(F32)
        dx = jnp.zeros_like(dc)
        for k in range(CONV_WIDTH):
            dx = dx + w_ref[k:k + 1, :] * _shift_rows(dc, pad - k, SEQ)
            dw_ref[k:k + 1, :] += jnp.sum(dc * _shift_rows(xv, k - pad, SEQ), axis=0, keepdims=True)
        db_ref[...] += jnp.sum(dc, axis=0, keepdims=True)
        dx_ref[...] = dx.astype(BF16)

    blk = pl.BlockSpec((SEQ, tc), lambda j, b: (b, j))
    return pl.pallas_call(
        body, name=name, grid=(xbc // tc, t // SEQ),
        in_specs=[blk, blk, pl.BlockSpec((SEQ, tc), lambda j, b, o=off // tc: (b, o + j)),
                  pl.BlockSpec((CONV_WIDTH, tc), lambda j, b: (0, j))],
        out_specs=[blk, pl.BlockSpec((CONV_WIDTH, tc), lambda j, b: (0, j)), pl.BlockSpec((1, tc), lambda j, b: (0, j))],
        out_shape=[jax.ShapeDtypeStruct((t, xbc), BF16), jax.ShapeDtypeStruct((CONV_WIDTH, xbc), F32),
                   jax.ShapeDtypeStruct((1, xbc), F32)],
        compiler_params=_params(("parallel", "arbitrary")),
    )(du, cpre, proj, conv_w)


def _dt_prep(dtr, bias, name):
    def body(r_ref, b_ref, o_ref):
        v = r_ref[...] + b_ref[...]
        o_ref[...] = jnp.maximum(v, 0.0) + jnp.log1p(jnp.exp(-jnp.abs(v)))

    return _rowwise(body, [dtr], [bias], [(dtr.shape[1], F32)], tile=512, name=name)[0]


def _scan_prelude(d, dt_ref, dtt_ref, al_ref, alt_ref, hg):
    p = SSD_HEAD_DIM
    ch = SSD_CHUNK
    a_row = -jnp.exp(al_ref[...])
    a_col = -jnp.exp(alt_ref[...])
    dtc = dt_ref[...]
    dt_x = _head_expand(dtc, hg, p)
    dta_x = dt_x * _head_expand(a_row, hg, p)
    dta_t = dtt_ref[...] * a_col
    ri = lax.broadcasted_iota(jnp.int32, (ch, ch), 0)
    ci = lax.broadcasted_iota(jnp.int32, (ch, ch), 1)
    mask = (ci <= ri) if d == 0 else (ci >= ri)
    mask_t = (ci >= ri) if d == 0 else (ci <= ri)
    tri = mask.astype(F32)
    phi_x = jnp.dot(tri, dta_x, preferred_element_type=F32, precision=HIGHEST)
    phi_r = lax.dot_general(dta_t, tri, NT, preferred_element_type=F32, precision=HIGHEST)
    tot_x = jnp.sum(dta_x, axis=0, keepdims=True)
    return dtc, dt_x, phi_x, phi_r, tot_x, mask, mask_t


def _scan_specs(d, nc, hg, dm):
    p, n, ch = SSD_HEAD_DIM, SSD_STATE, SSD_CHUNK
    w = hg * p
    b0 = dm["HI"] // n
    c0 = (dm["HI"] + dm["GN"]) // n

    def row(b, c):
        return b * nc + c

    return [
        pl.BlockSpec((ch, w), lambda b, g, c: (row(b, c), g)),
        pl.BlockSpec((ch, n), lambda b, g, c: (row(b, c), b0 + g)),
        pl.BlockSpec((ch, n), lambda b, g, c: (row(b, c), c0 + g)),
        pl.BlockSpec((None, None, ch, hg), lambda b, g, c: (d, g, row(b, c), 0)),
        pl.BlockSpec((None, None, 8, ch), lambda b, g, c: (d, g, 0, row(b, c))),
        pl.BlockSpec((None, None, 1, hg), lambda b, g, c: (d, g, 0, 0)),
        pl.BlockSpec((None, None, 8, 1), lambda b, g, c: (d, g, 0, 0)),
    ]


def _remap(spec, f):
    return pl.BlockSpec(spec.block_shape, lambda b, g, c, im=spec.index_map: im(b, g, f(c)))


def _scan_fwd(u, dtg, dttg, alg, altg, d, dm, name, side=None):
    t = u.shape[0]
    p, n, ch, hg = SSD_HEAD_DIM, SSD_STATE, SSD_CHUNK, dm["HG"]
    w = hg * p
    nc = SEQ // ch
    order = (lambda c: c) if d == 0 else (lambda c: nc - 1 - c)

    def body(x_ref, b_ref, c_ref, dt_ref, dtt_ref, al_ref, alt_ref, y_ref, sin_ref, s_ref):
        @pl.when(pl.program_id(2) == 0)
        def _():
            s_ref[...] = jnp.zeros_like(s_ref)

        dtc, dt_x, phi_x, phi_r, tot_x, mask, _ = _scan_prelude(d, dt_ref, dtt_ref, al_ref, alt_ref, hg)
        lane_head = lax.broadcasted_iota(jnp.int32, (1, w), 1) >> int(math.log2(p))
        cm, bm = c_ref[...], b_ref[...]
        cb = lax.dot_general(cm, bm, NT, preferred_element_type=F32)
        xdt = x_ref[...].astype(F32) * dt_x
        xdt_b = xdt.astype(BF16)
        ydiag = jnp.zeros((ch, w), F32)
        for j in range(hg):
            seg = phi_x[:, j * p:j * p + 1] - phi_r[j:j + 1, :]
            mj = (cb * jnp.exp(jnp.where(mask, seg, NEG_BIG))).astype(BF16)
            ydiag = ydiag + jnp.dot(mj, jnp.where(lane_head == j, xdt_b, jnp.zeros_like(xdt_b)), preferred_element_type=F32)
        s = s_ref[...]
        y_ref[...] = ydiag + jnp.dot(cm, s.astype(BF16), preferred_element_type=F32) * jnp.exp(phi_x)
        sin_ref[...] = s
        wm = (xdt * jnp.exp(tot_x - phi_x)).astype(BF16)
        s_ref[...] = s * jnp.exp(tot_x) + lax.dot_general(bm, wm, TN, preferred_element_type=F32)

    specs = [_remap(s, order) for s in _scan_specs(d, nc, hg, dm)]
    own, brought = _carry_call(
        side, body, name=name, grid=(t // SEQ, SSD_GROUPS, nc), in_specs=specs,
        out_specs=[_remap(pl.BlockSpec((ch, w), lambda b, g, c: (b * nc + c, g)), order),
                   _remap(pl.BlockSpec((None, None, n, w), lambda b, g, c: (b * nc + c, g, 0, 0)), order)],
        out_shape=[jax.ShapeDtypeStruct((t, dm["HI"]), F32), jax.ShapeDtypeStruct((t // ch, SSD_GROUPS, n, w), F32)],
        scratch_shapes=[pltpu.VMEM((n, w), F32)], sem=("parallel", "parallel", "arbitrary"),
        args=(u, u, u, dtg, dttg, alg, altg))
    return own if side is None else (own, brought)


def _scan_bwd(u, dtg, dttg, alg, altg, dy, sin, adds, d, dm, name, side=None):
    t = u.shape[0]
    p, n, ch, hg = SSD_HEAD_DIM, SSD_STATE, SSD_CHUNK, dm["HG"]
    w = hg * p
    nc = SEQ // ch
    order = (lambda c: nc - 1 - c) if d == 0 else (lambda c: c)
    has_bc_add = adds[1] is not None

    def body(*refs):
        x_ref, b_ref, c_ref, dt_ref, dtt_ref, al_ref, alt_ref, dy_ref, sin_ref, ax_ref = refs[:10]
        pos = 10
        ab_ref = ac_ref = None
        if has_bc_add:
            ab_ref, ac_ref = refs[10], refs[11]
            pos = 12
        dxs_ref, db_ref, dc_ref, rq_ref, xdx_ref, ds_ref = refs[pos:]

        @pl.when(pl.program_id(2) == 0)
        def _():
            ds_ref[...] = jnp.zeros_like(ds_ref)

        dtc, dt_x, phi_x, phi_r, tot_x, mask, mask_t = _scan_prelude(d, dt_ref, dtt_ref, al_ref, alt_ref, hg)
        lane_head = lax.broadcasted_iota(jnp.int32, (1, w), 1) >> int(math.log2(p))
        cm, bm = c_ref[...], b_ref[...]
        cb = lax.dot_general(cm, bm, NT, preferred_element_type=F32)
        cb_t = lax.dot_general(bm, cm, NT, preferred_element_type=F32)
        xs = x_ref[...].astype(F32)
        xdt = xs * dt_x
        xdt_b = xdt.astype(BF16)
        dy = dy_ref[...]
        dy_b = dy.astype(BF16)
        zero_b = jnp.zeros_like(dy_b)
        col = lax.broadcasted_iota(jnp.int32, (1, hg), 1)
        dxp = jnp.zeros((ch, w), F32)
        a_ls = jnp.zeros((ch, ch), F32)
        a_sl = jnp.zeros((ch, ch), F32)
        dphi = jnp.zeros((ch, hg), F32)
        for j in range(hg):
            pc = phi_x[:, j * p:j * p + 1]
            pr = phi_r[j:j + 1, :]
            l_ls = jnp.exp(jnp.where(mask, pc - pr, NEG_BIG))
            l_sl = jnp.exp(jnp.where(mask_t, pr - pc, NEG_BIG))
            dy_j = jnp.where(lane_head == j, dy_b, zero_b)
            xdt_j = jnp.where(lane_head == j, xdt_b, zero_b)
            dxp = dxp + jnp.dot((cb_t * l_sl).astype(BF16), dy_j, preferred_element_type=F32)
            g_ls = l_ls * lax.dot_general(dy_j, xdt_b, NT, preferred_element_type=F32)
            g_sl = l_sl * lax.dot_general(xdt_j, dy_b, NT, preferred_element_type=F32)
            a_ls = a_ls + g_ls
            a_sl = a_sl + g_sl
            pair = jnp.sum(g_ls * cb, axis=1, keepdims=True) - jnp.sum(g_sl * cb_t, axis=1, keepdims=True)
            dphi = jnp.where(col == j, pair, dphi)
        ds = ds_ref[...]
        ds_b = ds.astype(BF16)
        sin = sin_ref[...]
        sin_b = sin.astype(BF16)
        e_tp = jnp.exp(tot_x - phi_x)
        e_p = jnp.exp(phi_x)
        dxp_off = e_tp * jnp.dot(bm, ds_b, preferred_element_type=F32)
        dxp = dxp + dxp_off
        dxs_ref[...] = ax_ref[...] + dxp * dt_x
        xdx_ref[...] = _head_sums(xs * dxp, hg, p)
        y_off = jnp.dot(cm, sin_b, preferred_element_type=F32) * e_p
        st_t = _head_sums(xdt * dxp_off, hg, p)
        dphi = dphi + _head_sums(dy * y_off, hg, p) - st_t
        dtot = _head_sums(jnp.sum(ds * sin, axis=0, keepdims=True) * jnp.exp(tot_x), hg, p) + jnp.sum(st_t, axis=0, keepdims=True)
        cum = jnp.dot(mask_t.astype(F32), _head_expand(dphi, hg, p), preferred_element_type=F32, precision=HIGHEST)
        ddta = jnp.zeros((ch, hg), F32)
        for j in range(hg):
            ddta = jnp.where(col == j, cum[:, j * p:j * p + 1], ddta)
        rq_ref[...] = ddta + dtot
        dye = (dy * e_p).astype(BF16)
        dcv = jnp.dot(a_ls.astype(BF16), bm, preferred_element_type=F32)
        dcv = dcv + lax.dot_general(dye, sin_b, NT, preferred_element_type=F32)
        dbv = jnp.dot(a_sl.astype(BF16), cm, preferred_element_type=F32)
        dbv = dbv + lax.dot_general((xdt * e_tp).astype(BF16), ds_b, NT, preferred_element_type=F32)
        if has_bc_add:
            dcv = dcv + ac_ref[...]
            dbv = dbv + ab_ref[...]
        dc_ref[...] = dcv
        db_ref[...] = dbv
        ds_ref[...] = ds * jnp.exp(tot_x) + lax.dot_general(cm, dye, TN, preferred_element_type=F32)

    def sp(spec):
        return _remap(spec, order)

    xw = pl.BlockSpec((ch, w), lambda b, g, c: (b * nc + c, g))
    gn_blk = pl.BlockSpec((ch, n), lambda b, g, c: (b * nc + c, g))
    small = pl.BlockSpec((None, ch, hg), lambda b, g, c: (g, b * nc + c, 0))
    in_specs = [sp(s) for s in _scan_specs(d, nc, hg, dm)]
    in_specs += [sp(xw), sp(pl.BlockSpec((None, None, n, w), lambda b, g, c: (b * nc + c, g, 0, 0))), sp(xw)]
    args = [u, u, u, dtg, dttg, alg, altg, dy, sin, adds[0]]
    if has_bc_add:
        in_specs += [sp(gn_blk), sp(gn_blk)]
        args += [adds[1], adds[2]]
    own, brought = _carry_call(
        side, body, name=name, grid=(t // SEQ, SSD_GROUPS, nc), in_specs=in_specs,
        out_specs=[sp(xw), sp(gn_blk), sp(gn_blk), sp(small), sp(small)],
        out_shape=[jax.ShapeDtypeStruct((t, dm["HI"]), F32), jax.ShapeDtypeStruct((t, dm["GN"]), F32),
                   jax.ShapeDtypeStruct((t, dm["GN"]), F32), jax.ShapeDtypeStruct((SSD_GROUPS, t, hg), F32),
                   jax.ShapeDtypeStruct((SSD_GROUPS, t, hg), F32)],
        scratch_shapes=[pltpu.VMEM((n, w), F32)], sem=("parallel", "parallel", "arbitrary"), args=args)
    return own if side is None else (own, brought)


def _ssd_param_bwd(rq_f, rq_r, xdx, dtp, dtr, bias, alog, name):
    def body(rf_ref, rr_ref, xdx_ref, dt_ref, dtr_ref, b_ref, al_ref, o_ref, db_ref, da_ref):
        a = -jnp.exp(al_ref[...])
        d_dta = rf_ref[...] + rr_ref[...]
        ddt = a * d_dta + xdx_ref[...]
        ddtr = ddt * _sigmoid(dtr_ref[...] + b_ref[...])
        o_ref[...] = ddtr
        db_ref[...] += jnp.sum(ddtr, axis=0, keepdims=True)
        da_ref[...] += a * jnp.sum(dt_ref[...] * d_dta, axis=0, keepdims=True)

    return _rowwise(body, [rq_f, rq_r, xdx, dtp, dtr], [bias, alog], [(LANES, F32)],
                    [((1, LANES), F32), ((1, LANES), F32)], tile=512, name=name)


def _ssd_out_fwd(y_f, y_b, u, proj, dcols, gn, hi, name):
    def body(yf_ref, yb_ref, x_ref, z_ref, d_ref, g_ref, o_ref):
        ytot = yf_ref[...] + yb_ref[...] + d_ref[...] * x_ref[...].astype(F32)
        zv = z_ref[...].astype(F32)
        yz = ytot * (zv * _sigmoid(zv))
        rstd = lax.rsqrt(jnp.mean(yz * yz, axis=-1, keepdims=True) + EPS)
        o_ref[...] = (yz * rstd * g_ref[...]).astype(BF16)

    return _rowwise(body, [y_f, y_b, (u, 0, hi), (proj, 0, hi)], [dcols, gn], [(hi, BF16)], tile=256, name=name)[0]


def _ssd_out_bwd(dya, y_f, y_b, u, proj, dcols, gn, hi, name):
    def body(dy_ref, yf_ref, yb_ref, x_ref, z_ref, d_ref, g_ref, dyt_ref, dxs_ref, dz_ref, dg_ref, dd_ref):
        xv = x_ref[...].astype(F32)
        ytot = yf_ref[...] + yb_ref[...] + d_ref[...] * xv
        zv = z_ref[...].astype(F32)
        sg = _sigmoid(zv)
        sz = zv * sg
        yz = ytot * sz
        rstd = lax.rsqrt(jnp.mean(yz * yz, axis=-1, keepdims=True) + EPS)
        yn = yz * rstd
        dv = dy_ref[...]
        dg_ref[...] += jnp.sum(dv * yn, axis=0, keepdims=True)
        dn = dv * g_ref[...]
        dyz = rstd * (dn - yn * jnp.mean(dn * yn, axis=-1, keepdims=True))
        dyt = dyz * sz
        dyt_ref[...] = dyt
        dxs_ref[...] = dyt * d_ref[...]
        dz_ref[...] = (dyz * ytot * (sg * (1.0 + zv * (1.0 - sg)))).astype(BF16)
        dd_ref[...] += jnp.sum(dyt * xv, axis=0, keepdims=True)

    return _rowwise(body, [dya, y_f, y_b, (u, 0, hi), (proj, 0, hi)], [dcols, gn],
                    [(hi, F32), (hi, F32), (hi, BF16)], [((1, hi), F32), ((1, hi), F32)], tile=128, name=name)


def _gate_fwd(pa, pb, pc, proj, off, d, name):
    def body(a_ref, b_ref, c_ref, g0_ref, g1_ref, g2_ref, o_ref):
        acc = _sigmoid(g0_ref[...].astype(F32)) * a_ref[...]
        acc = acc + _sigmoid(g1_ref[...].astype(F32)) * b_ref[...]
        acc = acc + _sigmoid(g2_ref[...].astype(F32)) * c_ref[...]
        o_ref[...] = acc.astype(BF16)

    rows = [pa, pb, pc] + [(proj, off + k * d, d) for k in range(3)]
    return _rowwise(body, rows, [], [(d, BF16)], tile=256, name=name)[0]


def _gate_bwd(dm_, pa, pb, pc, proj, off, d, name):
    def body(dm_ref, a_ref, b_ref, c_ref, g0_ref, g1_ref, g2_ref, da_ref, db_ref, dc_ref, dg0_ref, dg1_ref, dg2_ref):
        dmv = dm_ref[...]
        for p_ref, g_ref, dp_ref, dg_ref in ((a_ref, g0_ref, da_ref, dg0_ref), (b_ref, g1_ref, db_ref, dg1_ref),
                                             (c_ref, g2_ref, dc_ref, dg2_ref)):
            sg = _sigmoid(g_ref[...].astype(F32))
            dp_ref[...] = (dmv * sg).astype(BF16)
            dg_ref[...] = (dmv * p_ref[...] * sg * (1.0 - sg)).astype(BF16)

    rows = [dm_, pa, pb, pc] + [(proj, off + k * d, d) for k in range(3)]
    return _rowwise(body, rows, [], [(d, BF16)] * 6, tile=128, name=name)


def _adamw(w, g, m, v, name):
    nl, rows, cols = w.shape
    tile = _pick(rows, ROW_TILES)
    c1 = 1.0 / (1.0 - ADAM_B1 ** ADAM_STEP)
    c2 = 1.0 / (1.0 - ADAM_B2 ** ADAM_STEP)

    def body(w_ref, g_ref, m_ref, v_ref, d_ref, nm_ref, nv_ref):
        gv = g_ref[...]
        nm = ADAM_B1 * m_ref[...] + (1.0 - ADAM_B1) * gv
        nv = ADAM_B2 * v_ref[...] + (1.0 - ADAM_B2) * (gv * gv)
        nm_ref[...] = nm
        nv_ref[...] = nv
        d_ref[...] = -ADAM_LR * ((nm * c1) / (jnp.sqrt(nv * c2) + ADAM_EPS) + ADAM_WD * w_ref[...])

    blk = pl.BlockSpec((None, tile, cols), lambda l, i: (l, i, 0))
    return pl.pallas_call(
        body, name=name, grid=(nl, rows // tile), in_specs=[blk] * 4, out_specs=[blk] * 3,
        out_shape=[jax.ShapeDtypeStruct(w.shape, F32)] * 3, compiler_params=_params(("parallel", "parallel")),
    )(w, g, m, v)


ANY = pl.BlockSpec(memory_space=pl.ANY)


def _place():
    x, y, c = lax.axis_index("x"), lax.axis_index("y"), lax.axis_index("c")
    chips = [(1 - x, y), (x, 1 - y), (1 - x, 1 - y)]
    return x, y, c, chips


def _gather_copies(src, out, ssem, rsem, base):
    x, y, c, chips = _place()
    k = 2 * x + y

    def copy(j, kk, layer, to, own=False):
        return pltpu.make_async_remote_copy(
            src_ref=src.at[layer] if own else out.at[layer, kk], dst_ref=out.at[layer, kk],
            send_sem=ssem.at[base + j], recv_sem=rsem.at[base + j], device_id=to, device_id_type=MESH)

    first = [copy(j, k, c, (cx, cy, c), own=True) for j, (cx, cy) in enumerate(chips)]
    passed = [copy(3 + j, 2 * cx + cy, c, (x, y, 1 - c)) for j, (cx, cy) in enumerate(chips)]
    landed = [copy(j, 2 * cx + cy, c, (x, y, c)) for j, (cx, cy) in enumerate(chips)]
    handed = [copy(3 + j, 2 * cx + cy, 1 - c, (x, y, c)) for j, (cx, cy) in enumerate(chips)]
    return first, passed, landed, handed


def _gather_start(src, out, ssem, rsem, base):
    for cp in _gather_copies(src, out, ssem, rsem, base)[0]:
        cp.start()


def _gather_finish(src, out, ssem, rsem, base):
    first, passed, landed, handed = _gather_copies(src, out, ssem, rsem, base)
    for arrived, onward in zip(landed, passed):
        arrived.wait_recv()
        onward.start()
    for arrived in handed:
        arrived.wait_recv()
    for cp in first + passed:
        cp.wait_send()


def _gather_side(arrs):
    n = len(arrs)

    def start(srcs, outs, sems):
        for a in range(n):
            _gather_start(srcs[a], outs[a], sems[0], sems[1], 6 * a)

    def finish(srcs, outs, sems):
        for a in range(n):
            _gather_finish(srcs[a], outs[a], sems[0], sems[1], 6 * a)

    outs = [jax.ShapeDtypeStruct((a.shape[0], 4) + a.shape[1:], a.dtype) for a in arrs]
    return _Side(list(arrs), outs, [pltpu.SemaphoreType.DMA((6 * n,)), pltpu.SemaphoreType.DMA((6 * n,))], start, finish)


def _put_own(st, arr):
    kchip = 2 * lax.axis_index("x") + lax.axis_index("y")
    return lax.dynamic_update_slice(st, arr[:, None], (0, kchip) + (0,) * (arr.ndim - 1))


def _chip_copies(src, out, ssem, rsem, base=0):
    x, y, c, chips = _place()
    return [pltpu.make_async_remote_copy(src_ref=src.at[2 * cx + cy], dst_ref=out.at[j], send_sem=ssem.at[base + j],
                                         recv_sem=rsem.at[base + j], device_id=(cx, cy, c), device_id_type=MESH)
            for j, (cx, cy) in enumerate(chips)]


def _chips_side(parts):
    n = len(parts)

    def start(srcs, outs, sems):
        for a in range(n):
            for cp in _chip_copies(srcs[a], outs[a], sems[0], sems[1], 3 * a):
                cp.start()

    def finish(srcs, outs, sems):
        for a in range(n):
            for cp in _chip_copies(srcs[a], outs[a], sems[0], sems[1], 3 * a):
                cp.wait()

    return _Side(list(parts), [jax.ShapeDtypeStruct((3,) + p.shape[1:], p.dtype) for p in parts],
                 [pltpu.SemaphoreType.DMA((3 * n,)), pltpu.SemaphoreType.DMA((3 * n,))], start, finish)


def _gather_chips(arr, name):
    def body(src, out, ssem, rsem):
        _gather_start(src, out, ssem, rsem, 0)
        _gather_finish(src, out, ssem, rsem, 0)

    st = pl.pallas_call(
        body, name=name, in_specs=[ANY], out_specs=ANY,
        out_shape=jax.ShapeDtypeStruct((arr.shape[0], 4) + arr.shape[1:], arr.dtype),
        scratch_shapes=[pltpu.SemaphoreType.DMA((6,)), pltpu.SemaphoreType.DMA((6,))],
    )(arr)
    return _put_own(st, arr)


def _pair_swap(g0, g1, name):
    def body(src0, src1, out, ssem, rsem):
        x, y, c, _ = _place()

        def swap(src):
            cp = pltpu.make_async_remote_copy(src_ref=src, dst_ref=out, send_sem=ssem, recv_sem=rsem,
                                              device_id=(x, y, 1 - c), device_id_type=MESH)
            cp.start()
            cp.wait()

        @pl.when(c == 0)
        def _():
            swap(src1)

        @pl.when(c == 1)
        def _():
            swap(src0)

    return pl.pallas_call(
        body, name=name, in_specs=[ANY, ANY], out_specs=ANY, out_shape=jax.ShapeDtypeStruct(g0.shape, g0.dtype),
        scratch_shapes=[pltpu.SemaphoreType.DMA, pltpu.SemaphoreType.DMA],
    )(g0, g1)


def _chip_exchange(p, name):
    def body(src, out, ssem, rsem):
        cps = _chip_copies(src, out, ssem, rsem)
        for cp in cps:
            cp.start()
        for cp in cps:
            cp.wait()

    return pl.pallas_call(
        body, name=name, in_specs=[ANY], out_specs=ANY, out_shape=jax.ShapeDtypeStruct((3,) + p.shape[1:], p.dtype),
        scratch_shapes=[pltpu.SemaphoreType.DMA((3,)), pltpu.SemaphoreType.DMA((3,))],
    )(p)


def _pair_share(r, name):
    def body(src, out, ssem, rsem):
        x, y, c, _ = _place()
        cp = pltpu.make_async_remote_copy(src_ref=src, dst_ref=out, send_sem=ssem, recv_sem=rsem,
                                          device_id=(x, y, 1 - c), device_id_type=MESH)
        cp.start()
        cp.wait()

    theirs = pl.pallas_call(
        body, name=name, in_specs=[ANY], out_specs=ANY, out_shape=jax.ShapeDtypeStruct(r.shape, r.dtype),
        scratch_shapes=[pltpu.SemaphoreType.DMA, pltpu.SemaphoreType.DMA],
    )(r)
    first = lax.axis_index("c") == 0
    return jnp.stack([jnp.where(first, r, theirs), jnp.where(first, theirs, r)])


def _pair_share_all(rs, name):
    n = len(rs)

    def body(*refs):
        srcs, outs, (ssem, rsem) = refs[:n], refs[n:2 * n], refs[2 * n:]
        x, y, c, _ = _place()
        cps = [pltpu.make_async_remote_copy(src_ref=srcs[a], dst_ref=outs[a], send_sem=ssem.at[a], recv_sem=rsem.at[a],
                                            device_id=(x, y, 1 - c), device_id_type=MESH) for a in range(n)]
        for cp in cps:
            cp.start()
        for cp in cps:
            cp.wait()

    theirs = pl.pallas_call(
        body, name=name, in_specs=[ANY] * n, out_specs=[ANY] * n,
        out_shape=[jax.ShapeDtypeStruct(r.shape, r.dtype) for r in rs],
        scratch_shapes=[pltpu.SemaphoreType.DMA((n,)), pltpu.SemaphoreType.DMA((n,))],
    )(*rs)
    first = lax.axis_index("c") == 0
    return [jnp.stack([jnp.where(first, r, t), jnp.where(first, t, r)]) for r, t in zip(rs, theirs)]


def _sum_pair(g0, g1, got, sel, name):
    four, rows, cols = g0.shape
    flat = four * rows
    tile = _pick(flat, ROW_TILES)

    def body(sel_ref, a0_ref, a1_ref, b_ref, o_ref):
        mine = jnp.where(sel_ref[0] == 0, a0_ref[...], a1_ref[...])
        o_ref[...] = (mine.astype(F32) + b_ref[...].astype(F32)).astype(BF16)

    def layer(l):
        return pl.BlockSpec((tile, cols), lambda i, s: (jnp.where(s[0] == l, i, 0), 0))

    blk = pl.BlockSpec((tile, cols), lambda i, s: (i, 0))
    return pl.pallas_call(
        body, name=name, out_shape=jax.ShapeDtypeStruct((flat, cols), BF16),
        grid_spec=pltpu.PrefetchScalarGridSpec(
            num_scalar_prefetch=1, grid=(flat // tile,), in_specs=[layer(0), layer(1), blk], out_specs=blk),
        compiler_params=_params(("arbitrary",)),
    )(sel, g0.reshape(flat, cols), g1.reshape(flat, cols), got.reshape(flat, cols)).reshape(four, rows, cols)


def _sum4(a, b, sel, name):
    _, rows, cols = a.shape
    tile = _pick(rows, ROW_TILES)

    def body(sel_ref, a_ref, b0_ref, b1_ref, b2_ref, o_ref):
        acc = a_ref[...].astype(F32) + b0_ref[...].astype(F32)
        acc = acc + b1_ref[...].astype(F32)
        o_ref[...] = acc + b2_ref[...].astype(F32)

    bspec = [pl.BlockSpec((None, tile, cols), lambda i, s, j=j: (j, i, 0)) for j in range(3)]
    return pl.pallas_call(
        body, name=name, out_shape=jax.ShapeDtypeStruct((rows, cols), F32),
        grid_spec=pltpu.PrefetchScalarGridSpec(
            num_scalar_prefetch=1, grid=(rows // tile,),
            in_specs=[pl.BlockSpec((None, tile, cols), lambda i, s: (s[0], i, 0))] + bspec,
            out_specs=pl.BlockSpec((tile, cols), lambda i, s: (i, 0))),
        compiler_params=_params(("parallel",)),
    )(sel, a, b, b, b)


def _reduce_scatter(g0, g1, name):
    part = _reduce_pair(g0, g1, name)
    return _reduce_chips(part, _chip_exchange(part, name + "_chips"), name)


def _reduce_pair(g0, g1, name):
    c = lax.axis_index("c").astype(jnp.int32).reshape(1)
    got = _pair_swap(g0, g1, name + "_pair")
    return _sum_pair(g0, g1, got, c, name + "_add2")


def _reduce_chips(part, others, name):
    k = (2 * lax.axis_index("x") + lax.axis_index("y")).astype(jnp.int32).reshape(1)
    total = _sum4(part, others, k, name + "_add4")
    return _pair_share(total, name + "_share")


class _EarlyReduce:
    def __init__(self, names):
        self.names, self.grads, self.part, self.others = names, None, {}, {}

    def early(self, n, g0):
        if n in self.names:
            shards = [g if n == "w_in" else _restack(n, g) for g in (g0, self.grads[1][n])]
            self.part[n] = _reduce_pair(*shards, "rs_" + n)

    def side(self, names):
        return _chips_side([self.part[n] for n in names])

    def took(self, names, brought):
        for n, o in zip(names, brought):
            self.others[n] = o

    def finish(self, n):
        return _reduce_chips(self.part[n], self.others[n], "rs_" + n)

    def finish_all(self):
        k = (2 * lax.axis_index("x") + lax.axis_index("y")).astype(jnp.int32).reshape(1)
        totals = [_sum4(self.part[n], self.others[n], k, "rs_" + n + "_add4") for n in self.names]
        return dict(zip(self.names, _pair_share_all(totals, "rs_share")))


def _all_reduce_small(buf, name):
    rows = buf.shape[0]

    def body(src, out, slots, ssem, rsem):
        x, y, c, _ = _place()
        me = 4 * x + 2 * y + c
        slots[me] = src[...]
        cps = []
        for j in range(1, 8):
            px, py, pc = x ^ (j >> 2), y ^ ((j >> 1) & 1), c ^ (j & 1)
            cps.append(pltpu.make_async_remote_copy(src_ref=src, dst_ref=slots.at[me], send_sem=ssem.at[j - 1],
                                                    recv_sem=rsem.at[j - 1], device_id=(px, py, pc), device_id_type=MESH))
        for cp in cps:
            cp.start()
        for j in range(1, 8):
            peer = me ^ j
            pltpu.make_async_remote_copy(src_ref=src, dst_ref=slots.at[peer], send_sem=ssem.at[j - 1], recv_sem=rsem.at[j - 1],
                                         device_id=(x, y, c), device_id_type=MESH).wait_recv()
        for cp in cps:
            cp.wait_send()
        acc = slots[0]
        for d in range(1, 8):
            acc = acc + slots[d]
        out[...] = acc

    vm = pl.BlockSpec(memory_space=pltpu.VMEM)
    return pl.pallas_call(
        body, name=name, in_specs=[vm], out_specs=vm, out_shape=jax.ShapeDtypeStruct((rows, LANES), F32),
        scratch_shapes=[pltpu.VMEM((8, rows, LANES), F32), pltpu.SemaphoreType.DMA((7,)), pltpu.SemaphoreType.DMA((7,))],
    )(buf)


def _pack(arrs):
    flat = jnp.concatenate([a.astype(F32).reshape(-1) for a in arrs])
    n = flat.shape[0]
    padded = -(-n // (8 * LANES)) * (8 * LANES)
    return jnp.pad(flat, (0, padded - n)).reshape(padded // LANES, LANES)


def _unpack(buf, like):
    flat = buf.reshape(-1)
    out, pos = [], 0
    for a in like:
        out.append(flat[pos:pos + a.size].reshape(a.shape))
        pos += a.size
    return out


def _stride(t2d, dil):
    t, w = t2d.shape
    b = t // SEQ
    return t2d.reshape(b, SEQ // dil, dil, w).transpose(0, 2, 1, 3).reshape(b * dil, SEQ // dil, w)


def _unstride(t3d, dil):
    bb, n, w = t3d.shape
    b = bb // dil
    return t3d.reshape(b, dil, n, w).transpose(0, 2, 1, 3).reshape(b * SEQ, w)


def _stat_cols(st, dil, heads):
    s3 = _stride(st, dil)
    return s3.transpose(0, 2, 1)[..., None]


def _stat_rows(col):
    bb, h, n, _ = col.shape
    return col.reshape(bb, h, 1, n)


def _scan_params(dtp, alog, dm):
    t = dtp.shape[0]
    g, hg = SSD_GROUPS, dm["HG"]
    dt4 = dtp[:, :dm["H2"]].reshape(t, 2, g, hg)
    dtg = dt4.transpose(1, 2, 0, 3)
    dttg = jnp.pad(dt4.transpose(1, 2, 3, 0), ((0, 0), (0, 0), (0, 8 - hg), (0, 0)))
    al = alog.reshape(2, g, 1, hg)
    alt = jnp.pad(alog.reshape(2, g, hg, 1), ((0, 0), (0, 0), (0, 8 - hg), (0, 0)))
    return dtg, dttg, al, alt


def _layer_fwd(x, wl, tabs, dm, li, late=None):
    d = dm["D"]
    nm = f"l{li}_"
    h = _rms_fwd(x, wl["g_mix"], nm + "rms1")
    late = late or {}
    qkv0 = dm["OFF_QKV"] // HEAD_DIM
    rope = (tabs[0], qkv0, qkv0 + 3 * dm["NG"] * DIL_HEADS + WIN_Q_HEADS + WIN_KV_HEADS)
    if "proj" not in late:
        proj = _mm(h, wl["w_main"], tb=True, rope=rope, name=nm + "proj")
    else:
        proj, brought = _mm(h, wl["w_main"], tb=True, rope=rope, side=late["proj"][0], name=nm + "proj")
        wl.update(late["proj"][1](brought))
    dtr = _mm(h, wl["w_dt"], tb=True, out_dtype=F32, name=nm + "proj_dt")
    cpre, u = _conv_fwd(proj, dm["OFF_XBC"], wl["conv_w"], wl["conv_b"], dm["XBC"], nm + "conv")
    dtp = _dt_prep(dtr, wl["dt_bias"], nm + "dt")
    sp = _scan_params(dtp, wl["a_log"], dm)
    scans = []
    for dirn, key in ((0, "scan_f"), (1, "scan_b")):
        if key in late:
            own, brought = _scan_fwd(u, *sp, dirn, dm, nm + key, side=late[key][0])
            wl.update(late[key][1](brought))
        else:
            own = _scan_fwd(u, *sp, dirn, dm, nm + key)
        scans.append(own)
    (y_f, s_f), (y_b, s_b) = scans
    y_a = _ssd_out_fwd(y_f, y_b, u, proj, wl["d_cols"], wl["ssd_norm"], dm["HI"], nm + "ssd_out")
    qkv = proj[:, dm["OFF_QKV"]:dm["OFF_QKV"] + dm["QW"]]
    ng, dw = dm["NG"], dm["DW"]
    outs, lses, xgs = [], [], []
    for gi, (window, dil) in enumerate(DIL_PATTERNS):
        cols = [qkv[:, s * ng * dw + gi * dw:s * ng * dw + (gi + 1) * dw] for s in range(3)]
        xg = _stride(jnp.concatenate(cols, axis=1), dil)
        o, lse = _attn_fwd(xg, xg, xg, 0, 1, 2, DIL_HEADS, 1, SEQ // dil, window // (2 * dil), None, F32, nm + f"dil{gi}")
        xgs.append(xg)
        outs.append(_unstride(o, dil))
        lses.append(_unstride(lse[..., 0].transpose(0, 2, 1), dil))
    y_bm, lse_tot = _dil_combine(outs, lses, nm + "dil_mix")
    bsz = x.shape[0] // SEQ
    xw = qkv[:, dm["QKVD"]:].reshape(bsz, SEQ, dm["WQ"] + 2 * dm["WK"])
    rep = WIN_Q_HEADS // WIN_KV_HEADS
    y_c3, lse_w = _attn_fwd(xw, xw, xw, 0, rep, rep + 1, WIN_Q_HEADS, rep, SEQ, WIN_HALF,
                            wl["sink"].reshape(WIN_Q_HEADS, 1, 1), BF16, nm + "win")
    y_c = y_c3.reshape(x.shape[0], dm["WQ"])
    pa = _mm(y_a, wl["w_a"], out_dtype=F32, name=nm + "pa")
    pb = _mm(y_bm, wl["w_b"], out_dtype=F32, name=nm + "pb")
    pc = _mm(y_c, wl["w_c"], out_dtype=F32, name=nm + "pc")
    merged = _gate_fwd(pa, pb, pc, proj, dm["OFF_GATE"], d, nm + "gate")
    x1 = _mm(merged, wl["w_out"], add=x, out_dtype=F32, name=nm + "out")
    hm = _rms_fwd(x1, wl["g_mlp"], nm + "rms2")
    up, act = _mm(hm, wl["w_up"], epi="relu2", name=nm + "up")
    x2 = _mm(act, wl["w_down"], add=x1, out_dtype=F32, name=nm + "down")
    saved = dict(x=x, h=h, proj=proj, dtr=dtr, cpre=cpre, u=u, dtp=dtp, y_f=y_f, y_b=y_b, s_f=s_f, s_b=s_b, y_a=y_a,
                 xw=xw, xgs=xgs, y_bm=y_bm, lse_tot=lse_tot, y_c=y_c, lse_w=lse_w, pa=pa, pb=pb, pc=pc,
                 merged=merged, x1=x1, hm=hm, up=up, act=act)
    return x2, saved


def _layer_bwd(dx2, wl, sv, tabs, dm, li, early=None):
    d = dm["D"]
    t = dx2.shape[0]
    bsz = t // SEQ
    nm = f"l{li}b_"
    gr = {}
    dup = _mm(dx2, wl["w_down"], tb=True, aux=sv["up"], epi="relu2_bwd", name=nm + "dup")
    gr["w_down"] = _mm(sv["act"], dx2, ta=True, name=nm + "gw_down")
    dhm = _mm(dup, wl["w_up"], tb=True, out_dtype=F32, name=nm + "dhm")
    gr["w_up"] = _mm(sv["hm"], dup, ta=True, name=nm + "gw_up")
    if early is not None:
        early.early("w_down", gr["w_down"])
        early.early("w_up", gr["w_up"])
    dx1, gmlp = _rms_bwd(sv["x1"], wl["g_mlp"], dhm, dx2, nm + "rms2")
    gr["g_mlp"] = gmlp[0]
    dmerged = _mm(dx1, wl["w_out"], tb=True, out_dtype=F32, name=nm + "dmerged")
    gr["w_out"] = _mm(sv["merged"], dx1, ta=True, name=nm + "gw_out")
    dpa, dpb, dpc, dg0, dg1, dg2 = _gate_bwd(dmerged, sv["pa"], sv["pb"], sv["pc"], sv["proj"], dm["OFF_GATE"], d, nm + "gate")
    dya = _mm(dpa, wl["w_a"], tb=True, out_dtype=F32, name=nm + "dya")
    gr["w_a"] = _mm(sv["y_a"], dpa, ta=True, name=nm + "gw_a")
    dyb = _mm(dpb, wl["w_b"], tb=True, out_dtype=F32, name=nm + "dyb")
    gr["w_b"] = _mm(sv["y_bm"], dpb, ta=True, name=nm + "gw_b")
    dyc = _mm(dpc, wl["w_c"], tb=True, out_dtype=F32, name=nm + "dyc")
    gr["w_c"] = _mm(sv["y_c"], dpc, ta=True, name=nm + "gw_c")
    if early is not None:
        for n in ("w_out", "w_a", "w_b", "w_c"):
            early.early(n, gr[n])
    ng, dw = dm["NG"], dm["DW"]
    xw = sv["xw"]
    rep = WIN_Q_HEADS // WIN_KV_HEADS
    delta_w = _head_dots(dyc, sv["y_c"], WIN_Q_HEADS, nm + "win_delta")
    dl_col = _stat_cols(delta_w, 1, WIN_Q_HEADS)
    lse_w = sv["lse_w"]
    dyc3 = dyc.reshape(bsz, SEQ, dm["WQ"])
    wargs = (0, rep, rep + 1, WIN_Q_HEADS, rep, SEQ, WIN_HALF)
    dq_w = _attn_dq(xw, xw, xw, dyc3, lse_w, dl_col, *wargs, nm + "win_dq")
    dk_w, dv_w = _attn_dkv(xw, xw, xw, dyc3, _stat_rows(lse_w), _stat_rows(dl_col), *wargs, nm + "win_dkv")
    lse_w2 = lse_w[..., 0].transpose(0, 2, 1).reshape(t, WIN_Q_HEADS)
    gr["sink"] = _sink_grad(lse_w2, delta_w, wl["sink"], nm + "sink")[0]
    delta_d = _head_dots(dyb, sv["y_bm"], DIL_HEADS, nm + "dil_delta")
    dqs, dks, dvs = [], [], []
    for gi, (window, dil) in enumerate(DIL_PATTERNS):
        xg = sv["xgs"][gi]
        n = SEQ // dil
        do_g = _stride(dyb, dil)
        lse_c = _stat_cols(sv["lse_tot"], dil, DIL_HEADS)
        dl_c = _stat_cols(delta_d, dil, DIL_HEADS)
        dargs = (0, 1, 2, DIL_HEADS, 1, n, window // (2 * dil))
        dq = _attn_dq(xg, xg, xg, do_g, lse_c, dl_c, *dargs, nm + f"dil{gi}_dq")
        dk, dv = _attn_dkv(xg, xg, xg, do_g, _stat_rows(lse_c), _stat_rows(dl_c), *dargs, nm + f"dil{gi}_dkv")
        dqs.append(_unstride(dq, dil))
        dks.append(_unstride(dk, dil))
        dvs.append(_unstride(dv, dil))
    dqkv_r = jnp.concatenate(dqs + dks + dvs + [dq_w.reshape(t, dm["WQ"]), dk_w.reshape(t, dm["WK"]), dv_w.reshape(t, dm["WK"])],
                             axis=1)
    dqkv = _rope(dqkv_r, 0, dm["QW"] // HEAD_DIM, tabs[1], dm, nm + "rope")
    hi, gn = dm["HI"], dm["GN"]
    dyt, dxs0, dz, gnorm, dd_cols = _ssd_out_bwd(dya, sv["y_f"], sv["y_b"], sv["u"], sv["proj"], wl["d_cols"],
                                                          wl["ssd_norm"], hi, nm + "ssd_out")
    gr["ssd_norm"] = gnorm[0]
    gr["d_skip"] = dd_cols.reshape(SSD_HEADS, SSD_HEAD_DIM).sum(axis=1)
    sp = _scan_params(sv["dtp"], wl["a_log"], dm)
    if early is None:
        dxs1, db1, dc1, rq_f, xdx_f = _scan_bwd(sv["u"], *sp, dyt, sv["s_f"], (dxs0, None, None), 0, dm, nm + "scan_f")
        dxs2, db2, dc2, rq_r, xdx_r = _scan_bwd(sv["u"], *sp, dyt, sv["s_b"], (dxs1, db1, dc1), 1, dm, nm + "scan_b")
    else:
        ride_f, ride_b = ("w_up", "w_out"), ("w_down", "w_b")
        (dxs1, db1, dc1, rq_f, xdx_f), got = _scan_bwd(sv["u"], *sp, dyt, sv["s_f"], (dxs0, None, None), 0, dm, nm + "scan_f",
                                                       side=early.side(ride_f))
        early.took(ride_f, got)
        (dxs2, db2, dc2, rq_r, xdx_r), got = _scan_bwd(sv["u"], *sp, dyt, sv["s_b"], (dxs1, db1, dc1), 1, dm, nm + "scan_b",
                                                       side=early.side(ride_b))
        early.took(ride_b, got)

    def heads(a):
        return a.transpose(1, 0, 2).reshape(t, SSD_HEADS)

    zpad = jnp.zeros((t, LANES - dm["H2"]), F32)
    zh = jnp.zeros((t, SSD_HEADS), F32)
    rqf_p = jnp.concatenate([heads(rq_f), zh, zpad], axis=1)
    rqr_p = jnp.concatenate([zh, heads(rq_r), zpad], axis=1)
    xdx_p = jnp.concatenate([heads(xdx_f), heads(xdx_r), zpad], axis=1)
    ddtr, dbias, dalog = _ssd_param_bwd(rqf_p, rqr_p, xdx_p, sv["dtp"], sv["dtr"], wl["dt_bias"], wl["a_log_p"], nm + "ssd_par")
    gr["dt_bias"] = dbias[0, :dm["H2"]].reshape(2, SSD_HEADS)
    gr["a_log"] = dalog[0, :dm["H2"]].reshape(2, SSD_HEADS)
    du = jnp.concatenate([dxs2, db2, dc2], axis=1)
    dxbc, gr["conv_w"], gcb = _conv_bwd(du, sv["cpre"], sv["proj"], dm["OFF_XBC"], wl["conv_w"], nm + "conv")
    gr["conv_b"] = gcb[0]
    dproj = jnp.concatenate([dz, dxbc, dqkv, dg0, dg1, dg2], axis=1)
    dh_dt = _mm(ddtr, wl["w_dt"], out_dtype=F32, name=nm + "dh_dt")
    if early is None:
        gw_main = _mm(dproj, sv["h"], ta=True, name=nm + "gw_main")
    else:
        ride_g = ("w_a", "w_c")
        gw_main, got = _mm(dproj, sv["h"], ta=True, side=early.side(ride_g), name=nm + "gw_main")
        early.took(ride_g, got)
    gw_dt = _mm(ddtr, sv["h"], ta=True, name=nm + "gw_dt")
    o1, h2 = dm["OFF_QKV"], dm["H2"]
    gw_in_t = jnp.concatenate([gw_main[:o1], gw_dt[:h2], gw_main[o1:]], axis=0)
    gr["w_in"] = gw_in_t.reshape(4, (dm["NM"] + h2) // 4, d)
    if early is None:
        dh = _mm(dproj, wl["w_main"], add=dh_dt, out_dtype=F32, name=nm + "dh")
    else:
        early.early("w_in", gr["w_in"])
        dh, got = _mm(dproj, wl["w_main"], add=dh_dt, out_dtype=F32, side=early.side(("w_in",)), name=nm + "dh")
        early.took(("w_in",), got)
    dx, gmix = _rms_bwd(sv["x"], wl["g_mix"], dh, dx1, nm + "rms1")
    gr["g_mix"] = gmix[0]
    return dx, gr


def _layer_weights(full, li, dm):
    st = full["w_in"]
    o1 = dm["OFF_QKV"]
    h2 = dm["H2"]
    d = dm["D"]
    w_in_t = st[li].reshape(4 * st.shape[2], d)
    wl = dict(
        w_main=jnp.concatenate([w_in_t[:o1], w_in_t[o1 + h2:]], axis=0),
        w_dt=jnp.pad(w_in_t[o1:o1 + h2], ((0, LANES - h2), (0, 0))),
        **{n: (full[n][li] if n in full else None) for n in ("w_a", "w_b", "w_c", "w_out", "w_up", "w_down")},
        conv_w=full["conv_w"][li], conv_b=full["conv_b"][li][None, :],
        g_mix=full["g_mix"][li][None, :], g_mlp=full["g_mlp"][li][None, :], ssd_norm=full["ssd_norm"][li][None, :],
        d_cols=jnp.repeat(full["d_skip"][li], SSD_HEAD_DIM)[None, :],
        sink=full["sink"][li][None, :],
        a_log=full["a_log"][li],
        a_log_p=jnp.pad(full["a_log"][li].reshape(1, h2), ((0, 0), (0, LANES - h2))),
        dt_bias=jnp.pad(full["dt_bias"][li].reshape(1, h2), ((0, 0), (0, LANES - h2))),
    )
    assert wl["w_main"].shape == (dm["NM"], d)
    return wl


def _local_step(x, target, full, depth, late=None, early=None):
    bsz, seq, d = x.shape
    assert seq == SEQ
    dm = _dims(d)
    assert dm["OFF_GATE"] % d == 0 and dm["HI"] % (dm["HG"] * SSD_HEAD_DIM) == 0 and dm["H2"] <= LANES
    tabs = (_rope_tables(1.0), _rope_tables(-1.0))
    xt = x.reshape(bsz * seq, d)
    wls, saves = [], []
    for li in range(depth):
        wl = _layer_weights(full, li, dm)
        hosts = {}
        if li == 0 and late is not None:
            for key, (side, arrived) in late.items():
                hosts[key] = (side, lambda brought, arrived=arrived: {n: full[n][0] for n in arrived(brought)})
        xt, sv = _layer_fwd(xt, wl, tabs, dm, li, late=hosts)
        wls.append(wl)
        saves.append(sv)
    dx, loss, g_final = _loss_head(xt, full["g_final"][None, :], target.reshape(bsz * seq, d), "loss_head")
    grads = [None] * depth
    if early is not None:
        early.grads = grads
    for li in reversed(range(depth)):
        dx, grads[li] = _layer_bwd(dx, wls[li], saves[li], tabs, dm, li, early=early if li == 0 else None)
    return loss, dx.reshape(bsz, seq, d), grads, g_final[0]


BIG = ("w_in", "w_a", "w_b", "w_c", "w_out", "w_up", "w_down")
COL_SHARDED = ("w_in", "w_b", "w_up")
SMALL = ("g_mix", "conv_w", "conv_b", "dt_bias", "a_log", "d_skip", "ssd_norm", "sink", "g_mlp", "g_final")
ORDER = ("g_mix", "w_in", "conv_w", "conv_b", "dt_bias", "a_log", "d_skip", "ssd_norm", "w_a", "w_b", "w_c", "sink",
         "w_out", "g_mlp", "w_up", "w_down", "g_final")


def _unstack(name, st):
    nl, _, r, c = st.shape
    if name in COL_SHARDED:
        return jnp.moveaxis(st, 1, 2).reshape(nl, r, 4 * c)
    return st.reshape(nl, 4 * r, c)


def _restack(name, gfull):
    r, c = gfull.shape
    if name in COL_SHARDED:
        return jnp.moveaxis(gfull.reshape(r, 4, c // 4), 1, 0)
    return gfull.reshape(4, r // 4, c)


def kernel(x, g_mix, w_in, conv_w, conv_b, dt_bias, a_log, d_skip, ssd_norm, w_a, w_b, w_c, sink, w_out, g_mlp, w_up, w_down, g_final, loss_target, m_g_mix, m_w_in, m_conv_w, m_conv_b, m_dt_bias, m_a_log, m_d_skip, m_ssd_norm, m_w_a, m_w_b, m_w_c, m_sink, m_w_out, m_g_mlp, m_w_up, m_w_down, m_g_final, v_g_mix, v_w_in, v_conv_w, v_conv_b, v_dt_bias, v_a_log, v_d_skip, v_ssd_norm, v_w_a, v_w_b, v_w_c, v_sink, v_w_out, v_g_mlp, v_w_up, v_w_down, v_g_final):
    w = dict(g_mix=g_mix, w_in=w_in, conv_w=conv_w, conv_b=conv_b, dt_bias=dt_bias, a_log=a_log, d_skip=d_skip,
             ssd_norm=ssd_norm, w_a=w_a, w_b=w_b, w_c=w_c, sink=sink, w_out=w_out, g_mlp=g_mlp, w_up=w_up, w_down=w_down,
             g_final=g_final)
    m = dict(g_mix=m_g_mix, w_in=m_w_in, conv_w=m_conv_w, conv_b=m_conv_b, dt_bias=m_dt_bias, a_log=m_a_log,
             d_skip=m_d_skip, ssd_norm=m_ssd_norm, w_a=m_w_a, w_b=m_w_b, w_c=m_w_c, sink=m_sink, w_out=m_w_out,
             g_mlp=m_g_mlp, w_up=m_w_up, w_down=m_w_down, g_final=m_g_final)
    v = dict(g_mix=v_g_mix, w_in=v_w_in, conv_w=v_conv_w, conv_b=v_conv_b, dt_bias=v_dt_bias, a_log=v_a_log,
             d_skip=v_d_skip, ssd_norm=v_ssd_norm, w_a=v_w_a, w_b=v_w_b, w_c=v_w_c, sink=v_sink, w_out=v_w_out,
             g_mlp=v_g_mlp, w_up=v_w_up, w_down=v_w_down, g_final=v_g_final)
    depth = w_in.shape[0]
    assert depth == 2
    kchip = 2 * lax.axis_index("x") + lax.axis_index("y")

    full = {n: w[n] for n in SMALL if n != "conv_w"}
    tr = lambda a: jnp.swapaxes(a, 1, 2)
    full["w_in"] = _gather_chips(tr(w_in).astype(BF16), "gather_w_in")
    cw = _gather_chips(conv_w, "gather_conv_w")
    full["conv_w"] = jnp.moveaxis(cw, 1, 2).reshape(depth, CONV_WIDTH, 4 * conv_w.shape[2])
    riders = {"proj": ("w_up", "w_a", "w_b"), "scan_f": ("w_down",), "scan_b": ("w_c", "w_out")}
    late = {}
    for key, names in riders.items():
        shards = [w[n].astype(BF16) for n in names]

        def arrived(brought, names=names, shards=shards):
            for n, shard, st in zip(names, shards, brought):
                full[n] = _unstack(n, _put_own(st, shard))
            return names

        late[key] = (_gather_side(shards), arrived)

    early = _EarlyReduce(BIG)
    loss_part, grad_x, grads, gg_final = _local_step(x, loss_target, full, depth, late=late, early=early)
    gsh = early.finish_all()
    small_names = [n for n in SMALL if n != "g_final"]
    small_g = [jnp.stack([grads[li][n] for li in range(depth)]) for n in small_names] + [gg_final, loss_part[0, :1]]
    red = _unpack(_all_reduce_small(_pack(small_g), "allreduce_small"), small_g)
    for n, a in zip(small_names + ["g_final"], red):
        gsh[n] = a
    loss = red[-1][0]
    cshard = conv_w.shape[2]
    gsh["conv_w"] = lax.dynamic_slice_in_dim(gsh["conv_w"], kchip * cshard, cshard, axis=2)

    delta, new_m, new_v = {}, {}, {}
    for n in BIG:
        if n == "w_in":
            outs_t = _adamw(tr(w[n]), gsh[n], tr(m[n]), tr(v[n]), "adamw_" + n)
            delta[n], new_m[n], new_v[n] = [tr(o) for o in outs_t]
            gsh[n] = tr(gsh[n])
        else:
            delta[n], new_m[n], new_v[n] = _adamw(w[n], gsh[n], m[n], v[n], "adamw_" + n)
    sm = list(SMALL)
    packed = [_pack([d_[n] for n in sm])[None] for d_ in (w, gsh, m, v)]
    outs = [o[0] for o in _adamw(*packed, "adamw_small")]
    for dst, buf in zip((delta, new_m, new_v), outs):
        for n, a in zip(sm, _unpack(buf, [w[n] for n in sm])):
            dst[n] = a
    return (loss, grad_x, *[gsh[n] for n in ORDER], *[delta[n] for n in ORDER], *[new_m[n] for n in ORDER],
            *[new_v[n] for n in ORDER])
```
